```python
import math
import jax, jax.numpy as jnp
from jax import lax
import numpy as np

D_MODEL = 1024
BATCH = 4
SEQ = 4096
DEPTH = 4

GRID_W = 64
CTX_LEN = 256

N_DIFF_HEADS = 4
DIFF_QK_DIM = 64
DIFF_V_DIM = 2 * DIFF_QK_DIM
DIFF_WIDTH = N_DIFF_HEADS * DIFF_V_DIM
CONV_WIDTH = D_MODEL - DIFF_WIDTH
SHORT_CONV_K = 3
Q_BLOCK = 128
ROPE_BASE = 10000.0
AB_IN_COLS = 3 * DIFF_WIDTH + 3 * CONV_WIDTH

SSD_D_INNER = 2 * D_MODEL
SSD_HEAD_DIM = 64
SSD_HEADS = SSD_D_INNER // SSD_HEAD_DIM
SSD_GROUPS = 4
SSD_STATE = 128
SSD_CONV_K = 3
SSD_CHUNK = 128
SSD_CONV_CH = SSD_D_INNER + 2 * SSD_GROUPS * SSD_STATE
SSD_IN_COLS = SSD_D_INNER + SSD_CONV_CH + 2 * SSD_HEADS

MOE_GROUPS = 4
MOE_EXPERTS_PER_GROUP = 8
MOE_EXPERTS = MOE_GROUPS * MOE_EXPERTS_PER_GROUP
MOE_TOP_K = 2
MOE_HIDDEN = 512
MOE_BLOCK = 128

kernel_name = 'hybrid_diffattn_shortconv_ssd_hmoe_prefix'


def rms_norm(x, g, eps=1e-6):
    xf = x.astype(jnp.float32)
    y = xf * lax.rsqrt(jnp.mean(xf * xf, axis=-1, keepdims=True) + eps)
    return (y * g.astype(jnp.float32)).astype(x.dtype)


def modulate(h, shift, scale):
    return h * (1 + scale) + shift


def dwconv_centred(u, w):
    k = w.shape[0]
    return lax.conv_general_dilated(u, w[:, None, :], window_strides=(1,),
                                    padding=((k // 2, k // 2),),
                                    dimension_numbers=('NWC', 'WIO', 'NWC'),
                                    feature_group_count=u.shape[-1])


def axial_rope_tables(n):
    rows = n // GRID_W
    row = jnp.broadcast_to(jnp.arange(rows, dtype=jnp.float32)[:, None], (rows, GRID_W)).reshape(n)
    col = jnp.broadcast_to(jnp.arange(GRID_W, dtype=jnp.float32)[None, :], (rows, GRID_W)).reshape(n)
    axis_dim = DIFF_QK_DIM // 2
    inv_freq = ROPE_BASE ** (-jnp.arange(0, axis_dim, 2, dtype=jnp.float32) / axis_dim)
    ang = jnp.stack([row[:, None] * inv_freq, col[:, None] * inv_freq], axis=1)
    return jnp.cos(ang), jnp.sin(ang)


def apply_axial_rope(x, cos, sin):
    xf = x.astype(jnp.float32).reshape(*x.shape[:-1], 2, 2, DIFF_QK_DIM // 4)
    x1, x2 = xf[..., 0, :], xf[..., 1, :]
    c = cos[None, :, None, None]
    s = sin[None, :, None, None]
    out = jnp.stack([x1 * c - x2 * s, x2 * c + x1 * s], axis=-2)
    return out.reshape(x.shape).astype(x.dtype)


def diff_attend(q, k, v, lam):
    s = jnp.einsum('bqhcd,bkhcd->bhcqk', q, k).astype(jnp.float32) * (DIFF_QK_DIM ** -0.5)
    p = jax.nn.softmax(s, axis=-1)
    pd = p[:, :, 0] - lam * p[:, :, 1]
    return jnp.einsum('bhqk,bkhd->bqhd', pd.astype(v.dtype), v)


def attn_conv_mixer(h_lat, h_ctx, w_in, w_out, lq1, lk1, lq2, lk2, subln_g, conv_w, layer_idx):
    lam_init = 0.8 - 0.6 * math.exp(-0.3 * layer_idx)
    f32 = jnp.float32
    lam = (jnp.exp(jnp.sum(lq1.astype(f32) * lk1.astype(f32)))
           - jnp.exp(jnp.sum(lq2.astype(f32) * lk2.astype(f32))) + lam_init)

    def project(h):
        b, n = h.shape[:2]
        q, k, v, gb, gc, xc = jnp.split(h @ w_in, [DIFF_WIDTH, 2 * DIFF_WIDTH, 3 * DIFF_WIDTH,
                                                   3 * DIFF_WIDTH + CONV_WIDTH,
                                                   3 * DIFF_WIDTH + 2 * CONV_WIDTH], axis=-1)
        q = q.reshape(b, n, N_DIFF_HEADS, 2, DIFF_QK_DIM)
        k = k.reshape(b, n, N_DIFF_HEADS, 2, DIFF_QK_DIM)
        v = v.reshape(b, n, N_DIFF_HEADS, DIFF_V_DIM)
        conv_out = gb * dwconv_centred(gc * xc, conv_w)
        return q, k, v, conv_out

    def finish(o, conv_out):
        o = rms_norm(o, subln_g) * (1 - lam_init)
        o = o.reshape(*o.shape[:2], DIFF_WIDTH)
        return jnp.concatenate([o, conv_out], axis=-1) @ w_out

    q_c, k_c, v_c, conv_c = project(h_ctx)
    o_c = diff_attend(q_c, k_c, v_c, lam)

    b, n = h_lat.shape[:2]
    q_l, k_l, v_l, conv_l = project(h_lat)
    cos, sin = axial_rope_tables(n)
    q_l = apply_axial_rope(q_l, cos, sin)
    k_l = apply_axial_rope(k_l, cos, sin)
    k_all = jnp.concatenate([k_c, k_l], axis=1)
    v_all = jnp.concatenate([v_c, v_l], axis=1)
    nb = n // Q_BLOCK
    qb = jnp.swapaxes(q_l.reshape(b, nb, Q_BLOCK, N_DIFF_HEADS, 2, DIFF_QK_DIM), 0, 1)
    o_l = lax.map(lambda qq: diff_attend(qq, k_all, v_all, lam), qb)
    o_l = jnp.swapaxes(o_l, 0, 1).reshape(b, n, N_DIFF_HEADS, DIFF_V_DIM)
    return finish(o_l, conv_l), finish(o_c, conv_c)


def ssd_chunked(x, dt, A, Bm, Cm, h0):
    b, L, H, P = x.shape
    G, N = Bm.shape[2], Bm.shape[3]
    E = H // G
    Q = SSD_CHUNK
    nc = L // Q
    f32 = jnp.float32
    xdt = (x.astype(f32) * dt[..., None]).reshape(b, nc, Q, G, E, P)
    a = (dt * A).reshape(b, nc, Q, G, E).transpose(0, 3, 4, 1, 2)
    a_cs = jnp.cumsum(a, axis=-1)
    Bc = Bm.astype(f32).reshape(b, nc, Q, G, N)
    Cc = Cm.astype(f32).reshape(b, nc, Q, G, N)
    lower = jnp.tril(jnp.ones((Q, Q), dtype=bool))
    seg = a_cs[..., :, None] - a_cs[..., None, :]
    Lmat = jnp.exp(jnp.where(lower, seg, -jnp.inf))
    CB = jnp.einsum('bclgn,bcsgn->bgcls', Cc, Bc)
    y_diag = jnp.einsum('bgcls,bgecls,bcsgep->bclgep', CB, Lmat, xdt)
    decay_to_end = jnp.exp(a_cs[..., -1:] - a_cs)
    chunk_states = jnp.einsum('bclgn,bgecl,bclgep->cbgepn', Bc, decay_to_end, xdt)
    chunk_decay = jnp.exp(a_cs[..., -1]).transpose(3, 0, 1, 2)

    def step(state, inp):
        dec, st = inp
        return state * dec[..., None, None] + st, state

    final, prev = lax.scan(step, h0.reshape(b, G, E, P, N), (chunk_decay, chunk_states))
    y_off = jnp.einsum('bclgn,cbgepn,bgecl->bclgep', Cc, prev, jnp.exp(a_cs))
    y = (y_diag + y_off).reshape(b, L, H, P)
    return y, final.reshape(b, H, P, N)


def ssd_mixer(h_lat, h_ctx, w_in, conv_w, conv_b, A_log, dt_bias, D_skip, norm_g, w_out):
    f32 = jnp.float32
    A = -jnp.exp(A_log.astype(f32))

    def prep(h):
        b, n = h.shape[:2]
        z, xbc, dt = jnp.split(h @ w_in, [SSD_D_INNER, SSD_D_INNER + SSD_CONV_CH], axis=-1)
        xbc = jax.nn.silu(dwconv_centred(xbc, conv_w) + conv_b)
        xs, Bm, Cm = jnp.split(xbc, [SSD_D_INNER, SSD_D_INNER + SSD_GROUPS * SSD_STATE], axis=-1)
        xs = xs.reshape(b, n, SSD_HEADS, SSD_HEAD_DIM)
        Bm = Bm.reshape(b, n, SSD_GROUPS, SSD_STATE)
        Cm = Cm.reshape(b, n, SSD_GROUPS, SSD_STATE)
        dt = jax.nn.softplus(dt.astype(f32).reshape(b, n, 2, SSD_HEADS) + dt_bias.astype(f32))
        return z, xs, Bm, Cm, dt

    def flip(t):
        return jnp.flip(t, axis=1)

    def bidir(xs, Bm, Cm, dt, h0_f, h0_b):
        y_f, s_f = ssd_chunked(xs, dt[:, :, 0], A[0], Bm, Cm, h0_f)
        y_b, s_b = ssd_chunked(flip(xs), flip(dt[:, :, 1]), A[1], flip(Bm), flip(Cm), h0_b)
        return y_f + flip(y_b), s_f, s_b

    def finish(z, xs, y):
        b, n = z.shape[:2]
        y = y + xs.astype(f32) * D_skip.astype(f32)[:, None]
        y = y.reshape(b, n, SSD_D_INNER) * jax.nn.silu(z.astype(f32))
        y = rms_norm(y.reshape(b, n, SSD_GROUPS, SSD_D_INNER // SSD_GROUPS),
                     norm_g.reshape(SSD_GROUPS, SSD_D_INNER // SSD_GROUPS))
        return y.reshape(b, n, SSD_D_INNER).astype(z.dtype) @ w_out

    z_c, x_c, B_c, C_c, dt_c = prep(h_ctx)
    zeros = jnp.zeros((h_ctx.shape[0], SSD_HEADS, SSD_HEAD_DIM, SSD_STATE), f32)
    y_c, s_cf, s_cb = bidir(x_c, B_c, C_c, dt_c, zeros, zeros)
    z_l, x_l, B_l, C_l, dt_l = prep(h_lat)
    y_l, _, _ = bidir(x_l, B_l, C_l, dt_l, s_cf, s_cb)
    return finish(z_l, x_l, y_l), finish(z_c, x_c, y_c)


def hier_moe(h, wg, bg, we, be, w_gate, w_up, w_down):
    T, D = h.shape
    gp = jax.nn.softmax((h @ wg + bg).astype(jnp.float32), axis=-1)
    g_idx = jnp.argmax(gp, axis=-1)
    g_w = jnp.max(gp, axis=-1)
    el = (h @ we + be).astype(jnp.float32).reshape(T, MOE_GROUPS, MOE_EXPERTS_PER_GROUP)
    el = jnp.take_along_axis(el, g_idx[:, None, None], axis=1)[:, 0]
    ep = jax.nn.softmax(el, axis=-1)
    top_p, top_i = lax.top_k(ep, MOE_TOP_K)
    wts = (g_w[:, None] * top_p / jnp.sum(top_p, axis=-1, keepdims=True)).reshape(-1)
    eid = (g_idx[:, None] * MOE_EXPERTS_PER_GROUP + top_i).reshape(-1).astype(jnp.int32)
    A = T * MOE_TOP_K
    tok = jnp.arange(A, dtype=jnp.int32) // MOE_TOP_K
    order = jnp.argsort(eid)
    s_eid, s_tok, s_w = eid[order], tok[order], wts[order]
    counts = jnp.bincount(eid, length=MOE_EXPERTS)
    start = jnp.cumsum(counts) - counts
    padded = (counts + MOE_BLOCK - 1) // MOE_BLOCK * MOE_BLOCK
    pend = jnp.cumsum(padded)
    dest = (pend - padded)[s_eid] + jnp.arange(A, dtype=jnp.int32) - start[s_eid]
    n_blocks = (A + MOE_EXPERTS * (MOE_BLOCK - 1)) // MOE_BLOCK
    slot_tok = jnp.zeros((n_blocks * MOE_BLOCK,), jnp.int32).at[dest].set(s_tok)
    slot_w = jnp.zeros((n_blocks * MOE_BLOCK,), jnp.float32).at[dest].set(s_w)
    block_eid = jnp.minimum(jnp.searchsorted(pend, jnp.arange(n_blocks) * MOE_BLOCK, side='right'),
                            MOE_EXPERTS - 1)
    xb = h[slot_tok].reshape(n_blocks, MOE_BLOCK, D)

    def expert_block(args):
        xblk, e = args
        return (jax.nn.silu(xblk @ w_gate[e]) * (xblk @ w_up[e])) @ w_down[e]

    yb = lax.map(expert_block, (xb, block_eid)).reshape(-1, D)
    return jnp.zeros_like(h).at[slot_tok].add(yb * slot_w[:, None].astype(h.dtype))


def setup_inputs(seed: int = 0) -> dict:
    key = jax.random.key(seed)
    ks = jax.random.split(key, 32)
    f32 = jnp.float32
    n_even = (DEPTH + 1) // 2
    n_odd = DEPTH // 2

    def nrm(k, shape, s):
        return jax.random.normal(k, shape, f32) * s

    def gain(k, shape):
        return 1.0 + 0.02 * jax.random.normal(k, shape, f32)

    dt0 = jnp.exp(jax.random.uniform(ks[20], (n_odd, 2, SSD_HEADS), f32,
                                     minval=math.log(1e-3), maxval=math.log(1e-1)))
    return {
        'x': nrm(ks[0], (BATCH, SEQ, D_MODEL), 1.0),
        'c': nrm(ks[1], (BATCH, D_MODEL), 1.0),
        'ctx': nrm(ks[2], (BATCH, CTX_LEN, D_MODEL), 1.0),
        'c_ctx': nrm(ks[3], (D_MODEL,), 1.0),
        'ada_w': nrm(ks[4], (DEPTH, D_MODEL, 6 * D_MODEL), 0.5 * D_MODEL ** -0.5),
        'ada_b': nrm(ks[5], (DEPTH, 6 * D_MODEL), 0.02),
        'norm1_g': gain(ks[6], (DEPTH, D_MODEL)),
        'norm2_g': gain(ks[7], (DEPTH, D_MODEL)),
        'ab_w_in': nrm(ks[8], (n_even, D_MODEL, AB_IN_COLS), D_MODEL ** -0.5),
        'ab_w_out': nrm(ks[9], (n_even, DIFF_WIDTH + CONV_WIDTH, D_MODEL), (DIFF_WIDTH + CONV_WIDTH) ** -0.5),
        'diff_lq1': nrm(ks[10], (n_even, DIFF_QK_DIM), 0.1),
        'diff_lk1': nrm(ks[11], (n_even, DIFF_QK_DIM), 0.1),
        'diff_lq2': nrm(ks[12], (n_even, DIFF_QK_DIM), 0.1),
        'diff_lk2': nrm(ks[13], (n_even, DIFF_QK_DIM), 0.1),
        'diff_subln_g': gain(ks[14], (n_even, DIFF_V_DIM)),
        'bconv_w': nrm(ks[15], (n_even, SHORT_CONV_K, CONV_WIDTH), SHORT_CONV_K ** -0.5),
        'ssd_w_in': nrm(ks[16], (n_odd, D_MODEL, SSD_IN_COLS), D_MODEL ** -0.5),
        'ssd_conv_w': nrm(ks[17], (n_odd, SSD_CONV_K, SSD_CONV_CH), SSD_CONV_K ** -0.5),
        'ssd_conv_b': nrm(ks[18], (n_odd, SSD_CONV_CH), 0.02),
        'ssd_A_log': jnp.log(jax.random.uniform(ks[19], (n_odd, 2, SSD_HEADS), f32, minval=1.0, maxval=16.0)),
        'ssd_dt_bias': dt0 + jnp.log(-jnp.expm1(-dt0)),
        'ssd_D': gain(ks[21], (n_odd, SSD_HEADS)),
        'ssd_norm_g': gain(ks[22], (n_odd, SSD_D_INNER)),
        'ssd_w_out': nrm(ks[23], (n_odd, SSD_D_INNER, D_MODEL), SSD_D_INNER ** -0.5),
        'moe_wg': nrm(ks[24], (DEPTH, D_MODEL, MOE_GROUPS), D_MODEL ** -0.5),
        'moe_bg': nrm(ks[25], (DEPTH, MOE_GROUPS), 0.01),
        'moe_we': nrm(ks[26], (DEPTH, D_MODEL, MOE_EXPERTS), D_MODEL ** -0.5),
        'moe_be': nrm(ks[27], (DEPTH, MOE_EXPERTS), 0.01),
        'moe_w_gate': nrm(ks[28], (DEPTH, MOE_EXPERTS, D_MODEL, MOE_HIDDEN), D_MODEL ** -0.5),
        'moe_w_up': nrm(ks[29], (DEPTH, MOE_EXPERTS, D_MODEL, MOE_HIDDEN), D_MODEL ** -0.5),
        'moe_w_down': nrm(ks[30], (DEPTH, MOE_EXPERTS, MOE_HIDDEN, D_MODEL), MOE_HIDDEN ** -0.5),
        'final_norm_g': gain(ks[31], (D_MODEL,)),
    }


def reference(x, c, ctx, c_ctx, ada_w, ada_b, norm1_g, norm2_g, ab_w_in, ab_w_out,
              diff_lq1, diff_lk1, diff_lq2, diff_lk2, diff_subln_g, bconv_w,
              ssd_w_in, ssd_conv_w, ssd_conv_b, ssd_A_log, ssd_dt_bias, ssd_D, ssd_norm_g, ssd_w_out,
              moe_wg, moe_bg, moe_we, moe_be, moe_w_gate, moe_w_up, moe_w_down, final_norm_g):
    x_lat, x_ctx = x, ctx
    b, n, d = x.shape
    silu_c = jax.nn.silu(c)
    silu_cc = jax.nn.silu(c_ctx)
    for layer in range(DEPTH):
        last = layer == DEPTH - 1
        mod_l = jnp.split((silu_c @ ada_w[layer] + ada_b[layer])[:, None, :], 6, axis=-1)
        mod_c = jnp.split(silu_cc @ ada_w[layer] + ada_b[layer], 6, axis=-1)
        h_lat = modulate(rms_norm(x_lat, norm1_g[layer]), mod_l[0], mod_l[1])
        h_ctx = modulate(rms_norm(x_ctx, norm1_g[layer]), mod_c[0], mod_c[1])
        if layer % 2 == 0:
            i = layer // 2
            m_lat, m_ctx = attn_conv_mixer(h_lat, h_ctx, ab_w_in[i], ab_w_out[i], diff_lq1[i], diff_lk1[i],
                                           diff_lq2[i], diff_lk2[i], diff_subln_g[i], bconv_w[i], layer)
        else:
            i = layer // 2
            m_lat, m_ctx = ssd_mixer(h_lat, h_ctx, ssd_w_in[i], ssd_conv_w[i], ssd_conv_b[i], ssd_A_log[i],
                                     ssd_dt_bias[i], ssd_D[i], ssd_norm_g[i], ssd_w_out[i])
        x_lat = x_lat + mod_l[2] * m_lat
        h2_lat = modulate(rms_norm(x_lat, norm2_g[layer]), mod_l[3], mod_l[4]).reshape(-1, d)
        moe_args = (moe_wg[layer], moe_bg[layer], moe_we[layer], moe_be[layer],
                    moe_w_gate[layer], moe_w_up[layer], moe_w_down[layer])
        if last:
            f_lat = hier_moe(h2_lat, *moe_args)
        else:
            x_ctx = x_ctx + mod_c[2] * m_ctx
            h2_ctx = modulate(rms_norm(x_ctx, norm2_g[layer]), mod_c[3], mod_c[4]).reshape(-1, d)
            f_all = hier_moe(jnp.concatenate([h2_lat, h2_ctx], axis=0), *moe_args)
            f_lat = f_all[:b * n]
            x_ctx = x_ctx + mod_c[5] * f_all[b * n:].reshape(x_ctx.shape)
        x_lat = x_lat + mod_l[5] * f_lat.reshape(b, n, d)
    return rms_norm(x_lat, final_norm_g)
```

```python
import functools
import math
from typing import NamedTuple

import jax
import jax.numpy as jnp
from jax import lax
from jax.experimental import pallas as pl
from jax.experimental.pallas import tpu as pltpu

F32 = jnp.float32
BF16 = jnp.bfloat16
HIGHEST = lax.Precision.HIGHEST

LANES_V7X = 128
SUBLANES_V7X = 8
VMEM_LIMIT_BYTES_V7X = 56 * 1024 * 1024

RMS_EPS = 1e-6
GRID_W = 64
N_DIFF_HEADS = 4
DIFF_QK_DIM = 64
DIFF_V_DIM = 128
DIFF_WIDTH = 512
CONV_WIDTH = 512
ROPE_BASE = 10000.0
SSD_D_INNER = 2048
SSD_HEAD_DIM = 64
SSD_HEADS = 32
SSD_GROUPS = 4
SSD_STATE = 128
SSD_CHUNK = 128
SSD_CONV_CH = SSD_D_INNER + 2 * SSD_GROUPS * SSD_STATE
MOE_GROUPS = 4
MOE_EXPERTS_PER_GROUP = 8
MOE_EXPERTS = 32
MOE_HIDDEN = 512
ROUTE_LANES = LANES_V7X


class Cfg(NamedTuple):
    batch: int
    n_lat: int
    n_ctx: int
    d_model: int
    depth: int
    tm: int
    tq: int
    moe_tile: int

    @property
    def t_lat(self):
        return self.batch * self.n_lat

    @property
    def t_ctx(self):
        return self.batch * self.n_ctx

    @property
    def t_all(self):
        return self.t_lat + self.t_ctx


def _cparams(sem):
    return pltpu.CompilerParams(dimension_semantics=sem, vmem_limit_bytes=VMEM_LIMIT_BYTES_V7X)


def _silu(v):
    return v * (1.0 / (1.0 + jnp.exp(-v)))


def _const_spec(shape):
    nd = len(shape)
    return pl.BlockSpec(shape, lambda *_: (0,) * nd)


def _mod_spec(cfg):
    return pl.BlockSpec((1, 6, cfg.d_model),
                        lambda i: (jnp.minimum((i * cfg.tm) // cfg.n_lat, cfg.batch), 0, 0))


def _seq_pos(cfg, tm):
    r0 = pl.program_id(0) * tm
    row = r0 + lax.broadcasted_iota(jnp.int32, (tm, 1), 0)
    seqlen = jnp.where(r0 >= cfg.t_lat, cfg.n_ctx, cfg.n_lat)
    return row & (seqlen - 1), seqlen


def _norm_mod(x, g, shift, scale):
    ms = jnp.mean(x * x, axis=-1, keepdims=True)
    return (x * lax.rsqrt(ms + RMS_EPS) * g) * (1.0 + scale) + shift


def _adaln_kernel(c_ref, w_ref, b_ref, o_ref):
    sc = _silu(c_ref[...])
    o_ref[0] = jnp.dot(sc, w_ref[0], precision=HIGHEST, preferred_element_type=F32) + b_ref[0]


def _adaln(cfg, c_all, ada_w, ada_b):
    d = cfg.d_model
    out = pl.pallas_call(
        _adaln_kernel,
        grid=(cfg.depth, 6),
        in_specs=[_const_spec((SUBLANES_V7X, d)),
                  pl.BlockSpec((1, d, d), lambda l, j: (l, 0, j)),
                  pl.BlockSpec((1, 1, d), lambda l, j: (l, 0, j))],
        out_specs=pl.BlockSpec((1, SUBLANES_V7X, d), lambda l, j: (l, 0, j)),
        out_shape=jax.ShapeDtypeStruct((cfg.depth, SUBLANES_V7X, 6 * d), F32),
        compiler_params=_cparams(("arbitrary", "arbitrary")),
        name="adaln",
    )(c_all, ada_w, ada_b.reshape(cfg.depth, 1, 6 * d))
    return out.reshape(cfg.depth, SUBLANES_V7X, 6, d)


def _rope_tables(cfg):
    n = cfg.n_lat
    rows = n // GRID_W
    row = jnp.broadcast_to(jnp.arange(rows, dtype=F32)[:, None], (rows, GRID_W)).reshape(n)
    col = jnp.broadcast_to(jnp.arange(GRID_W, dtype=F32)[None, :], (rows, GRID_W)).reshape(n)
    axis_dim = DIFF_QK_DIM // 2
    inv_freq = ROPE_BASE ** (-jnp.arange(0, axis_dim, 2, dtype=F32) / axis_dim)
    ang_r = row[:, None] * inv_freq
    ang_c = col[:, None] * inv_freq
    zeros = jnp.zeros_like(ang_r)
    cos64 = jnp.concatenate([jnp.cos(ang_r), jnp.cos(ang_r), jnp.cos(ang_c), jnp.cos(ang_c)], axis=1)
    sa64 = jnp.concatenate([zeros, jnp.sin(ang_r), zeros, jnp.sin(ang_c)], axis=1)
    sb64 = jnp.concatenate([-jnp.sin(ang_r), zeros, -jnp.sin(ang_c), zeros], axis=1)
    ident = cfg.tm
    cos = jnp.concatenate([jnp.tile(cos64, (1, 2)), jnp.ones((ident, LANES_V7X), F32)], axis=0)
    sa = jnp.concatenate([jnp.tile(sa64, (1, 2)), jnp.zeros((ident, LANES_V7X), F32)], axis=0)
    sb = jnp.concatenate([jnp.tile(sb64, (1, 2)), jnp.zeros((ident, LANES_V7X), F32)], axis=0)
    return cos, sa, sb


def _ab_in_kernel(x_ref, mod_ref, g_ref, w_ref, cos_ref, sa_ref, sb_ref,
                  q_ref, k_ref, v_ref, gb_ref, u_ref):
    m = mod_ref[0]
    hb = _norm_mod(x_ref[...], g_ref[...], m[0:1], m[1:2]).astype(BF16)
    cos, sa, sb = cos_ref[...], sa_ref[...], sb_ref[...]

    def proj(j):
        return jnp.dot(hb, w_ref[:, j * DIFF_WIDTH:(j + 1) * DIFF_WIDTH], preferred_element_type=F32)

    def rope(t, out_ref, scale):
        for gi in range(DIFF_WIDTH // LANES_V7X):
            xg = t[:, gi * LANES_V7X:(gi + 1) * LANES_V7X]
            r = xg * cos + pltpu.roll(xg, 16, 1) * sa + pltpu.roll(xg, LANES_V7X - 16, 1) * sb
            out_ref[:, gi * LANES_V7X:(gi + 1) * LANES_V7X] = (r * scale).astype(out_ref.dtype)

    rope(proj(0), q_ref, DIFF_QK_DIM ** -0.5)
    rope(proj(1), k_ref, 1.0)
    v_ref[...] = proj(2).astype(v_ref.dtype)
    gb_ref[...] = proj(3)
    u_ref[...] = proj(4) * proj(5)


def _ab_in(cfg, x, mod_l, g1, w_in_bf16, tables):
    t, d, tm = cfg.t_all, cfg.d_model, cfg.tm
    n_lat_tiles, per_seq = cfg.t_lat // tm, cfg.n_lat // tm
    tab_spec = pl.BlockSpec((tm, LANES_V7X), lambda i: (jnp.where(i < n_lat_tiles, i % per_seq, per_seq), 0))
    row = lambda w: pl.BlockSpec((tm, w), lambda i: (i, 0))
    return pl.pallas_call(
        _ab_in_kernel,
        grid=(t // tm,),
        in_specs=[row(d), _mod_spec(cfg), _const_spec((1, d)), _const_spec(w_in_bf16.shape),
                  tab_spec, tab_spec, tab_spec],
        out_specs=[row(DIFF_WIDTH)] * 5,
        out_shape=[jax.ShapeDtypeStruct((t, DIFF_WIDTH), BF16)] * 3
        + [jax.ShapeDtypeStruct((t, CONV_WIDTH), F32)] * 2,
        compiler_params=_cparams(("arbitrary",)),
        name="ab_in",
    )(x, mod_l, g1, w_in_bf16, *tables)


def _attn_kernel(lvec_ref, q_ref, kc_ref, kl_ref, vc_ref, vl_ref, g_ref, o_ref, *, tq, n_lat_blocks, lam_init):
    lv = lvec_ref[...]
    lam = (jnp.exp(jnp.sum(lv[0:1] * lv[1:2], axis=-1, keepdims=True))
           - jnp.exp(jnp.sum(lv[2:3] * lv[3:4], axis=-1, keepdims=True)) + lam_init)

    q = q_ref[...].astype(F32)
    lane = lax.broadcasted_iota(jnp.int32, q.shape, 1)
    half = DIFF_QK_DIM
    q2 = jnp.concatenate([jnp.where(lane < half, q, 0.0), jnp.where(lane >= half, q, 0.0)], axis=0).astype(BF16)
    nt = (((1,), (1,)), ((), ()))

    def finish(pv):
        o = pv * lax.rsqrt(jnp.mean(pv * pv, axis=-1, keepdims=True) + RMS_EPS)
        o_ref[...] = ((o * g_ref[...]) * (1.0 - lam_init)).astype(o_ref.dtype)

    def coef(l):
        r = 1.0 / l
        return r[:tq], lam * r[tq:]

    @pl.when(pl.program_id(2) < n_lat_blocks)
    def _():
        s_c = lax.dot_general(q2, kc_ref[...], nt, preferred_element_type=F32)
        s_l = lax.dot_general(q2, kl_ref[...], nt, preferred_element_type=F32)
        mx = jnp.maximum(jnp.max(s_c, axis=-1, keepdims=True), jnp.max(s_l, axis=-1, keepdims=True))
        p_c = jnp.exp(s_c - mx)
        p_l = jnp.exp(s_l - mx)
        c0, c1 = coef(jnp.sum(p_c, axis=-1, keepdims=True) + jnp.sum(p_l, axis=-1, keepdims=True))
        pd_c = (p_c[:tq] * c0 - p_c[tq:] * c1).astype(BF16)
        pd_l = (p_l[:tq] * c0 - p_l[tq:] * c1).astype(BF16)
        finish(jnp.dot(pd_c, vc_ref[...], preferred_element_type=F32)
               + jnp.dot(pd_l, vl_ref[...], preferred_element_type=F32))

    @pl.when(pl.program_id(2) >= n_lat_blocks)
    def _():
        s_c = lax.dot_general(q2, kc_ref[...], nt, preferred_element_type=F32)
        p_c = jnp.exp(s_c - jnp.max(s_c, axis=-1, keepdims=True))
        c0, c1 = coef(jnp.sum(p_c, axis=-1, keepdims=True))
        pd_c = (p_c[:tq] * c0 - p_c[tq:] * c1).astype(BF16)
        finish(jnp.dot(pd_c, vc_ref[...], preferred_element_type=F32))


def _attention(cfg, q, k, v, lvec, subln_g, lam_init):
    tq = cfg.tq
    assert cfg.n_ctx == tq, "context queries are processed as one query tile per batch"
    nqb = cfg.n_lat // tq
    ctx_blk0 = cfg.t_lat // cfg.n_ctx
    hw = DIFF_V_DIM
    q_spec = pl.BlockSpec((tq, hw), lambda b, h, i: (jnp.where(i < nqb, b * nqb + i, ctx_blk0 + b), h))
    ctx_spec = pl.BlockSpec((cfg.n_ctx, hw), lambda b, h, i: (ctx_blk0 + b, h))
    lat_spec = pl.BlockSpec((cfg.n_lat, hw), lambda b, h, i: (b, h))
    return pl.pallas_call(
        functools.partial(_attn_kernel, tq=tq, n_lat_blocks=nqb, lam_init=lam_init),
        grid=(cfg.batch, N_DIFF_HEADS, nqb + 1),
        in_specs=[_const_spec(lvec.shape), q_spec, ctx_spec, lat_spec, ctx_spec, lat_spec,
                  _const_spec((1, hw))],
        out_specs=q_spec,
        out_shape=jax.ShapeDtypeStruct((cfg.t_all, DIFF_WIDTH), BF16),
        compiler_params=_cparams(("arbitrary", "arbitrary", "arbitrary")),
        name="diff_attn",
    )(lvec, q, k, k, v, v, subln_g)


def _route_from_logits(lg):
    lane = lax.broadcasted_iota(jnp.int32, lg.shape, 1)
    neg = -jnp.inf
    big = jnp.int32(ROUTE_LANES)
    gl = jnp.where(lane < MOE_GROUPS, lg, neg)
    gmax = jnp.max(gl, axis=-1, keepdims=True)
    gidx = jnp.min(jnp.where(gl == gmax, lane, big), axis=-1, keepdims=True)
    gw = 1.0 / jnp.sum(jnp.exp(gl - gmax), axis=-1, keepdims=True)
    lo = MOE_GROUPS + gidx * MOE_EXPERTS_PER_GROUP
    el = jnp.where((lane >= lo) & (lane < lo + MOE_EXPERTS_PER_GROUP), lg, neg)
    m1 = jnp.max(el, axis=-1, keepdims=True)
    i1 = jnp.min(jnp.where(el == m1, lane, big), axis=-1, keepdims=True)
    el2 = jnp.where(lane == i1, neg, el)
    m2 = jnp.max(el2, axis=-1, keepdims=True)
    i2 = jnp.min(jnp.where(el2 == m2, lane, big), axis=-1, keepdims=True)
    w1 = gw / (1.0 + jnp.exp(m2 - m1))
    w2 = gw - w1
    e1 = (i1 - MOE_GROUPS).astype(F32)
    e2 = (i2 - MOE_GROUPS).astype(F32)
    return jnp.where(lane == 0, e1, jnp.where(lane == 1, e2, jnp.where(lane == 2, w1, jnp.where(lane == 3, w2, 0.0))))


def _residual_norm_route(x, mix, m, g2, wr_ref, br_ref, xo_ref, h2_ref, rt_ref):
    xn = x + m[2:3] * mix
    xo_ref[...] = xn
    h2 = _norm_mod(xn, g2, m[3:4], m[4:5])
    h2_ref[...] = h2.astype(h2_ref.dtype)
    lg = jnp.dot(h2, wr_ref[...], precision=HIGHEST, preferred_element_type=F32) + br_ref[...]
    rt_ref[...] = _route_from_logits(lg)


def _epilogue_specs(cfg):
    d, tm = cfg.d_model, cfg.tm
    row = lambda w: pl.BlockSpec((tm, w), lambda i: (i, 0))
    in_specs = [row(d), _mod_spec(cfg), _const_spec((1, d)), _const_spec((d, ROUTE_LANES)),
                _const_spec((1, ROUTE_LANES))]
    out_specs = [row(d), row(d), row(ROUTE_LANES)]
    out_shape = [jax.ShapeDtypeStruct((cfg.t_all, d), F32), jax.ShapeDtypeStruct((cfg.t_all, d), BF16),
                 jax.ShapeDtypeStruct((cfg.t_all, ROUTE_LANES), F32)]
    return in_specs, out_specs, out_shape


def _halo_specs(cfg, width, col_block=0):
    per = cfg.tm // SUBLANES_V7X
    last = cfg.t_all // SUBLANES_V7X - 1
    prev = pl.BlockSpec((SUBLANES_V7X, width), lambda i, *_: (jnp.maximum(i * per - 1, 0), col_block))
    nxt = pl.BlockSpec((SUBLANES_V7X, width), lambda i, *_: (jnp.minimum((i + 1) * per, last), col_block))
    return prev, nxt


def _shifted_rows(pad_ref, u, prev_blk, next_blk, pos, seqlen):
    tm = u.shape[0]
    s = SUBLANES_V7X
    pad_ref[s:s + tm, :] = u
    pad_ref[0:s, :] = prev_blk
    pad_ref[s + tm:2 * s + tm, :] = next_blk
    um1 = jnp.where(pos == 0, 0.0, pad_ref[s - 1:s - 1 + tm, :])
    up1 = jnp.where(pos == seqlen - 1, 0.0, pad_ref[s + 1:s + 1 + tm, :])
    return um1, up1


def _ab_out_kernel(o_ref, gb_ref, u_ref, up_ref, un_ref, cw_ref, wo_ref, wc_ref,
                   x_ref, mod_ref, g2_ref, wr_ref, br_ref,
                   xo_ref, h2_ref, rt_ref, pad_ref, *, cfg):
    pos, seqlen = _seq_pos(cfg, cfg.tm)
    u = u_ref[...]
    um1, up1 = _shifted_rows(pad_ref, u, up_ref[...], un_ref[...], pos, seqlen)
    cw = cw_ref[...]
    conv = gb_ref[...] * (um1 * cw[0:1] + u * cw[1:2] + up1 * cw[2:3])
    mix = (jnp.dot(o_ref[...], wo_ref[...], preferred_element_type=F32)
           + jnp.dot(conv.astype(BF16), wc_ref[...], preferred_element_type=F32))
    _residual_norm_route(x_ref[...], mix, mod_ref[0], g2_ref[...], wr_ref, br_ref, xo_ref, h2_ref, rt_ref)


def _ab_out(cfg, o, gb, u, conv_w, wo, wc, x, mod_l, g2, wr, br):
    tm = cfg.tm
    row = lambda w: pl.BlockSpec((tm, w), lambda i: (i, 0))
    prev, nxt = _halo_specs(cfg, CONV_WIDTH)
    ep_in, ep_out, ep_shape = _epilogue_specs(cfg)
    return pl.pallas_call(
        functools.partial(_ab_out_kernel, cfg=cfg),
        grid=(cfg.t_all // tm,),
        in_specs=[row(DIFF_WIDTH), row(CONV_WIDTH), row(CONV_WIDTH), prev, nxt, _const_spec(conv_w.shape),
                  _const_spec(wo.shape), _const_spec(wc.shape)] + ep_in,
        out_specs=ep_out,
        out_shape=ep_shape,
        scratch_shapes=[pltpu.VMEM((tm + 2 * SUBLANES_V7X, CONV_WIDTH), F32)],
        compiler_params=_cparams(("arbitrary",)),
        name="ab_out",
    )(o, gb, u, u, u, conv_w, wo, wc, x, mod_l, g2, wr, br)


def _ssd_in_kernel(x_ref, mod_ref, g_ref, w_ref, wdt_ref, z_ref, xbc_ref, dt_ref):
    m = mod_ref[0]
    hb = _norm_mod(x_ref[...], g_ref[...], m[0:1], m[1:2]).astype(BF16)
    cw = 512
    for j in range(SSD_D_INNER // cw):
        z_ref[:, j * cw:(j + 1) * cw] = jnp.dot(hb, w_ref[:, j * cw:(j + 1) * cw],
                                                preferred_element_type=F32).astype(z_ref.dtype)
    for j in range(SSD_CONV_CH // cw):
        c0 = SSD_D_INNER + j * cw
        xbc_ref[:, j * cw:(j + 1) * cw] = jnp.dot(hb, w_ref[:, c0:c0 + cw], preferred_element_type=F32)
    dt_ref[...] = jnp.dot(hb, wdt_ref[...], preferred_element_type=F32)


def _ssd_in(cfg, x, mod_l, g1, w_main, w_dt):
    t, d, tm = cfg.t_all, cfg.d_model, cfg.tm
    row = lambda w: pl.BlockSpec((tm, w), lambda i: (i, 0))
    return pl.pallas_call(
        _ssd_in_kernel,
        grid=(t // tm,),
        in_specs=[row(d), _mod_spec(cfg), _const_spec((1, d)), _const_spec(w_main.shape), _const_spec(w_dt.shape)],
        out_specs=[row(SSD_D_INNER), row(SSD_CONV_CH), row(LANES_V7X)],
        out_shape=[jax.ShapeDtypeStruct((t, SSD_D_INNER), F32), jax.ShapeDtypeStruct((t, SSD_CONV_CH), F32),
                   jax.ShapeDtypeStruct((t, LANES_V7X), F32)],
        compiler_params=_cparams(("arbitrary",)),
        name="ssd_in",
    )(x, mod_l, g1, w_main, w_dt)


def _ssd_conv_kernel(u_ref, up_ref, un_ref, cw_ref, cb_ref, o_ref, pad_ref, *, cfg):
    pos, seqlen = _seq_pos(cfg, cfg.tm)
    u = u_ref[...]
    um1, up1 = _shifted_rows(pad_ref, u, up_ref[...], un_ref[...], pos, seqlen)
    cw = cw_ref[...]
    o_ref[...] = _silu(um1 * cw[0:1] + u * cw[1:2] + up1 * cw[2:3] + cb_ref[...]).astype(o_ref.dtype)


def _ssd_conv(cfg, xbc, conv_w, conv_b):
    tm, cw = cfg.tm, 512
    per = tm // SUBLANES_V7X
    last = cfg.t_all // SUBLANES_V7X - 1
    return pl.pallas_call(
        functools.partial(_ssd_conv_kernel, cfg=cfg),
        grid=(cfg.t_all // tm, SSD_CONV_CH // cw),
        in_specs=[pl.BlockSpec((tm, cw), lambda i, j: (i, j)),
                  pl.BlockSpec((SUBLANES_V7X, cw), lambda i, j: (jnp.maximum(i * per - 1, 0), j)),
                  pl.BlockSpec((SUBLANES_V7X, cw), lambda i, j: (jnp.minimum((i + 1) * per, last), j)),
                  pl.BlockSpec((3, cw), lambda i, j: (0, j)),
                  pl.BlockSpec((1, cw), lambda i, j: (0, j))],
        out_specs=pl.BlockSpec((tm, cw), lambda i, j: (i, j)),
        out_shape=jax.ShapeDtypeStruct((cfg.t_all, SSD_CONV_CH), BF16),
        scratch_shapes=[pltpu.VMEM((tm + 2 * SUBLANES_V7X, cw), F32)],
        compiler_params=_cparams(("arbitrary", "arbitrary")),
        name="ssd_conv",
    )(xbc, xbc, xbc, conv_w, conv_b)


def _split3(v):
    b1 = v.astype(BF16)
    r1 = v - b1.astype(F32)
    b2 = r1.astype(BF16)
    b3 = (r1 - b2.astype(F32)).astype(BF16)
    return jnp.concatenate([b1, b2, b3], axis=1)


def _softplus(v):
    return jnp.maximum(v, 0.0) + jnp.log(1.0 + jnp.exp(-jnp.abs(v)))


def _ssd_scan_kernel(xs_ref, b_ref, c_ref, dt_ref, dtt_ref, bias_ref, biast_ref, alog_ref, alogt_ref,
                     ebig_ref, ewide_ref, y_ref, state_ref):
    q = SSD_CHUNK
    d = pl.program_id(1)
    fwd = d == 0

    @pl.when(pl.program_id(2) == 0)
    def _():
        state_ref[...] = jnp.zeros_like(state_ref)

    dt = _softplus(dt_ref[0] + bias_ref[0])
    dtt = _softplus(dtt_ref[0] + biast_ref[0])
    a = dt * (-jnp.exp(alog_ref[0]))
    at = dtt * (-jnp.exp(alogt_ref[0]))

    ri = lax.broadcasted_iota(jnp.int32, (q, q), 0)
    ci = lax.broadcasted_iota(jnp.int32, (q, q), 1)
    keep = (ri - ci) * jnp.where(fwd, 1, -1) >= 0
    tri = jnp.where(keep, 1.0, 0.0)
    cs = jnp.dot(tri, a, precision=HIGHEST, preferred_element_type=F32)
    cst = lax.dot_general(at, tri, (((1,), (1,)), ((), ())), precision=HIGHEST,
                          preferred_element_type=F32)
    tot = jnp.sum(a, axis=0, keepdims=True)

    ebig = ebig_ref[...]
    ewide = ewide_ref[...]
    cs_big = jnp.dot(_split3(cs), ebig, preferred_element_type=F32)
    w_wide = jnp.dot(_split3(dt * jnp.exp(tot - cs)), ewide, preferred_element_type=F32)
    dec_wide = jnp.dot(_split3(jnp.broadcast_to(jnp.exp(tot), (SUBLANES_V7X, SSD_HEADS))), ewide,
                       preferred_element_type=F32)[0:1]

    lane = lax.broadcasted_iota(jnp.int32, (q, LANES_V7X), 1)
    lo = lane < SSD_HEAD_DIM
    hpg = SSD_HEADS // SSD_GROUPS
    gw = hpg * SSD_HEAD_DIM
    nt = (((1,), (1,)), ((), ()))
    tn = (((0,), (0,)), ((), ()))
    for g in range(SSD_GROUPS):
        bm = b_ref[:, g * SSD_STATE:(g + 1) * SSD_STATE]
        cm = c_ref[:, g * SSD_STATE:(g + 1) * SSD_STATE]
        cb = lax.dot_general(cm, bm, nt, preferred_element_type=F32)
        cmf = cm.astype(F32)
        for pr in range(hpg // 2):
            lhs, rhs = [], []
            c0 = g * gw + pr * LANES_V7X
            xs_pair = xs_ref[:, c0:c0 + LANES_V7X]
            st_pair = state_ref[:, c0:c0 + LANES_V7X].astype(BF16)
            zero = jnp.zeros_like(xs_pair)
            for k in range(2):
                h = g * hpg + pr * 2 + k
                csb = cs_big[:, h * q:(h + 1) * q]
                seg = csb - cst[h:h + 1, :]
                lmat = jnp.exp(jnp.where(keep, seg, -jnp.inf))
                lhs.append((cb * lmat * dtt[h:h + 1, :]).astype(BF16))
                lhs.append((cmf * jnp.exp(csb)).astype(BF16))
                sel = lo if k == 0 else jnp.logical_not(lo)
                rhs.append(jnp.where(sel, xs_pair, zero))
                rhs.append(jnp.where(sel, st_pair, zero))
            y = jnp.dot(jnp.concatenate(lhs, axis=1), jnp.concatenate(rhs, axis=0), preferred_element_type=F32)
            y_ref[0, :, c0:c0 + LANES_V7X] = y.astype(y_ref.dtype)
        sl = slice(g * gw, (g + 1) * gw)
        x2 = (xs_ref[:, sl].astype(F32) * w_wide[:, sl]).astype(BF16)
        upd = lax.dot_general(bm, x2, tn, preferred_element_type=F32)
        state_ref[:, sl] = state_ref[:, sl] * dec_wide[:, sl] + upd


def _ssd_scan(cfg, xbc_act, dt_raw, dt_bias, a_log):
    q = SSD_CHUNK
    n_cc, n_lc = cfg.n_ctx // q, cfg.n_lat // q
    ctx0 = cfg.t_lat // q
    h = SSD_HEADS
    dt2 = dt_raw[:, :2 * h].reshape(cfg.t_all, 2, h).transpose(1, 0, 2)
    dt2t = dt2.transpose(0, 2, 1)
    bias = dt_bias.reshape(2, 1, h)
    biast = dt_bias.reshape(2, h, 1)
    alog = a_log.reshape(2, 1, h)
    alogt = a_log.reshape(2, h, 1)
    head_of_row = jnp.tile(jnp.arange(h), 3)[:, None]
    ebig = (head_of_row == (jnp.arange(h * q) // q)[None, :]).astype(BF16)
    ewide = (head_of_row == (jnp.arange(SSD_D_INNER) // SSD_HEAD_DIM)[None, :]).astype(BF16)

    def blk(b, d, s):
        cs = jnp.where(d == 0, s, n_cc - 1 - s)
        ls = jnp.where(d == 0, s - n_cc, n_lc - 1 - (s - n_cc))
        return jnp.where(s < n_cc, ctx0 + b * n_cc + cs, b * n_lc + ls)

    gs = SSD_GROUPS * SSD_STATE
    col = lambda w, cblk: pl.BlockSpec((q, w), lambda b, d, s: (blk(b, d, s), cblk))
    per_dir = lambda shape: pl.BlockSpec((1,) + shape, lambda b, d, s: (d, 0, 0))
    return pl.pallas_call(
        _ssd_scan_kernel,
        grid=(cfg.batch, 2, n_cc + n_lc),
        in_specs=[col(SSD_D_INNER, 0), col(gs, SSD_D_INNER // gs), col(gs, SSD_D_INNER // gs + 1),
                  pl.BlockSpec((1, q, h), lambda b, d, s: (d, blk(b, d, s), 0)),
                  pl.BlockSpec((1, h, q), lambda b, d, s: (d, 0, blk(b, d, s))),
                  per_dir((1, h)), per_dir((h, 1)), per_dir((1, h)), per_dir((h, 1)),
                  _const_spec(ebig.shape), _const_spec(ewide.shape)],
        out_specs=pl.BlockSpec((1, q, SSD_D_INNER), lambda b, d, s: (d, blk(b, d, s), 0)),
        out_shape=jax.ShapeDtypeStruct((2, cfg.t_all, SSD_D_INNER), BF16),
        scratch_shapes=[pltpu.VMEM((SSD_STATE, SSD_D_INNER), F32)],
        compiler_params=_cparams(("arbitrary", "arbitrary", "arbitrary")),
        name="ssd_scan",
    )(xbc_act, xbc_act, xbc_act, dt2, dt2t, bias, biast, alog, alogt, ebig, ewide)


def _ssd_out_kernel(yf_ref, yb_ref, xs_ref, z_ref, dw_ref, ng_ref, wo_ref,
                    x_ref, mod_ref, g2_ref, wr_ref, br_ref, xo_ref, h2_ref, rt_ref):
    gw = SSD_D_INNER // SSD_GROUPS
    mix = None
    for g in range(SSD_GROUPS):
        sl = slice(g * gw, (g + 1) * gw)
        y = yf_ref[0, :, sl].astype(F32) + yb_ref[0, :, sl].astype(F32) + xs_ref[:, sl].astype(F32) * dw_ref[:, sl]
        y = y * _silu(z_ref[:, sl])
        y = (y * lax.rsqrt(jnp.mean(y * y, axis=-1, keepdims=True) + RMS_EPS)) * ng_ref[:, sl]
        part = jnp.dot(y.astype(BF16), wo_ref[sl, :], preferred_element_type=F32)
        mix = part if mix is None else mix + part
    _residual_norm_route(x_ref[...], mix, mod_ref[0], g2_ref[...], wr_ref, br_ref, xo_ref, h2_ref, rt_ref)


def _ssd_out(cfg, y2, xbc_act, z, d_wide, norm_g, wo, x, mod_l, g2, wr, br):
    tm = cfg.tm
    row = lambda w: pl.BlockSpec((tm, w), lambda i: (i, 0))
    ydir = lambda d: pl.BlockSpec((1, tm, SSD_D_INNER), lambda i: (d, i, 0))
    ep_in, ep_out, ep_shape = _epilogue_specs(cfg)
    return pl.pallas_call(
        _ssd_out_kernel,
        grid=(cfg.t_all // tm,),
        in_specs=[ydir(0), ydir(1), row(SSD_D_INNER), row(SSD_D_INNER), _const_spec((1, SSD_D_INNER)),
                  _const_spec((1, SSD_D_INNER)), _const_spec(wo.shape)] + ep_in,
        out_specs=ep_out,
        out_shape=ep_shape,
        compiler_params=_cparams(("arbitrary",)),
        name="ssd_out",
    )(y2, y2, xbc_act, z, d_wide, norm_g, wo, x, mod_l, g2, wr, br)


def _expert_kernel(eid_ref, nused_ref, xb_ref, wg_ref, wu_ref, wd_ref, y_ref, wg_s, wu_s, wd_s):
    i = pl.program_id(0)
    changed = jnp.logical_or(i == 0, eid_ref[i] != eid_ref[jnp.maximum(i - 1, 0)])

    @pl.when(jnp.logical_and(changed, i < nused_ref[0]))
    def _():
        wg_s[...] = wg_ref[0].astype(BF16)
        wu_s[...] = wu_ref[0].astype(BF16)
        wd_s[...] = wd_ref[0].astype(BF16)

    @pl.when(i < nused_ref[0])
    def _():
        xb = xb_ref[...]
        hg = jnp.dot(xb, wg_s[...], preferred_element_type=F32)
        hu = jnp.dot(xb, wu_s[...], preferred_element_type=F32)
        y_ref[...] = jnp.dot((_silu(hg) * hu).astype(BF16), wd_s[...], preferred_element_type=F32)

    @pl.when(i >= nused_ref[0])
    def _():
        y_ref[...] = jnp.zeros_like(y_ref)


def _experts(cfg, block_eid, n_used, xb, w_gate, w_up, w_down):
    d, tile = cfg.d_model, cfg.moe_tile
    n_blocks = xb.shape[0] // tile
    grid_spec = pltpu.PrefetchScalarGridSpec(
        num_scalar_prefetch=2,
        grid=(n_blocks,),
        in_specs=[pl.BlockSpec((tile, d), lambda i, e, n: (i, 0)),
                  pl.BlockSpec((1, d, MOE_HIDDEN), lambda i, e, n: (e[i], 0, 0)),
                  pl.BlockSpec((1, d, MOE_HIDDEN), lambda i, e, n: (e[i], 0, 0)),
                  pl.BlockSpec((1, MOE_HIDDEN, d), lambda i, e, n: (e[i], 0, 0))],
        out_specs=pl.BlockSpec((tile, d), lambda i, e, n: (i, 0)),
        scratch_shapes=[pltpu.VMEM((d, MOE_HIDDEN), BF16), pltpu.VMEM((d, MOE_HIDDEN), BF16),
                        pltpu.VMEM((MOE_HIDDEN, d), BF16)],
    )
    return pl.pallas_call(
        _expert_kernel,
        grid_spec=grid_spec,
        out_shape=jax.ShapeDtypeStruct((n_blocks * tile, d), F32),
        compiler_params=_cparams(("arbitrary",)),
        name="moe_experts",
    )(block_eid, n_used, xb, w_gate, w_up, w_down)


def _dispatch(cfg, route):
    tile = cfg.moe_tile
    t = route.shape[0]
    eid = route[:, 0:2].astype(jnp.int32).reshape(-1)
    a_total = eid.shape[0]
    onehot = (eid[:, None] == jnp.arange(MOE_EXPERTS, dtype=jnp.int32)[None, :]).astype(jnp.int32)
    csum = jnp.cumsum(onehot, axis=0)
    counts = csum[-1]
    rank = jnp.take_along_axis(csum, eid[:, None], axis=1)[:, 0] - 1
    padded = (counts + tile - 1) // tile * tile
    pend = jnp.cumsum(padded)
    dest = (pend - padded)[eid] + rank
    n_blocks = (a_total + MOE_EXPERTS * (tile - 1)) // tile
    slot_tok = jnp.zeros((n_blocks * tile,), jnp.int32).at[dest].set(jnp.arange(a_total, dtype=jnp.int32) // 2)
    block_eid = jnp.minimum(jnp.searchsorted(pend, jnp.arange(n_blocks, dtype=jnp.int32) * tile, side='right'),
                            MOE_EXPERTS - 1).astype(jnp.int32)
    n_used = (pend[-1] // tile).astype(jnp.int32).reshape(1)
    return slot_tok, block_eid, n_used, dest.reshape(t, 2)


def _combine_kernel(x_ref, y1_ref, y2_ref, rt_ref, mod_ref, g_ref, o_ref, *, final):
    rt = rt_ref[...]
    f = rt[:, 2:3] * y1_ref[...] + rt[:, 3:4] * y2_ref[...]
    xn = x_ref[...] + mod_ref[0][5:6] * f
    if final:
        xn = (xn * lax.rsqrt(jnp.mean(xn * xn, axis=-1, keepdims=True) + RMS_EPS)) * g_ref[...]
    o_ref[...] = xn


def _combine(cfg, x, y1, y2, route, mod_l, g, final):
    d, tm = cfg.d_model, cfg.tm
    row = lambda w: pl.BlockSpec((tm, w), lambda i: (i, 0))
    return pl.pallas_call(
        functools.partial(_combine_kernel, final=final),
        grid=(cfg.t_all // tm,),
        in_specs=[row(d), row(d), row(d), row(ROUTE_LANES), _mod_spec(cfg), _const_spec((1, d))],
        out_specs=row(d),
        out_shape=jax.ShapeDtypeStruct((cfg.t_all, d), F32),
        compiler_params=_cparams(("arbitrary",)),
        name="moe_combine",
    )(x, y1, y2, route, mod_l, g)


def _forward(cfg, x, c, ctx, c_ctx, ada_w, ada_b, norm1_g, norm2_g, ab_w_in, ab_w_out,
             diff_lq1, diff_lk1, diff_lq2, diff_lk2, diff_subln_g, bconv_w,
             ssd_w_in, ssd_conv_w, ssd_conv_b, ssd_A_log, ssd_dt_bias, ssd_D, ssd_norm_g, ssd_w_out,
             moe_wg, moe_bg, moe_we, moe_be, moe_w_gate, moe_w_up, moe_w_down, final_norm_g):
    d = cfg.d_model
    b = cfg.batch
    xa = jnp.concatenate([x.reshape(cfg.t_lat, d), ctx.reshape(cfg.t_ctx, d)], axis=0)
    c_all = jnp.zeros((SUBLANES_V7X, d), F32).at[:b].set(c).at[b].set(c_ctx)
    mod = _adaln(cfg, c_all, ada_w, ada_b)
    tables = _rope_tables(cfg)

    for layer in range(cfg.depth):
        i = layer // 2
        mod_l = mod[layer]
        g1 = norm1_g[layer].reshape(1, d)
        g2 = norm2_g[layer].reshape(1, d)
        pad = ROUTE_LANES - MOE_GROUPS - MOE_EXPERTS
        wr = jnp.concatenate([moe_wg[layer], moe_we[layer], jnp.zeros((d, pad), F32)], axis=1)
        br = jnp.concatenate([moe_bg[layer], moe_be[layer], jnp.zeros((pad,), F32)]).reshape(1, ROUTE_LANES)
        if layer % 2 == 0:
            lam_init = 0.8 - 0.6 * math.exp(-0.3 * layer)
            q, k, v, gb, u = _ab_in(cfg, xa, mod_l, g1, ab_w_in[i].astype(BF16), tables)
            lvec = jnp.stack([diff_lq1[i], diff_lk1[i], diff_lq2[i], diff_lk2[i]])
            o = _attention(cfg, q, k, v, lvec, diff_subln_g[i].reshape(1, DIFF_V_DIM), lam_init)
            w_out = ab_w_out[i].astype(BF16)
            xa, h2, route = _ab_out(cfg, o, gb, u, bconv_w[i], w_out[:DIFF_WIDTH], w_out[DIFF_WIDTH:],
                                    xa, mod_l, g2, wr, br)
        else:
            w_in = ssd_w_in[i].astype(BF16)
            n_main = SSD_D_INNER + SSD_CONV_CH
            w_dt = jnp.pad(w_in[:, n_main:], ((0, 0), (0, LANES_V7X - 2 * SSD_HEADS)))
            z, xbc, dt_raw = _ssd_in(cfg, xa, mod_l, g1, w_in[:, :n_main], w_dt)
            xbc_act = _ssd_conv(cfg, xbc, ssd_conv_w[i], ssd_conv_b[i].reshape(1, SSD_CONV_CH))
            y2 = _ssd_scan(cfg, xbc_act, dt_raw, ssd_dt_bias[i], ssd_A_log[i])
            d_wide = jnp.repeat(ssd_D[i], SSD_HEAD_DIM).reshape(1, SSD_D_INNER)
            xa, h2, route = _ssd_out(cfg, y2, xbc_act, z, d_wide, ssd_norm_g[i].reshape(1, SSD_D_INNER),
                                     ssd_w_out[i].astype(BF16), xa, mod_l, g2, wr, br)
        slot_tok, block_eid, n_used, pos = _dispatch(cfg, route)
        yb = _experts(cfg, block_eid, n_used, h2[slot_tok], moe_w_gate[layer], moe_w_up[layer], moe_w_down[layer])
        last = layer == cfg.depth - 1
        xa = _combine(cfg, xa, yb[pos[:, 0]], yb[pos[:, 1]], route, mod_l, final_norm_g.reshape(1, d), last)
    return xa[:cfg.t_lat].reshape(b, cfg.n_lat, d)


def kernel(x, c, ctx, c_ctx, ada_w, ada_b, norm1_g, norm2_g, ab_w_in, ab_w_out, diff_lq1, diff_lk1, diff_lq2, diff_lk2, diff_subln_g, bconv_w, ssd_w_in, ssd_conv_w, ssd_conv_b, ssd_A_log, ssd_dt_bias, ssd_D, ssd_norm_g, ssd_w_out, moe_wg, moe_bg, moe_we, moe_be, moe_w_gate, moe_w_up, moe_w_down, final_norm_g):
    cfg = Cfg(batch=x.shape[0], n_lat=x.shape[1], n_ctx=ctx.shape[1], d_model=x.shape[2], depth=ada_w.shape[0],
              tm=256, tq=256, moe_tile=256)
    return _forward(cfg, x, c, ctx, c_ctx, ada_w, ada_b, norm1_g, norm2_g, ab_w_in, ab_w_out,
                    diff_lq1, diff_lk1, diff_lq2, diff_lk2, diff_subln_g, bconv_w,
                    ssd_w_in, ssd_conv_w, ssd_conv_b, ssd_A_log, ssd_dt_bias, ssd_D, ssd_norm_g, ssd_w_out,
                    moe_wg, moe_bg, moe_we, moe_be, moe_w_gate, moe_w_up, moe_w_down, final_norm_g)
```

```python
import functools
import math
from typing import NamedTuple

import jax
import jax.numpy as jnp
from jax import lax
from jax.experimental import pallas as pl
from jax.experimental.pallas import tpu as pltpu

F32 = jnp.float32
BF16 = jnp.bfloat16
HIGHEST = lax.Precision.HIGHEST

LANES_V7X = 128
SUBLANES_V7X = 8
VMEM_LIMIT_BYTES_V7X = 56 * 1024 * 1024

RMS_EPS = 1e-6
GRID_W = 64
N_DIFF_HEADS = 4
DIFF_QK_DIM = 64
DIFF_V_DIM = 128
DIFF_WIDTH = 512
CONV_WIDTH = 512
ROPE_BASE = 10000.0
SSD_D_INNER = 2048
SSD_HEAD_DIM = 64
SSD_HEADS = 32
SSD_GROUPS = 4
SSD_STATE = 128
SSD_CHUNK = 128
SSD_CONV_CH = SSD_D_INNER + 2 * SSD_GROUPS * SSD_STATE
MOE_GROUPS = 4
MOE_EXPERTS_PER_GROUP = 8
MOE_EXPERTS = 32
MOE_HIDDEN = 512
ROUTE_LANES = LANES_V7X


class Cfg(NamedTuple):
    batch: int
    n_lat: int
    n_ctx: int
    d_model: int
    depth: int
    tm: int
    tq: int
    moe_tile: int

    @property
    def t_lat(self):
        return self.batch * self.n_lat

    @property
    def t_ctx(self):
        return self.batch * self.n_ctx

    @property
    def t_all(self):
        return self.t_lat + self.t_ctx


def _cparams(sem):
    return pltpu.CompilerParams(dimension_semantics=sem, vmem_limit_bytes=VMEM_LIMIT_BYTES_V7X)


def _silu(v):
    return v * (1.0 / (1.0 + jnp.exp(-v)))


def _const_spec(shape):
    nd = len(shape)
    return pl.BlockSpec(shape, lambda *_: (0,) * nd)


def _mod_spec(cfg):
    return pl.BlockSpec((1, 6, cfg.d_model),
                        lambda i: (jnp.minimum((i * cfg.tm) // cfg.n_lat, cfg.batch), 0, 0))


def _seq_pos(cfg, tm):
    r0 = pl.program_id(0) * tm
    row = r0 + lax.broadcasted_iota(jnp.int32, (tm, 1), 0)
    seqlen = jnp.where(r0 >= cfg.t_lat, cfg.n_ctx, cfg.n_lat)
    return row & (seqlen - 1), seqlen


def _norm_mod(x, g, shift, scale):
    ms = jnp.mean(x * x, axis=-1, keepdims=True)
    return (x * lax.rsqrt(ms + RMS_EPS) * g) * (1.0 + scale) + shift


def _adaln_kernel(c_ref, w_ref, b_ref, o_ref):
    sc = _silu(c_ref[...])
    o_ref[0] = jnp.dot(sc, w_ref[0], precision=HIGHEST, preferred_element_type=F32) + b_ref[0]


def _adaln(cfg, c_all, ada_w, ada_b):
    d = cfg.d_model
    out = pl.pallas_call(
        _adaln_kernel,
        grid=(cfg.depth, 6),
        in_specs=[_const_spec((SUBLANES_V7X, d)),
                  pl.BlockSpec((1, d, d), lambda l, j: (l, 0, j)),
                  pl.BlockSpec((1, 1, d), lambda l, j: (l, 0, j))],
        out_specs=pl.BlockSpec((1, SUBLANES_V7X, d), lambda l, j: (l, 0, j)),
        out_shape=jax.ShapeDtypeStruct((cfg.depth, SUBLANES_V7X, 6 * d), F32),
        compiler_params=_cparams(("arbitrary", "arbitrary")),
        name="adaln",
    )(c_all, ada_w, ada_b.reshape(cfg.depth, 1, 6 * d))
    return out.reshape(cfg.depth, SUBLANES_V7X, 6, d)


def _rope_tables(cfg):
    n = cfg.n_lat
    rows = n // GRID_W
    row = jnp.broadcast_to(jnp.arange(rows, dtype=F32)[:, None], (rows, GRID_W)).reshape(n)
    col = jnp.broadcast_to(jnp.arange(GRID_W, dtype=F32)[None, :], (rows, GRID_W)).reshape(n)
    axis_dim = DIFF_QK_DIM // 2
    inv_freq = ROPE_BASE ** (-jnp.arange(0, axis_dim, 2, dtype=F32) / axis_dim)
    ang_r = row[:, None] * inv_freq
    ang_c = col[:, None] * inv_freq
    zeros = jnp.zeros_like(ang_r)
    cos64 = jnp.concatenate([jnp.cos(ang_r), jnp.cos(ang_r), jnp.cos(ang_c), jnp.cos(ang_c)], axis=1)
    sa64 = jnp.concatenate([zeros, jnp.sin(ang_r), zeros, jnp.sin(ang_c)], axis=1)
    sb64 = jnp.concatenate([-jnp.sin(ang_r), zeros, -jnp.sin(ang_c), zeros], axis=1)
    ident = cfg.tm
    cos = jnp.concatenate([jnp.tile(cos64, (1, 2)), jnp.ones((ident, LANES_V7X), F32)], axis=0)
    sa = jnp.concatenate([jnp.tile(sa64, (1, 2)), jnp.zeros((ident, LANES_V7X), F32)], axis=0)
    sb = jnp.concatenate([jnp.tile(sb64, (1, 2)), jnp.zeros((ident, LANES_V7X), F32)], axis=0)
    return cos, sa, sb


def _ab_in_kernel(x_ref, mod_ref, g_ref, w_ref, cos_ref, sa_ref, sb_ref,
                  q_ref, k_ref, v_ref, gb_ref, u_ref):
    m = mod_ref[0]
    hb = _norm_mod(x_ref[...], g_ref[...], m[0:1], m[1:2]).astype(BF16)
    cos, sa, sb = cos_ref[...], sa_ref[...], sb_ref[...]

    def proj(j):
        return jnp.dot(hb, w_ref[:, j * DIFF_WIDTH:(j + 1) * DIFF_WIDTH], preferred_element_type=F32)

    def rope(t, out_ref, scale):
        for gi in range(DIFF_WIDTH // LANES_V7X):
            xg = t[:, gi * LANES_V7X:(gi + 1) * LANES_V7X]
            r = xg * cos + pltpu.roll(xg, 16, 1) * sa + pltpu.roll(xg, LANES_V7X - 16, 1) * sb
            out_ref[:, gi * LANES_V7X:(gi + 1) * LANES_V7X] = (r * scale).astype(out_ref.dtype)

    rope(proj(0), q_ref, DIFF_QK_DIM ** -0.5)
    rope(proj(1), k_ref, 1.0)
    v_ref[...] = proj(2).astype(v_ref.dtype)
    gb_ref[...] = proj(3)
    u_ref[...] = proj(4) * proj(5)


def _ab_in(cfg, x, mod_l, g1, w_in_bf16, tables):
    t, d, tm = cfg.t_all, cfg.d_model, cfg.tm
    n_lat_tiles, per_seq = cfg.t_lat // tm, cfg.n_lat // tm
    tab_spec = pl.BlockSpec((tm, LANES_V7X), lambda i: (jnp.where(i < n_lat_tiles, i % per_seq, per_seq), 0))
    row = lambda w: pl.BlockSpec((tm, w), lambda i: (i, 0))
    return pl.pallas_call(
        _ab_in_kernel,
        grid=(t // tm,),
        in_specs=[row(d), _mod_spec(cfg), _const_spec((1, d)), _const_spec(w_in_bf16.shape),
                  tab_spec, tab_spec, tab_spec],
        out_specs=[row(DIFF_WIDTH)] * 5,
        out_shape=[jax.ShapeDtypeStruct((t, DIFF_WIDTH), BF16)] * 3
        + [jax.ShapeDtypeStruct((t, CONV_WIDTH), F32)] * 2,
        compiler_params=_cparams(("arbitrary",)),
        name="ab_in",
    )(x, mod_l, g1, w_in_bf16, *tables)


def _attn_kernel(lvec_ref, q_ref, kc_ref, kl_ref, vc_ref, vl_ref, g_ref, o_ref, *, tq, n_lat_blocks, lam_init):
    lv = lvec_ref[...]
    lam = (jnp.exp(jnp.sum(lv[0:1] * lv[1:2], axis=-1, keepdims=True))
           - jnp.exp(jnp.sum(lv[2:3] * lv[3:4], axis=-1, keepdims=True)) + lam_init)

    q = q_ref[...].astype(F32)
    lane = lax.broadcasted_iota(jnp.int32, q.shape, 1)
    half = DIFF_QK_DIM
    q2 = jnp.concatenate([jnp.where(lane < half, q, 0.0), jnp.where(lane >= half, q, 0.0)], axis=0).astype(BF16)
    nt = (((1,), (1,)), ((), ()))

    def finish(pv):
        o = pv * lax.rsqrt(jnp.mean(pv * pv, axis=-1, keepdims=True) + RMS_EPS)
        o_ref[...] = ((o * g_ref[...]) * (1.0 - lam_init)).astype(o_ref.dtype)

    def coef(l):
        r = 1.0 / l
        return r[:tq], lam * r[tq:]

    @pl.when(pl.program_id(2) < n_lat_blocks)
    def _():
        s_c = lax.dot_general(q2, kc_ref[...], nt, preferred_element_type=F32)
        s_l = lax.dot_general(q2, kl_ref[...], nt, preferred_element_type=F32)
        mx = jnp.maximum(jnp.max(s_c, axis=-1, keepdims=True), jnp.max(s_l, axis=-1, keepdims=True))
        p_c = jnp.exp(s_c - mx)
        p_l = jnp.exp(s_l - mx)
        c0, c1 = coef(jnp.sum(p_c, axis=-1, keepdims=True) + jnp.sum(p_l, axis=-1, keepdims=True))
        pd_c = (p_c[:tq] * c0 - p_c[tq:] * c1).astype(BF16)
        pd_l = (p_l[:tq] * c0 - p_l[tq:] * c1).astype(BF16)
        finish(jnp.dot(pd_c, vc_ref[...], preferred_element_type=F32)
               + jnp.dot(pd_l, vl_ref[...], preferred_element_type=F32))

    @pl.when(pl.program_id(2) >= n_lat_blocks)
    def _():
        s_c = lax.dot_general(q2, kc_ref[...], nt, preferred_element_type=F32)
        p_c = jnp.exp(s_c - jnp.max(s_c, axis=-1, keepdims=True))
        c0, c1 = coef(jnp.sum(p_c, axis=-1, keepdims=True))
        pd_c = (p_c[:tq] * c0 - p_c[tq:] * c1).astype(BF16)
        finish(jnp.dot(pd_c, vc_ref[...], preferred_element_type=F32))


def _attention(cfg, q, k, v, lvec, subln_g, lam_init):
    tq = cfg.tq
    assert cfg.n_ctx == tq, "context queries are processed as one query tile per batch"
    nqb = cfg.n_lat // tq
    ctx_blk0 = cfg.t_lat // cfg.n_ctx
    hw = DIFF_V_DIM
    q_spec = pl.BlockSpec((tq, hw), lambda b, h, i: (jnp.where(i < nqb, b * nqb + i, ctx_blk0 + b), h))
    ctx_spec = pl.BlockSpec((cfg.n_ctx, hw), lambda b, h, i: (ctx_blk0 + b, h))
    lat_spec = pl.BlockSpec((cfg.n_lat, hw), lambda b, h, i: (b, h))
    return pl.pallas_call(
        functools.partial(_attn_kernel, tq=tq, n_lat_blocks=nqb, lam_init=lam_init),
        grid=(cfg.batch, N_DIFF_HEADS, nqb + 1),
        in_specs=[_const_spec(lvec.shape), q_spec, ctx_spec, lat_spec, ctx_spec, lat_spec,
                  _const_spec((1, hw))],
        out_specs=q_spec,
        out_shape=jax.ShapeDtypeStruct((cfg.t_all, DIFF_WIDTH), BF16),
        compiler_params=_cparams(("arbitrary", "arbitrary", "arbitrary")),
        name="diff_attn",
    )(lvec, q, k, k, v, v, subln_g)


def _route_from_logits(lg):
    lane = lax.broadcasted_iota(jnp.int32, lg.shape, 1)
    neg = -jnp.inf
    big = jnp.int32(ROUTE_LANES)
    gl = jnp.where(lane < MOE_GROUPS, lg, neg)
    gmax = jnp.max(gl, axis=-1, keepdims=True)
    gidx = jnp.min(jnp.where(gl == gmax, lane, big), axis=-1, keepdims=True)
    gw = 1.0 / jnp.sum(jnp.exp(gl - gmax), axis=-1, keepdims=True)
    lo = MOE_GROUPS + gidx * MOE_EXPERTS_PER_GROUP
    el = jnp.where((lane >= lo) & (lane < lo + MOE_EXPERTS_PER_GROUP), lg, neg)
    m1 = jnp.max(el, axis=-1, keepdims=True)
    i1 = jnp.min(jnp.where(el == m1, lane, big), axis=-1, keepdims=True)
    el2 = jnp.where(lane == i1, neg, el)
    m2 = jnp.max(el2, axis=-1, keepdims=True)
    i2 = jnp.min(jnp.where(el2 == m2, lane, big), axis=-1, keepdims=True)
    w1 = gw / (1.0 + jnp.exp(m2 - m1))
    w2 = gw - w1
    e1 = (i1 - MOE_GROUPS).astype(F32)
    e2 = (i2 - MOE_GROUPS).astype(F32)
    return jnp.where(lane == 0, e1, jnp.where(lane == 1, e2, jnp.where(lane == 2, w1, jnp.where(lane == 3, w2, 0.0))))


def _residual_norm_route(x, mix, m, g2, wr_ref, br_ref, xo_ref, h2_ref, rt_ref):
    xn = x + m[2:3] * mix
    xo_ref[...] = xn
    h2 = _norm_mod(xn, g2, m[3:4], m[4:5])
    h2_ref[...] = h2.astype(h2_ref.dtype)
    lg = jnp.dot(h2, wr_ref[...], precision=HIGHEST, preferred_element_type=F32) + br_ref[...]
    rt_ref[...] = _route_from_logits(lg)


def _epilogue_specs(cfg):
    d, tm = cfg.d_model, cfg.tm
    row = lambda w: pl.BlockSpec((tm, w), lambda i: (i, 0))
    in_specs = [row(d), _mod_spec(cfg), _const_spec((1, d)), _const_spec((d, ROUTE_LANES)),
                _const_spec((1, ROUTE_LANES))]
    out_specs = [row(d), row(d), row(ROUTE_LANES)]
    out_shape = [jax.ShapeDtypeStruct((cfg.t_all, d), F32), jax.ShapeDtypeStruct((cfg.t_all, d), BF16),
                 jax.ShapeDtypeStruct((cfg.t_all, ROUTE_LANES), F32)]
    return in_specs, out_specs, out_shape


def _halo_specs(cfg, width, col_block=0):
    per = cfg.tm // SUBLANES_V7X
    last = cfg.t_all // SUBLANES_V7X - 1
    prev = pl.BlockSpec((SUBLANES_V7X, width), lambda i, *_: (jnp.maximum(i * per - 1, 0), col_block))
    nxt = pl.BlockSpec((SUBLANES_V7X, width), lambda i, *_: (jnp.minimum((i + 1) * per, last), col_block))
    return prev, nxt


def _shifted_rows(pad_ref, u, prev_blk, next_blk, pos, seqlen):
    tm = u.shape[0]
    s = SUBLANES_V7X
    pad_ref[s:s + tm, :] = u
    pad_ref[0:s, :] = prev_blk
    pad_ref[s + tm:2 * s + tm, :] = next_blk
    um1 = jnp.where(pos == 0, 0.0, pad_ref[s - 1:s - 1 + tm, :])
    up1 = jnp.where(pos == seqlen - 1, 0.0, pad_ref[s + 1:s + 1 + tm, :])
    return um1, up1


def _ab_out_kernel(o_ref, gb_ref, u_ref, up_ref, un_ref, cw_ref, wo_ref, wc_ref,
                   x_ref, mod_ref, g2_ref, wr_ref, br_ref,
                   xo_ref, h2_ref, rt_ref, pad_ref, *, cfg):
    pos, seqlen = _seq_pos(cfg, cfg.tm)
    u = u_ref[...]
    um1, up1 = _shifted_rows(pad_ref, u, up_ref[...], un_ref[...], pos, seqlen)
    cw = cw_ref[...]
    conv = gb_ref[...] * (um1 * cw[0:1] + u * cw[1:2] + up1 * cw[2:3])
    mix = (jnp.dot(o_ref[...], wo_ref[...], preferred_element_type=F32)
           + jnp.dot(conv.astype(BF16), wc_ref[...], preferred_element_type=F32))
    _residual_norm_route(x_ref[...], mix, mod_ref[0], g2_ref[...], wr_ref, br_ref, xo_ref, h2_ref, rt_ref)


def _ab_out(cfg, o, gb, u, conv_w, wo, wc, x, mod_l, g2, wr, br):
    tm = cfg.tm
    row = lambda w: pl.BlockSpec((tm, w), lambda i: (i, 0))
    prev, nxt = _halo_specs(cfg, CONV_WIDTH)
    ep_in, ep_out, ep_shape = _epilogue_specs(cfg)
    return pl.pallas_call(
        functools.partial(_ab_out_kernel, cfg=cfg),
        grid=(cfg.t_all // tm,),
        in_specs=[row(DIFF_WIDTH), row(CONV_WIDTH), row(CONV_WIDTH), prev, nxt, _const_spec(conv_w.shape),
                  _const_spec(wo.shape), _const_spec(wc.shape)] + ep_in,
        out_specs=ep_out,
        out_shape=ep_shape,
        scratch_shapes=[pltpu.VMEM((tm + 2 * SUBLANES_V7X, CONV_WIDTH), F32)],
        compiler_params=_cparams(("arbitrary",)),
        name="ab_out",
    )(o, gb, u, u, u, conv_w, wo, wc, x, mod_l, g2, wr, br)


def _ssd_in_kernel(x_ref, mod_ref, g_ref, w_ref, wdt_ref, z_ref, xbc_ref, dt_ref):
    m = mod_ref[0]
    hb = _norm_mod(x_ref[...], g_ref[...], m[0:1], m[1:2]).astype(BF16)
    cw = 512
    for j in range(SSD_D_INNER // cw):
        z_ref[:, j * cw:(j + 1) * cw] = jnp.dot(hb, w_ref[:, j * cw:(j + 1) * cw],
                                                preferred_element_type=F32).astype(z_ref.dtype)
    for j in range(SSD_CONV_CH // cw):
        c0 = SSD_D_INNER + j * cw
        xbc_ref[:, j * cw:(j + 1) * cw] = jnp.dot(hb, w_ref[:, c0:c0 + cw], preferred_element_type=F32)
    dt_ref[...] = jnp.dot(hb, wdt_ref[...], preferred_element_type=F32)


def _ssd_in(cfg, x, mod_l, g1, w_main, w_dt):
    t, d, tm = cfg.t_all, cfg.d_model, cfg.tm
    row = lambda w: pl.BlockSpec((tm, w), lambda i: (i, 0))
    return pl.pallas_call(
        _ssd_in_kernel,
        grid=(t // tm,),
        in_specs=[row(d), _mod_spec(cfg), _const_spec((1, d)), _const_spec(w_main.shape), _const_spec(w_dt.shape)],
        out_specs=[row(SSD_D_INNER), row(SSD_CONV_CH), row(LANES_V7X)],
        out_shape=[jax.ShapeDtypeStruct((t, SSD_D_INNER), F32), jax.ShapeDtypeStruct((t, SSD_CONV_CH), F32),
                   jax.ShapeDtypeStruct((t, LANES_V7X), F32)],
        compiler_params=_cparams(("arbitrary",)),
        name="ssd_in",
    )(x, mod_l, g1, w_main, w_dt)


def _ssd_conv_kernel(u_ref, up_ref, un_ref, cw_ref, cb_ref, o_ref, pad_ref, *, cfg):
    pos, seqlen = _seq_pos(cfg, cfg.tm)
    u = u_ref[...]
    um1, up1 = _shifted_rows(pad_ref, u, up_ref[...], un_ref[...], pos, seqlen)
    cw = cw_ref[...]
    o_ref[...] = _silu(um1 * cw[0:1] + u * cw[1:2] + up1 * cw[2:3] + cb_ref[...]).astype(o_ref.dtype)


def _ssd_conv(cfg, xbc, conv_w, conv_b):
    tm, cw = cfg.tm, 512
    per = tm // SUBLANES_V7X
    last = cfg.t_all // SUBLANES_V7X - 1
    return pl.pallas_call(
        functools.partial(_ssd_conv_kernel, cfg=cfg),
        grid=(cfg.t_all // tm, SSD_CONV_CH // cw),
        in_specs=[pl.BlockSpec((tm, cw), lambda i, j: (i, j)),
                  pl.BlockSpec((SUBLANES_V7X, cw), lambda i, j: (jnp.maximum(i * per - 1, 0), j)),
                  pl.BlockSpec((SUBLANES_V7X, cw), lambda i, j: (jnp.minimum((i + 1) * per, last), j)),
                  pl.BlockSpec((3, cw), lambda i, j: (0, j)),
                  pl.BlockSpec((1, cw), lambda i, j: (0, j))],
        out_specs=pl.BlockSpec((tm, cw), lambda i, j: (i, j)),
        out_shape=jax.ShapeDtypeStruct((cfg.t_all, SSD_CONV_CH), BF16),
        scratch_shapes=[pltpu.VMEM((tm + 2 * SUBLANES_V7X, cw), F32)],
        compiler_params=_cparams(("arbitrary", "arbitrary")),
        name="ssd_conv",
    )(xbc, xbc, xbc, conv_w, conv_b)


def _split3(v):
    b1 = v.astype(BF16)
    r1 = v - b1.astype(F32)
    b2 = r1.astype(BF16)
    b3 = (r1 - b2.astype(F32)).astype(BF16)
    return jnp.concatenate([b1, b2, b3], axis=1)


def _softplus(v):
    return jnp.maximum(v, 0.0) + jnp.log(1.0 + jnp.exp(-jnp.abs(v)))


def _ssd_scan_kernel(xs_ref, b_ref, c_ref, dt_ref, dtt_ref, bias_ref, biast_ref, alog_ref, alogt_ref,
                     ebig_ref, ewide_ref, y_ref, state_ref):
    q = SSD_CHUNK
    d = pl.program_id(1)
    fwd = d == 0

    @pl.when(pl.program_id(2) == 0)
    def _():
        state_ref[...] = jnp.zeros_like(state_ref)

    dt = _softplus(dt_ref[0] + bias_ref[0])
    dtt = _softplus(dtt_ref[0] + biast_ref[0])
    a = dt * (-jnp.exp(alog_ref[0]))
    at = dtt * (-jnp.exp(alogt_ref[0]))

    ri = lax.broadcasted_iota(jnp.int32, (q, q), 0)
    ci = lax.broadcasted_iota(jnp.int32, (q, q), 1)
    keep = (ri - ci) * jnp.where(fwd, 1, -1) >= 0
    tri = jnp.where(keep, 1.0, 0.0)
    cs = jnp.dot(tri, a, precision=HIGHEST, preferred_element_type=F32)
    cst = lax.dot_general(at, tri, (((1,), (1,)), ((), ())), precision=HIGHEST,
                          preferred_element_type=F32)
    tot = jnp.sum(a, axis=0, keepdims=True)

    ebig = ebig_ref[...]
    ewide = ewide_ref[...]
    cs_big = jnp.dot(_split3(cs), ebig, preferred_element_type=F32)
    w_wide = jnp.dot(_split3(dt * jnp.exp(tot - cs)), ewide, preferred_element_type=F32)
    dec_wide = jnp.dot(_split3(jnp.broadcast_to(jnp.exp(tot), (SUBLANES_V7X, SSD_HEADS))), ewide,
                       preferred_element_type=F32)[0:1]

    lane = lax.broadcasted_iota(jnp.int32, (q, LANES_V7X), 1)
    lo = lane < SSD_HEAD_DIM
    hpg = SSD_HEADS // SSD_GROUPS
    gw = hpg * SSD_HEAD_DIM
    nt = (((1,), (1,)), ((), ()))
    tn = (((0,), (0,)), ((), ()))
    for g in range(SSD_GROUPS):
        bm = b_ref[:, g * SSD_STATE:(g + 1) * SSD_STATE]
        cm = c_ref[:, g * SSD_STATE:(g + 1) * SSD_STATE]
        cb = lax.dot_general(cm, bm, nt, preferred_element_type=F32)
        cmf = cm.astype(F32)
        for pr in range(hpg // 2):
            lhs, rhs = [], []
            c0 = g * gw + pr * LANES_V7X
            xs_pair = xs_ref[:, c0:c0 + LANES_V7X]
            st_pair = state_ref[:, c0:c0 + LANES_V7X].astype(BF16)
            zero = jnp.zeros_like(xs_pair)
            for k in range(2):
                h = g * hpg + pr * 2 + k
                csb = cs_big[:, h * q:(h + 1) * q]
                seg = csb - cst[h:h + 1, :]
                lmat = jnp.exp(jnp.where(keep, seg, -jnp.inf))
                lhs.append((cb * lmat * dtt[h:h + 1, :]).astype(BF16))
                lhs.append((cmf * jnp.exp(csb)).astype(BF16))
                sel = lo if k == 0 else jnp.logical_not(lo)
                rhs.append(jnp.where(sel, xs_pair, zero))
                rhs.append(jnp.where(sel, st_pair, zero))
            y = jnp.dot(jnp.concatenate(lhs, axis=1), jnp.concatenate(rhs, axis=0), preferred_element_type=F32)
            y_ref[0, :, c0:c0 + LANES_V7X] = y.astype(y_ref.dtype)
        sl = slice(g * gw, (g + 1) * gw)
        x2 = (xs_ref[:, sl].astype(F32) * w_wide[:, sl]).astype(BF16)
        upd = lax.dot_general(bm, x2, tn, preferred_element_type=F32)
        state_ref[:, sl] = state_ref[:, sl] * dec_wide[:, sl] + upd


def _ssd_scan(cfg, xbc_act, dt_raw, dt_bias, a_log):
    q = SSD_CHUNK
    n_cc, n_lc = cfg.n_ctx // q, cfg.n_lat // q
    ctx0 = cfg.t_lat // q
    h = SSD_HEADS
    dt2 = dt_raw[:, :2 * h].reshape(cfg.t_all, 2, h).transpose(1, 0, 2)
    dt2t = dt2.transpose(0, 2, 1)
    bias = dt_bias.reshape(2, 1, h)
    biast = dt_bias.reshape(2, h, 1)
    alog = a_log.reshape(2, 1, h)
    alogt = a_log.reshape(2, h, 1)
    head_of_row = jnp.tile(jnp.arange(h), 3)[:, None]
    ebig = (head_of_row == (jnp.arange(h * q) // q)[None, :]).astype(BF16)
    ewide = (head_of_row == (jnp.arange(SSD_D_INNER) // SSD_HEAD_DIM)[None, :]).astype(BF16)

    def blk(b, d, s):
        cs = jnp.where(d == 0, s, n_cc - 1 - s)
        ls = jnp.where(d == 0, s - n_cc, n_lc - 1 - (s - n_cc))
        return jnp.where(s < n_cc, ctx0 + b * n_cc + cs, b * n_lc + ls)

    gs = SSD_GROUPS * SSD_STATE
    col = lambda w, cblk: pl.BlockSpec((q, w), lambda b, d, s: (blk(b, d, s), cblk))
    per_dir = lambda shape: pl.BlockSpec((1,) + shape, lambda b, d, s: (d, 0, 0))
    return pl.pallas_call(
        _ssd_scan_kernel,
        grid=(cfg.batch, 2, n_cc + n_lc),
        in_specs=[col(SSD_D_INNER, 0), col(gs, SSD_D_INNER // gs), col(gs, SSD_D_INNER // gs + 1),
                  pl.BlockSpec((1, q, h), lambda b, d, s: (d, blk(b, d, s), 0)),
                  pl.BlockSpec((1, h, q), lambda b, d, s: (d, 0, blk(b, d, s))),
                  per_dir((1, h)), per_dir((h, 1)), per_dir((1, h)), per_dir((h, 1)),
                  _const_spec(ebig.shape), _const_spec(ewide.shape)],
        out_specs=pl.BlockSpec((1, q, SSD_D_INNER), lambda b, d, s: (d, blk(b, d, s), 0)),
        out_shape=jax.ShapeDtypeStruct((2, cfg.t_all, SSD_D_INNER), BF16),
        scratch_shapes=[pltpu.VMEM((SSD_STATE, SSD_D_INNER), F32)],
        compiler_params=_cparams(("arbitrary", "arbitrary", "arbitrary")),
        name="ssd_scan",
    )(xbc_act, xbc_act, xbc_act, dt2, dt2t, bias, biast, alog, alogt, ebig, ewide)


def _ssd_out_kernel(yf_ref, yb_ref, xs_ref, z_ref, dw_ref, ng_ref, wo_ref,
                    x_ref, mod_ref, g2_ref, wr_ref, br_ref, xo_ref, h2_ref, rt_ref):
    gw = SSD_D_INNER // SSD_GROUPS
    mix = None
    for g in range(SSD_GROUPS):
        sl = slice(g * gw, (g + 1) * gw)
        y = yf_ref[0, :, sl].astype(F32) + yb_ref[0, :, sl].astype(F32) + xs_ref[:, sl].astype(F32) * dw_ref[:, sl]
        y = y * _silu(z_ref[:, sl])
        y = (y * lax.rsqrt(jnp.mean(y * y, axis=-1, keepdims=True) + RMS_EPS)) * ng_ref[:, sl]
        part = jnp.dot(y.astype(BF16), wo_ref[sl, :], preferred_element_type=F32)
        mix = part if mix is None else mix + part
    _residual_norm_route(x_ref[...], mix, mod_ref[0], g2_ref[...], wr_ref, br_ref, xo_ref, h2_ref, rt_ref)


def _ssd_out(cfg, y2, xbc_act, z, d_wide, norm_g, wo, x, mod_l, g2, wr, br):
    tm = cfg.tm
    row = lambda w: pl.BlockSpec((tm, w), lambda i: (i, 0))
    ydir = lambda d: pl.BlockSpec((1, tm, SSD_D_INNER), lambda i: (d, i, 0))
    ep_in, ep_out, ep_shape = _epilogue_specs(cfg)
    return pl.pallas_call(
        _ssd_out_kernel,
        grid=(cfg.t_all // tm,),
        in_specs=[ydir(0), ydir(1), row(SSD_D_INNER), row(SSD_D_INNER), _const_spec((1, SSD_D_INNER)),
                  _const_spec((1, SSD_D_INNER)), _const_spec(wo.shape)] + ep_in,
        out_specs=ep_out,
        out_shape=ep_shape,
        compiler_params=_cparams(("arbitrary",)),
        name="ssd_out",
    )(y2, y2, xbc_act, z, d_wide, norm_g, wo, x, mod_l, g2, wr, br)


def _expert_kernel(eid_ref, nused_ref, xb_ref, wg_ref, wu_ref, wd_ref, y_ref, wg_s, wu_s, wd_s):
    i = pl.program_id(0)
    changed = jnp.logical_or(i == 0, eid_ref[i] != eid_ref[jnp.maximum(i - 1, 0)])

    @pl.when(jnp.logical_and(changed, i < nused_ref[0]))
    def _():
        wg_s[...] = wg_ref[0, 0].astype(BF16)
        wu_s[...] = wu_ref[0, 0].astype(BF16)
        wd_s[...] = wd_ref[0, 0].astype(BF16)

    @pl.when(i < nused_ref[0])
    def _():
        xb = xb_ref[...]
        hg = jnp.dot(xb, wg_s[...], preferred_element_type=F32)
        hu = jnp.dot(xb, wu_s[...], preferred_element_type=F32)
        y_ref[...] = jnp.dot((_silu(hg) * hu).astype(BF16), wd_s[...], preferred_element_type=F32)

    @pl.when(i >= nused_ref[0])
    def _():
        y_ref[...] = jnp.zeros_like(y_ref)


def _experts(cfg, layer, block_eid, n_used, xb, w_gate, w_up, w_down):
    d, tile = cfg.d_model, cfg.moe_tile
    n_blocks = xb.shape[0] // tile
    grid_spec = pltpu.PrefetchScalarGridSpec(
        num_scalar_prefetch=2,
        grid=(n_blocks,),
        in_specs=[pl.BlockSpec((tile, d), lambda i, e, n: (i, 0)),
                  pl.BlockSpec((1, 1, d, MOE_HIDDEN), lambda i, e, n: (layer, e[i], 0, 0)),
                  pl.BlockSpec((1, 1, d, MOE_HIDDEN), lambda i, e, n: (layer, e[i], 0, 0)),
                  pl.BlockSpec((1, 1, MOE_HIDDEN, d), lambda i, e, n: (layer, e[i], 0, 0))],
        out_specs=pl.BlockSpec((tile, d), lambda i, e, n: (i, 0)),
        scratch_shapes=[pltpu.VMEM((d, MOE_HIDDEN), BF16), pltpu.VMEM((d, MOE_HIDDEN), BF16),
                        pltpu.VMEM((MOE_HIDDEN, d), BF16)],
    )
    return pl.pallas_call(
        _expert_kernel,
        grid_spec=grid_spec,
        out_shape=jax.ShapeDtypeStruct((n_blocks * tile, d), F32),
        compiler_params=_cparams(("arbitrary",)),
        name="moe_experts",
    )(block_eid, n_used, xb, w_gate, w_up, w_down)


def _dispatch_kernel(rt_ref, pos_ref, cnt_ref, carry_ref, start_ref, *, tile):
    p, i = pl.program_id(0), pl.program_id(1)
    tm = rt_ref.shape[0]
    rt = rt_ref[...]
    lane = lax.broadcasted_iota(jnp.int32, rt.shape, 1)
    o1 = jnp.where(lane == rt[:, 0:1].astype(jnp.int32), 1.0, 0.0)
    o2 = jnp.where(lane == rt[:, 1:2].astype(jnp.int32), 1.0, 0.0)
    cnt1 = jnp.sum(o1, axis=0, keepdims=True)
    cnt2 = jnp.sum(o2, axis=0, keepdims=True)

    @pl.when(jnp.logical_and(p == 0, i == 0))
    def _():
        cnt_ref[...] = jnp.zeros_like(cnt_ref)

    @pl.when(p == 0)
    def _():
        cnt_ref[...] += cnt1 + cnt2

    @pl.when(jnp.logical_and(p == 1, i == 0))
    def _():
        padded = jnp.floor((cnt_ref[...] + (tile - 1)) * (1.0 / tile)) * tile
        r = lax.broadcasted_iota(jnp.int32, (ROUTE_LANES, ROUTE_LANES), 0)
        c = lax.broadcasted_iota(jnp.int32, (ROUTE_LANES, ROUTE_LANES), 1)
        excl = jnp.where(r < c, 1.0, 0.0)
        start_ref[...] = jnp.dot(padded, excl, precision=HIGHEST, preferred_element_type=F32)
        carry_ref[...] = jnp.zeros_like(carry_ref)

    @pl.when(p == 1)
    def _():
        ri = lax.broadcasted_iota(jnp.int32, (tm, tm), 0)
        ci = lax.broadcasted_iota(jnp.int32, (tm, tm), 1)
        earlier = jnp.where(ci < ri, 1.0, 0.0).astype(BF16)
        p1 = jnp.dot(earlier, o1.astype(BF16), preferred_element_type=F32)
        p2 = jnp.dot(earlier, o2.astype(BF16), preferred_element_type=F32)
        base = start_ref[0:1] + carry_ref[0:1]
        pos1 = jnp.sum(o1 * (base + p1), axis=-1, keepdims=True)
        pos2 = jnp.sum(o2 * (base + cnt1 + p2), axis=-1, keepdims=True)
        pos_ref[...] = jnp.where(lane == 0, pos1, jnp.where(lane == 1, pos2, 0.0)).astype(jnp.int32)
        carry_ref[...] += cnt1 + cnt2


def _dispatch(cfg, route):
    tile, tm = cfg.moe_tile, cfg.tm
    t = cfg.t_all
    a_total = 2 * t
    pos, cnt = pl.pallas_call(
        functools.partial(_dispatch_kernel, tile=tile),
        grid=(2, t // tm),
        in_specs=[pl.BlockSpec((tm, ROUTE_LANES), lambda p, i: (i, 0))],
        out_specs=[pl.BlockSpec((tm, ROUTE_LANES), lambda p, i: (i * p, 0)),
                   pl.BlockSpec((SUBLANES_V7X, ROUTE_LANES), lambda p, i: (0, 0))],
        out_shape=[jax.ShapeDtypeStruct((t, ROUTE_LANES), jnp.int32),
                   jax.ShapeDtypeStruct((SUBLANES_V7X, ROUTE_LANES), F32)],
        scratch_shapes=[pltpu.VMEM((SUBLANES_V7X, ROUTE_LANES), F32), pltpu.VMEM((SUBLANES_V7X, ROUTE_LANES), F32)],
        compiler_params=_cparams(("arbitrary", "arbitrary")),
        name="moe_dispatch",
    )(route)
    counts = cnt[0, :MOE_EXPERTS].astype(jnp.int32)
    pend = jnp.cumsum((counts + tile - 1) // tile * tile)
    n_blocks = (a_total + MOE_EXPERTS * (tile - 1)) // tile
    block_eid = jnp.minimum(jnp.sum(pend[None, :] <= (jnp.arange(n_blocks, dtype=jnp.int32) * tile)[:, None], axis=1),
                            MOE_EXPERTS - 1).astype(jnp.int32)
    n_used = (pend[-1] // tile).astype(jnp.int32).reshape(1)
    pos2 = pos[:, 0:2]
    slot_tok = jnp.zeros((n_blocks * tile,), jnp.int32).at[pos2.reshape(-1)].set(
        jnp.arange(a_total, dtype=jnp.int32) // 2)
    return slot_tok, block_eid, n_used, pos2


def _combine_kernel(x_ref, y1_ref, y2_ref, rt_ref, mod_ref, g_ref, o_ref, *, final):
    rt = rt_ref[...]
    f = rt[:, 2:3] * y1_ref[...] + rt[:, 3:4] * y2_ref[...]
    xn = x_ref[...] + mod_ref[0][5:6] * f
    if final:
        xn = (xn * lax.rsqrt(jnp.mean(xn * xn, axis=-1, keepdims=True) + RMS_EPS)) * g_ref[...]
    o_ref[...] = xn


def _combine(cfg, x, y1, y2, route, mod_l, g, final):
    d, tm = cfg.d_model, cfg.tm
    row = lambda w: pl.BlockSpec((tm, w), lambda i: (i, 0))
    return pl.pallas_call(
        functools.partial(_combine_kernel, final=final),
        grid=(cfg.t_all // tm,),
        in_specs=[row(d), row(d), row(d), row(ROUTE_LANES), _mod_spec(cfg), _const_spec((1, d))],
        out_specs=row(d),
        out_shape=jax.ShapeDtypeStruct((cfg.t_all, d), F32),
        compiler_params=_cparams(("arbitrary",)),
        name="moe_combine",
    )(x, y1, y2, route, mod_l, g)


def _forward(cfg, x, c, ctx, c_ctx, ada_w, ada_b, norm1_g, norm2_g, ab_w_in, ab_w_out,
             diff_lq1, diff_lk1, diff_lq2, diff_lk2, diff_subln_g, bconv_w,
             ssd_w_in, ssd_conv_w, ssd_conv_b, ssd_A_log, ssd_dt_bias, ssd_D, ssd_norm_g, ssd_w_out,
             moe_wg, moe_bg, moe_we, moe_be, moe_w_gate, moe_w_up, moe_w_down, final_norm_g):
    d = cfg.d_model
    b = cfg.batch
    xa = jnp.concatenate([x.reshape(cfg.t_lat, d), ctx.reshape(cfg.t_ctx, d)], axis=0)
    c_all = jnp.zeros((SUBLANES_V7X, d), F32).at[:b].set(c).at[b].set(c_ctx)
    mod = _adaln(cfg, c_all, ada_w, ada_b)
    tables = _rope_tables(cfg)

    for layer in range(cfg.depth):
        i = layer // 2
        mod_l = mod[layer]
        g1 = norm1_g[layer].reshape(1, d)
        g2 = norm2_g[layer].reshape(1, d)
        pad = ROUTE_LANES - MOE_GROUPS - MOE_EXPERTS
        wr = jnp.concatenate([moe_wg[layer], moe_we[layer], jnp.zeros((d, pad), F32)], axis=1)
        br = jnp.concatenate([moe_bg[layer], moe_be[layer], jnp.zeros((pad,), F32)]).reshape(1, ROUTE_LANES)
        if layer % 2 == 0:
            lam_init = 0.8 - 0.6 * math.exp(-0.3 * layer)
            q, k, v, gb, u = _ab_in(cfg, xa, mod_l, g1, ab_w_in[i].astype(BF16), tables)
            lvec = jnp.stack([diff_lq1[i], diff_lk1[i], diff_lq2[i], diff_lk2[i]])
            o = _attention(cfg, q, k, v, lvec, diff_subln_g[i].reshape(1, DIFF_V_DIM), lam_init)
            w_out = ab_w_out[i].astype(BF16)
            xa, h2, route = _ab_out(cfg, o, gb, u, bconv_w[i], w_out[:DIFF_WIDTH], w_out[DIFF_WIDTH:],
                                    xa, mod_l, g2, wr, br)
        else:
            w_in = ssd_w_in[i].astype(BF16)
            n_main = SSD_D_INNER + SSD_CONV_CH
            w_dt = jnp.pad(w_in[:, n_main:], ((0, 0), (0, LANES_V7X - 2 * SSD_HEADS)))
            z, xbc, dt_raw = _ssd_in(cfg, xa, mod_l, g1, w_in[:, :n_main], w_dt)
            xbc_act = _ssd_conv(cfg, xbc, ssd_conv_w[i], ssd_conv_b[i].reshape(1, SSD_CONV_CH))
            y2 = _ssd_scan(cfg, xbc_act, dt_raw, ssd_dt_bias[i], ssd_A_log[i])
            d_wide = jnp.repeat(ssd_D[i], SSD_HEAD_DIM).reshape(1, SSD_D_INNER)
            xa, h2, route = _ssd_out(cfg, y2, xbc_act, z, d_wide, ssd_norm_g[i].reshape(1, SSD_D_INNER),
                                     ssd_w_out[i].astype(BF16), xa, mod_l, g2, wr, br)
        slot_tok, block_eid, n_used, pos = _dispatch(cfg, route)
        yb = _experts(cfg, layer, block_eid, n_used, h2[slot_tok], moe_w_gate, moe_w_up, moe_w_down)
        last = layer == cfg.depth - 1
        xa = _combine(cfg, xa, yb[pos[:, 0]], yb[pos[:, 1]], route, mod_l, final_norm_g.reshape(1, d), last)
    return xa[:cfg.t_lat].reshape(b, cfg.n_lat, d)


def kernel(x, c, ctx, c_ctx, ada_w, ada_b, norm1_g, norm2_g, ab_w_in, ab_w_out, diff_lq1, diff_lk1, diff_lq2, diff_lk2, diff_subln_g, bconv_w, ssd_w_in, ssd_conv_w, ssd_conv_b, ssd_A_log, ssd_dt_bias, ssd_D, ssd_norm_g, ssd_w_out, moe_wg, moe_bg, moe_we, moe_be, moe_w_gate, moe_w_up, moe_w_down, final_norm_g):
    cfg = Cfg(batch=x.shape[0], n_lat=x.shape[1], n_ctx=ctx.shape[1], d_model=x.shape[2], depth=ada_w.shape[0],
              tm=256, tq=256, moe_tile=256)
    return _forward(cfg, x, c, ctx, c_ctx, ada_w, ada_b, norm1_g, norm2_g, ab_w_in, ab_w_out,
                    diff_lq1, diff_lk1, diff_lq2, diff_lk2, diff_subln_g, bconv_w,
                    ssd_w_in, ssd_conv_w, ssd_conv_b, ssd_A_log, ssd_dt_bias, ssd_D, ssd_norm_g, ssd_w_out,
                    moe_wg, moe_bg, moe_we, moe_be, moe_w_gate, moe_w_up, moe_w_down, final_norm_g)
```

```python
import functools
import math
from typing import NamedTuple

import jax
import jax.numpy as jnp
from jax import lax
from jax.experimental import pallas as pl
from jax.experimental.pallas import tpu as pltpu

F32 = jnp.float32
BF16 = jnp.bfloat16
HIGHEST = lax.Precision.HIGHEST

LANES_V7X = 128
SUBLANES_V7X = 8
VMEM_LIMIT_BYTES_V7X = 56 * 1024 * 1024

RMS_EPS = 1e-6
GRID_W = 64
N_DIFF_HEADS = 4
DIFF_QK_DIM = 64
DIFF_V_DIM = 128
DIFF_WIDTH = 512
CONV_WIDTH = 512
ROPE_BASE = 10000.0
SSD_D_INNER = 2048
SSD_HEAD_DIM = 64
SSD_HEADS = 32
SSD_GROUPS = 4
SSD_STATE = 128
SSD_CHUNK = 128
SSD_CONV_CH = SSD_D_INNER + 2 * SSD_GROUPS * SSD_STATE
MOE_GROUPS = 4
MOE_EXPERTS_PER_GROUP = 8
MOE_EXPERTS = 32
MOE_HIDDEN = 512
ROUTE_LANES = LANES_V7X


class Cfg(NamedTuple):
    batch: int
    n_lat: int
    n_ctx: int
    d_model: int
    depth: int
    tm: int
    tq: int
    moe_tile: int
    kchunk: int

    @property
    def t_lat(self):
        return self.batch * self.n_lat

    @property
    def t_ctx(self):
        return self.batch * self.n_ctx

    @property
    def t_all(self):
        return self.t_lat + self.t_ctx


def _cparams(sem):
    return pltpu.CompilerParams(dimension_semantics=sem, vmem_limit_bytes=VMEM_LIMIT_BYTES_V7X)


def _silu(v):
    return v * (1.0 / (1.0 + jnp.exp(-v)))


def _const_spec(shape):
    nd = len(shape)
    return pl.BlockSpec(shape, lambda *_: (0,) * nd)


def _mod_spec(cfg):
    return pl.BlockSpec((1, 6, cfg.d_model),
                        lambda i: (jnp.minimum((i * cfg.tm) // cfg.n_lat, cfg.batch), 0, 0))


def _seq_pos(cfg, tm):
    r0 = pl.program_id(0) * tm
    row = r0 + lax.broadcasted_iota(jnp.int32, (tm, 1), 0)
    seqlen = jnp.where(r0 >= cfg.t_lat, cfg.n_ctx, cfg.n_lat)
    return row & (seqlen - 1), seqlen


def _norm_mod(x, g, shift, scale):
    ms = jnp.mean(x * x, axis=-1, keepdims=True)
    return (x * lax.rsqrt(ms + RMS_EPS) * g) * (1.0 + scale) + shift


def _adaln_kernel(c_ref, w_ref, b_ref, o_ref):
    sc = _silu(c_ref[...])
    o_ref[0] = jnp.dot(sc, w_ref[0], precision=HIGHEST, preferred_element_type=F32) + b_ref[0]


def _adaln(cfg, c_all, ada_w, ada_b):
    d = cfg.d_model
    out = pl.pallas_call(
        _adaln_kernel,
        grid=(cfg.depth, 6),
        in_specs=[_const_spec((SUBLANES_V7X, d)),
                  pl.BlockSpec((1, d, d), lambda l, j: (l, 0, j)),
                  pl.BlockSpec((1, 1, d), lambda l, j: (l, 0, j))],
        out_specs=pl.BlockSpec((1, SUBLANES_V7X, d), lambda l, j: (l, 0, j)),
        out_shape=jax.ShapeDtypeStruct((cfg.depth, SUBLANES_V7X, 6 * d), F32),
        compiler_params=_cparams(("arbitrary", "arbitrary")),
        name="adaln",
    )(c_all, ada_w, ada_b.reshape(cfg.depth, 1, 6 * d))
    return out.reshape(cfg.depth, SUBLANES_V7X, 6, d)


def _rope_tables(cfg):
    n = cfg.n_lat
    rows = n // GRID_W
    row = jnp.broadcast_to(jnp.arange(rows, dtype=F32)[:, None], (rows, GRID_W)).reshape(n)
    col = jnp.broadcast_to(jnp.arange(GRID_W, dtype=F32)[None, :], (rows, GRID_W)).reshape(n)
    axis_dim = DIFF_QK_DIM // 2
    inv_freq = ROPE_BASE ** (-jnp.arange(0, axis_dim, 2, dtype=F32) / axis_dim)
    ang_r = row[:, None] * inv_freq
    ang_c = col[:, None] * inv_freq
    zeros = jnp.zeros_like(ang_r)
    cos64 = jnp.concatenate([jnp.cos(ang_r), jnp.cos(ang_r), jnp.cos(ang_c), jnp.cos(ang_c)], axis=1)
    sa64 = jnp.concatenate([zeros, jnp.sin(ang_r), zeros, jnp.sin(ang_c)], axis=1)
    sb64 = jnp.concatenate([-jnp.sin(ang_r), zeros, -jnp.sin(ang_c), zeros], axis=1)
    ident = cfg.tm
    cos = jnp.concatenate([jnp.tile(cos64, (1, 2)), jnp.ones((ident, LANES_V7X), F32)], axis=0)
    sa = jnp.concatenate([jnp.tile(sa64, (1, 2)), jnp.zeros((ident, LANES_V7X), F32)], axis=0)
    sb = jnp.concatenate([jnp.tile(sb64, (1, 2)), jnp.zeros((ident, LANES_V7X), F32)], axis=0)
    return cos, sa, sb


def _ab_in_kernel(x_ref, mod_ref, g_ref, w_ref, cos_ref, sa_ref, sb_ref,
                  q_ref, k_ref, v_ref, gb_ref, u_ref):
    m = mod_ref[0]
    hb = _norm_mod(x_ref[...], g_ref[...], m[0:1], m[1:2]).astype(BF16)
    cos, sa, sb = cos_ref[...], sa_ref[...], sb_ref[...]

    def proj(j):
        return jnp.dot(hb, w_ref[:, j * DIFF_WIDTH:(j + 1) * DIFF_WIDTH], preferred_element_type=F32)

    def rope(t, out_ref, scale):
        for gi in range(DIFF_WIDTH // LANES_V7X):
            xg = t[:, gi * LANES_V7X:(gi + 1) * LANES_V7X]
            r = xg * cos + pltpu.roll(xg, 16, 1) * sa + pltpu.roll(xg, LANES_V7X - 16, 1) * sb
            out_ref[:, gi * LANES_V7X:(gi + 1) * LANES_V7X] = (r * scale).astype(out_ref.dtype)

    rope(proj(0), q_ref, math.log2(math.e) * DIFF_QK_DIM ** -0.5)
    rope(proj(1), k_ref, 1.0)
    v_ref[...] = proj(2).astype(v_ref.dtype)
    gb_ref[...] = proj(3)
    u_ref[...] = proj(4) * proj(5)


def _ab_in(cfg, x, mod_l, g1, w_in_bf16, tables):
    t, d, tm = cfg.t_all, cfg.d_model, cfg.tm
    n_lat_tiles, per_seq = cfg.t_lat // tm, cfg.n_lat // tm
    tab_spec = pl.BlockSpec((tm, LANES_V7X), lambda i: (jnp.where(i < n_lat_tiles, i % per_seq, per_seq), 0))
    row = lambda w: pl.BlockSpec((tm, w), lambda i: (i, 0))
    return pl.pallas_call(
        _ab_in_kernel,
        grid=(t // tm,),
        in_specs=[row(d), _mod_spec(cfg), _const_spec((1, d)), _const_spec(w_in_bf16.shape),
                  tab_spec, tab_spec, tab_spec],
        out_specs=[row(DIFF_WIDTH)] * 5,
        out_shape=[jax.ShapeDtypeStruct((t, DIFF_WIDTH), BF16)] * 3
        + [jax.ShapeDtypeStruct((t, CONV_WIDTH), F32)] * 2,
        compiler_params=_cparams(("arbitrary",)),
        name="ab_in",
    )(x, mod_l, g1, w_in_bf16, *tables)


def _attn_kernel(lvec_ref, q_ref, kc_ref, kl_ref, vc_ref, vl_ref, g_ref, o_ref, m_ref, l_ref, acc_ref,
                 *, tq, n_lat_blocks, lam_init, kchunk):
    lv = lvec_ref[...]
    lam = (jnp.exp(jnp.sum(lv[0:1] * lv[1:2], axis=-1, keepdims=True))
           - jnp.exp(jnp.sum(lv[2:3] * lv[3:4], axis=-1, keepdims=True)) + lam_init)

    q = q_ref[...].astype(F32)
    lane = lax.broadcasted_iota(jnp.int32, q.shape, 1)
    half = DIFF_QK_DIM
    q2 = jnp.concatenate([jnp.where(lane < half, q, 0.0), jnp.where(lane >= half, q, 0.0)], axis=0).astype(BF16)
    nt = (((1,), (1,)), ((), ()))

    m_ref[...] = jnp.full_like(m_ref, -jnp.inf)
    l_ref[...] = jnp.zeros_like(l_ref)
    acc_ref[...] = jnp.zeros_like(acc_ref)

    def update(k, v):
        s = lax.dot_general(q2, k, nt, preferred_element_type=F32)
        m_prev = m_ref[...]
        m_next = jnp.maximum(m_prev, jnp.max(s, axis=-1, keepdims=True))
        alpha = jnp.exp2(m_prev - m_next)
        nrep = s.shape[1] // LANES_V7X
        p = jnp.exp2(s - jnp.concatenate([m_next] * nrep, axis=1))
        psum = p[:, 0:LANES_V7X]
        for j in range(1, nrep):
            psum = psum + p[:, j * LANES_V7X:(j + 1) * LANES_V7X]
        l_ref[...] = alpha * l_ref[...] + psum
        acc_ref[...] = alpha * acc_ref[...] + jnp.dot(p.astype(BF16), v, preferred_element_type=F32)
        m_ref[...] = m_next

    update(kc_ref[...], vc_ref[...])

    @pl.when(pl.program_id(2) < n_lat_blocks)
    def _():
        for c in range(kl_ref.shape[0] // kchunk):
            update(kl_ref[c * kchunk:(c + 1) * kchunk, :], vl_ref[c * kchunk:(c + 1) * kchunk, :])

    r = 1.0 / jnp.sum(l_ref[...], axis=-1, keepdims=True)
    acc = acc_ref[...]
    pv = acc[:tq] * r[:tq] - (lam * r[tq:]) * acc[tq:]
    o = pv * lax.rsqrt(jnp.mean(pv * pv, axis=-1, keepdims=True) + RMS_EPS)
    o_ref[...] = ((o * g_ref[...]) * (1.0 - lam_init)).astype(o_ref.dtype)


def _attention(cfg, q, k, v, lvec, subln_g, lam_init):
    tq = cfg.tq
    assert cfg.n_ctx == tq, "context queries are processed as one query tile per batch"
    nqb = cfg.n_lat // tq
    ctx_blk0 = cfg.t_lat // cfg.n_ctx
    hw = DIFF_V_DIM
    kchunk = min(cfg.kchunk, cfg.n_lat)
    q_spec = pl.BlockSpec((tq, hw), lambda b, h, i: (jnp.where(i < nqb, b * nqb + i, ctx_blk0 + b), h))
    ctx_spec = pl.BlockSpec((cfg.n_ctx, hw), lambda b, h, i: (ctx_blk0 + b, h))
    lat_spec = pl.BlockSpec((cfg.n_lat, hw), lambda b, h, i: (b, h))
    return pl.pallas_call(
        functools.partial(_attn_kernel, tq=tq, n_lat_blocks=nqb, lam_init=lam_init, kchunk=kchunk),
        grid=(cfg.batch, N_DIFF_HEADS, nqb + 1),
        in_specs=[_const_spec(lvec.shape), q_spec, ctx_spec, lat_spec, ctx_spec, lat_spec,
                  _const_spec((1, hw))],
        out_specs=q_spec,
        out_shape=jax.ShapeDtypeStruct((cfg.t_all, DIFF_WIDTH), BF16),
        scratch_shapes=[pltpu.VMEM((2 * tq, LANES_V7X), F32), pltpu.VMEM((2 * tq, LANES_V7X), F32),
                        pltpu.VMEM((2 * tq, hw), F32)],
        compiler_params=_cparams(("arbitrary", "arbitrary", "arbitrary")),
        name="diff_attn",
    )(lvec, q, k, k, v, v, subln_g)


def _route_from_logits(lg):
    lane = lax.broadcasted_iota(jnp.int32, lg.shape, 1)
    neg = -jnp.inf
    big = jnp.int32(ROUTE_LANES)
    gl = jnp.where(lane < MOE_GROUPS, lg, neg)
    gmax = jnp.max(gl, axis=-1, keepdims=True)
    gidx = jnp.min(jnp.where(gl == gmax, lane, big), axis=-1, keepdims=True)
    gw = 1.0 / jnp.sum(jnp.exp(gl - gmax), axis=-1, keepdims=True)
    lo = MOE_GROUPS + gidx * MOE_EXPERTS_PER_GROUP
    el = jnp.where((lane >= lo) & (lane < lo + MOE_EXPERTS_PER_GROUP), lg, neg)
    m1 = jnp.max(el, axis=-1, keepdims=True)
    i1 = jnp.min(jnp.where(el == m1, lane, big), axis=-1, keepdims=True)
    el2 = jnp.where(lane == i1, neg, el)
    m2 = jnp.max(el2, axis=-1, keepdims=True)
    i2 = jnp.min(jnp.where(el2 == m2, lane, big), axis=-1, keepdims=True)
    w1 = gw / (1.0 + jnp.exp(m2 - m1))
    w2 = gw - w1
    e1 = (i1 - MOE_GROUPS).astype(F32)
    e2 = (i2 - MOE_GROUPS).astype(F32)
    return jnp.where(lane == 0, e1, jnp.where(lane == 1, e2, jnp.where(lane == 2, w1, jnp.where(lane == 3, w2, 0.0))))


def _residual_norm_route(x, mix, m, g2, wr_ref, br_ref, xo_ref, h2_ref, rt_ref):
    xn = x + m[2:3] * mix
    xo_ref[...] = xn
    h2 = _norm_mod(xn, g2, m[3:4], m[4:5])
    h2_ref[...] = h2.astype(h2_ref.dtype)
    lg = jnp.dot(h2, wr_ref[...], precision=HIGHEST, preferred_element_type=F32) + br_ref[...]
    rt_ref[...] = _route_from_logits(lg)


def _epilogue_specs(cfg):
    d, tm = cfg.d_model, cfg.tm
    row = lambda w: pl.BlockSpec((tm, w), lambda i: (i, 0))
    in_specs = [row(d), _mod_spec(cfg), _const_spec((1, d)), _const_spec((d, ROUTE_LANES)),
                _const_spec((1, ROUTE_LANES))]
    out_specs = [row(d), row(d), row(ROUTE_LANES)]
    out_shape = [jax.ShapeDtypeStruct((cfg.t_all, d), F32), jax.ShapeDtypeStruct((cfg.t_all, d), F32),
                 jax.ShapeDtypeStruct((cfg.t_all, ROUTE_LANES), F32)]
    return in_specs, out_specs, out_shape


def _halo_specs(cfg, width, col_block=0):
    per = cfg.tm // SUBLANES_V7X
    last = cfg.t_all // SUBLANES_V7X - 1
    prev = pl.BlockSpec((SUBLANES_V7X, width), lambda i, *_: (jnp.maximum(i * per - 1, 0), col_block))
    nxt = pl.BlockSpec((SUBLANES_V7X, width), lambda i, *_: (jnp.minimum((i + 1) * per, last), col_block))
    return prev, nxt


def _shifted_rows(pad_ref, u, prev_blk, next_blk, pos, seqlen):
    tm = u.shape[0]
    s = SUBLANES_V7X
    pad_ref[s:s + tm, :] = u
    pad_ref[0:s, :] = prev_blk
    pad_ref[s + tm:2 * s + tm, :] = next_blk
    um1 = jnp.where(pos == 0, 0.0, pad_ref[s - 1:s - 1 + tm, :])
    up1 = jnp.where(pos == seqlen - 1, 0.0, pad_ref[s + 1:s + 1 + tm, :])
    return um1, up1


def _ab_out_kernel(o_ref, gb_ref, u_ref, up_ref, un_ref, cw_ref, wo_ref, wc_ref,
                   x_ref, mod_ref, g2_ref, wr_ref, br_ref,
                   xo_ref, h2_ref, rt_ref, pad_ref, *, cfg):
    pos, seqlen = _seq_pos(cfg, cfg.tm)
    u = u_ref[...]
    um1, up1 = _shifted_rows(pad_ref, u, up_ref[...], un_ref[...], pos, seqlen)
    cw = cw_ref[...]
    conv = gb_ref[...] * (um1 * cw[0:1] + u * cw[1:2] + up1 * cw[2:3])
    mix = (jnp.dot(o_ref[...], wo_ref[...], preferred_element_type=F32)
           + jnp.dot(conv.astype(BF16), wc_ref[...], preferred_element_type=F32))
    _residual_norm_route(x_ref[...], mix, mod_ref[0], g2_ref[...], wr_ref, br_ref, xo_ref, h2_ref, rt_ref)


def _ab_out(cfg, o, gb, u, conv_w, wo, wc, x, mod_l, g2, wr, br):
    tm = cfg.tm
    row = lambda w: pl.BlockSpec((tm, w), lambda i: (i, 0))
    prev, nxt = _halo_specs(cfg, CONV_WIDTH)
    ep_in, ep_out, ep_shape = _epilogue_specs(cfg)
    return pl.pallas_call(
        functools.partial(_ab_out_kernel, cfg=cfg),
        grid=(cfg.t_all // tm,),
        in_specs=[row(DIFF_WIDTH), row(CONV_WIDTH), row(CONV_WIDTH), prev, nxt, _const_spec(conv_w.shape),
                  _const_spec(wo.shape), _const_spec(wc.shape)] + ep_in,
        out_specs=ep_out,
        out_shape=ep_shape,
        scratch_shapes=[pltpu.VMEM((tm + 2 * SUBLANES_V7X, CONV_WIDTH), F32)],
        compiler_params=_cparams(("arbitrary",)),
        name="ab_out",
    )(o, gb, u, u, u, conv_w, wo, wc, x, mod_l, g2, wr, br)


def _ssd_in_kernel(x_ref, xp_ref, xn_ref, mod_ref, g_ref, w_ref, wdt_ref, cw_ref, cb_ref,
                   z_ref, xbc_ref, dt_ref, pad_ref, *, cfg):
    tm = cfg.tm
    s = SUBLANES_V7X
    m = mod_ref[0]
    g = g_ref[...]
    h = _norm_mod(x_ref[...], g, m[0:1], m[1:2])
    hb = h.astype(BF16)
    h_ext = jnp.concatenate([_norm_mod(xp_ref[...], g, m[0:1], m[1:2]), h,
                             _norm_mod(xn_ref[...], g, m[0:1], m[1:2])], axis=0).astype(BF16)
    pos, seqlen = _seq_pos(cfg, tm)
    cw = 512
    for j in range(SSD_D_INNER // cw):
        z_ref[:, j * cw:(j + 1) * cw] = jnp.dot(hb, w_ref[:, j * cw:(j + 1) * cw],
                                                preferred_element_type=F32).astype(z_ref.dtype)
    for j in range(SSD_CONV_CH // cw):
        c0 = SSD_D_INNER + j * cw
        pad = pad_ref.at[j]
        pad[...] = jnp.dot(h_ext, w_ref[:, c0:c0 + cw], preferred_element_type=F32)
        taps = cw_ref[:, j * cw:(j + 1) * cw]
        um1 = jnp.where(pos == 0, 0.0, pad[s - 1:s - 1 + tm, :])
        up1 = jnp.where(pos == seqlen - 1, 0.0, pad[s + 1:s + 1 + tm, :])
        conv = um1 * taps[0:1] + pad[s:s + tm, :] * taps[1:2] + up1 * taps[2:3] + cb_ref[:, j * cw:(j + 1) * cw]
        xbc_ref[:, j * cw:(j + 1) * cw] = _silu(conv).astype(xbc_ref.dtype)
    dt_ref[...] = jnp.dot(hb, wdt_ref[...], preferred_element_type=F32)


def _ssd_in(cfg, x, mod_l, g1, w_main, w_dt, conv_w, conv_b):
    t, d, tm = cfg.t_all, cfg.d_model, cfg.tm
    row = lambda w: pl.BlockSpec((tm, w), lambda i: (i, 0))
    prev, nxt = _halo_specs(cfg, d)
    return pl.pallas_call(
        functools.partial(_ssd_in_kernel, cfg=cfg),
        grid=(t // tm,),
        in_specs=[row(d), prev, nxt, _mod_spec(cfg), _const_spec((1, d)), _const_spec(w_main.shape),
                  _const_spec(w_dt.shape), _const_spec(conv_w.shape), _const_spec(conv_b.shape)],
        out_specs=[row(SSD_D_INNER), row(SSD_CONV_CH), row(LANES_V7X)],
        out_shape=[jax.ShapeDtypeStruct((t, SSD_D_INNER), BF16), jax.ShapeDtypeStruct((t, SSD_CONV_CH), BF16),
                   jax.ShapeDtypeStruct((t, LANES_V7X), F32)],
        scratch_shapes=[pltpu.VMEM((SSD_CONV_CH // 512, tm + 2 * SUBLANES_V7X, 512), F32)],
        compiler_params=_cparams(("arbitrary",)),
        name="ssd_in",
    )(x, x, x, mod_l, g1, w_main, w_dt, conv_w, conv_b)


def _split3(v):
    b1 = v.astype(BF16)
    r1 = v - b1.astype(F32)
    b2 = r1.astype(BF16)
    b3 = (r1 - b2.astype(F32)).astype(BF16)
    return jnp.concatenate([b1, b2, b3], axis=1)


def _softplus(v):
    return jnp.maximum(v, 0.0) + jnp.log(1.0 + jnp.exp(-jnp.abs(v)))


def _ssd_scan_kernel(xs_ref, b_ref, c_ref, dt_ref, dtt_ref, bias_ref, biast_ref, alog_ref, alogt_ref,
                     ebig_ref, ewide_ref, y_ref, state_ref):
    q = SSD_CHUNK
    d = pl.program_id(1)
    fwd = d == 0

    @pl.when(pl.program_id(2) == 0)
    def _():
        state_ref[...] = jnp.zeros_like(state_ref)

    dt = _softplus(dt_ref[0] + bias_ref[0])
    dtt = _softplus(dtt_ref[0] + biast_ref[0])
    a = dt * (-jnp.exp(alog_ref[0]))
    at = dtt * (-jnp.exp(alogt_ref[0]))

    ri = lax.broadcasted_iota(jnp.int32, (q, q), 0)
    ci = lax.broadcasted_iota(jnp.int32, (q, q), 1)
    keep = (ri - ci) * jnp.where(fwd, 1, -1) >= 0
    tri = jnp.where(keep, 1.0, 0.0)
    cs = jnp.dot(tri, a, precision=HIGHEST, preferred_element_type=F32)
    cst = lax.dot_general(at, tri, (((1,), (1,)), ((), ())), precision=HIGHEST,
                          preferred_element_type=F32)
    tot = jnp.sum(a, axis=0, keepdims=True)

    ebig = ebig_ref[...]
    ewide = ewide_ref[...]
    cs_big = jnp.dot(_split3(cs), ebig, preferred_element_type=F32)
    w_wide = jnp.dot(_split3(dt * jnp.exp(tot - cs)), ewide, preferred_element_type=F32)
    dec_wide = jnp.dot(_split3(jnp.broadcast_to(jnp.exp(tot), (SUBLANES_V7X, SSD_HEADS))), ewide,
                       preferred_element_type=F32)[0:1]

    lane = lax.broadcasted_iota(jnp.int32, (q, LANES_V7X), 1)
    lo = lane < SSD_HEAD_DIM
    hpg = SSD_HEADS // SSD_GROUPS
    gw = hpg * SSD_HEAD_DIM
    nt = (((1,), (1,)), ((), ()))
    tn = (((0,), (0,)), ((), ()))
    for g in range(SSD_GROUPS):
        bm = b_ref[:, g * SSD_STATE:(g + 1) * SSD_STATE]
        cm = c_ref[:, g * SSD_STATE:(g + 1) * SSD_STATE]
        cb = lax.dot_general(cm, bm, nt, preferred_element_type=F32)
        cmf = cm.astype(F32)
        for pr in range(hpg // 2):
            lhs, rhs = [], []
            c0 = g * gw + pr * LANES_V7X
            xs_pair = xs_ref[:, c0:c0 + LANES_V7X]
            st_pair = state_ref[:, c0:c0 + LANES_V7X].astype(BF16)
            zero = jnp.zeros_like(xs_pair)
            for k in range(2):
                h = g * hpg + pr * 2 + k
                csb = cs_big[:, h * q:(h + 1) * q]
                seg = csb - cst[h:h + 1, :]
                lmat = jnp.exp(jnp.where(keep, seg, -jnp.inf))
                lhs.append((cb * lmat * dtt[h:h + 1, :]).astype(BF16))
                lhs.append((cmf * jnp.exp(csb)).astype(BF16))
                sel = lo if k == 0 else jnp.logical_not(lo)
                rhs.append(jnp.where(sel, xs_pair, zero))
                rhs.append(jnp.where(sel, st_pair, zero))
            y = jnp.dot(jnp.concatenate(lhs, axis=1), jnp.concatenate(rhs, axis=0), preferred_element_type=F32)
            y_ref[0, :, c0:c0 + LANES_V7X] = y.astype(y_ref.dtype)
        sl = slice(g * gw, (g + 1) * gw)
        x2 = (xs_ref[:, sl].astype(F32) * w_wide[:, sl]).astype(BF16)
        upd = lax.dot_general(bm, x2, tn, preferred_element_type=F32)
        state_ref[:, sl] = state_ref[:, sl] * dec_wide[:, sl] + upd


def _ssd_scan(cfg, xbc_act, dt_raw, dt_bias, a_log):
    q = SSD_CHUNK
    n_cc, n_lc = cfg.n_ctx // q, cfg.n_lat // q
    ctx0 = cfg.t_lat // q
    h = SSD_HEADS
    dt2 = dt_raw[:, :2 * h].reshape(cfg.t_all, 2, h).transpose(1, 0, 2)
    dt2t = dt2.transpose(0, 2, 1)
    bias = dt_bias.reshape(2, 1, h)
    biast = dt_bias.reshape(2, h, 1)
    alog = a_log.reshape(2, 1, h)
    alogt = a_log.reshape(2, h, 1)
    head_of_row = jnp.tile(jnp.arange(h), 3)[:, None]
    ebig = (head_of_row == (jnp.arange(h * q) // q)[None, :]).astype(BF16)
    ewide = (head_of_row == (jnp.arange(SSD_D_INNER) // SSD_HEAD_DIM)[None, :]).astype(BF16)

    def blk(b, d, s):
        cs = jnp.where(d == 0, s, n_cc - 1 - s)
        ls = jnp.where(d == 0, s - n_cc, n_lc - 1 - (s - n_cc))
        return jnp.where(s < n_cc, ctx0 + b * n_cc + cs, b * n_lc + ls)

    gs = SSD_GROUPS * SSD_STATE
    col = lambda w, cblk: pl.BlockSpec((q, w), lambda b, d, s: (blk(b, d, s), cblk))
    per_dir = lambda shape: pl.BlockSpec((1,) + shape, lambda b, d, s: (d, 0, 0))
    return pl.pallas_call(
        _ssd_scan_kernel,
        grid=(cfg.batch, 2, n_cc + n_lc),
        in_specs=[col(SSD_D_INNER, 0), col(gs, SSD_D_INNER // gs), col(gs, SSD_D_INNER // gs + 1),
                  pl.BlockSpec((1, q, h), lambda b, d, s: (d, blk(b, d, s), 0)),
                  pl.BlockSpec((1, h, q), lambda b, d, s: (d, 0, blk(b, d, s))),
                  per_dir((1, h)), per_dir((h, 1)), per_dir((1, h)), per_dir((h, 1)),
                  _const_spec(ebig.shape), _const_spec(ewide.shape)],
        out_specs=pl.BlockSpec((1, q, SSD_D_INNER), lambda b, d, s: (d, blk(b, d, s), 0)),
        out_shape=jax.ShapeDtypeStruct((2, cfg.t_all, SSD_D_INNER), BF16),
        scratch_shapes=[pltpu.VMEM((SSD_STATE, SSD_D_INNER), F32)],
        compiler_params=_cparams(("arbitrary", "arbitrary", "arbitrary")),
        name="ssd_scan",
    )(xbc_act, xbc_act, xbc_act, dt2, dt2t, bias, biast, alog, alogt, ebig, ewide)


def _ssd_out_kernel(yf_ref, yb_ref, xs_ref, z_ref, dw_ref, ng_ref, wo_ref,
                    x_ref, mod_ref, g2_ref, wr_ref, br_ref, xo_ref, h2_ref, rt_ref):
    gw = SSD_D_INNER // SSD_GROUPS
    mix = None
    for g in range(SSD_GROUPS):
        sl = slice(g * gw, (g + 1) * gw)
        y = yf_ref[0, :, sl].astype(F32) + yb_ref[0, :, sl].astype(F32) + xs_ref[:, sl].astype(F32) * dw_ref[:, sl]
        y = y * _silu(z_ref[:, sl].astype(F32))
        y = (y * lax.rsqrt(jnp.mean(y * y, axis=-1, keepdims=True) + RMS_EPS)) * ng_ref[:, sl]
        part = jnp.dot(y.astype(BF16), wo_ref[sl, :], preferred_element_type=F32)
        mix = part if mix is None else mix + part
    _residual_norm_route(x_ref[...], mix, mod_ref[0], g2_ref[...], wr_ref, br_ref, xo_ref, h2_ref, rt_ref)


def _ssd_out(cfg, y2, xbc_act, z, d_wide, norm_g, wo, x, mod_l, g2, wr, br):
    tm = cfg.tm
    row = lambda w: pl.BlockSpec((tm, w), lambda i: (i, 0))
    ydir = lambda d: pl.BlockSpec((1, tm, SSD_D_INNER), lambda i: (d, i, 0))
    ep_in, ep_out, ep_shape = _epilogue_specs(cfg)
    return pl.pallas_call(
        _ssd_out_kernel,
        grid=(cfg.t_all // tm,),
        in_specs=[ydir(0), ydir(1), row(SSD_D_INNER), row(SSD_D_INNER), _const_spec((1, SSD_D_INNER)),
                  _const_spec((1, SSD_D_INNER)), _const_spec(wo.shape)] + ep_in,
        out_specs=ep_out,
        out_shape=ep_shape,
        compiler_params=_cparams(("arbitrary",)),
        name="ssd_out",
    )(y2, y2, xbc_act, z, d_wide, norm_g, wo, x, mod_l, g2, wr, br)


def _expert_kernel(eid_ref, nused_ref, xb_ref, wg_ref, wu_ref, wd_ref, y_ref, wg_s, wu_s, wd_s):
    i = pl.program_id(0)
    changed = jnp.logical_or(i == 0, eid_ref[i] != eid_ref[jnp.maximum(i - 1, 0)])

    @pl.when(jnp.logical_and(changed, i < nused_ref[0]))
    def _():
        wg_s[...] = wg_ref[0, 0].astype(BF16)
        wu_s[...] = wu_ref[0, 0].astype(BF16)
        wd_s[...] = wd_ref[0, 0].astype(BF16)

    @pl.when(i < nused_ref[0])
    def _():
        xb = xb_ref[...].astype(BF16)
        hg = jnp.dot(xb, wg_s[...], preferred_element_type=F32)
        hu = jnp.dot(xb, wu_s[...], preferred_element_type=F32)
        y_ref[...] = jnp.dot((_silu(hg) * hu).astype(BF16), wd_s[...], preferred_element_type=F32)

    @pl.when(i >= nused_ref[0])
    def _():
        y_ref[...] = jnp.zeros_like(y_ref)


def _experts(cfg, layer, block_eid, n_used, xb, w_gate, w_up, w_down):
    d, tile = cfg.d_model, cfg.moe_tile
    n_blocks = xb.shape[0] // tile
    grid_spec = pltpu.PrefetchScalarGridSpec(
        num_scalar_prefetch=2,
        grid=(n_blocks,),
        in_specs=[pl.BlockSpec((tile, d), lambda i, e, n: (i, 0)),
                  pl.BlockSpec((1, 1, d, MOE_HIDDEN), lambda i, e, n: (layer, e[i], 0, 0)),
                  pl.BlockSpec((1, 1, d, MOE_HIDDEN), lambda i, e, n: (layer, e[i], 0, 0)),
                  pl.BlockSpec((1, 1, MOE_HIDDEN, d), lambda i, e, n: (layer, e[i], 0, 0))],
        out_specs=pl.BlockSpec((tile, d), lambda i, e, n: (i, 0)),
        scratch_shapes=[pltpu.VMEM((d, MOE_HIDDEN), BF16), pltpu.VMEM((d, MOE_HIDDEN), BF16),
                        pltpu.VMEM((MOE_HIDDEN, d), BF16)],
    )
    return pl.pallas_call(
        _expert_kernel,
        grid_spec=grid_spec,
        out_shape=jax.ShapeDtypeStruct((n_blocks * tile, d), F32),
        compiler_params=_cparams(("arbitrary",)),
        name="moe_experts",
    )(block_eid, n_used, xb, w_gate, w_up, w_down)


def _dispatch_kernel(rt_ref, pos_ref, cnt_ref, carry_ref, start_ref, *, tile):
    p, i = pl.program_id(0), pl.program_id(1)
    tm = rt_ref.shape[0]
    rt = rt_ref[...]
    lane = lax.broadcasted_iota(jnp.int32, rt.shape, 1)
    o1 = jnp.where(lane == rt[:, 0:1].astype(jnp.int32), 1.0, 0.0)
    o2 = jnp.where(lane == rt[:, 1:2].astype(jnp.int32), 1.0, 0.0)
    cnt1 = jnp.sum(o1, axis=0, keepdims=True)
    cnt2 = jnp.sum(o2, axis=0, keepdims=True)

    @pl.when(jnp.logical_and(p == 0, i == 0))
    def _():
        cnt_ref[...] = jnp.zeros_like(cnt_ref)

    @pl.when(p == 0)
    def _():
        cnt_ref[...] += cnt1 + cnt2

    @pl.when(jnp.logical_and(p == 1, i == 0))
    def _():
        padded = jnp.floor((cnt_ref[...] + (tile - 1)) * (1.0 / tile)) * tile
        r = lax.broadcasted_iota(jnp.int32, (ROUTE_LANES, ROUTE_LANES), 0)
        c = lax.broadcasted_iota(jnp.int32, (ROUTE_LANES, ROUTE_LANES), 1)
        excl = jnp.where(r < c, 1.0, 0.0)
        start_ref[...] = jnp.dot(padded, excl, precision=HIGHEST, preferred_element_type=F32)
        carry_ref[...] = jnp.zeros_like(carry_ref)

    @pl.when(p == 1)
    def _():
        ri = lax.broadcasted_iota(jnp.int32, (tm, tm), 0)
        ci = lax.broadcasted_iota(jnp.int32, (tm, tm), 1)
        earlier = jnp.where(ci < ri, 1.0, 0.0).astype(BF16)
        p1 = jnp.dot(earlier, o1.astype(BF16), preferred_element_type=F32)
        p2 = jnp.dot(earlier, o2.astype(BF16), preferred_element_type=F32)
        base = start_ref[0:1] + carry_ref[0:1]
        pos1 = jnp.sum(o1 * (base + p1), axis=-1, keepdims=True)
        pos2 = jnp.sum(o2 * (base + cnt1 + p2), axis=-1, keepdims=True)
        pos_ref[...] = jnp.where(lane == 0, pos1, jnp.where(lane == 1, pos2, 0.0)).astype(jnp.int32)
        carry_ref[...] += cnt1 + cnt2


def _dispatch(cfg, route):
    tile, tm = cfg.moe_tile, cfg.tm
    t = cfg.t_all
    a_total = 2 * t
    pos, cnt = pl.pallas_call(
        functools.partial(_dispatch_kernel, tile=tile),
        grid=(2, t // tm),
        in_specs=[pl.BlockSpec((tm, ROUTE_LANES), lambda p, i: (i, 0))],
        out_specs=[pl.BlockSpec((tm, ROUTE_LANES), lambda p, i: (i * p, 0)),
                   pl.BlockSpec((SUBLANES_V7X, ROUTE_LANES), lambda p, i: (0, 0))],
        out_shape=[jax.ShapeDtypeStruct((t, ROUTE_LANES), jnp.int32),
                   jax.ShapeDtypeStruct((SUBLANES_V7X, ROUTE_LANES), F32)],
        scratch_shapes=[pltpu.VMEM((SUBLANES_V7X, ROUTE_LANES), F32), pltpu.VMEM((SUBLANES_V7X, ROUTE_LANES), F32)],
        compiler_params=_cparams(("arbitrary", "arbitrary")),
        name="moe_dispatch",
    )(route)
    counts = cnt[0, :MOE_EXPERTS].astype(jnp.int32)
    pend = jnp.cumsum((counts + tile - 1) // tile * tile)
    n_blocks = (a_total + MOE_EXPERTS * (tile - 1)) // tile
    block_eid = jnp.minimum(jnp.sum(pend[None, :] <= (jnp.arange(n_blocks, dtype=jnp.int32) * tile)[:, None], axis=1),
                            MOE_EXPERTS - 1).astype(jnp.int32)
    n_used = (pend[-1] // tile).astype(jnp.int32).reshape(1)
    pos2 = pos[:, 0:2]
    slot_tok = jnp.zeros((n_blocks * tile,), jnp.int32).at[pos2.reshape(-1)].set(
        jnp.arange(a_total, dtype=jnp.int32) // 2)
    return slot_tok, block_eid, n_used, pos2


def _combine_kernel(x_ref, y1_ref, y2_ref, rt_ref, mod_ref, g_ref, o_ref, *, final):
    rt = rt_ref[...]
    f = rt[:, 2:3] * y1_ref[...] + rt[:, 3:4] * y2_ref[...]
    xn = x_ref[...] + mod_ref[0][5:6] * f
    if final:
        xn = (xn * lax.rsqrt(jnp.mean(xn * xn, axis=-1, keepdims=True) + RMS_EPS)) * g_ref[...]
    o_ref[...] = xn


def _combine(cfg, x, y1, y2, route, mod_l, g, final):
    d, tm = cfg.d_model, cfg.tm
    row = lambda w: pl.BlockSpec((tm, w), lambda i: (i, 0))
    return pl.pallas_call(
        functools.partial(_combine_kernel, final=final),
        grid=(cfg.t_all // tm,),
        in_specs=[row(d), row(d), row(d), row(ROUTE_LANES), _mod_spec(cfg), _const_spec((1, d))],
        out_specs=row(d),
        out_shape=jax.ShapeDtypeStruct((cfg.t_all, d), F32),
        compiler_params=_cparams(("arbitrary",)),
        name="moe_combine",
    )(x, y1, y2, route, mod_l, g)


def _forward(cfg, x, c, ctx, c_ctx, ada_w, ada_b, norm1_g, norm2_g, ab_w_in, ab_w_out,
             diff_lq1, diff_lk1, diff_lq2, diff_lk2, diff_subln_g, bconv_w,
             ssd_w_in, ssd_conv_w, ssd_conv_b, ssd_A_log, ssd_dt_bias, ssd_D, ssd_norm_g, ssd_w_out,
             moe_wg, moe_bg, moe_we, moe_be, moe_w_gate, moe_w_up, moe_w_down, final_norm_g):
    d = cfg.d_model
    b = cfg.batch
    xa = jnp.concatenate([x.reshape(cfg.t_lat, d), ctx.reshape(cfg.t_ctx, d)], axis=0)
    c_all = jnp.zeros((SUBLANES_V7X, d), F32).at[:b].set(c).at[b].set(c_ctx)
    mod = _adaln(cfg, c_all, ada_w, ada_b)
    tables = _rope_tables(cfg)

    for layer in range(cfg.depth):
        i = layer // 2
        mod_l = mod[layer]
        g1 = norm1_g[layer].reshape(1, d)
        g2 = norm2_g[layer].reshape(1, d)
        pad = ROUTE_LANES - MOE_GROUPS - MOE_EXPERTS
        wr = jnp.concatenate([moe_wg[layer], moe_we[layer], jnp.zeros((d, pad), F32)], axis=1)
        br = jnp.concatenate([moe_bg[layer], moe_be[layer], jnp.zeros((pad,), F32)]).reshape(1, ROUTE_LANES)
        if layer % 2 == 0:
            lam_init = 0.8 - 0.6 * math.exp(-0.3 * layer)
            q, k, v, gb, u = _ab_in(cfg, xa, mod_l, g1, ab_w_in[i].astype(BF16), tables)
            lvec = jnp.stack([diff_lq1[i], diff_lk1[i], diff_lq2[i], diff_lk2[i]])
            o = _attention(cfg, q, k, v, lvec, diff_subln_g[i].reshape(1, DIFF_V_DIM), lam_init)
            w_out = ab_w_out[i].astype(BF16)
            xa, h2, route = _ab_out(cfg, o, gb, u, bconv_w[i], w_out[:DIFF_WIDTH], w_out[DIFF_WIDTH:],
                                    xa, mod_l, g2, wr, br)
        else:
            w_in = ssd_w_in[i].astype(BF16)
            n_main = SSD_D_INNER + SSD_CONV_CH
            w_dt = jnp.pad(w_in[:, n_main:], ((0, 0), (0, LANES_V7X - 2 * SSD_HEADS)))
            z, xbc_act, dt_raw = _ssd_in(cfg, xa, mod_l, g1, w_in[:, :n_main], w_dt, ssd_conv_w[i],
                                         ssd_conv_b[i].reshape(1, SSD_CONV_CH))
            y2 = _ssd_scan(cfg, xbc_act, dt_raw, ssd_dt_bias[i], ssd_A_log[i])
            d_wide = jnp.repeat(ssd_D[i], SSD_HEAD_DIM).reshape(1, SSD_D_INNER)
            xa, h2, route = _ssd_out(cfg, y2, xbc_act, z, d_wide, ssd_norm_g[i].reshape(1, SSD_D_INNER),
                                     ssd_w_out[i].astype(BF16), xa, mod_l, g2, wr, br)
        slot_tok, block_eid, n_used, pos = _dispatch(cfg, route)
        yb = _experts(cfg, layer, block_eid, n_used, h2[slot_tok], moe_w_gate, moe_w_up, moe_w_down)
        last = layer == cfg.depth - 1
        xa = _combine(cfg, xa, yb[pos[:, 0]], yb[pos[:, 1]], route, mod_l, final_norm_g.reshape(1, d), last)
    return xa[:cfg.t_lat].reshape(b, cfg.n_lat, d)


def kernel(x, c, ctx, c_ctx, ada_w, ada_b, norm1_g, norm2_g, ab_w_in, ab_w_out, diff_lq1, diff_lk1, diff_lq2, diff_lk2, diff_subln_g, bconv_w, ssd_w_in, ssd_conv_w, ssd_conv_b, ssd_A_log, ssd_dt_bias, ssd_D, ssd_norm_g, ssd_w_out, moe_wg, moe_bg, moe_we, moe_be, moe_w_gate, moe_w_up, moe_w_down, final_norm_g):
    cfg = Cfg(batch=x.shape[0], n_lat=x.shape[1], n_ctx=ctx.shape[1], d_model=x.shape[2], depth=ada_w.shape[0],
              tm=256, tq=256, moe_tile=256, kchunk=512)
    return _forward(cfg, x, c, ctx, c_ctx, ada_w, ada_b, norm1_g, norm2_g, ab_w_in, ab_w_out,
                    diff_lq1, diff_lk1, diff_lq2, diff_lk2, diff_subln_g, bconv_w,
                    ssd_w_in, ssd_conv_w, ssd_conv_b, ssd_A_log, ssd_dt_bias, ssd_D, ssd_norm_g, ssd_w_out,
                    moe_wg, moe_bg, moe_we, moe_be, moe_w_gate, moe_w_up, moe_w_down, final_norm_g)
```

```python
import functools
import math
from typing import NamedTuple

import jax
import jax.numpy as jnp
from jax import lax
from jax.experimental import pallas as pl
from jax.experimental.pallas import tpu as pltpu

F32 = jnp.float32
BF16 = jnp.bfloat16
HIGHEST = lax.Precision.HIGHEST

LANES_V7X = 128
SUBLANES_V7X = 8
VMEM_LIMIT_BYTES_V7X = 56 * 1024 * 1024

RMS_EPS = 1e-6
GRID_W = 64
N_DIFF_HEADS = 4
DIFF_QK_DIM = 64
DIFF_V_DIM = 128
DIFF_WIDTH = 512
CONV_WIDTH = 512
ROPE_BASE = 10000.0
SSD_D_INNER = 2048
SSD_HEAD_DIM = 64
SSD_HEADS = 32
SSD_GROUPS = 4
SSD_STATE = 128
SSD_CHUNK = 128
SSD_CONV_CH = SSD_D_INNER + 2 * SSD_GROUPS * SSD_STATE
MOE_GROUPS = 4
MOE_EXPERTS_PER_GROUP = 8
MOE_EXPERTS = 32
MOE_HIDDEN = 512
ROUTE_LANES = LANES_V7X


class Cfg(NamedTuple):
    batch: int
    n_lat: int
    n_ctx: int
    d_model: int
    depth: int
    tm: int
    tq: int
    moe_tile: int
    attn_subtiles: int

    @property
    def t_lat(self):
        return self.batch * self.n_lat

    @property
    def t_ctx(self):
        return self.batch * self.n_ctx

    @property
    def t_all(self):
        return self.t_lat + self.t_ctx


def _cparams(sem):
    return pltpu.CompilerParams(dimension_semantics=sem, vmem_limit_bytes=VMEM_LIMIT_BYTES_V7X)


def _silu(v):
    return v * (1.0 / (1.0 + jnp.exp(-v)))


def _const_spec(shape):
    nd = len(shape)
    return pl.BlockSpec(shape, lambda *_: (0,) * nd)


def _mod_spec(cfg):
    return pl.BlockSpec((1, 6, cfg.d_model),
                        lambda i: (jnp.minimum((i * cfg.tm) // cfg.n_lat, cfg.batch), 0, 0))


def _seq_pos(cfg, tm):
    r0 = pl.program_id(0) * tm
    row = r0 + lax.broadcasted_iota(jnp.int32, (tm, 1), 0)
    seqlen = jnp.where(r0 >= cfg.t_lat, cfg.n_ctx, cfg.n_lat)
    return row & (seqlen - 1), seqlen


def _norm_mod(x, g, shift, scale):
    ms = jnp.mean(x * x, axis=-1, keepdims=True)
    return (x * lax.rsqrt(ms + RMS_EPS) * g) * (1.0 + scale) + shift


def _adaln_kernel(c_ref, w_ref, b_ref, o_ref):
    sc = _silu(c_ref[...])
    o_ref[0] = jnp.dot(sc, w_ref[0], precision=HIGHEST, preferred_element_type=F32) + b_ref[0]


def _adaln(cfg, c_all, ada_w, ada_b):
    d = cfg.d_model
    out = pl.pallas_call(
        _adaln_kernel,
        grid=(cfg.depth, 6),
        in_specs=[_const_spec((SUBLANES_V7X, d)),
                  pl.BlockSpec((1, d, d), lambda l, j: (l, 0, j)),
                  pl.BlockSpec((1, 1, d), lambda l, j: (l, 0, j))],
        out_specs=pl.BlockSpec((1, SUBLANES_V7X, d), lambda l, j: (l, 0, j)),
        out_shape=jax.ShapeDtypeStruct((cfg.depth, SUBLANES_V7X, 6 * d), F32),
        compiler_params=_cparams(("arbitrary", "arbitrary")),
        name="adaln",
    )(c_all, ada_w, ada_b.reshape(cfg.depth, 1, 6 * d))
    return out.reshape(cfg.depth, SUBLANES_V7X, 6, d)


def _rope_tables(cfg):
    n = cfg.n_lat
    rows = n // GRID_W
    row = jnp.broadcast_to(jnp.arange(rows, dtype=F32)[:, None], (rows, GRID_W)).reshape(n)
    col = jnp.broadcast_to(jnp.arange(GRID_W, dtype=F32)[None, :], (rows, GRID_W)).reshape(n)
    axis_dim = DIFF_QK_DIM // 2
    inv_freq = ROPE_BASE ** (-jnp.arange(0, axis_dim, 2, dtype=F32) / axis_dim)
    ang_r = row[:, None] * inv_freq
    ang_c = col[:, None] * inv_freq
    zeros = jnp.zeros_like(ang_r)
    cos64 = jnp.concatenate([jnp.cos(ang_r), jnp.cos(ang_r), jnp.cos(ang_c), jnp.cos(ang_c)], axis=1)
    sa64 = jnp.concatenate([zeros, jnp.sin(ang_r), zeros, jnp.sin(ang_c)], axis=1)
    sb64 = jnp.concatenate([-jnp.sin(ang_r), zeros, -jnp.sin(ang_c), zeros], axis=1)
    ident = cfg.tm
    cos = jnp.concatenate([jnp.tile(cos64, (1, 2)), jnp.ones((ident, LANES_V7X), F32)], axis=0)
    sa = jnp.concatenate([jnp.tile(sa64, (1, 2)), jnp.zeros((ident, LANES_V7X), F32)], axis=0)
    sb = jnp.concatenate([jnp.tile(sb64, (1, 2)), jnp.zeros((ident, LANES_V7X), F32)], axis=0)
    return cos, sa, sb


def _ab_in_kernel(x_ref, mod_ref, g_ref, w_ref, cos_ref, sa_ref, sb_ref,
                  q_ref, k_ref, v_ref, gb_ref, u_ref):
    m = mod_ref[0]
    hb = _norm_mod(x_ref[...], g_ref[...], m[0:1], m[1:2]).astype(BF16)
    cos, sa, sb = cos_ref[...], sa_ref[...], sb_ref[...]

    def proj(j):
        return jnp.dot(hb, w_ref[:, j * DIFF_WIDTH:(j + 1) * DIFF_WIDTH], preferred_element_type=F32)

    def rope(t, out_ref, scale):
        for gi in range(DIFF_WIDTH // LANES_V7X):
            xg = t[:, gi * LANES_V7X:(gi + 1) * LANES_V7X]
            r = xg * cos + pltpu.roll(xg, 16, 1) * sa + pltpu.roll(xg, LANES_V7X - 16, 1) * sb
            out_ref[:, gi * LANES_V7X:(gi + 1) * LANES_V7X] = (r * scale).astype(out_ref.dtype)

    rope(proj(0), q_ref, math.log2(math.e) * DIFF_QK_DIM ** -0.5)
    rope(proj(1), k_ref, 1.0)
    v_ref[...] = proj(2).astype(v_ref.dtype)
    gb_ref[...] = proj(3)
    u_ref[...] = proj(4) * proj(5)


def _ab_in(cfg, x, mod_l, g1, w_in_bf16, tables):
    t, d, tm = cfg.t_all, cfg.d_model, cfg.tm
    n_lat_tiles, per_seq = cfg.t_lat // tm, cfg.n_lat // tm
    tab_spec = pl.BlockSpec((tm, LANES_V7X), lambda i: (jnp.where(i < n_lat_tiles, i % per_seq, per_seq), 0))
    row = lambda w: pl.BlockSpec((tm, w), lambda i: (i, 0))
    return pl.pallas_call(
        _ab_in_kernel,
        grid=(t // tm,),
        in_specs=[row(d), _mod_spec(cfg), _const_spec((1, d)), _const_spec(w_in_bf16.shape),
                  tab_spec, tab_spec, tab_spec],
        out_specs=[row(DIFF_WIDTH)] * 5,
        out_shape=[jax.ShapeDtypeStruct((t, DIFF_WIDTH), BF16)] * 3
        + [jax.ShapeDtypeStruct((t, CONV_WIDTH), F32)] * 2,
        compiler_params=_cparams(("arbitrary",)),
        name="ab_in",
    )(x, mod_l, g1, w_in_bf16, *tables)


def _attn_body(lvec_ref, q_ref, kc_ref, kl_ref, vc_ref, vl_ref, g_ref, o_ref, *, tq, lam_init):
    lv = lvec_ref[...]
    lam = (jnp.exp(jnp.sum(lv[0:1] * lv[1:2], axis=-1, keepdims=True))
           - jnp.exp(jnp.sum(lv[2:3] * lv[3:4], axis=-1, keepdims=True)) + lam_init)
    nt = (((1,), (1,)), ((), ()))
    half = DIFF_QK_DIM
    kv = [(kc_ref, vc_ref)] + ([(kl_ref, vl_ref)] if kl_ref is not None else [])
    for j in range(q_ref.shape[0] // tq):
        q = q_ref[j * tq:(j + 1) * tq, :].astype(F32)
        lane = lax.broadcasted_iota(jnp.int32, q.shape, 1)
        q2 = jnp.concatenate([jnp.where(lane < half, q, 0.0), jnp.where(lane >= half, q, 0.0)],
                             axis=0).astype(BF16)
        s = [lax.dot_general(q2, k_ref[...], nt, preferred_element_type=F32) for k_ref, _ in kv]
        mx = functools.reduce(jnp.maximum, [jnp.max(x, axis=-1, keepdims=True) for x in s])
        p = [jnp.exp2(x - mx) for x in s]
        r = 1.0 / functools.reduce(jnp.add, [jnp.sum(x, axis=-1, keepdims=True) for x in p])
        c0, c1 = r[:tq], lam * r[tq:]
        pv = functools.reduce(jnp.add, [
            jnp.dot((x[:tq] * c0 - x[tq:] * c1).astype(BF16), v_ref[...], preferred_element_type=F32)
            for x, (_, v_ref) in zip(p, kv)])
        o = pv * lax.rsqrt(jnp.mean(pv * pv, axis=-1, keepdims=True) + RMS_EPS)
        o_ref[j * tq:(j + 1) * tq, :] = ((o * g_ref[...]) * (1.0 - lam_init)).astype(o_ref.dtype)


def _attn_lat_kernel(lvec_ref, q_ref, kc_ref, kl_ref, vc_ref, vl_ref, g_ref, o_ref, **kw):
    _attn_body(lvec_ref, q_ref, kc_ref, kl_ref, vc_ref, vl_ref, g_ref, o_ref, **kw)


def _attn_ctx_kernel(lvec_ref, q_ref, kc_ref, vc_ref, g_ref, o_ref, **kw):
    _attn_body(lvec_ref, q_ref, kc_ref, None, vc_ref, None, g_ref, o_ref, **kw)


def _attention(cfg, q, k, v, lvec, subln_g, lam_init):
    tq, nsub = cfg.tq, cfg.attn_subtiles
    tstep = tq * nsub
    nqb = cfg.n_lat // tstep
    ctx_blk0 = cfg.t_lat // cfg.n_ctx
    hw = DIFF_V_DIM
    ctx_spec = pl.BlockSpec((cfg.n_ctx, hw), lambda b, h, *_: (ctx_blk0 + b, h))
    lat_spec = pl.BlockSpec((cfg.n_lat, hw), lambda b, h, *_: (b, h))
    q_spec = pl.BlockSpec((tstep, hw), lambda b, h, i: (b * nqb + i, h))
    kw = dict(tq=tq, lam_init=lam_init)
    o_lat = pl.pallas_call(
        functools.partial(_attn_lat_kernel, **kw),
        grid=(cfg.batch, N_DIFF_HEADS, nqb),
        in_specs=[_const_spec(lvec.shape), q_spec, ctx_spec, lat_spec, ctx_spec, lat_spec, _const_spec((1, hw))],
        out_specs=q_spec,
        out_shape=jax.ShapeDtypeStruct((cfg.t_lat, DIFF_WIDTH), BF16),
        compiler_params=_cparams(("arbitrary", "arbitrary", "arbitrary")),
        name="diff_attn",
    )(lvec, q, k, k, v, v, subln_g)
    tqc = min(tq, cfg.n_ctx)
    o_ctx = pl.pallas_call(
        functools.partial(_attn_ctx_kernel, tq=tqc, lam_init=lam_init),
        grid=(cfg.batch, N_DIFF_HEADS),
        in_specs=[_const_spec(lvec.shape), ctx_spec, ctx_spec, ctx_spec, _const_spec((1, hw))],
        out_specs=pl.BlockSpec((cfg.n_ctx, hw), lambda b, h: (b, h)),
        out_shape=jax.ShapeDtypeStruct((cfg.t_ctx, DIFF_WIDTH), BF16),
        compiler_params=_cparams(("arbitrary", "arbitrary")),
        name="diff_attn_ctx",
    )(lvec, q, k, v, subln_g)
    return jnp.concatenate([o_lat, o_ctx], axis=0)


def _route_from_logits(lg):
    lane = lax.broadcasted_iota(jnp.int32, lg.shape, 1)
    neg = -jnp.inf
    big = jnp.int32(ROUTE_LANES)
    gl = jnp.where(lane < MOE_GROUPS, lg, neg)
    gmax = jnp.max(gl, axis=-1, keepdims=True)
    gidx = jnp.min(jnp.where(gl == gmax, lane, big), axis=-1, keepdims=True)
    gw = 1.0 / jnp.sum(jnp.exp(gl - gmax), axis=-1, keepdims=True)
    lo = MOE_GROUPS + gidx * MOE_EXPERTS_PER_GROUP
    el = jnp.where((lane >= lo) & (lane < lo + MOE_EXPERTS_PER_GROUP), lg, neg)
    m1 = jnp.max(el, axis=-1, keepdims=True)
    i1 = jnp.min(jnp.where(el == m1, lane, big), axis=-1, keepdims=True)
    el2 = jnp.where(lane == i1, neg, el)
    m2 = jnp.max(el2, axis=-1, keepdims=True)
    i2 = jnp.min(jnp.where(el2 == m2, lane, big), axis=-1, keepdims=True)
    w1 = gw / (1.0 + jnp.exp(m2 - m1))
    w2 = gw - w1
    e1 = (i1 - MOE_GROUPS).astype(F32)
    e2 = (i2 - MOE_GROUPS).astype(F32)
    return jnp.where(lane == 0, e1, jnp.where(lane == 1, e2, jnp.where(lane == 2, w1, jnp.where(lane == 3, w2, 0.0))))


def _residual_norm_route(x, mix, m, g2, wr_ref, br_ref, xo_ref, h2_ref, rt_ref):
    xn = x + m[2:3] * mix
    xo_ref[...] = xn
    h2 = _norm_mod(xn, g2, m[3:4], m[4:5])
    h2_ref[...] = h2.astype(h2_ref.dtype)
    lg = jnp.dot(h2, wr_ref[...], precision=HIGHEST, preferred_element_type=F32) + br_ref[...]
    rt_ref[...] = _route_from_logits(lg)


def _epilogue_specs(cfg):
    d, tm = cfg.d_model, cfg.tm
    row = lambda w: pl.BlockSpec((tm, w), lambda i: (i, 0))
    in_specs = [row(d), _mod_spec(cfg), _const_spec((1, d)), _const_spec((d, ROUTE_LANES)),
                _const_spec((1, ROUTE_LANES))]
    out_specs = [row(d), row(d), row(ROUTE_LANES)]
    out_shape = [jax.ShapeDtypeStruct((cfg.t_all, d), F32), jax.ShapeDtypeStruct((cfg.t_all, d), F32),
                 jax.ShapeDtypeStruct((cfg.t_all, ROUTE_LANES), F32)]
    return in_specs, out_specs, out_shape


def _halo_specs(cfg, width, col_block=0):
    per = cfg.tm // SUBLANES_V7X
    last = cfg.t_all // SUBLANES_V7X - 1
    prev = pl.BlockSpec((SUBLANES_V7X, width), lambda i, *_: (jnp.maximum(i * per - 1, 0), col_block))
    nxt = pl.BlockSpec((SUBLANES_V7X, width), lambda i, *_: (jnp.minimum((i + 1) * per, last), col_block))
    return prev, nxt


def _shifted_rows(pad_ref, u, prev_blk, next_blk, pos, seqlen):
    tm = u.shape[0]
    s = SUBLANES_V7X
    pad_ref[s:s + tm, :] = u
    pad_ref[0:s, :] = prev_blk
    pad_ref[s + tm:2 * s + tm, :] = next_blk
    um1 = jnp.where(pos == 0, 0.0, pad_ref[s - 1:s - 1 + tm, :])
    up1 = jnp.where(pos == seqlen - 1, 0.0, pad_ref[s + 1:s + 1 + tm, :])
    return um1, up1


def _ab_out_kernel(o_ref, gb_ref, u_ref, up_ref, un_ref, cw_ref, wo_ref, wc_ref,
                   x_ref, mod_ref, g2_ref, wr_ref, br_ref,
                   xo_ref, h2_ref, rt_ref, pad_ref, *, cfg):
    pos, seqlen = _seq_pos(cfg, cfg.tm)
    u = u_ref[...]
    um1, up1 = _shifted_rows(pad_ref, u, up_ref[...], un_ref[...], pos, seqlen)
    cw = cw_ref[...]
    conv = gb_ref[...] * (um1 * cw[0:1] + u * cw[1:2] + up1 * cw[2:3])
    mix = (jnp.dot(o_ref[...], wo_ref[...], preferred_element_type=F32)
           + jnp.dot(conv.astype(BF16), wc_ref[...], preferred_element_type=F32))
    _residual_norm_route(x_ref[...], mix, mod_ref[0], g2_ref[...], wr_ref, br_ref, xo_ref, h2_ref, rt_ref)


def _ab_out(cfg, o, gb, u, conv_w, wo, wc, x, mod_l, g2, wr, br):
    tm = cfg.tm
    row = lambda w: pl.BlockSpec((tm, w), lambda i: (i, 0))
    prev, nxt = _halo_specs(cfg, CONV_WIDTH)
    ep_in, ep_out, ep_shape = _epilogue_specs(cfg)
    return pl.pallas_call(
        functools.partial(_ab_out_kernel, cfg=cfg),
        grid=(cfg.t_all // tm,),
        in_specs=[row(DIFF_WIDTH), row(CONV_WIDTH), row(CONV_WIDTH), prev, nxt, _const_spec(conv_w.shape),
                  _const_spec(wo.shape), _const_spec(wc.shape)] + ep_in,
        out_specs=ep_out,
        out_shape=ep_shape,
        scratch_shapes=[pltpu.VMEM((tm + 2 * SUBLANES_V7X, CONV_WIDTH), F32)],
        compiler_params=_cparams(("arbitrary",)),
        name="ab_out",
    )(o, gb, u, u, u, conv_w, wo, wc, x, mod_l, g2, wr, br)


def _ssd_in_kernel(x_ref, xp_ref, xn_ref, mod_ref, g_ref, w_ref, wdt_ref, cw_ref, cb_ref,
                   z_ref, xbc_ref, dt_ref, pad_ref, *, cfg):
    tm = cfg.tm
    s = SUBLANES_V7X
    m = mod_ref[0]
    g = g_ref[...]
    h = _norm_mod(x_ref[...], g, m[0:1], m[1:2])
    hb = h.astype(BF16)
    h_ext = jnp.concatenate([_norm_mod(xp_ref[...], g, m[0:1], m[1:2]), h,
                             _norm_mod(xn_ref[...], g, m[0:1], m[1:2])], axis=0).astype(BF16)
    pos, seqlen = _seq_pos(cfg, tm)
    cw = 512
    for j in range(SSD_D_INNER // cw):
        z_ref[:, j * cw:(j + 1) * cw] = jnp.dot(hb, w_ref[:, j * cw:(j + 1) * cw],
                                                preferred_element_type=F32).astype(z_ref.dtype)
    for j in range(SSD_CONV_CH // cw):
        c0 = SSD_D_INNER + j * cw
        pad = pad_ref.at[j]
        pad[...] = jnp.dot(h_ext, w_ref[:, c0:c0 + cw], preferred_element_type=F32)
        taps = cw_ref[:, j * cw:(j + 1) * cw]
        um1 = jnp.where(pos == 0, 0.0, pad[s - 1:s - 1 + tm, :])
        up1 = jnp.where(pos == seqlen - 1, 0.0, pad[s + 1:s + 1 + tm, :])
        conv = um1 * taps[0:1] + pad[s:s + tm, :] * taps[1:2] + up1 * taps[2:3] + cb_ref[:, j * cw:(j + 1) * cw]
        xbc_ref[:, j * cw:(j + 1) * cw] = _silu(conv).astype(xbc_ref.dtype)
    dt_ref[...] = jnp.dot(hb, wdt_ref[...], preferred_element_type=F32)


def _ssd_in(cfg, x, mod_l, g1, w_main, w_dt, conv_w, conv_b):
    t, d, tm = cfg.t_all, cfg.d_model, cfg.tm
    row = lambda w: pl.BlockSpec((tm, w), lambda i: (i, 0))
    prev, nxt = _halo_specs(cfg, d)
    return pl.pallas_call(
        functools.partial(_ssd_in_kernel, cfg=cfg),
        grid=(t // tm,),
        in_specs=[row(d), prev, nxt, _mod_spec(cfg), _const_spec((1, d)), _const_spec(w_main.shape),
                  _const_spec(w_dt.shape), _const_spec(conv_w.shape), _const_spec(conv_b.shape)],
        out_specs=[row(SSD_D_INNER), row(SSD_CONV_CH), row(LANES_V7X)],
        out_shape=[jax.ShapeDtypeStruct((t, SSD_D_INNER), BF16), jax.ShapeDtypeStruct((t, SSD_CONV_CH), BF16),
                   jax.ShapeDtypeStruct((t, LANES_V7X), F32)],
        scratch_shapes=[pltpu.VMEM((SSD_CONV_CH // 512, tm + 2 * SUBLANES_V7X, 512), F32)],
        compiler_params=_cparams(("arbitrary",)),
        name="ssd_in",
    )(x, x, x, mod_l, g1, w_main, w_dt, conv_w, conv_b)


def _split3(v):
    b1 = v.astype(BF16)
    r1 = v - b1.astype(F32)
    b2 = r1.astype(BF16)
    b3 = (r1 - b2.astype(F32)).astype(BF16)
    return jnp.concatenate([b1, b2, b3], axis=1)


def _softplus(v):
    return jnp.maximum(v, 0.0) + jnp.log(1.0 + jnp.exp(-jnp.abs(v)))


def _ssd_scan_kernel(xs_ref, b_ref, c_ref, dt_ref, dtt_ref, bias_ref, biast_ref, alog_ref, alogt_ref,
                     ebig_ref, ewide_ref, y_ref, state_ref):
    q = SSD_CHUNK
    d = pl.program_id(1)
    fwd = d == 0

    @pl.when(pl.program_id(2) == 0)
    def _():
        state_ref[...] = jnp.zeros_like(state_ref)

    dt = _softplus(dt_ref[0] + bias_ref[0])
    dtt = _softplus(dtt_ref[0] + biast_ref[0])
    a = dt * (-jnp.exp(alog_ref[0]))
    at = dtt * (-jnp.exp(alogt_ref[0]))

    ri = lax.broadcasted_iota(jnp.int32, (q, q), 0)
    ci = lax.broadcasted_iota(jnp.int32, (q, q), 1)
    keep = (ri - ci) * jnp.where(fwd, 1, -1) >= 0
    tri = jnp.where(keep, 1.0, 0.0)
    cs = jnp.dot(tri, a, precision=HIGHEST, preferred_element_type=F32)
    cst = lax.dot_general(at, tri, (((1,), (1,)), ((), ())), precision=HIGHEST,
                          preferred_element_type=F32)
    tot = jnp.sum(a, axis=0, keepdims=True)

    ebig = ebig_ref[...]
    ewide = ewide_ref[...]
    cs_big = jnp.dot(_split3(cs), ebig, preferred_element_type=F32)
    w_wide = jnp.dot(_split3(dt * jnp.exp(tot - cs)), ewide, preferred_element_type=F32)
    dec_wide = jnp.dot(_split3(jnp.broadcast_to(jnp.exp(tot), (SUBLANES_V7X, SSD_HEADS))), ewide,
                       preferred_element_type=F32)[0:1]

    lane = lax.broadcasted_iota(jnp.int32, (q, LANES_V7X), 1)
    lo = lane < SSD_HEAD_DIM
    hpg = SSD_HEADS // SSD_GROUPS
    gw = hpg * SSD_HEAD_DIM
    nt = (((1,), (1,)), ((), ()))
    tn = (((0,), (0,)), ((), ()))
    for g in range(SSD_GROUPS):
        bm = b_ref[:, g * SSD_STATE:(g + 1) * SSD_STATE]
        cm = c_ref[:, g * SSD_STATE:(g + 1) * SSD_STATE]
        cb = lax.dot_general(cm, bm, nt, preferred_element_type=F32)
        cmf = cm.astype(F32)
        for pr in range(hpg // 2):
            lhs, rhs = [], []
            c0 = g * gw + pr * LANES_V7X
            xs_pair = xs_ref[:, c0:c0 + LANES_V7X]
            st_pair = state_ref[:, c0:c0 + LANES_V7X].astype(BF16)
            zero = jnp.zeros_like(xs_pair)
            for k in range(2):
                h = g * hpg + pr * 2 + k
                csb = cs_big[:, h * q:(h + 1) * q]
                seg = csb - cst[h:h + 1, :]
                lmat = jnp.exp(jnp.where(keep, seg, -jnp.inf))
                lhs.append((cb * lmat * dtt[h:h + 1, :]).astype(BF16))
                lhs.append((cmf * jnp.exp(csb)).astype(BF16))
                sel = lo if k == 0 else jnp.logical_not(lo)
                rhs.append(jnp.where(sel, xs_pair, zero))
                rhs.append(jnp.where(sel, st_pair, zero))
            y = jnp.dot(jnp.concatenate(lhs, axis=1), jnp.concatenate(rhs, axis=0), preferred_element_type=F32)
            y_ref[0, :, c0:c0 + LANES_V7X] = y.astype(y_ref.dtype)
        sl = slice(g * gw, (g + 1) * gw)
        x2 = (xs_ref[:, sl].astype(F32) * w_wide[:, sl]).astype(BF16)
        upd = lax.dot_general(bm, x2, tn, preferred_element_type=F32)
        state_ref[:, sl] = state_ref[:, sl] * dec_wide[:, sl] + upd


def _ssd_scan(cfg, xbc_act, dt_raw, dt_bias, a_log):
    q = SSD_CHUNK
    n_cc, n_lc = cfg.n_ctx // q, cfg.n_lat // q
    ctx0 = cfg.t_lat // q
    h = SSD_HEADS
    dt2 = dt_raw[:, :2 * h].reshape(cfg.t_all, 2, h).transpose(1, 0, 2)
    dt2t = dt2.transpose(0, 2, 1)
    bias = dt_bias.reshape(2, 1, h)
    biast = dt_bias.reshape(2, h, 1)
    alog = a_log.reshape(2, 1, h)
    alogt = a_log.reshape(2, h, 1)
    head_of_row = jnp.tile(jnp.arange(h), 3)[:, None]
    ebig = (head_of_row == (jnp.arange(h * q) // q)[None, :]).astype(BF16)
    ewide = (head_of_row == (jnp.arange(SSD_D_INNER) // SSD_HEAD_DIM)[None, :]).astype(BF16)

    def blk(b, d, s):
        cs = jnp.where(d == 0, s, n_cc - 1 - s)
        ls = jnp.where(d == 0, s - n_cc, n_lc - 1 - (s - n_cc))
        return jnp.where(s < n_cc, ctx0 + b * n_cc + cs, b * n_lc + ls)

    gs = SSD_GROUPS * SSD_STATE
    col = lambda w, cblk: pl.BlockSpec((q, w), lambda b, d, s: (blk(b, d, s), cblk))
    per_dir = lambda shape: pl.BlockSpec((1,) + shape, lambda b, d, s: (d, 0, 0))
    return pl.pallas_call(
        _ssd_scan_kernel,
        grid=(cfg.batch, 2, n_cc + n_lc),
        in_specs=[col(SSD_D_INNER, 0), col(gs, SSD_D_INNER // gs), col(gs, SSD_D_INNER // gs + 1),
                  pl.BlockSpec((1, q, h), lambda b, d, s: (d, blk(b, d, s), 0)),
                  pl.BlockSpec((1, h, q), lambda b, d, s: (d, 0, blk(b, d, s))),
                  per_dir((1, h)), per_dir((h, 1)), per_dir((1, h)), per_dir((h, 1)),
                  _const_spec(ebig.shape), _const_spec(ewide.shape)],
        out_specs=pl.BlockSpec((1, q, SSD_D_INNER), lambda b, d, s: (d, blk(b, d, s), 0)),
        out_shape=jax.ShapeDtypeStruct((2, cfg.t_all, SSD_D_INNER), BF16),
        scratch_shapes=[pltpu.VMEM((SSD_STATE, SSD_D_INNER), F32)],
        compiler_params=_cparams(("arbitrary", "arbitrary", "arbitrary")),
        name="ssd_scan",
    )(xbc_act, xbc_act, xbc_act, dt2, dt2t, bias, biast, alog, alogt, ebig, ewide)


def _ssd_out_kernel(yf_ref, yb_ref, xs_ref, z_ref, dw_ref, ng_ref, wo_ref,
                    x_ref, mod_ref, g2_ref, wr_ref, br_ref, xo_ref, h2_ref, rt_ref):
    gw = SSD_D_INNER // SSD_GROUPS
    mix = None
    for g in range(SSD_GROUPS):
        sl = slice(g * gw, (g + 1) * gw)
        y = yf_ref[0, :, sl].astype(F32) + yb_ref[0, :, sl].astype(F32) + xs_ref[:, sl].astype(F32) * dw_ref[:, sl]
        y = y * _silu(z_ref[:, sl].astype(F32))
        y = (y * lax.rsqrt(jnp.mean(y * y, axis=-1, keepdims=True) + RMS_EPS)) * ng_ref[:, sl]
        part = jnp.dot(y.astype(BF16), wo_ref[sl, :], preferred_element_type=F32)
        mix = part if mix is None else mix + part
    _residual_norm_route(x_ref[...], mix, mod_ref[0], g2_ref[...], wr_ref, br_ref, xo_ref, h2_ref, rt_ref)


def _ssd_out(cfg, y2, xbc_act, z, d_wide, norm_g, wo, x, mod_l, g2, wr, br):
    tm = cfg.tm
    row = lambda w: pl.BlockSpec((tm, w), lambda i: (i, 0))
    ydir = lambda d: pl.BlockSpec((1, tm, SSD_D_INNER), lambda i: (d, i, 0))
    ep_in, ep_out, ep_shape = _epilogue_specs(cfg)
    return pl.pallas_call(
        _ssd_out_kernel,
        grid=(cfg.t_all // tm,),
        in_specs=[ydir(0), ydir(1), row(SSD_D_INNER), row(SSD_D_INNER), _const_spec((1, SSD_D_INNER)),
                  _const_spec((1, SSD_D_INNER)), _const_spec(wo.shape)] + ep_in,
        out_specs=ep_out,
        out_shape=ep_shape,
        compiler_params=_cparams(("arbitrary",)),
        name="ssd_out",
    )(y2, y2, xbc_act, z, d_wide, norm_g, wo, x, mod_l, g2, wr, br)


def _expert_kernel(eid_ref, nused_ref, xb_ref, wg_ref, wu_ref, wd_ref, y_ref, wg_s, wu_s, wd_s):
    i = pl.program_id(0)
    changed = jnp.logical_or(i == 0, eid_ref[i] != eid_ref[jnp.maximum(i - 1, 0)])

    @pl.when(jnp.logical_and(changed, i < nused_ref[0]))
    def _():
        wg_s[...] = wg_ref[0, 0].astype(BF16)
        wu_s[...] = wu_ref[0, 0].astype(BF16)
        wd_s[...] = wd_ref[0, 0].astype(BF16)

    @pl.when(i < nused_ref[0])
    def _():
        xb = xb_ref[...].astype(BF16)
        hg = jnp.dot(xb, wg_s[...], preferred_element_type=F32)
        hu = jnp.dot(xb, wu_s[...], preferred_element_type=F32)
        y_ref[...] = jnp.dot((_silu(hg) * hu).astype(BF16), wd_s[...], preferred_element_type=F32)

    @pl.when(i >= nused_ref[0])
    def _():
        y_ref[...] = jnp.zeros_like(y_ref)


def _experts(cfg, layer, block_eid, n_used, xb, w_gate, w_up, w_down):
    d, tile = cfg.d_model, cfg.moe_tile
    n_blocks = xb.shape[0] // tile
    grid_spec = pltpu.PrefetchScalarGridSpec(
        num_scalar_prefetch=2,
        grid=(n_blocks,),
        in_specs=[pl.BlockSpec((tile, d), lambda i, e, n: (i, 0)),
                  pl.BlockSpec((1, 1, d, MOE_HIDDEN), lambda i, e, n: (layer, e[i], 0, 0)),
                  pl.BlockSpec((1, 1, d, MOE_HIDDEN), lambda i, e, n: (layer, e[i], 0, 0)),
                  pl.BlockSpec((1, 1, MOE_HIDDEN, d), lambda i, e, n: (layer, e[i], 0, 0))],
        out_specs=pl.BlockSpec((tile, d), lambda i, e, n: (i, 0)),
        scratch_shapes=[pltpu.VMEM((d, MOE_HIDDEN), BF16), pltpu.VMEM((d, MOE_HIDDEN), BF16),
                        pltpu.VMEM((MOE_HIDDEN, d), BF16)],
    )
    return pl.pallas_call(
        _expert_kernel,
        grid_spec=grid_spec,
        out_shape=jax.ShapeDtypeStruct((n_blocks * tile, d), F32),
        compiler_params=_cparams(("arbitrary",)),
        name="moe_experts",
    )(block_eid, n_used, xb, w_gate, w_up, w_down)


def _dispatch_kernel(rt_ref, pos_ref, cnt_ref, carry_ref, start_ref, *, tile):
    p, i = pl.program_id(0), pl.program_id(1)
    tm = rt_ref.shape[0]
    rt = rt_ref[...]
    lane = lax.broadcasted_iota(jnp.int32, rt.shape, 1)
    o1 = jnp.where(lane == rt[:, 0:1].astype(jnp.int32), 1.0, 0.0)
    o2 = jnp.where(lane == rt[:, 1:2].astype(jnp.int32), 1.0, 0.0)
    cnt1 = jnp.sum(o1, axis=0, keepdims=True)
    cnt2 = jnp.sum(o2, axis=0, keepdims=True)

    @pl.when(jnp.logical_and(p == 0, i == 0))
    def _():
        cnt_ref[...] = jnp.zeros_like(cnt_ref)

    @pl.when(p == 0)
    def _():
        cnt_ref[...] += cnt1 + cnt2

    @pl.when(jnp.logical_and(p == 1, i == 0))
    def _():
        padded = jnp.floor((cnt_ref[...] + (tile - 1)) * (1.0 / tile)) * tile
        r = lax.broadcasted_iota(jnp.int32, (ROUTE_LANES, ROUTE_LANES), 0)
        c = lax.broadcasted_iota(jnp.int32, (ROUTE_LANES, ROUTE_LANES), 1)
        excl = jnp.where(r < c, 1.0, 0.0)
        start_ref[...] = jnp.dot(padded, excl, precision=HIGHEST, preferred_element_type=F32)
        carry_ref[...] = jnp.zeros_like(carry_ref)

    @pl.when(p == 1)
    def _():
        ri = lax.broadcasted_iota(jnp.int32, (tm, tm), 0)
        ci = lax.broadcasted_iota(jnp.int32, (tm, tm), 1)
        earlier = jnp.where(ci < ri, 1.0, 0.0).astype(BF16)
        p1 = jnp.dot(earlier, o1.astype(BF16), preferred_element_type=F32)
        p2 = jnp.dot(earlier, o2.astype(BF16), preferred_element_type=F32)
        base = start_ref[0:1] + carry_ref[0:1]
        pos1 = jnp.sum(o1 * (base + p1), axis=-1, keepdims=True)
        pos2 = jnp.sum(o2 * (base + cnt1 + p2), axis=-1, keepdims=True)
        pos_ref[...] = jnp.where(lane == 0, pos1, jnp.where(lane == 1, pos2, 0.0)).astype(jnp.int32)
        carry_ref[...] += cnt1 + cnt2


def _dispatch(cfg, route):
    tile, tm = cfg.moe_tile, cfg.tm
    t = cfg.t_all
    a_total = 2 * t
    pos, cnt = pl.pallas_call(
        functools.partial(_dispatch_kernel, tile=tile),
        grid=(2, t // tm),
        in_specs=[pl.BlockSpec((tm, ROUTE_LANES), lambda p, i: (i, 0))],
        out_specs=[pl.BlockSpec((tm, ROUTE_LANES), lambda p, i: (i * p, 0)),
                   pl.BlockSpec((SUBLANES_V7X, ROUTE_LANES), lambda p, i: (0, 0))],
        out_shape=[jax.ShapeDtypeStruct((t, ROUTE_LANES), jnp.int32),
                   jax.ShapeDtypeStruct((SUBLANES_V7X, ROUTE_LANES), F32)],
        scratch_shapes=[pltpu.VMEM((SUBLANES_V7X, ROUTE_LANES), F32), pltpu.VMEM((SUBLANES_V7X, ROUTE_LANES), F32)],
        compiler_params=_cparams(("arbitrary", "arbitrary")),
        name="moe_dispatch",
    )(route)
    counts = cnt[0, :MOE_EXPERTS].astype(jnp.int32)
    pend = jnp.cumsum((counts + tile - 1) // tile * tile)
    n_blocks = (a_total + MOE_EXPERTS * (tile - 1)) // tile
    block_eid = jnp.minimum(jnp.sum(pend[None, :] <= (jnp.arange(n_blocks, dtype=jnp.int32) * tile)[:, None], axis=1),
                            MOE_EXPERTS - 1).astype(jnp.int32)
    n_used = (pend[-1] // tile).astype(jnp.int32).reshape(1)
    pos2 = pos[:, 0:2]
    slot_tok = (jnp.arange(n_blocks * tile, dtype=jnp.int32) % t).at[pos2.reshape(-1)].set(
        jnp.arange(a_total, dtype=jnp.int32) // 2)
    return slot_tok, block_eid, n_used, pos2


def _combine_kernel(x_ref, y1_ref, y2_ref, rt_ref, mod_ref, g_ref, o_ref, *, final):
    rt = rt_ref[...]
    f = rt[:, 2:3] * y1_ref[...] + rt[:, 3:4] * y2_ref[...]
    xn = x_ref[...] + mod_ref[0][5:6] * f
    if final:
        xn = (xn * lax.rsqrt(jnp.mean(xn * xn, axis=-1, keepdims=True) + RMS_EPS)) * g_ref[...]
    o_ref[...] = xn


def _combine(cfg, x, y1, y2, route, mod_l, g, final):
    d, tm = cfg.d_model, cfg.tm
    row = lambda w: pl.BlockSpec((tm, w), lambda i: (i, 0))
    return pl.pallas_call(
        functools.partial(_combine_kernel, final=final),
        grid=(cfg.t_all // tm,),
        in_specs=[row(d), row(d), row(d), row(ROUTE_LANES), _mod_spec(cfg), _const_spec((1, d))],
        out_specs=row(d),
        out_shape=jax.ShapeDtypeStruct((cfg.t_all, d), F32),
        compiler_params=_cparams(("arbitrary",)),
        name="moe_combine",
    )(x, y1, y2, route, mod_l, g)


def _forward(cfg, x, c, ctx, c_ctx, ada_w, ada_b, norm1_g, norm2_g, ab_w_in, ab_w_out,
             diff_lq1, diff_lk1, diff_lq2, diff_lk2, diff_subln_g, bconv_w,
             ssd_w_in, ssd_conv_w, ssd_conv_b, ssd_A_log, ssd_dt_bias, ssd_D, ssd_norm_g, ssd_w_out,
             moe_wg, moe_bg, moe_we, moe_be, moe_w_gate, moe_w_up, moe_w_down, final_norm_g):
    d = cfg.d_model
    b = cfg.batch
    xa = jnp.concatenate([x.reshape(cfg.t_lat, d), ctx.reshape(cfg.t_ctx, d)], axis=0)
    c_all = jnp.zeros((SUBLANES_V7X, d), F32).at[:b].set(c).at[b].set(c_ctx)
    mod = _adaln(cfg, c_all, ada_w, ada_b)
    tables = _rope_tables(cfg)

    for layer in range(cfg.depth):
        i = layer // 2
        mod_l = mod[layer]
        g1 = norm1_g[layer].reshape(1, d)
        g2 = norm2_g[layer].reshape(1, d)
        pad = ROUTE_LANES - MOE_GROUPS - MOE_EXPERTS
        wr = jnp.concatenate([moe_wg[layer], moe_we[layer], jnp.zeros((d, pad), F32)], axis=1)
        br = jnp.concatenate([moe_bg[layer], moe_be[layer], jnp.zeros((pad,), F32)]).reshape(1, ROUTE_LANES)
        if layer % 2 == 0:
            lam_init = 0.8 - 0.6 * math.exp(-0.3 * layer)
            q, k, v, gb, u = _ab_in(cfg, xa, mod_l, g1, ab_w_in[i].astype(BF16), tables)
            lvec = jnp.stack([diff_lq1[i], diff_lk1[i], diff_lq2[i], diff_lk2[i]])
            o = _attention(cfg, q, k, v, lvec, diff_subln_g[i].reshape(1, DIFF_V_DIM), lam_init)
            w_out = ab_w_out[i].astype(BF16)
            xa, h2, route = _ab_out(cfg, o, gb, u, bconv_w[i], w_out[:DIFF_WIDTH], w_out[DIFF_WIDTH:],
                                    xa, mod_l, g2, wr, br)
        else:
            w_in = ssd_w_in[i].astype(BF16)
            n_main = SSD_D_INNER + SSD_CONV_CH
            w_dt = jnp.pad(w_in[:, n_main:], ((0, 0), (0, LANES_V7X - 2 * SSD_HEADS)))
            z, xbc_act, dt_raw = _ssd_in(cfg, xa, mod_l, g1, w_in[:, :n_main], w_dt, ssd_conv_w[i],
                                         ssd_conv_b[i].reshape(1, SSD_CONV_CH))
            y2 = _ssd_scan(cfg, xbc_act, dt_raw, ssd_dt_bias[i], ssd_A_log[i])
            d_wide = jnp.repeat(ssd_D[i], SSD_HEAD_DIM).reshape(1, SSD_D_INNER)
            xa, h2, route = _ssd_out(cfg, y2, xbc_act, z, d_wide, ssd_norm_g[i].reshape(1, SSD_D_INNER),
                                     ssd_w_out[i].astype(BF16), xa, mod_l, g2, wr, br)
        slot_tok, block_eid, n_used, pos = _dispatch(cfg, route)
        yb = _experts(cfg, layer, block_eid, n_used, h2[slot_tok], moe_w_gate, moe_w_up, moe_w_down)
        last = layer == cfg.depth - 1
        xa = _combine(cfg, xa, yb[pos[:, 0]], yb[pos[:, 1]], route, mod_l, final_norm_g.reshape(1, d), last)
    return xa[:cfg.t_lat].reshape(b, cfg.n_lat, d)


def kernel(x, c, ctx, c_ctx, ada_w, ada_b, norm1_g, norm2_g, ab_w_in, ab_w_out, diff_lq1, diff_lk1, diff_lq2, diff_lk2, diff_subln_g, bconv_w, ssd_w_in, ssd_conv_w, ssd_conv_b, ssd_A_log, ssd_dt_bias, ssd_D, ssd_norm_g, ssd_w_out, moe_wg, moe_bg, moe_we, moe_be, moe_w_gate, moe_w_up, moe_w_down, final_norm_g):
    cfg = Cfg(batch=x.shape[0], n_lat=x.shape[1], n_ctx=ctx.shape[1], d_model=x.shape[2], depth=ada_w.shape[0],
              tm=512, tq=256, moe_tile=256, attn_subtiles=2)
    return _forward(cfg, x, c, ctx, c_ctx, ada_w, ada_b, norm1_g, norm2_g, ab_w_in, ab_w_out,
                    diff_lq1, diff_lk1, diff_lq2, diff_lk2, diff_subln_g, bconv_w,
                    ssd_w_in, ssd_conv_w, ssd_conv_b, ssd_A_log, ssd_dt_bias, ssd_D, ssd_norm_g, ssd_w_out,
                    moe_wg, moe_bg, moe_we, moe_be, moe_w_gate, moe_w_up, moe_w_down, final_norm_g)
```

```python
import functools
import math
from typing import NamedTuple

import jax
import jax.numpy as jnp
from jax import lax
from jax.experimental import pallas as pl
from jax.experimental.pallas import tpu as pltpu

F32 = jnp.float32
BF16 = jnp.bfloat16
HIGHEST = lax.Precision.HIGHEST

LANES_V7X = 128
SUBLANES_V7X = 8
VMEM_LIMIT_BYTES_V7X = 56 * 1024 * 1024

RMS_EPS = 1e-6
GRID_W = 64
N_DIFF_HEADS = 4
DIFF_QK_DIM = 64
DIFF_V_DIM = 128
DIFF_WIDTH = 512
CONV_WIDTH = 512
ROPE_BASE = 10000.0
SSD_D_INNER = 2048
SSD_HEAD_DIM = 64
SSD_HEADS = 32
SSD_GROUPS = 4
SSD_STATE = 128
SSD_CHUNK = 128
SSD_CONV_CH = SSD_D_INNER + 2 * SSD_GROUPS * SSD_STATE
MOE_GROUPS = 4
MOE_EXPERTS_PER_GROUP = 8
MOE_EXPERTS = 32
MOE_HIDDEN = 512
ROUTE_LANES = LANES_V7X


class Cfg(NamedTuple):
    batch: int
    n_lat: int
    n_ctx: int
    d_model: int
    depth: int
    tm: int
    tq: int
    moe_tile: int
    attn_subtiles: int

    @property
    def t_lat(self):
        return self.batch * self.n_lat

    @property
    def t_ctx(self):
        return self.batch * self.n_ctx

    @property
    def t_all(self):
        return self.t_lat + self.t_ctx


def _cparams(sem):
    return pltpu.CompilerParams(dimension_semantics=sem, vmem_limit_bytes=VMEM_LIMIT_BYTES_V7X)


def _silu(v):
    return v * (1.0 / (1.0 + jnp.exp(-v)))


def _const_spec(shape):
    nd = len(shape)
    return pl.BlockSpec(shape, lambda *_: (0,) * nd)


def _mod_spec(cfg):
    return pl.BlockSpec((1, 6, cfg.d_model),
                        lambda i: (jnp.minimum((i * cfg.tm) // cfg.n_lat, cfg.batch), 0, 0))


def _seq_pos(cfg, tm):
    r0 = pl.program_id(0) * tm
    row = r0 + lax.broadcasted_iota(jnp.int32, (tm, 1), 0)
    seqlen = jnp.where(r0 >= cfg.t_lat, cfg.n_ctx, cfg.n_lat)
    return row & (seqlen - 1), seqlen


def _norm_mod(x, g, shift, scale):
    ms = jnp.mean(x * x, axis=-1, keepdims=True)
    return (x * lax.rsqrt(ms + RMS_EPS) * g) * (1.0 + scale) + shift


def _adaln_kernel(c_ref, w_ref, b_ref, o_ref):
    sc = _silu(c_ref[...])
    o_ref[0] = jnp.dot(sc, w_ref[0], precision=HIGHEST, preferred_element_type=F32) + b_ref[0]


def _adaln(cfg, c_all, ada_w, ada_b):
    d = cfg.d_model
    out = pl.pallas_call(
        _adaln_kernel,
        grid=(cfg.depth, 6),
        in_specs=[_const_spec((SUBLANES_V7X, d)),
                  pl.BlockSpec((1, d, d), lambda l, j: (l, 0, j)),
                  pl.BlockSpec((1, 1, d), lambda l, j: (l, 0, j))],
        out_specs=pl.BlockSpec((1, SUBLANES_V7X, d), lambda l, j: (l, 0, j)),
        out_shape=jax.ShapeDtypeStruct((cfg.depth, SUBLANES_V7X, 6 * d), F32),
        compiler_params=_cparams(("arbitrary", "arbitrary")),
        name="adaln",
    )(c_all, ada_w, ada_b.reshape(cfg.depth, 1, 6 * d))
    return out.reshape(cfg.depth, SUBLANES_V7X, 6, d)


def _rope_tables(cfg):
    n = cfg.n_lat
    rows = n // GRID_W
    row = jnp.broadcast_to(jnp.arange(rows, dtype=F32)[:, None], (rows, GRID_W)).reshape(n)
    col = jnp.broadcast_to(jnp.arange(GRID_W, dtype=F32)[None, :], (rows, GRID_W)).reshape(n)
    axis_dim = DIFF_QK_DIM // 2
    inv_freq = ROPE_BASE ** (-jnp.arange(0, axis_dim, 2, dtype=F32) / axis_dim)
    ang_r = row[:, None] * inv_freq
    ang_c = col[:, None] * inv_freq
    zeros = jnp.zeros_like(ang_r)
    cos64 = jnp.concatenate([jnp.cos(ang_r), jnp.cos(ang_r), jnp.cos(ang_c), jnp.cos(ang_c)], axis=1)
    sa64 = jnp.concatenate([zeros, jnp.sin(ang_r), zeros, jnp.sin(ang_c)], axis=1)
    sb64 = jnp.concatenate([-jnp.sin(ang_r), zeros, -jnp.sin(ang_c), zeros], axis=1)
    ident = cfg.tm
    cos = jnp.concatenate([jnp.tile(cos64, (1, 2)), jnp.ones((ident, LANES_V7X), F32)], axis=0)
    sa = jnp.concatenate([jnp.tile(sa64, (1, 2)), jnp.zeros((ident, LANES_V7X), F32)], axis=0)
    sb = jnp.concatenate([jnp.tile(sb64, (1, 2)), jnp.zeros((ident, LANES_V7X), F32)], axis=0)
    return cos, sa, sb


def _ab_in_kernel(x_ref, mod_ref, g_ref, w_ref, cos_ref, sa_ref, sb_ref,
                  q_ref, k_ref, v_ref, gb_ref, u_ref):
    m = mod_ref[0]
    hb = _norm_mod(x_ref[...], g_ref[...], m[0:1], m[1:2]).astype(BF16)
    cos, sa, sb = cos_ref[...], sa_ref[...], sb_ref[...]

    def proj(j):
        return jnp.dot(hb, w_ref[:, j * DIFF_WIDTH:(j + 1) * DIFF_WIDTH], preferred_element_type=F32)

    def rope(t, out_ref, scale):
        for gi in range(DIFF_WIDTH // LANES_V7X):
            xg = t[:, gi * LANES_V7X:(gi + 1) * LANES_V7X]
            r = xg * cos + pltpu.roll(xg, 16, 1) * sa + pltpu.roll(xg, LANES_V7X - 16, 1) * sb
            out_ref[:, gi * LANES_V7X:(gi + 1) * LANES_V7X] = (r * scale).astype(out_ref.dtype)

    rope(proj(0), q_ref, math.log2(math.e) * DIFF_QK_DIM ** -0.5)
    rope(proj(1), k_ref, 1.0)
    v_ref[...] = proj(2).astype(v_ref.dtype)
    gb_ref[...] = proj(3)
    u_ref[...] = proj(4) * proj(5)


def _ab_in(cfg, x, mod_l, g1, w_in_bf16, tables):
    t, d, tm = cfg.t_all, cfg.d_model, cfg.tm
    n_lat_tiles, per_seq = cfg.t_lat // tm, cfg.n_lat // tm
    tab_spec = pl.BlockSpec((tm, LANES_V7X), lambda i: (jnp.where(i < n_lat_tiles, i % per_seq, per_seq), 0))
    row = lambda w: pl.BlockSpec((tm, w), lambda i: (i, 0))
    return pl.pallas_call(
        _ab_in_kernel,
        grid=(t // tm,),
        in_specs=[row(d), _mod_spec(cfg), _const_spec((1, d)), _const_spec(w_in_bf16.shape),
                  tab_spec, tab_spec, tab_spec],
        out_specs=[row(DIFF_WIDTH)] * 5,
        out_shape=[jax.ShapeDtypeStruct((t, DIFF_WIDTH), BF16)] * 3
        + [jax.ShapeDtypeStruct((t, CONV_WIDTH), F32)] * 2,
        compiler_params=_cparams(("arbitrary",)),
        name="ab_in",
    )(x, mod_l, g1, w_in_bf16, *tables)


def _attn_body(lvec_ref, q_ref, kc_ref, kl_ref, vc_ref, vl_ref, g_ref, o_ref, *, tq, lam_init):
    lv = lvec_ref[...]
    lam = (jnp.exp(jnp.sum(lv[0:1] * lv[1:2], axis=-1, keepdims=True))
           - jnp.exp(jnp.sum(lv[2:3] * lv[3:4], axis=-1, keepdims=True)) + lam_init)
    nt = (((1,), (1,)), ((), ()))
    half = DIFF_QK_DIM
    kv = [(kc_ref, vc_ref)] + ([(kl_ref, vl_ref)] if kl_ref is not None else [])
    for j in range(q_ref.shape[0] // tq):
        q = q_ref[j * tq:(j + 1) * tq, :].astype(F32)
        lane = lax.broadcasted_iota(jnp.int32, q.shape, 1)
        q2 = jnp.concatenate([jnp.where(lane < half, q, 0.0), jnp.where(lane >= half, q, 0.0)],
                             axis=0).astype(BF16)
        s = [lax.dot_general(q2, k_ref[...], nt, preferred_element_type=F32) for k_ref, _ in kv]
        mx = functools.reduce(jnp.maximum, [jnp.max(x, axis=-1, keepdims=True) for x in s])
        p = [jnp.exp2(x - mx) for x in s]
        r = 1.0 / functools.reduce(jnp.add, [jnp.sum(x, axis=-1, keepdims=True) for x in p])
        c0, c1 = r[:tq], lam * r[tq:]
        pv = functools.reduce(jnp.add, [
            jnp.dot((x[:tq] * c0 - x[tq:] * c1).astype(BF16), v_ref[...], preferred_element_type=F32)
            for x, (_, v_ref) in zip(p, kv)])
        o = pv * lax.rsqrt(jnp.mean(pv * pv, axis=-1, keepdims=True) + RMS_EPS)
        o_ref[j * tq:(j + 1) * tq, :] = ((o * g_ref[...]) * (1.0 - lam_init)).astype(o_ref.dtype)


def _attn_lat_kernel(lvec_ref, q_ref, kc_ref, kl_ref, vc_ref, vl_ref, g_ref, o_ref, **kw):
    _attn_body(lvec_ref, q_ref, kc_ref, kl_ref, vc_ref, vl_ref, g_ref, o_ref, **kw)


def _attn_ctx_kernel(lvec_ref, q_ref, kc_ref, vc_ref, g_ref, o_ref, **kw):
    _attn_body(lvec_ref, q_ref, kc_ref, None, vc_ref, None, g_ref, o_ref, **kw)


def _attention(cfg, q, k, v, lvec, subln_g, lam_init):
    tq, nsub = cfg.tq, cfg.attn_subtiles
    tstep = tq * nsub
    nqb = cfg.n_lat // tstep
    ctx_blk0 = cfg.t_lat // cfg.n_ctx
    hw = DIFF_V_DIM
    ctx_spec = pl.BlockSpec((cfg.n_ctx, hw), lambda b, h, *_: (ctx_blk0 + b, h))
    lat_spec = pl.BlockSpec((cfg.n_lat, hw), lambda b, h, *_: (b, h))
    q_spec = pl.BlockSpec((tstep, hw), lambda b, h, i: (b * nqb + i, h))
    kw = dict(tq=tq, lam_init=lam_init)
    o_lat = pl.pallas_call(
        functools.partial(_attn_lat_kernel, **kw),
        grid=(cfg.batch, N_DIFF_HEADS, nqb),
        in_specs=[_const_spec(lvec.shape), q_spec, ctx_spec, lat_spec, ctx_spec, lat_spec, _const_spec((1, hw))],
        out_specs=q_spec,
        out_shape=jax.ShapeDtypeStruct((cfg.t_lat, DIFF_WIDTH), BF16),
        compiler_params=_cparams(("arbitrary", "arbitrary", "arbitrary")),
        name="diff_attn",
    )(lvec, q, k, k, v, v, subln_g)
    tqc = min(tq, cfg.n_ctx)
    o_ctx = pl.pallas_call(
        functools.partial(_attn_ctx_kernel, tq=tqc, lam_init=lam_init),
        grid=(cfg.batch, N_DIFF_HEADS),
        in_specs=[_const_spec(lvec.shape), ctx_spec, ctx_spec, ctx_spec, _const_spec((1, hw))],
        out_specs=pl.BlockSpec((cfg.n_ctx, hw), lambda b, h: (b, h)),
        out_shape=jax.ShapeDtypeStruct((cfg.t_ctx, DIFF_WIDTH), BF16),
        compiler_params=_cparams(("arbitrary", "arbitrary")),
        name="diff_attn_ctx",
    )(lvec, q, k, v, subln_g)
    return jnp.concatenate([o_lat, o_ctx], axis=0)


def _route_from_logits(lg):
    lane = lax.broadcasted_iota(jnp.int32, lg.shape, 1)
    neg = -jnp.inf
    big = jnp.int32(ROUTE_LANES)
    gl = jnp.where(lane < MOE_GROUPS, lg, neg)
    gmax = jnp.max(gl, axis=-1, keepdims=True)
    gidx = jnp.min(jnp.where(gl == gmax, lane, big), axis=-1, keepdims=True)
    gw = 1.0 / jnp.sum(jnp.exp(gl - gmax), axis=-1, keepdims=True)
    lo = MOE_GROUPS + gidx * MOE_EXPERTS_PER_GROUP
    el = jnp.where((lane >= lo) & (lane < lo + MOE_EXPERTS_PER_GROUP), lg, neg)
    m1 = jnp.max(el, axis=-1, keepdims=True)
    i1 = jnp.min(jnp.where(el == m1, lane, big), axis=-1, keepdims=True)
    el2 = jnp.where(lane == i1, neg, el)
    m2 = jnp.max(el2, axis=-1, keepdims=True)
    i2 = jnp.min(jnp.where(el2 == m2, lane, big), axis=-1, keepdims=True)
    w1 = gw / (1.0 + jnp.exp(m2 - m1))
    w2 = gw - w1
    e1 = (i1 - MOE_GROUPS).astype(F32)
    e2 = (i2 - MOE_GROUPS).astype(F32)
    return jnp.where(lane == 0, e1, jnp.where(lane == 1, e2, jnp.where(lane == 2, w1, jnp.where(lane == 3, w2, 0.0))))


def _pack_bf16_pairs(v):
    w = v.shape[1] // 2
    lo = pltpu.bitcast(v[:, :w].astype(BF16).astype(F32), jnp.uint32)
    hi = pltpu.bitcast(v[:, w:].astype(BF16).astype(F32), jnp.uint32)
    return (hi & jnp.uint32(0xFFFF0000)) | (lo >> 16)


def _unpack_bf16_pairs(u):
    lo = pltpu.bitcast(u << 16, F32)
    hi = pltpu.bitcast(u & jnp.uint32(0xFFFF0000), F32)
    return lo, hi


def _router_weights(wg, bg, we, be):
    d = wg.shape[0]
    pad = ROUTE_LANES - MOE_GROUPS - MOE_EXPERTS
    wr = jnp.concatenate([wg, we, jnp.zeros((d, pad), F32)], axis=1)
    hi = wr.astype(BF16)
    lo = (wr - hi.astype(F32)).astype(BF16)
    br = jnp.concatenate([bg, be, jnp.zeros((pad,), F32)]).reshape(1, ROUTE_LANES)
    return jnp.concatenate([hi, lo], axis=1), hi, br


def _residual_norm_route(x, mix, m, g2, wr2_ref, wrh_ref, br_ref, xo_ref, h2_ref, rt_ref):
    xn = x + m[2:3] * mix
    xo_ref[...] = xn
    h2 = _norm_mod(xn, g2, m[3:4], m[4:5])
    h2_ref[...] = _pack_bf16_pairs(h2)
    hh = h2.astype(BF16)
    hl = (h2 - hh.astype(F32)).astype(BF16)
    a = jnp.dot(hh, wr2_ref[...], preferred_element_type=F32)
    lg = (a[:, :ROUTE_LANES] + a[:, ROUTE_LANES:]) + jnp.dot(hl, wrh_ref[...], preferred_element_type=F32)
    rt_ref[...] = _route_from_logits(lg + br_ref[...])


def _epilogue_specs(cfg):
    d, tm = cfg.d_model, cfg.tm
    row = lambda w: pl.BlockSpec((tm, w), lambda i: (i, 0))
    in_specs = [row(d), _mod_spec(cfg), _const_spec((1, d)), _const_spec((d, 2 * ROUTE_LANES)),
                _const_spec((d, ROUTE_LANES)), _const_spec((1, ROUTE_LANES))]
    out_specs = [row(d), row(d // 2), row(ROUTE_LANES)]
    out_shape = [jax.ShapeDtypeStruct((cfg.t_all, d), F32), jax.ShapeDtypeStruct((cfg.t_all, d // 2), jnp.uint32),
                 jax.ShapeDtypeStruct((cfg.t_all, ROUTE_LANES), F32)]
    return in_specs, out_specs, out_shape


def _halo_specs(cfg, width, col_block=0):
    per = cfg.tm // SUBLANES_V7X
    last = cfg.t_all // SUBLANES_V7X - 1
    prev = pl.BlockSpec((SUBLANES_V7X, width), lambda i, *_: (jnp.maximum(i * per - 1, 0), col_block))
    nxt = pl.BlockSpec((SUBLANES_V7X, width), lambda i, *_: (jnp.minimum((i + 1) * per, last), col_block))
    return prev, nxt


def _shifted_rows(pad_ref, u, prev_blk, next_blk, pos, seqlen):
    tm = u.shape[0]
    s = SUBLANES_V7X
    pad_ref[s:s + tm, :] = u
    pad_ref[0:s, :] = prev_blk
    pad_ref[s + tm:2 * s + tm, :] = next_blk
    um1 = jnp.where(pos == 0, 0.0, pad_ref[s - 1:s - 1 + tm, :])
    up1 = jnp.where(pos == seqlen - 1, 0.0, pad_ref[s + 1:s + 1 + tm, :])
    return um1, up1


def _ab_out_kernel(o_ref, gb_ref, u_ref, up_ref, un_ref, cw_ref, wo_ref, wc_ref,
                   x_ref, mod_ref, g2_ref, wr2_ref, wrh_ref, br_ref,
                   xo_ref, h2_ref, rt_ref, pad_ref, *, cfg):
    pos, seqlen = _seq_pos(cfg, cfg.tm)
    u = u_ref[...]
    um1, up1 = _shifted_rows(pad_ref, u, up_ref[...], un_ref[...], pos, seqlen)
    cw = cw_ref[...]
    conv = gb_ref[...] * (um1 * cw[0:1] + u * cw[1:2] + up1 * cw[2:3])
    mix = (jnp.dot(o_ref[...], wo_ref[...], preferred_element_type=F32)
           + jnp.dot(conv.astype(BF16), wc_ref[...], preferred_element_type=F32))
    _residual_norm_route(x_ref[...], mix, mod_ref[0], g2_ref[...], wr2_ref, wrh_ref, br_ref,
                         xo_ref, h2_ref, rt_ref)


def _ab_out(cfg, o, gb, u, conv_w, wo, wc, x, mod_l, g2, router):
    tm = cfg.tm
    row = lambda w: pl.BlockSpec((tm, w), lambda i: (i, 0))
    prev, nxt = _halo_specs(cfg, CONV_WIDTH)
    ep_in, ep_out, ep_shape = _epilogue_specs(cfg)
    return pl.pallas_call(
        functools.partial(_ab_out_kernel, cfg=cfg),
        grid=(cfg.t_all // tm,),
        in_specs=[row(DIFF_WIDTH), row(CONV_WIDTH), row(CONV_WIDTH), prev, nxt, _const_spec(conv_w.shape),
                  _const_spec(wo.shape), _const_spec(wc.shape)] + ep_in,
        out_specs=ep_out,
        out_shape=ep_shape,
        scratch_shapes=[pltpu.VMEM((tm + 2 * SUBLANES_V7X, CONV_WIDTH), F32)],
        compiler_params=_cparams(("arbitrary",)),
        name="ab_out",
    )(o, gb, u, u, u, conv_w, wo, wc, x, mod_l, g2, *router)


def _ssd_in_kernel(x_ref, xp_ref, xn_ref, mod_ref, g_ref, w_ref, wdt_ref, cw_ref, cb_ref,
                   z_ref, xbc_ref, dt_ref, pad_ref, *, cfg):
    tm = cfg.tm
    s = SUBLANES_V7X
    m = mod_ref[0]
    g = g_ref[...]
    h = _norm_mod(x_ref[...], g, m[0:1], m[1:2])
    hb = h.astype(BF16)
    h_ext = jnp.concatenate([_norm_mod(xp_ref[...], g, m[0:1], m[1:2]), h,
                             _norm_mod(xn_ref[...], g, m[0:1], m[1:2])], axis=0).astype(BF16)
    pos, seqlen = _seq_pos(cfg, tm)
    cw = 512
    for j in range(SSD_D_INNER // cw):
        z_ref[:, j * cw:(j + 1) * cw] = jnp.dot(hb, w_ref[:, j * cw:(j + 1) * cw],
                                                preferred_element_type=F32).astype(z_ref.dtype)
    for j in range(SSD_CONV_CH // cw):
        c0 = SSD_D_INNER + j * cw
        pad = pad_ref.at[j]
        pad[...] = jnp.dot(h_ext, w_ref[:, c0:c0 + cw], preferred_element_type=F32)
        taps = cw_ref[:, j * cw:(j + 1) * cw]
        um1 = jnp.where(pos == 0, 0.0, pad[s - 1:s - 1 + tm, :])
        up1 = jnp.where(pos == seqlen - 1, 0.0, pad[s + 1:s + 1 + tm, :])
        conv = um1 * taps[0:1] + pad[s:s + tm, :] * taps[1:2] + up1 * taps[2:3] + cb_ref[:, j * cw:(j + 1) * cw]
        xbc_ref[:, j * cw:(j + 1) * cw] = _silu(conv).astype(xbc_ref.dtype)
    dt_ref[...] = jnp.dot(hb, wdt_ref[...], preferred_element_type=F32)


def _ssd_in(cfg, x, mod_l, g1, w_main, w_dt, conv_w, conv_b):
    t, d, tm = cfg.t_all, cfg.d_model, cfg.tm
    row = lambda w: pl.BlockSpec((tm, w), lambda i: (i, 0))
    prev, nxt = _halo_specs(cfg, d)
    return pl.pallas_call(
        functools.partial(_ssd_in_kernel, cfg=cfg),
        grid=(t // tm,),
        in_specs=[row(d), prev, nxt, _mod_spec(cfg), _const_spec((1, d)), _const_spec(w_main.shape),
                  _const_spec(w_dt.shape), _const_spec(conv_w.shape), _const_spec(conv_b.shape)],
        out_specs=[row(SSD_D_INNER), row(SSD_CONV_CH), row(LANES_V7X)],
        out_shape=[jax.ShapeDtypeStruct((t, SSD_D_INNER), BF16), jax.ShapeDtypeStruct((t, SSD_CONV_CH), BF16),
                   jax.ShapeDtypeStruct((t, LANES_V7X), F32)],
        scratch_shapes=[pltpu.VMEM((SSD_CONV_CH // 512, tm + 2 * SUBLANES_V7X, 512), F32)],
        compiler_params=_cparams(("arbitrary",)),
        name="ssd_in",
    )(x, x, x, mod_l, g1, w_main, w_dt, conv_w, conv_b)


def _split3(v):
    b1 = v.astype(BF16)
    r1 = v - b1.astype(F32)
    b2 = r1.astype(BF16)
    b3 = (r1 - b2.astype(F32)).astype(BF16)
    return jnp.concatenate([b1, b2, b3], axis=1)


def _softplus(v):
    return jnp.maximum(v, 0.0) + jnp.log(1.0 + jnp.exp(-jnp.abs(v)))


def _ssd_scan_kernel(xs_ref, b_ref, c_ref, dt_ref, dtt_ref, bias_ref, biast_ref, alog_ref, alogt_ref,
                     ebig_ref, ewide_ref, y_ref, state_ref):
    q = SSD_CHUNK
    d = pl.program_id(1)
    fwd = d == 0

    @pl.when(pl.program_id(2) == 0)
    def _():
        state_ref[...] = jnp.zeros_like(state_ref)

    dt = _softplus(dt_ref[0] + bias_ref[0])
    dtt = _softplus(dtt_ref[0] + biast_ref[0])
    a = dt * (-jnp.exp(alog_ref[0]))
    at = dtt * (-jnp.exp(alogt_ref[0]))

    ri = lax.broadcasted_iota(jnp.int32, (q, q), 0)
    ci = lax.broadcasted_iota(jnp.int32, (q, q), 1)
    keep = (ri - ci) * jnp.where(fwd, 1, -1) >= 0
    tri = jnp.where(keep, 1.0, 0.0)
    cs = jnp.dot(tri, a, precision=HIGHEST, preferred_element_type=F32)
    cst = lax.dot_general(at, tri, (((1,), (1,)), ((), ())), precision=HIGHEST,
                          preferred_element_type=F32)
    tot = jnp.sum(a, axis=0, keepdims=True)

    ebig = ebig_ref[...]
    ewide = ewide_ref[...]
    cs_big = jnp.dot(_split3(cs), ebig, preferred_element_type=F32)
    w_wide = jnp.dot(_split3(dt * jnp.exp(tot - cs)), ewide, preferred_element_type=F32)
    dec_wide = jnp.dot(_split3(jnp.broadcast_to(jnp.exp(tot), (SUBLANES_V7X, SSD_HEADS))), ewide,
                       preferred_element_type=F32)[0:1]

    lane = lax.broadcasted_iota(jnp.int32, (q, LANES_V7X), 1)
    lo = lane < SSD_HEAD_DIM
    hpg = SSD_HEADS // SSD_GROUPS
    gw = hpg * SSD_HEAD_DIM
    nt = (((1,), (1,)), ((), ()))
    tn = (((0,), (0,)), ((), ()))
    for g in range(SSD_GROUPS):
        bm = b_ref[:, g * SSD_STATE:(g + 1) * SSD_STATE]
        cm = c_ref[:, g * SSD_STATE:(g + 1) * SSD_STATE]
        cb = lax.dot_general(cm, bm, nt, preferred_element_type=F32)
        cmf = cm.astype(F32)
        for pr in range(hpg // 2):
            lhs, rhs = [], []
            c0 = g * gw + pr * LANES_V7X
            xs_pair = xs_ref[:, c0:c0 + LANES_V7X]
            st_pair = state_ref[:, c0:c0 + LANES_V7X].astype(BF16)
            zero = jnp.zeros_like(xs_pair)
            for k in range(2):
                h = g * hpg + pr * 2 + k
                csb = cs_big[:, h * q:(h + 1) * q]
                seg = csb - cst[h:h + 1, :]
                lmat = jnp.exp(jnp.where(keep, seg, -jnp.inf))
                lhs.append((cb * lmat * dtt[h:h + 1, :]).astype(BF16))
                lhs.append((cmf * jnp.exp(csb)).astype(BF16))
                sel = lo if k == 0 else jnp.logical_not(lo)
                rhs.append(jnp.where(sel, xs_pair, zero))
                rhs.append(jnp.where(sel, st_pair, zero))
            y = jnp.dot(jnp.concatenate(lhs, axis=1), jnp.concatenate(rhs, axis=0), preferred_element_type=F32)
            y_ref[0, :, c0:c0 + LANES_V7X] = y.astype(y_ref.dtype)
        sl = slice(g * gw, (g + 1) * gw)
        x2 = (xs_ref[:, sl].astype(F32) * w_wide[:, sl]).astype(BF16)
        upd = lax.dot_general(bm, x2, tn, preferred_element_type=F32)
        state_ref[:, sl] = state_ref[:, sl] * dec_wide[:, sl] + upd


def _ssd_scan(cfg, xbc_act, dt_raw, dt_bias, a_log):
    q = SSD_CHUNK
    n_cc, n_lc = cfg.n_ctx // q, cfg.n_lat // q
    ctx0 = cfg.t_lat // q
    h = SSD_HEADS
    dt2 = dt_raw[:, :2 * h].reshape(cfg.t_all, 2, h).transpose(1, 0, 2)
    dt2t = dt2.transpose(0, 2, 1)
    bias = dt_bias.reshape(2, 1, h)
    biast = dt_bias.reshape(2, h, 1)
    alog = a_log.reshape(2, 1, h)
    alogt = a_log.reshape(2, h, 1)
    head_of_row = jnp.tile(jnp.arange(h), 3)[:, None]
    ebig = (head_of_row == (jnp.arange(h * q) // q)[None, :]).astype(BF16)
    ewide = (head_of_row == (jnp.arange(SSD_D_INNER) // SSD_HEAD_DIM)[None, :]).astype(BF16)

    def blk(b, d, s):
        cs = jnp.where(d == 0, s, n_cc - 1 - s)
        ls = jnp.where(d == 0, s - n_cc, n_lc - 1 - (s - n_cc))
        return jnp.where(s < n_cc, ctx0 + b * n_cc + cs, b * n_lc + ls)

    gs = SSD_GROUPS * SSD_STATE
    col = lambda w, cblk: pl.BlockSpec((q, w), lambda b, d, s: (blk(b, d, s), cblk))
    per_dir = lambda shape: pl.BlockSpec((1,) + shape, lambda b, d, s: (d, 0, 0))
    return pl.pallas_call(
        _ssd_scan_kernel,
        grid=(cfg.batch, 2, n_cc + n_lc),
        in_specs=[col(SSD_D_INNER, 0), col(gs, SSD_D_INNER // gs), col(gs, SSD_D_INNER // gs + 1),
                  pl.BlockSpec((1, q, h), lambda b, d, s: (d, blk(b, d, s), 0)),
                  pl.BlockSpec((1, h, q), lambda b, d, s: (d, 0, blk(b, d, s))),
                  per_dir((1, h)), per_dir((h, 1)), per_dir((1, h)), per_dir((h, 1)),
                  _const_spec(ebig.shape), _const_spec(ewide.shape)],
        out_specs=pl.BlockSpec((1, q, SSD_D_INNER), lambda b, d, s: (d, blk(b, d, s), 0)),
        out_shape=jax.ShapeDtypeStruct((2, cfg.t_all, SSD_D_INNER), BF16),
        scratch_shapes=[pltpu.VMEM((SSD_STATE, SSD_D_INNER), F32)],
        compiler_params=_cparams(("arbitrary", "arbitrary", "arbitrary")),
        name="ssd_scan",
    )(xbc_act, xbc_act, xbc_act, dt2, dt2t, bias, biast, alog, alogt, ebig, ewide)


def _ssd_out_kernel(yf_ref, yb_ref, xs_ref, z_ref, dw_ref, ng_ref, wo_ref,
                    x_ref, mod_ref, g2_ref, wr2_ref, wrh_ref, br_ref, xo_ref, h2_ref, rt_ref):
    gw = SSD_D_INNER // SSD_GROUPS
    mix = None
    for g in range(SSD_GROUPS):
        sl = slice(g * gw, (g + 1) * gw)
        y = yf_ref[0, :, sl].astype(F32) + yb_ref[0, :, sl].astype(F32) + xs_ref[:, sl].astype(F32) * dw_ref[:, sl]
        y = y * _silu(z_ref[:, sl].astype(F32))
        y = (y * lax.rsqrt(jnp.mean(y * y, axis=-1, keepdims=True) + RMS_EPS)) * ng_ref[:, sl]
        part = jnp.dot(y.astype(BF16), wo_ref[sl, :], preferred_element_type=F32)
        mix = part if mix is None else mix + part
    _residual_norm_route(x_ref[...], mix, mod_ref[0], g2_ref[...], wr2_ref, wrh_ref, br_ref,
                         xo_ref, h2_ref, rt_ref)


def _ssd_out(cfg, y2, xbc_act, z, d_wide, norm_g, wo, x, mod_l, g2, router):
    tm = cfg.tm
    row = lambda w: pl.BlockSpec((tm, w), lambda i: (i, 0))
    ydir = lambda d: pl.BlockSpec((1, tm, SSD_D_INNER), lambda i: (d, i, 0))
    ep_in, ep_out, ep_shape = _epilogue_specs(cfg)
    return pl.pallas_call(
        _ssd_out_kernel,
        grid=(cfg.t_all // tm,),
        in_specs=[ydir(0), ydir(1), row(SSD_D_INNER), row(SSD_D_INNER), _const_spec((1, SSD_D_INNER)),
                  _const_spec((1, SSD_D_INNER)), _const_spec(wo.shape)] + ep_in,
        out_specs=ep_out,
        out_shape=ep_shape,
        compiler_params=_cparams(("arbitrary",)),
        name="ssd_out",
    )(y2, y2, xbc_act, z, d_wide, norm_g, wo, x, mod_l, g2, *router)


def _expert_kernel(eid_ref, nused_ref, xb_ref, wg_ref, wu_ref, wd_ref, y_ref, wg_s, wu_s, wd_s):
    i = pl.program_id(0)
    changed = jnp.logical_or(i == 0, eid_ref[i] != eid_ref[jnp.maximum(i - 1, 0)])

    @pl.when(jnp.logical_and(changed, i < nused_ref[0]))
    def _():
        wg_s[...] = wg_ref[0, 0].astype(BF16)
        wu_s[...] = wu_ref[0, 0].astype(BF16)
        wd_s[...] = wd_ref[0, 0].astype(BF16)

    @pl.when(i < nused_ref[0])
    def _():
        half = wg_s.shape[0] // 2
        lo, hi = _unpack_bf16_pairs(xb_ref[...])
        lo, hi = lo.astype(BF16), hi.astype(BF16)
        hg = (jnp.dot(lo, wg_s[:half, :], preferred_element_type=F32)
              + jnp.dot(hi, wg_s[half:, :], preferred_element_type=F32))
        hu = (jnp.dot(lo, wu_s[:half, :], preferred_element_type=F32)
              + jnp.dot(hi, wu_s[half:, :], preferred_element_type=F32))
        y = jnp.dot((_silu(hg) * hu).astype(BF16), wd_s[...], preferred_element_type=F32)
        y_ref[...] = _pack_bf16_pairs(y)

    @pl.when(i >= nused_ref[0])
    def _():
        y_ref[...] = jnp.zeros_like(y_ref)


def _experts(cfg, layer, block_eid, n_used, xb, w_gate, w_up, w_down):
    d, tile = cfg.d_model, cfg.moe_tile
    n_blocks = xb.shape[0] // tile
    grid_spec = pltpu.PrefetchScalarGridSpec(
        num_scalar_prefetch=2,
        grid=(n_blocks,),
        in_specs=[pl.BlockSpec((tile, d // 2), lambda i, e, n: (i, 0)),
                  pl.BlockSpec((1, 1, d, MOE_HIDDEN), lambda i, e, n: (layer, e[i], 0, 0)),
                  pl.BlockSpec((1, 1, d, MOE_HIDDEN), lambda i, e, n: (layer, e[i], 0, 0)),
                  pl.BlockSpec((1, 1, MOE_HIDDEN, d), lambda i, e, n: (layer, e[i], 0, 0))],
        out_specs=pl.BlockSpec((tile, d // 2), lambda i, e, n: (i, 0)),
        scratch_shapes=[pltpu.VMEM((d, MOE_HIDDEN), BF16), pltpu.VMEM((d, MOE_HIDDEN), BF16),
                        pltpu.VMEM((MOE_HIDDEN, d), BF16)],
    )
    return pl.pallas_call(
        _expert_kernel,
        grid_spec=grid_spec,
        out_shape=jax.ShapeDtypeStruct((n_blocks * tile, d // 2), jnp.uint32),
        compiler_params=_cparams(("arbitrary",)),
        name="moe_experts",
    )(block_eid, n_used, xb, w_gate, w_up, w_down)


def _dispatch_kernel(rt_ref, pos_ref, cnt_ref, carry_ref, start_ref, *, tile):
    p, i = pl.program_id(0), pl.program_id(1)
    tm = rt_ref.shape[0]
    rt = rt_ref[...]
    lane = lax.broadcasted_iota(jnp.int32, rt.shape, 1)
    o1 = jnp.where(lane == rt[:, 0:1].astype(jnp.int32), 1.0, 0.0)
    o2 = jnp.where(lane == rt[:, 1:2].astype(jnp.int32), 1.0, 0.0)
    cnt1 = jnp.sum(o1, axis=0, keepdims=True)
    cnt2 = jnp.sum(o2, axis=0, keepdims=True)

    @pl.when(jnp.logical_and(p == 0, i == 0))
    def _():
        cnt_ref[...] = jnp.zeros_like(cnt_ref)

    @pl.when(p == 0)
    def _():
        cnt_ref[...] += cnt1 + cnt2

    @pl.when(jnp.logical_and(p == 1, i == 0))
    def _():
        padded = jnp.floor((cnt_ref[...] + (tile - 1)) * (1.0 / tile)) * tile
        r = lax.broadcasted_iota(jnp.int32, (ROUTE_LANES, ROUTE_LANES), 0)
        c = lax.broadcasted_iota(jnp.int32, (ROUTE_LANES, ROUTE_LANES), 1)
        excl = jnp.where(r < c, 1.0, 0.0)
        start_ref[...] = jnp.dot(padded, excl, precision=HIGHEST, preferred_element_type=F32)
        carry_ref[...] = jnp.zeros_like(carry_ref)

    @pl.when(p == 1)
    def _():
        ri = lax.broadcasted_iota(jnp.int32, (tm, tm), 0)
        ci = lax.broadcasted_iota(jnp.int32, (tm, tm), 1)
        earlier = jnp.where(ci < ri, 1.0, 0.0).astype(BF16)
        p1 = jnp.dot(earlier, o1.astype(BF16), preferred_element_type=F32)
        p2 = jnp.dot(earlier, o2.astype(BF16), preferred_element_type=F32)
        base = start_ref[0:1] + carry_ref[0:1]
        pos1 = jnp.sum(o1 * (base + p1), axis=-1, keepdims=True)
        pos2 = jnp.sum(o2 * (base + cnt1 + p2), axis=-1, keepdims=True)
        pos_ref[...] = jnp.where(lane == 0, pos1, jnp.where(lane == 1, pos2, 0.0)).astype(jnp.int32)
        carry_ref[...] += cnt1 + cnt2


def _dispatch(cfg, route):
    tile, tm = cfg.moe_tile, cfg.tm
    t = cfg.t_all
    a_total = 2 * t
    pos, cnt = pl.pallas_call(
        functools.partial(_dispatch_kernel, tile=tile),
        grid=(2, t // tm),
        in_specs=[pl.BlockSpec((tm, ROUTE_LANES), lambda p, i: (i, 0))],
        out_specs=[pl.BlockSpec((tm, ROUTE_LANES), lambda p, i: (i * p, 0)),
                   pl.BlockSpec((SUBLANES_V7X, ROUTE_LANES), lambda p, i: (0, 0))],
        out_shape=[jax.ShapeDtypeStruct((t, ROUTE_LANES), jnp.int32),
                   jax.ShapeDtypeStruct((SUBLANES_V7X, ROUTE_LANES), F32)],
        scratch_shapes=[pltpu.VMEM((SUBLANES_V7X, ROUTE_LANES), F32), pltpu.VMEM((SUBLANES_V7X, ROUTE_LANES), F32)],
        compiler_params=_cparams(("arbitrary", "arbitrary")),
        name="moe_dispatch",
    )(route)
    counts = cnt[0, :MOE_EXPERTS].astype(jnp.int32)
    pend = jnp.cumsum((counts + tile - 1) // tile * tile)
    n_blocks = (a_total + MOE_EXPERTS * (tile - 1)) // tile
    block_eid = jnp.minimum(jnp.sum(pend[None, :] <= (jnp.arange(n_blocks, dtype=jnp.int32) * tile)[:, None], axis=1),
                            MOE_EXPERTS - 1).astype(jnp.int32)
    n_used = (pend[-1] // tile).astype(jnp.int32).reshape(1)
    pos2 = pos[:, 0:2]
    slot_tok = (jnp.arange(n_blocks * tile, dtype=jnp.int32) % t).at[pos2.reshape(-1)].set(
        jnp.arange(a_total, dtype=jnp.int32) // 2, unique_indices=True, mode="promise_in_bounds")
    return slot_tok, block_eid, n_used, pos2


def _combine_kernel(x_ref, y1_ref, y2_ref, rt_ref, mod_ref, g_ref, o_ref, *, final):
    rt = rt_ref[...]
    a_lo, a_hi = _unpack_bf16_pairs(y1_ref[...])
    b_lo, b_hi = _unpack_bf16_pairs(y2_ref[...])
    w1, w2 = rt[:, 2:3], rt[:, 3:4]
    f = jnp.concatenate([w1 * a_lo + w2 * b_lo, w1 * a_hi + w2 * b_hi], axis=1)
    xn = x_ref[...] + mod_ref[0][5:6] * f
    if final:
        xn = (xn * lax.rsqrt(jnp.mean(xn * xn, axis=-1, keepdims=True) + RMS_EPS)) * g_ref[...]
    o_ref[...] = xn


def _combine(cfg, x, y1, y2, route, mod_l, g, final):
    d, tm = cfg.d_model, cfg.tm
    row = lambda w: pl.BlockSpec((tm, w), lambda i: (i, 0))
    return pl.pallas_call(
        functools.partial(_combine_kernel, final=final),
        grid=(cfg.t_all // tm,),
        in_specs=[row(d), row(d // 2), row(d // 2), row(ROUTE_LANES), _mod_spec(cfg), _const_spec((1, d))],
        out_specs=row(d),
        out_shape=jax.ShapeDtypeStruct((cfg.t_all, d), F32),
        compiler_params=_cparams(("arbitrary",)),
        name="moe_combine",
    )(x, y1, y2, route, mod_l, g)


def _forward(cfg, x, c, ctx, c_ctx, ada_w, ada_b, norm1_g, norm2_g, ab_w_in, ab_w_out,
             diff_lq1, diff_lk1, diff_lq2, diff_lk2, diff_subln_g, bconv_w,
             ssd_w_in, ssd_conv_w, ssd_conv_b, ssd_A_log, ssd_dt_bias, ssd_D, ssd_norm_g, ssd_w_out,
             moe_wg, moe_bg, moe_we, moe_be, moe_w_gate, moe_w_up, moe_w_down, final_norm_g):
    d = cfg.d_model
    b = cfg.batch
    xa = jnp.concatenate([x.reshape(cfg.t_lat, d), ctx.reshape(cfg.t_ctx, d)], axis=0)
    c_all = jnp.zeros((SUBLANES_V7X, d), F32).at[:b].set(c).at[b].set(c_ctx)
    mod = _adaln(cfg, c_all, ada_w, ada_b)
    tables = _rope_tables(cfg)

    for layer in range(cfg.depth):
        i = layer // 2
        mod_l = mod[layer]
        g1 = norm1_g[layer].reshape(1, d)
        g2 = norm2_g[layer].reshape(1, d)
        router = _router_weights(moe_wg[layer], moe_bg[layer], moe_we[layer], moe_be[layer])
        if layer % 2 == 0:
            lam_init = 0.8 - 0.6 * math.exp(-0.3 * layer)
            q, k, v, gb, u = _ab_in(cfg, xa, mod_l, g1, ab_w_in[i].astype(BF16), tables)
            lvec = jnp.stack([diff_lq1[i], diff_lk1[i], diff_lq2[i], diff_lk2[i]])
            o = _attention(cfg, q, k, v, lvec, diff_subln_g[i].reshape(1, DIFF_V_DIM), lam_init)
            w_out = ab_w_out[i].astype(BF16)
            xa, h2, route = _ab_out(cfg, o, gb, u, bconv_w[i], w_out[:DIFF_WIDTH], w_out[DIFF_WIDTH:],
                                    xa, mod_l, g2, router)
        else:
            w_in = ssd_w_in[i].astype(BF16)
            n_main = SSD_D_INNER + SSD_CONV_CH
            w_dt = jnp.pad(w_in[:, n_main:], ((0, 0), (0, LANES_V7X - 2 * SSD_HEADS)))
            z, xbc_act, dt_raw = _ssd_in(cfg, xa, mod_l, g1, w_in[:, :n_main], w_dt, ssd_conv_w[i],
                                         ssd_conv_b[i].reshape(1, SSD_CONV_CH))
            y2 = _ssd_scan(cfg, xbc_act, dt_raw, ssd_dt_bias[i], ssd_A_log[i])
            d_wide = jnp.repeat(ssd_D[i], SSD_HEAD_DIM).reshape(1, SSD_D_INNER)
            xa, h2, route = _ssd_out(cfg, y2, xbc_act, z, d_wide, ssd_norm_g[i].reshape(1, SSD_D_INNER),
                                     ssd_w_out[i].astype(BF16), xa, mod_l, g2, router)
        slot_tok, block_eid, n_used, pos = _dispatch(cfg, route)
        yb = _experts(cfg, layer, block_eid, n_used, h2[slot_tok], moe_w_gate, moe_w_up, moe_w_down)
        last = layer == cfg.depth - 1
        xa = _combine(cfg, xa, yb[pos[:, 0]], yb[pos[:, 1]], route, mod_l, final_norm_g.reshape(1, d), last)
    return xa[:cfg.t_lat].reshape(b, cfg.n_lat, d)


def kernel(x, c, ctx, c_ctx, ada_w, ada_b, norm1_g, norm2_g, ab_w_in, ab_w_out, diff_lq1, diff_lk1, diff_lq2, diff_lk2, diff_subln_g, bconv_w, ssd_w_in, ssd_conv_w, ssd_conv_b, ssd_A_log, ssd_dt_bias, ssd_D, ssd_norm_g, ssd_w_out, moe_wg, moe_bg, moe_we, moe_be, moe_w_gate, moe_w_up, moe_w_down, final_norm_g):
    cfg = Cfg(batch=x.shape[0], n_lat=x.shape[1], n_ctx=ctx.shape[1], d_model=x.shape[2], depth=ada_w.shape[0],
              tm=512, tq=256, moe_tile=256, attn_subtiles=2)
    return _forward(cfg, x, c, ctx, c_ctx, ada_w, ada_b, norm1_g, norm2_g, ab_w_in, ab_w_out,
                    diff_lq1, diff_lk1, diff_lq2, diff_lk2, diff_subln_g, bconv_w,
                    ssd_w_in, ssd_conv_w, ssd_conv_b, ssd_A_log, ssd_dt_bias, ssd_D, ssd_norm_g, ssd_w_out,
                    moe_wg, moe_bg, moe_we, moe_be, moe_w_gate, moe_w_up, moe_w_down, final_norm_g)
```

```python
import functools
import math
from typing import NamedTuple

import jax
import jax.numpy as jnp
from jax import lax
from jax.experimental import pallas as pl
from jax.experimental.pallas import tpu as pltpu

F32 = jnp.float32
BF16 = jnp.bfloat16
HIGHEST = lax.Precision.HIGHEST

LANES_V7X = 128
SUBLANES_V7X = 8
VMEM_LIMIT_BYTES_V7X = 56 * 1024 * 1024

RMS_EPS = 1e-6
GRID_W = 64
N_DIFF_HEADS = 4
DIFF_QK_DIM = 64
DIFF_V_DIM = 128
DIFF_WIDTH = 512
CONV_WIDTH = 512
ROPE_BASE = 10000.0
SSD_D_INNER = 2048
SSD_HEAD_DIM = 64
SSD_HEADS = 32
SSD_GROUPS = 4
SSD_STATE = 128
SSD_CHUNK = 128
SSD_CONV_CH = SSD_D_INNER + 2 * SSD_GROUPS * SSD_STATE
MOE_GROUPS = 4
MOE_EXPERTS_PER_GROUP = 8
MOE_EXPERTS = 32
MOE_HIDDEN = 512
ROUTE_LANES = LANES_V7X


class Cfg(NamedTuple):
    batch: int
    n_lat: int
    n_ctx: int
    d_model: int
    depth: int
    tm: int
    tq: int
    moe_tile: int
    attn_subtiles: int
    kchunk: int

    @property
    def t_lat(self):
        return self.batch * self.n_lat

    @property
    def t_ctx(self):
        return self.batch * self.n_ctx

    @property
    def t_all(self):
        return self.t_lat + self.t_ctx


def _cparams(sem):
    return pltpu.CompilerParams(dimension_semantics=sem, vmem_limit_bytes=VMEM_LIMIT_BYTES_V7X)


def _silu(v):
    return v * (1.0 / (1.0 + jnp.exp(-v)))


def _const_spec(shape):
    nd = len(shape)
    return pl.BlockSpec(shape, lambda *_: (0,) * nd)


def _mod_spec(cfg):
    return pl.BlockSpec((1, 6, cfg.d_model),
                        lambda i: (jnp.minimum((i * cfg.tm) // cfg.n_lat, cfg.batch), 0, 0))


def _seq_pos(cfg, tm):
    r0 = pl.program_id(0) * tm
    row = r0 + lax.broadcasted_iota(jnp.int32, (tm, 1), 0)
    seqlen = jnp.where(r0 >= cfg.t_lat, cfg.n_ctx, cfg.n_lat)
    return row & (seqlen - 1), seqlen


def _norm_mod(x, g, shift, scale):
    ms = jnp.mean(x * x, axis=-1, keepdims=True)
    return (x * lax.rsqrt(ms + RMS_EPS) * g) * (1.0 + scale) + shift


def _adaln_kernel(c_ref, w_ref, b_ref, o_ref):
    sc = _silu(c_ref[...])
    o_ref[0] = jnp.dot(sc, w_ref[0], precision=HIGHEST, preferred_element_type=F32) + b_ref[0]


def _adaln(cfg, c_all, ada_w, ada_b):
    d = cfg.d_model
    out = pl.pallas_call(
        _adaln_kernel,
        grid=(cfg.depth, 6),
        in_specs=[_const_spec((SUBLANES_V7X, d)),
                  pl.BlockSpec((1, d, d), lambda l, j: (l, 0, j)),
                  pl.BlockSpec((1, 1, d), lambda l, j: (l, 0, j))],
        out_specs=pl.BlockSpec((1, SUBLANES_V7X, d), lambda l, j: (l, 0, j)),
        out_shape=jax.ShapeDtypeStruct((cfg.depth, SUBLANES_V7X, 6 * d), F32),
        compiler_params=_cparams(("arbitrary", "arbitrary")),
        name="adaln",
    )(c_all, ada_w, ada_b.reshape(cfg.depth, 1, 6 * d))
    return out.reshape(cfg.depth, SUBLANES_V7X, 6, d)


def _rope_tables(cfg):
    n = cfg.n_lat
    rows = n // GRID_W
    row = jnp.broadcast_to(jnp.arange(rows, dtype=F32)[:, None], (rows, GRID_W)).reshape(n)
    col = jnp.broadcast_to(jnp.arange(GRID_W, dtype=F32)[None, :], (rows, GRID_W)).reshape(n)
    axis_dim = DIFF_QK_DIM // 2
    inv_freq = ROPE_BASE ** (-jnp.arange(0, axis_dim, 2, dtype=F32) / axis_dim)
    ang_r = row[:, None] * inv_freq
    ang_c = col[:, None] * inv_freq
    zeros = jnp.zeros_like(ang_r)
    cos64 = jnp.concatenate([jnp.cos(ang_r), jnp.cos(ang_r), jnp.cos(ang_c), jnp.cos(ang_c)], axis=1)
    sa64 = jnp.concatenate([zeros, jnp.sin(ang_r), zeros, jnp.sin(ang_c)], axis=1)
    sb64 = jnp.concatenate([-jnp.sin(ang_r), zeros, -jnp.sin(ang_c), zeros], axis=1)
    ident = cfg.tm
    cos = jnp.concatenate([jnp.tile(cos64, (1, 2)), jnp.ones((ident, LANES_V7X), F32)], axis=0)
    sa = jnp.concatenate([jnp.tile(sa64, (1, 2)), jnp.zeros((ident, LANES_V7X), F32)], axis=0)
    sb = jnp.concatenate([jnp.tile(sb64, (1, 2)), jnp.zeros((ident, LANES_V7X), F32)], axis=0)
    return cos, sa, sb


def _ab_in_kernel(x_ref, mod_ref, g_ref, w_ref, cos_ref, sa_ref, sb_ref,
                  q_ref, k_ref, v_ref, gb_ref, u_ref):
    m = mod_ref[0]
    hb = _norm_mod(x_ref[...], g_ref[...], m[0:1], m[1:2]).astype(BF16)
    cos, sa, sb = cos_ref[...], sa_ref[...], sb_ref[...]

    def proj(j):
        return jnp.dot(hb, w_ref[:, j * DIFF_WIDTH:(j + 1) * DIFF_WIDTH], preferred_element_type=F32)

    def rope(t, out_ref, scale):
        for gi in range(DIFF_WIDTH // LANES_V7X):
            xg = t[:, gi * LANES_V7X:(gi + 1) * LANES_V7X]
            r = xg * cos + pltpu.roll(xg, 16, 1) * sa + pltpu.roll(xg, LANES_V7X - 16, 1) * sb
            out_ref[:, gi * LANES_V7X:(gi + 1) * LANES_V7X] = (r * scale).astype(out_ref.dtype)

    rope(proj(0), q_ref, math.log2(math.e) * DIFF_QK_DIM ** -0.5)
    rope(proj(1), k_ref, 1.0)
    v_ref[...] = proj(2).astype(v_ref.dtype)
    gb_ref[...] = proj(3)
    u_ref[...] = proj(4) * proj(5)


def _ab_in(cfg, x, mod_l, g1, w_in_bf16, tables):
    t, d, tm = cfg.t_all, cfg.d_model, cfg.tm
    n_lat_tiles, per_seq = cfg.t_lat // tm, cfg.n_lat // tm
    tab_spec = pl.BlockSpec((tm, LANES_V7X), lambda i: (jnp.where(i < n_lat_tiles, i % per_seq, per_seq), 0))
    row = lambda w: pl.BlockSpec((tm, w), lambda i: (i, 0))
    return pl.pallas_call(
        _ab_in_kernel,
        grid=(t // tm,),
        in_specs=[row(d), _mod_spec(cfg), _const_spec((1, d)), _const_spec(w_in_bf16.shape),
                  tab_spec, tab_spec, tab_spec],
        out_specs=[row(DIFF_WIDTH)] * 5,
        out_shape=[jax.ShapeDtypeStruct((t, DIFF_WIDTH), BF16)] * 3
        + [jax.ShapeDtypeStruct((t, CONV_WIDTH), F32)] * 2,
        compiler_params=_cparams(("arbitrary",)),
        name="ab_in",
    )(x, mod_l, g1, w_in_bf16, *tables)


def _attn_body(lvec_ref, q_ref, kc_ref, kl_ref, vc_ref, vl_ref, g_ref, o_ref, *, tq, lam_init, kchunk):
    lv = lvec_ref[...]
    lam = (jnp.exp(jnp.sum(lv[0:1] * lv[1:2], axis=-1, keepdims=True))
           - jnp.exp(jnp.sum(lv[2:3] * lv[3:4], axis=-1, keepdims=True)) + lam_init)
    nt = (((1,), (1,)), ((), ()))
    half = DIFF_QK_DIM
    chunks = [(kc_ref, vc_ref, 0, kc_ref.shape[0])]
    if kl_ref is not None:
        chunks += [(kl_ref, vl_ref, r0, kchunk) for r0 in range(0, kl_ref.shape[0], kchunk)]
    for j in range(q_ref.shape[0] // tq):
        q = q_ref[j * tq:(j + 1) * tq, :].astype(F32)
        lane = lax.broadcasted_iota(jnp.int32, q.shape, 1)
        q2 = jnp.concatenate([jnp.where(lane < half, q, 0.0), jnp.where(lane >= half, q, 0.0)],
                             axis=0).astype(BF16)
        ps, ms, ls = [], [], []
        for k_ref, _, r0, n in chunks:
            s = lax.dot_general(q2, k_ref[r0:r0 + n, :], nt, preferred_element_type=F32)
            m_c = jnp.max(s, axis=-1, keepdims=True)
            p = jnp.exp2(s - m_c)
            ps.append(p)
            ms.append(m_c)
            ls.append(jnp.sum(p, axis=-1, keepdims=True))
        mx = functools.reduce(jnp.maximum, ms)
        scale = [jnp.exp2(m_c - mx) for m_c in ms]
        r = 1.0 / functools.reduce(jnp.add, [sc * l_c for sc, l_c in zip(scale, ls)])
        pv = None
        for p, sc, (_, v_ref, r0, n) in zip(ps, scale, chunks):
            w = sc * r
            pd = (p[:tq] * w[:tq] - p[tq:] * (lam * w[tq:])).astype(BF16)
            part = jnp.dot(pd, v_ref[r0:r0 + n, :], preferred_element_type=F32)
            pv = part if pv is None else pv + part
        o = pv * lax.rsqrt(jnp.mean(pv * pv, axis=-1, keepdims=True) + RMS_EPS)
        o_ref[j * tq:(j + 1) * tq, :] = ((o * g_ref[...]) * (1.0 - lam_init)).astype(o_ref.dtype)


def _attn_lat_kernel(lvec_ref, q_ref, kc_ref, kl_ref, vc_ref, vl_ref, g_ref, o_ref, **kw):
    _attn_body(lvec_ref, q_ref, kc_ref, kl_ref, vc_ref, vl_ref, g_ref, o_ref, **kw)


def _attn_ctx_kernel(lvec_ref, q_ref, kc_ref, vc_ref, g_ref, o_ref, **kw):
    _attn_body(lvec_ref, q_ref, kc_ref, None, vc_ref, None, g_ref, o_ref, **kw)


def _attention(cfg, q, k, v, lvec, subln_g, lam_init):
    tq, nsub = cfg.tq, cfg.attn_subtiles
    tstep = tq * nsub
    nqb = cfg.n_lat // tstep
    ctx_blk0 = cfg.t_lat // cfg.n_ctx
    hw = DIFF_V_DIM
    ctx_spec = pl.BlockSpec((cfg.n_ctx, hw), lambda b, h, *_: (ctx_blk0 + b, h))
    lat_spec = pl.BlockSpec((cfg.n_lat, hw), lambda b, h, *_: (b, h))
    q_spec = pl.BlockSpec((tstep, hw), lambda b, h, i: (b * nqb + i, h))
    kw = dict(tq=tq, lam_init=lam_init, kchunk=min(cfg.kchunk, cfg.n_lat))
    o_lat = pl.pallas_call(
        functools.partial(_attn_lat_kernel, **kw),
        grid=(cfg.batch, N_DIFF_HEADS, nqb),
        in_specs=[_const_spec(lvec.shape), q_spec, ctx_spec, lat_spec, ctx_spec, lat_spec, _const_spec((1, hw))],
        out_specs=q_spec,
        out_shape=jax.ShapeDtypeStruct((cfg.t_lat, DIFF_WIDTH), BF16),
        compiler_params=_cparams(("arbitrary", "arbitrary", "arbitrary")),
        name="diff_attn",
    )(lvec, q, k, k, v, v, subln_g)
    tqc = min(tq, cfg.n_ctx)
    o_ctx = pl.pallas_call(
        functools.partial(_attn_ctx_kernel, tq=tqc, lam_init=lam_init, kchunk=cfg.n_ctx),
        grid=(cfg.batch, N_DIFF_HEADS),
        in_specs=[_const_spec(lvec.shape), ctx_spec, ctx_spec, ctx_spec, _const_spec((1, hw))],
        out_specs=pl.BlockSpec((cfg.n_ctx, hw), lambda b, h: (b, h)),
        out_shape=jax.ShapeDtypeStruct((cfg.t_ctx, DIFF_WIDTH), BF16),
        compiler_params=_cparams(("arbitrary", "arbitrary")),
        name="diff_attn_ctx",
    )(lvec, q, k, v, subln_g)
    return jnp.concatenate([o_lat, o_ctx], axis=0)


def _route_from_logits(lg):
    lane = lax.broadcasted_iota(jnp.int32, lg.shape, 1)
    neg = -jnp.inf
    big = jnp.int32(ROUTE_LANES)
    gl = jnp.where(lane < MOE_GROUPS, lg, neg)
    gmax = jnp.max(gl, axis=-1, keepdims=True)
    gidx = jnp.min(jnp.where(gl == gmax, lane, big), axis=-1, keepdims=True)
    gw = 1.0 / jnp.sum(jnp.exp(gl - gmax), axis=-1, keepdims=True)
    lo = MOE_GROUPS + gidx * MOE_EXPERTS_PER_GROUP
    el = jnp.where((lane >= lo) & (lane < lo + MOE_EXPERTS_PER_GROUP), lg, neg)
    m1 = jnp.max(el, axis=-1, keepdims=True)
    i1 = jnp.min(jnp.where(el == m1, lane, big), axis=-1, keepdims=True)
    el2 = jnp.where(lane == i1, neg, el)
    m2 = jnp.max(el2, axis=-1, keepdims=True)
    i2 = jnp.min(jnp.where(el2 == m2, lane, big), axis=-1, keepdims=True)
    w1 = gw / (1.0 + jnp.exp(m2 - m1))
    w2 = gw - w1
    e1 = (i1 - MOE_GROUPS).astype(F32)
    e2 = (i2 - MOE_GROUPS).astype(F32)
    return jnp.where(lane == 0, e1, jnp.where(lane == 1, e2, jnp.where(lane == 2, w1, jnp.where(lane == 3, w2, 0.0))))


def _pack_bf16_pairs(v):
    w = v.shape[1] // 2
    lo = pltpu.bitcast(v[:, :w].astype(BF16).astype(F32), jnp.uint32)
    hi = pltpu.bitcast(v[:, w:].astype(BF16).astype(F32), jnp.uint32)
    return pltpu.bitcast((hi & jnp.uint32(0xFFFF0000)) | (lo >> 16), F32)


def _unpack_bf16_pairs(words):
    u = pltpu.bitcast(words, jnp.uint32)
    lo = pltpu.bitcast(u << 16, F32)
    hi = pltpu.bitcast(u & jnp.uint32(0xFFFF0000), F32)
    return lo, hi


def _router_weights(wg, bg, we, be):
    d = wg.shape[0]
    pad = ROUTE_LANES - MOE_GROUPS - MOE_EXPERTS
    wr = jnp.concatenate([wg, we, jnp.zeros((d, pad), F32)], axis=1)
    hi = wr.astype(BF16)
    lo = (wr - hi.astype(F32)).astype(BF16)
    br = jnp.concatenate([bg, be, jnp.zeros((pad,), F32)]).reshape(1, ROUTE_LANES)
    return jnp.concatenate([hi, lo], axis=1), hi, br


def _residual_norm_route(x, mix, m, g2, wr2_ref, wrh_ref, br_ref, xo_ref, h2_ref, rt_ref):
    xn = x + m[2:3] * mix
    xo_ref[...] = xn
    h2 = _norm_mod(xn, g2, m[3:4], m[4:5])
    h2_ref[...] = _pack_bf16_pairs(h2)
    hh = h2.astype(BF16)
    hl = (h2 - hh.astype(F32)).astype(BF16)
    a = jnp.dot(hh, wr2_ref[...], preferred_element_type=F32)
    lg = (a[:, :ROUTE_LANES] + a[:, ROUTE_LANES:]) + jnp.dot(hl, wrh_ref[...], preferred_element_type=F32)
    rt_ref[...] = _route_from_logits(lg + br_ref[...])


def _epilogue_specs(cfg):
    d, tm = cfg.d_model, cfg.tm
    row = lambda w: pl.BlockSpec((tm, w), lambda i: (i, 0))
    in_specs = [row(d), _mod_spec(cfg), _const_spec((1, d)), _const_spec((d, 2 * ROUTE_LANES)),
                _const_spec((d, ROUTE_LANES)), _const_spec((1, ROUTE_LANES))]
    out_specs = [row(d), row(d // 2), row(ROUTE_LANES)]
    out_shape = [jax.ShapeDtypeStruct((cfg.t_all, d), F32), jax.ShapeDtypeStruct((cfg.t_all, d // 2), F32),
                 jax.ShapeDtypeStruct((cfg.t_all, ROUTE_LANES), F32)]
    return in_specs, out_specs, out_shape


def _halo_specs(cfg, width, col_block=0):
    per = cfg.tm // SUBLANES_V7X
    last = cfg.t_all // SUBLANES_V7X - 1
    prev = pl.BlockSpec((SUBLANES_V7X, width), lambda i, *_: (jnp.maximum(i * per - 1, 0), col_block))
    nxt = pl.BlockSpec((SUBLANES_V7X, width), lambda i, *_: (jnp.minimum((i + 1) * per, last), col_block))
    return prev, nxt


def _shifted_rows(pad_ref, u, prev_blk, next_blk, pos, seqlen):
    tm = u.shape[0]
    s = SUBLANES_V7X
    pad_ref[s:s + tm, :] = u
    pad_ref[0:s, :] = prev_blk
    pad_ref[s + tm:2 * s + tm, :] = next_blk
    um1 = jnp.where(pos == 0, 0.0, pad_ref[s - 1:s - 1 + tm, :])
    up1 = jnp.where(pos == seqlen - 1, 0.0, pad_ref[s + 1:s + 1 + tm, :])
    return um1, up1


def _ab_out_kernel(o_ref, gb_ref, u_ref, up_ref, un_ref, cw_ref, wo_ref, wc_ref,
                   x_ref, mod_ref, g2_ref, wr2_ref, wrh_ref, br_ref,
                   xo_ref, h2_ref, rt_ref, pad_ref, *, cfg):
    pos, seqlen = _seq_pos(cfg, cfg.tm)
    u = u_ref[...]
    um1, up1 = _shifted_rows(pad_ref, u, up_ref[...], un_ref[...], pos, seqlen)
    cw = cw_ref[...]
    conv = gb_ref[...] * (um1 * cw[0:1] + u * cw[1:2] + up1 * cw[2:3])
    mix = (jnp.dot(o_ref[...], wo_ref[...], preferred_element_type=F32)
           + jnp.dot(conv.astype(BF16), wc_ref[...], preferred_element_type=F32))
    _residual_norm_route(x_ref[...], mix, mod_ref[0], g2_ref[...], wr2_ref, wrh_ref, br_ref,
                         xo_ref, h2_ref, rt_ref)


def _ab_out(cfg, o, gb, u, conv_w, wo, wc, x, mod_l, g2, router):
    tm = cfg.tm
    row = lambda w: pl.BlockSpec((tm, w), lambda i: (i, 0))
    prev, nxt = _halo_specs(cfg, CONV_WIDTH)
    ep_in, ep_out, ep_shape = _epilogue_specs(cfg)
    return pl.pallas_call(
        functools.partial(_ab_out_kernel, cfg=cfg),
        grid=(cfg.t_all // tm,),
        in_specs=[row(DIFF_WIDTH), row(CONV_WIDTH), row(CONV_WIDTH), prev, nxt, _const_spec(conv_w.shape),
                  _const_spec(wo.shape), _const_spec(wc.shape)] + ep_in,
        out_specs=ep_out,
        out_shape=ep_shape,
        scratch_shapes=[pltpu.VMEM((tm + 2 * SUBLANES_V7X, CONV_WIDTH), F32)],
        compiler_params=_cparams(("arbitrary",)),
        name="ab_out",
    )(o, gb, u, u, u, conv_w, wo, wc, x, mod_l, g2, *router)


def _ssd_in_kernel(x_ref, xp_ref, xn_ref, mod_ref, g_ref, w_ref, wdt_ref, cw_ref, cb_ref,
                   z_ref, xbc_ref, dt_ref, pad_ref, *, cfg):
    tm = cfg.tm
    s = SUBLANES_V7X
    m = mod_ref[0]
    g = g_ref[...]
    h = _norm_mod(x_ref[...], g, m[0:1], m[1:2])
    hb = h.astype(BF16)
    h_ext = jnp.concatenate([_norm_mod(xp_ref[...], g, m[0:1], m[1:2]), h,
                             _norm_mod(xn_ref[...], g, m[0:1], m[1:2])], axis=0).astype(BF16)
    pos, seqlen = _seq_pos(cfg, tm)
    cw = 512
    for j in range(SSD_D_INNER // cw):
        z_ref[:, j * cw:(j + 1) * cw] = jnp.dot(hb, w_ref[:, j * cw:(j + 1) * cw],
                                                preferred_element_type=F32).astype(z_ref.dtype)
    for j in range(SSD_CONV_CH // cw):
        c0 = SSD_D_INNER + j * cw
        pad = pad_ref.at[j]
        pad[...] = jnp.dot(h_ext, w_ref[:, c0:c0 + cw], preferred_element_type=F32)
        taps = cw_ref[:, j * cw:(j + 1) * cw]
        um1 = jnp.where(pos == 0, 0.0, pad[s - 1:s - 1 + tm, :])
        up1 = jnp.where(pos == seqlen - 1, 0.0, pad[s + 1:s + 1 + tm, :])
        conv = um1 * taps[0:1] + pad[s:s + tm, :] * taps[1:2] + up1 * taps[2:3] + cb_ref[:, j * cw:(j + 1) * cw]
        xbc_ref[:, j * cw:(j + 1) * cw] = _silu(conv).astype(xbc_ref.dtype)
    dt_ref[...] = jnp.dot(hb, wdt_ref[...], preferred_element_type=F32)


def _ssd_in(cfg, x, mod_l, g1, w_main, w_dt, conv_w, conv_b):
    t, d, tm = cfg.t_all, cfg.d_model, cfg.tm
    row = lambda w: pl.BlockSpec((tm, w), lambda i: (i, 0))
    prev, nxt = _halo_specs(cfg, d)
    return pl.pallas_call(
        functools.partial(_ssd_in_kernel, cfg=cfg),
        grid=(t // tm,),
        in_specs=[row(d), prev, nxt, _mod_spec(cfg), _const_spec((1, d)), _const_spec(w_main.shape),
                  _const_spec(w_dt.shape), _const_spec(conv_w.shape), _const_spec(conv_b.shape)],
        out_specs=[row(SSD_D_INNER), row(SSD_CONV_CH), row(LANES_V7X)],
        out_shape=[jax.ShapeDtypeStruct((t, SSD_D_INNER), BF16), jax.ShapeDtypeStruct((t, SSD_CONV_CH), BF16),
                   jax.ShapeDtypeStruct((t, LANES_V7X), F32)],
        scratch_shapes=[pltpu.VMEM((SSD_CONV_CH // 512, tm + 2 * SUBLANES_V7X, 512), F32)],
        compiler_params=_cparams(("arbitrary",)),
        name="ssd_in",
    )(x, x, x, mod_l, g1, w_main, w_dt, conv_w, conv_b)


def _split3(v):
    b1 = v.astype(BF16)
    r1 = v - b1.astype(F32)
    b2 = r1.astype(BF16)
    b3 = (r1 - b2.astype(F32)).astype(BF16)
    return jnp.concatenate([b1, b2, b3], axis=1)


def _softplus(v):
    return jnp.maximum(v, 0.0) + jnp.log(1.0 + jnp.exp(-jnp.abs(v)))


def _ssd_scan_kernel(xs_ref, b_ref, c_ref, dt_ref, dtt_ref, bias_ref, biast_ref, alog_ref, alogt_ref,
                     ewide_ref, y_ref, state_ref):
    q = SSD_CHUNK
    d = pl.program_id(1)
    fwd = d == 0

    @pl.when(pl.program_id(2) == 0)
    def _():
        state_ref[...] = jnp.zeros_like(state_ref)

    dt = _softplus(dt_ref[0] + bias_ref[0])
    dtt = _softplus(dtt_ref[0] + biast_ref[0])
    a = dt * (-jnp.exp(alog_ref[0]))
    at = dtt * (-jnp.exp(alogt_ref[0]))

    ri = lax.broadcasted_iota(jnp.int32, (q, q), 0)
    ci = lax.broadcasted_iota(jnp.int32, (q, q), 1)
    keep = (ri - ci) * jnp.where(fwd, 1, -1) >= 0
    tri = jnp.where(keep, 1.0, 0.0)
    cs = jnp.dot(tri, a, precision=HIGHEST, preferred_element_type=F32)
    cst = lax.dot_general(at, tri, (((1,), (1,)), ((), ())), precision=HIGHEST,
                          preferred_element_type=F32)
    tot = jnp.sum(a, axis=0, keepdims=True)

    ewide = ewide_ref[...]
    w_wide = jnp.dot(_split3(dt * jnp.exp(tot - cs)), ewide, preferred_element_type=F32)
    ecs_wide = jnp.dot(_split3(jnp.exp(cs)), ewide, preferred_element_type=F32)
    dec_wide = jnp.dot(_split3(jnp.broadcast_to(jnp.exp(tot), (SUBLANES_V7X, SSD_HEADS))), ewide,
                       preferred_element_type=F32)[0:1]

    lane = lax.broadcasted_iota(jnp.int32, (q, LANES_V7X), 1)
    lo = lane < SSD_HEAD_DIM
    hpg = SSD_HEADS // SSD_GROUPS
    gw = hpg * SSD_HEAD_DIM
    nt = (((1,), (1,)), ((), ()))
    tn = (((0,), (0,)), ((), ()))
    for g in range(SSD_GROUPS):
        bm = b_ref[:, g * SSD_STATE:(g + 1) * SSD_STATE]
        cm = c_ref[:, g * SSD_STATE:(g + 1) * SSD_STATE]
        cb = lax.dot_general(cm, bm, nt, preferred_element_type=F32)
        sl = slice(g * gw, (g + 1) * gw)
        y_off = jnp.dot(cm, state_ref[:, sl].astype(BF16), preferred_element_type=F32) * ecs_wide[:, sl]
        for pr in range(hpg // 2):
            lhs, rhs = [], []
            c0 = g * gw + pr * LANES_V7X
            xs_pair = xs_ref[:, c0:c0 + LANES_V7X]
            zero = jnp.zeros_like(xs_pair)
            for k in range(2):
                h = g * hpg + pr * 2 + k
                seg = cs[:, h:h + 1] - cst[h:h + 1, :]
                lmat = jnp.exp(jnp.where(keep, seg, -jnp.inf))
                lhs.append((cb * lmat * dtt[h:h + 1, :]).astype(BF16))
                rhs.append(jnp.where(lo if k == 0 else jnp.logical_not(lo), xs_pair, zero))
            y = jnp.dot(jnp.concatenate(lhs, axis=1), jnp.concatenate(rhs, axis=0), preferred_element_type=F32)
            y_ref[0, :, c0:c0 + LANES_V7X] = (y + y_off[:, pr * LANES_V7X:(pr + 1) * LANES_V7X]).astype(y_ref.dtype)
        x2 = (xs_ref[:, sl].astype(F32) * w_wide[:, sl]).astype(BF16)
        upd = lax.dot_general(bm, x2, tn, preferred_element_type=F32)
        state_ref[:, sl] = state_ref[:, sl] * dec_wide[:, sl] + upd


def _ssd_scan(cfg, xbc_act, dt_raw, dt_bias, a_log):
    q = SSD_CHUNK
    n_cc, n_lc = cfg.n_ctx // q, cfg.n_lat // q
    ctx0 = cfg.t_lat // q
    h = SSD_HEADS
    dt2 = dt_raw[:, :2 * h].reshape(cfg.t_all, 2, h).transpose(1, 0, 2)
    dt2t = dt2.transpose(0, 2, 1)
    bias = dt_bias.reshape(2, 1, h)
    biast = dt_bias.reshape(2, h, 1)
    alog = a_log.reshape(2, 1, h)
    alogt = a_log.reshape(2, h, 1)
    head_of_row = jnp.tile(jnp.arange(h), 3)[:, None]
    ewide = (head_of_row == (jnp.arange(SSD_D_INNER) // SSD_HEAD_DIM)[None, :]).astype(BF16)

    def blk(b, d, s):
        cs = jnp.where(d == 0, s, n_cc - 1 - s)
        ls = jnp.where(d == 0, s - n_cc, n_lc - 1 - (s - n_cc))
        return jnp.where(s < n_cc, ctx0 + b * n_cc + cs, b * n_lc + ls)

    gs = SSD_GROUPS * SSD_STATE
    col = lambda w, cblk: pl.BlockSpec((q, w), lambda b, d, s: (blk(b, d, s), cblk))
    per_dir = lambda shape: pl.BlockSpec((1,) + shape, lambda b, d, s: (d, 0, 0))
    return pl.pallas_call(
        _ssd_scan_kernel,
        grid=(cfg.batch, 2, n_cc + n_lc),
        in_specs=[col(SSD_D_INNER, 0), col(gs, SSD_D_INNER // gs), col(gs, SSD_D_INNER // gs + 1),
                  pl.BlockSpec((1, q, h), lambda b, d, s: (d, blk(b, d, s), 0)),
                  pl.BlockSpec((1, h, q), lambda b, d, s: (d, 0, blk(b, d, s))),
                  per_dir((1, h)), per_dir((h, 1)), per_dir((1, h)), per_dir((h, 1)),
                  _const_spec(ewide.shape)],
        out_specs=pl.BlockSpec((1, q, SSD_D_INNER), lambda b, d, s: (d, blk(b, d, s), 0)),
        out_shape=jax.ShapeDtypeStruct((2, cfg.t_all, SSD_D_INNER), BF16),
        scratch_shapes=[pltpu.VMEM((SSD_STATE, SSD_D_INNER), F32)],
        compiler_params=_cparams(("arbitrary", "arbitrary", "arbitrary")),
        name="ssd_scan",
    )(xbc_act, xbc_act, xbc_act, dt2, dt2t, bias, biast, alog, alogt, ewide)


def _ssd_out_kernel(yf_ref, yb_ref, xs_ref, z_ref, dw_ref, ng_ref, wo_ref,
                    x_ref, mod_ref, g2_ref, wr2_ref, wrh_ref, br_ref, xo_ref, h2_ref, rt_ref):
    gw = SSD_D_INNER // SSD_GROUPS
    mix = None
    for g in range(SSD_GROUPS):
        sl = slice(g * gw, (g + 1) * gw)
        y = yf_ref[0, :, sl].astype(F32) + yb_ref[0, :, sl].astype(F32) + xs_ref[:, sl].astype(F32) * dw_ref[:, sl]
        y = y * _silu(z_ref[:, sl].astype(F32))
        y = (y * lax.rsqrt(jnp.mean(y * y, axis=-1, keepdims=True) + RMS_EPS)) * ng_ref[:, sl]
        part = jnp.dot(y.astype(BF16), wo_ref[sl, :], preferred_element_type=F32)
        mix = part if mix is None else mix + part
    _residual_norm_route(x_ref[...], mix, mod_ref[0], g2_ref[...], wr2_ref, wrh_ref, br_ref,
                         xo_ref, h2_ref, rt_ref)


def _ssd_out(cfg, y2, xbc_act, z, d_wide, norm_g, wo, x, mod_l, g2, router):
    tm = cfg.tm
    row = lambda w: pl.BlockSpec((tm, w), lambda i: (i, 0))
    ydir = lambda d: pl.BlockSpec((1, tm, SSD_D_INNER), lambda i: (d, i, 0))
    ep_in, ep_out, ep_shape = _epilogue_specs(cfg)
    return pl.pallas_call(
        _ssd_out_kernel,
        grid=(cfg.t_all // tm,),
        in_specs=[ydir(0), ydir(1), row(SSD_D_INNER), row(SSD_D_INNER), _const_spec((1, SSD_D_INNER)),
                  _const_spec((1, SSD_D_INNER)), _const_spec(wo.shape)] + ep_in,
        out_specs=ep_out,
        out_shape=ep_shape,
        compiler_params=_cparams(("arbitrary",)),
        name="ssd_out",
    )(y2, y2, xbc_act, z, d_wide, norm_g, wo, x, mod_l, g2, *router)


def _expert_kernel(eid_ref, nused_ref, xb_ref, wg_ref, wu_ref, wd_ref, y_ref, wg_s, wu_s, wd_s):
    i = pl.program_id(0)
    changed = jnp.logical_or(i == 0, eid_ref[i] != eid_ref[jnp.maximum(i - 1, 0)])

    @pl.when(jnp.logical_and(changed, i < nused_ref[0]))
    def _():
        wg_s[...] = wg_ref[0, 0].astype(BF16)
        wu_s[...] = wu_ref[0, 0].astype(BF16)
        wd_s[...] = wd_ref[0, 0].astype(BF16)

    @pl.when(i < nused_ref[0])
    def _():
        half = wg_s.shape[0] // 2
        lo, hi = _unpack_bf16_pairs(xb_ref[...])
        lo, hi = lo.astype(BF16), hi.astype(BF16)
        hg = (jnp.dot(lo, wg_s[:half, :], preferred_element_type=F32)
              + jnp.dot(hi, wg_s[half:, :], preferred_element_type=F32))
        hu = (jnp.dot(lo, wu_s[:half, :], preferred_element_type=F32)
              + jnp.dot(hi, wu_s[half:, :], preferred_element_type=F32))
        y = jnp.dot((_silu(hg) * hu).astype(BF16), wd_s[...], preferred_element_type=F32)
        y_ref[...] = _pack_bf16_pairs(y)

    @pl.when(i >= nused_ref[0])
    def _():
        y_ref[...] = jnp.zeros_like(y_ref)


def _experts(cfg, layer, block_eid, n_used, xb, w_gate, w_up, w_down):
    d, tile = cfg.d_model, cfg.moe_tile
    n_blocks = xb.shape[0] // tile
    grid_spec = pltpu.PrefetchScalarGridSpec(
        num_scalar_prefetch=2,
        grid=(n_blocks,),
        in_specs=[pl.BlockSpec((tile, d // 2), lambda i, e, n: (i, 0)),
                  pl.BlockSpec((1, 1, d, MOE_HIDDEN), lambda i, e, n: (layer, e[i], 0, 0)),
                  pl.BlockSpec((1, 1, d, MOE_HIDDEN), lambda i, e, n: (layer, e[i], 0, 0)),
                  pl.BlockSpec((1, 1, MOE_HIDDEN, d), lambda i, e, n: (layer, e[i], 0, 0))],
        out_specs=pl.BlockSpec((tile, d // 2), lambda i, e, n: (i, 0)),
        scratch_shapes=[pltpu.VMEM((d, MOE_HIDDEN), BF16), pltpu.VMEM((d, MOE_HIDDEN), BF16),
                        pltpu.VMEM((MOE_HIDDEN, d), BF16)],
    )
    return pl.pallas_call(
        _expert_kernel,
        grid_spec=grid_spec,
        out_shape=jax.ShapeDtypeStruct((n_blocks * tile, d // 2), F32),
        compiler_params=_cparams(("arbitrary",)),
        name="moe_experts",
    )(block_eid, n_used, xb, w_gate, w_up, w_down)


def _dispatch_kernel(rt_ref, pos_ref, cnt_ref, carry_ref, start_ref, *, tile):
    p, i = pl.program_id(0), pl.program_id(1)
    tm = rt_ref.shape[0]
    rt = rt_ref[...]
    lane = lax.broadcasted_iota(jnp.int32, rt.shape, 1)
    o1 = jnp.where(lane == rt[:, 0:1].astype(jnp.int32), 1.0, 0.0)
    o2 = jnp.where(lane == rt[:, 1:2].astype(jnp.int32), 1.0, 0.0)
    cnt1 = jnp.sum(o1, axis=0, keepdims=True)
    cnt2 = jnp.sum(o2, axis=0, keepdims=True)

    @pl.when(jnp.logical_and(p == 0, i == 0))
    def _():
        cnt_ref[...] = jnp.zeros_like(cnt_ref)

    @pl.when(p == 0)
    def _():
        cnt_ref[...] += cnt1 + cnt2

    @pl.when(jnp.logical_and(p == 1, i == 0))
    def _():
        padded = jnp.floor((cnt_ref[...] + (tile - 1)) * (1.0 / tile)) * tile
        r = lax.broadcasted_iota(jnp.int32, (ROUTE_LANES, ROUTE_LANES), 0)
        c = lax.broadcasted_iota(jnp.int32, (ROUTE_LANES, ROUTE_LANES), 1)
        excl = jnp.where(r < c, 1.0, 0.0)
        start_ref[...] = jnp.dot(padded, excl, precision=HIGHEST, preferred_element_type=F32)
        carry_ref[...] = jnp.zeros_like(carry_ref)

    @pl.when(p == 1)
    def _():
        ri = lax.broadcasted_iota(jnp.int32, (tm, tm), 0)
        ci = lax.broadcasted_iota(jnp.int32, (tm, tm), 1)
        earlier = jnp.where(ci < ri, 1.0, 0.0).astype(BF16)
        p1 = jnp.dot(earlier, o1.astype(BF16), preferred_element_type=F32)
        p2 = jnp.dot(earlier, o2.astype(BF16), preferred_element_type=F32)
        base = start_ref[0:1] + carry_ref[0:1]
        pos1 = jnp.sum(o1 * (base + p1), axis=-1, keepdims=True)
        pos2 = jnp.sum(o2 * (base + cnt1 + p2), axis=-1, keepdims=True)
        pos_ref[...] = jnp.where(lane == 0, pos1, jnp.where(lane == 1, pos2, 0.0)).astype(jnp.int32)
        carry_ref[...] += cnt1 + cnt2


def _dispatch(cfg, route):
    tile, tm = cfg.moe_tile, cfg.tm
    t = cfg.t_all
    a_total = 2 * t
    pos, cnt = pl.pallas_call(
        functools.partial(_dispatch_kernel, tile=tile),
        grid=(2, t // tm),
        in_specs=[pl.BlockSpec((tm, ROUTE_LANES), lambda p, i: (i, 0))],
        out_specs=[pl.BlockSpec((tm, ROUTE_LANES), lambda p, i: (i * p, 0)),
                   pl.BlockSpec((SUBLANES_V7X, ROUTE_LANES), lambda p, i: (0, 0))],
        out_shape=[jax.ShapeDtypeStruct((t, ROUTE_LANES), jnp.int32),
                   jax.ShapeDtypeStruct((SUBLANES_V7X, ROUTE_LANES), F32)],
        scratch_shapes=[pltpu.VMEM((SUBLANES_V7X, ROUTE_LANES), F32), pltpu.VMEM((SUBLANES_V7X, ROUTE_LANES), F32)],
        compiler_params=_cparams(("arbitrary", "arbitrary")),
        name="moe_dispatch",
    )(route)
    counts = cnt[0, :MOE_EXPERTS].astype(jnp.int32)
    pend = jnp.cumsum((counts + tile - 1) // tile * tile)
    n_blocks = (a_total + MOE_EXPERTS * (tile - 1)) // tile
    block_eid = jnp.minimum(jnp.sum(pend[None, :] <= (jnp.arange(n_blocks, dtype=jnp.int32) * tile)[:, None], axis=1),
                            MOE_EXPERTS - 1).astype(jnp.int32)
    n_used = (pend[-1] // tile).astype(jnp.int32).reshape(1)
    pos2 = pos[:, 0:2]
    slot_tok = (jnp.arange(n_blocks * tile, dtype=jnp.int32) % t).at[pos2.reshape(-1)].set(
        jnp.arange(a_total, dtype=jnp.int32) // 2, unique_indices=True, mode="promise_in_bounds")
    return slot_tok, block_eid, n_used, pos2


def _combine_kernel(x_ref, y1_ref, y2_ref, rt_ref, mod_ref, g_ref, o_ref, *, final):
    rt = rt_ref[...]
    a_lo, a_hi = _unpack_bf16_pairs(y1_ref[...])
    b_lo, b_hi = _unpack_bf16_pairs(y2_ref[...])
    w1, w2 = rt[:, 2:3], rt[:, 3:4]
    f = jnp.concatenate([w1 * a_lo + w2 * b_lo, w1 * a_hi + w2 * b_hi], axis=1)
    xn = x_ref[...] + mod_ref[0][5:6] * f
    if final:
        xn = (xn * lax.rsqrt(jnp.mean(xn * xn, axis=-1, keepdims=True) + RMS_EPS)) * g_ref[...]
    o_ref[...] = xn


def _combine(cfg, x, y1, y2, route, mod_l, g, final):
    d, tm = cfg.d_model, cfg.tm
    row = lambda w: pl.BlockSpec((tm, w), lambda i: (i, 0))
    return pl.pallas_call(
        functools.partial(_combine_kernel, final=final),
        grid=(cfg.t_all // tm,),
        in_specs=[row(d), row(d // 2), row(d // 2), row(ROUTE_LANES), _mod_spec(cfg), _const_spec((1, d))],
        out_specs=row(d),
        out_shape=jax.ShapeDtypeStruct((cfg.t_all, d), F32),
        compiler_params=_cparams(("arbitrary",)),
        name="moe_combine",
    )(x, y1, y2, route, mod_l, g)


def _forward(cfg, x, c, ctx, c_ctx, ada_w, ada_b, norm1_g, norm2_g, ab_w_in, ab_w_out,
             diff_lq1, diff_lk1, diff_lq2, diff_lk2, diff_subln_g, bconv_w,
             ssd_w_in, ssd_conv_w, ssd_conv_b, ssd_A_log, ssd_dt_bias, ssd_D, ssd_norm_g, ssd_w_out,
             moe_wg, moe_bg, moe_we, moe_be, moe_w_gate, moe_w_up, moe_w_down, final_norm_g):
    d = cfg.d_model
    b = cfg.batch
    xa = jnp.concatenate([x.reshape(cfg.t_lat, d), ctx.reshape(cfg.t_ctx, d)], axis=0)
    c_all = jnp.zeros((SUBLANES_V7X, d), F32).at[:b].set(c).at[b].set(c_ctx)
    mod = _adaln(cfg, c_all, ada_w, ada_b)
    tables = _rope_tables(cfg)

    for layer in range(cfg.depth):
        i = layer // 2
        mod_l = mod[layer]
        g1 = norm1_g[layer].reshape(1, d)
        g2 = norm2_g[layer].reshape(1, d)
        router = _router_weights(moe_wg[layer], moe_bg[layer], moe_we[layer], moe_be[layer])
        if layer % 2 == 0:
            lam_init = 0.8 - 0.6 * math.exp(-0.3 * layer)
            q, k, v, gb, u = _ab_in(cfg, xa, mod_l, g1, ab_w_in[i].astype(BF16), tables)
            lvec = jnp.stack([diff_lq1[i], diff_lk1[i], diff_lq2[i], diff_lk2[i]])
            o = _attention(cfg, q, k, v, lvec, diff_subln_g[i].reshape(1, DIFF_V_DIM), lam_init)
            w_out = ab_w_out[i].astype(BF16)
            xa, h2, route = _ab_out(cfg, o, gb, u, bconv_w[i], w_out[:DIFF_WIDTH], w_out[DIFF_WIDTH:],
                                    xa, mod_l, g2, router)
        else:
            w_in = ssd_w_in[i].astype(BF16)
            n_main = SSD_D_INNER + SSD_CONV_CH
            w_dt = jnp.pad(w_in[:, n_main:], ((0, 0), (0, LANES_V7X - 2 * SSD_HEADS)))
            z, xbc_act, dt_raw = _ssd_in(cfg, xa, mod_l, g1, w_in[:, :n_main], w_dt, ssd_conv_w[i],
                                         ssd_conv_b[i].reshape(1, SSD_CONV_CH))
            y2 = _ssd_scan(cfg, xbc_act, dt_raw, ssd_dt_bias[i], ssd_A_log[i])
            d_wide = jnp.repeat(ssd_D[i], SSD_HEAD_DIM).reshape(1, SSD_D_INNER)
            xa, h2, route = _ssd_out(cfg, y2, xbc_act, z, d_wide, ssd_norm_g[i].reshape(1, SSD_D_INNER),
                                     ssd_w_out[i].astype(BF16), xa, mod_l, g2, router)
        slot_tok, block_eid, n_used, pos = _dispatch(cfg, route)
        yb = _experts(cfg, layer, block_eid, n_used, h2[slot_tok], moe_w_gate, moe_w_up, moe_w_down)
        last = layer == cfg.depth - 1
        xa = _combine(cfg, xa, yb[pos[:, 0]], yb[pos[:, 1]], route, mod_l, final_norm_g.reshape(1, d), last)
    return xa[:cfg.t_lat].reshape(b, cfg.n_lat, d)


def kernel(x, c, ctx, c_ctx, ada_w, ada_b, norm1_g, norm2_g, ab_w_in, ab_w_out, diff_lq1, diff_lk1, diff_lq2, diff_lk2, diff_subln_g, bconv_w, ssd_w_in, ssd_conv_w, ssd_conv_b, ssd_A_log, ssd_dt_bias, ssd_D, ssd_norm_g, ssd_w_out, moe_wg, moe_bg, moe_we, moe_be, moe_w_gate, moe_w_up, moe_w_down, final_norm_g):
    cfg = Cfg(batch=x.shape[0], n_lat=x.shape[1], n_ctx=ctx.shape[1], d_model=x.shape[2], depth=ada_w.shape[0],
              tm=512, tq=256, moe_tile=512, attn_subtiles=2, kchunk=1024)
    return _forward(cfg, x, c, ctx, c_ctx, ada_w, ada_b, norm1_g, norm2_g, ab_w_in, ab_w_out,
                    diff_lq1, diff_lk1, diff_lq2, diff_lk2, diff_subln_g, bconv_w,
                    ssd_w_in, ssd_conv_w, ssd_conv_b, ssd_A_log, ssd_dt_bias, ssd_D, ssd_norm_g, ssd_w_out,
                    moe_wg, moe_bg, moe_we, moe_be, moe_w_gate, moe_w_up, moe_w_down, final_norm_g)
```

```python
import functools
import math
from typing import NamedTuple

import jax
import jax.numpy as jnp
from jax import lax
from jax.experimental import pallas as pl
from jax.experimental.pallas import tpu as pltpu

F32 = jnp.float32
BF16 = jnp.bfloat16
HIGHEST = lax.Precision.HIGHEST

LANES_V7X = 128
SUBLANES_V7X = 8
VMEM_LIMIT_BYTES_V7X = 56 * 1024 * 1024

RMS_EPS = 1e-6
GRID_W = 64
N_DIFF_HEADS = 4
DIFF_QK_DIM = 64
DIFF_V_DIM = 128
DIFF_WIDTH = 512
CONV_WIDTH = 512
ROPE_BASE = 10000.0
SSD_D_INNER = 2048
SSD_HEAD_DIM = 64
SSD_HEADS = 32
SSD_GROUPS = 4
SSD_STATE = 128
SSD_CHUNK = 128
SSD_CONV_CH = SSD_D_INNER + 2 * SSD_GROUPS * SSD_STATE
MOE_GROUPS = 4
MOE_EXPERTS_PER_GROUP = 8
MOE_EXPERTS = 32
MOE_HIDDEN = 512
ROUTE_LANES = LANES_V7X


class Cfg(NamedTuple):
    batch: int
    n_lat: int
    n_ctx: int
    d_model: int
    depth: int
    tm: int
    tq: int
    moe_tile: int
    attn_subtiles: int
    kchunk: int

    @property
    def t_lat(self):
        return self.batch * self.n_lat

    @property
    def t_ctx(self):
        return self.batch * self.n_ctx

    @property
    def t_all(self):
        return self.t_lat + self.t_ctx


def _cparams(sem):
    return pltpu.CompilerParams(dimension_semantics=sem, vmem_limit_bytes=VMEM_LIMIT_BYTES_V7X)


def _silu(v):
    return v * (1.0 / (1.0 + jnp.exp(-v)))


def _const_spec(shape):
    nd = len(shape)
    return pl.BlockSpec(shape, lambda *_: (0,) * nd)


def _mod_spec(cfg):
    return pl.BlockSpec((1, 6, cfg.d_model),
                        lambda i: (jnp.minimum((i * cfg.tm) // cfg.n_lat, cfg.batch), 0, 0))


def _seq_pos(cfg, tm):
    r0 = pl.program_id(0) * tm
    row = r0 + lax.broadcasted_iota(jnp.int32, (tm, 1), 0)
    seqlen = jnp.where(r0 >= cfg.t_lat, cfg.n_ctx, cfg.n_lat)
    return row & (seqlen - 1), seqlen


def _norm_mod(x, g, shift, scale):
    ms = jnp.mean(x * x, axis=-1, keepdims=True)
    return (x * lax.rsqrt(ms + RMS_EPS) * g) * (1.0 + scale) + shift


def _adaln_kernel(c_ref, w_ref, b_ref, o_ref):
    sc = _silu(c_ref[...])
    o_ref[0] = jnp.dot(sc, w_ref[0], precision=HIGHEST, preferred_element_type=F32) + b_ref[0]


def _adaln(cfg, c_all, ada_w, ada_b):
    d = cfg.d_model
    out = pl.pallas_call(
        _adaln_kernel,
        grid=(cfg.depth, 6),
        in_specs=[_const_spec((SUBLANES_V7X, d)),
                  pl.BlockSpec((1, d, d), lambda l, j: (l, 0, j)),
                  pl.BlockSpec((1, 1, d), lambda l, j: (l, 0, j))],
        out_specs=pl.BlockSpec((1, SUBLANES_V7X, d), lambda l, j: (l, 0, j)),
        out_shape=jax.ShapeDtypeStruct((cfg.depth, SUBLANES_V7X, 6 * d), F32),
        compiler_params=_cparams(("arbitrary", "arbitrary")),
        name="adaln",
    )(c_all, ada_w, ada_b.reshape(cfg.depth, 1, 6 * d))
    return out.reshape(cfg.depth, SUBLANES_V7X, 6, d)


def _rope_tables(cfg):
    n = cfg.n_lat
    rows = n // GRID_W
    row = jnp.broadcast_to(jnp.arange(rows, dtype=F32)[:, None], (rows, GRID_W)).reshape(n)
    col = jnp.broadcast_to(jnp.arange(GRID_W, dtype=F32)[None, :], (rows, GRID_W)).reshape(n)
    axis_dim = DIFF_QK_DIM // 2
    inv_freq = ROPE_BASE ** (-jnp.arange(0, axis_dim, 2, dtype=F32) / axis_dim)
    ang_r = row[:, None] * inv_freq
    ang_c = col[:, None] * inv_freq
    zeros = jnp.zeros_like(ang_r)
    cos64 = jnp.concatenate([jnp.cos(ang_r), jnp.cos(ang_r), jnp.cos(ang_c), jnp.cos(ang_c)], axis=1)
    sa64 = jnp.concatenate([zeros, jnp.sin(ang_r), zeros, jnp.sin(ang_c)], axis=1)
    sb64 = jnp.concatenate([-jnp.sin(ang_r), zeros, -jnp.sin(ang_c), zeros], axis=1)
    ident = cfg.tm
    cos = jnp.concatenate([jnp.tile(cos64, (1, 2)), jnp.ones((ident, LANES_V7X), F32)], axis=0)
    sa = jnp.concatenate([jnp.tile(sa64, (1, 2)), jnp.zeros((ident, LANES_V7X), F32)], axis=0)
    sb = jnp.concatenate([jnp.tile(sb64, (1, 2)), jnp.zeros((ident, LANES_V7X), F32)], axis=0)
    return cos, sa, sb


def _ab_in_kernel(x_ref, mod_ref, g_ref, w_ref, cos_ref, sa_ref, sb_ref,
                  q_ref, k_ref, v_ref, gb_ref, u_ref):
    m = mod_ref[0]
    hb = _norm_mod(x_ref[...], g_ref[...], m[0:1], m[1:2]).astype(BF16)
    cos, sa, sb = cos_ref[...], sa_ref[...], sb_ref[...]

    def proj(j):
        return jnp.dot(hb, w_ref[:, j * DIFF_WIDTH:(j + 1) * DIFF_WIDTH], preferred_element_type=F32)

    def rope(t, out_ref, scale):
        for gi in range(DIFF_WIDTH // LANES_V7X):
            xg = t[:, gi * LANES_V7X:(gi + 1) * LANES_V7X]
            r = xg * cos + pltpu.roll(xg, 16, 1) * sa + pltpu.roll(xg, LANES_V7X - 16, 1) * sb
            out_ref[:, gi * LANES_V7X:(gi + 1) * LANES_V7X] = (r * scale).astype(out_ref.dtype)

    rope(proj(0), q_ref, math.log2(math.e) * DIFF_QK_DIM ** -0.5)
    rope(proj(1), k_ref, 1.0)
    v_ref[...] = proj(2).astype(v_ref.dtype)
    gb_ref[...] = proj(3)
    u_ref[...] = proj(4) * proj(5)


def _ab_in(cfg, x, mod_l, g1, w_in_bf16, tables):
    t, d, tm = cfg.t_all, cfg.d_model, cfg.tm
    n_lat_tiles, per_seq = cfg.t_lat // tm, cfg.n_lat // tm
    tab_spec = pl.BlockSpec((tm, LANES_V7X), lambda i: (jnp.where(i < n_lat_tiles, i % per_seq, per_seq), 0))
    row = lambda w: pl.BlockSpec((tm, w), lambda i: (i, 0))
    return pl.pallas_call(
        _ab_in_kernel,
        grid=(t // tm,),
        in_specs=[row(d), _mod_spec(cfg), _const_spec((1, d)), _const_spec(w_in_bf16.shape),
                  tab_spec, tab_spec, tab_spec],
        out_specs=[row(DIFF_WIDTH)] * 5,
        out_shape=[jax.ShapeDtypeStruct((t, DIFF_WIDTH), BF16)] * 3
        + [jax.ShapeDtypeStruct((t, CONV_WIDTH), F32)] * 2,
        compiler_params=_cparams(("arbitrary",)),
        name="ab_in",
    )(x, mod_l, g1, w_in_bf16, *tables)


def _attn_body(lvec_ref, q_ref, kc_ref, kl_ref, vc_ref, vl_ref, g_ref, o_ref, *, tq, lam_init, kchunk):
    lv = lvec_ref[...]
    lam = (jnp.exp(jnp.sum(lv[0:1] * lv[1:2], axis=-1, keepdims=True))
           - jnp.exp(jnp.sum(lv[2:3] * lv[3:4], axis=-1, keepdims=True)) + lam_init)
    nt = (((1,), (1,)), ((), ()))
    half = DIFF_QK_DIM
    chunks = [(kc_ref, vc_ref, 0, kc_ref.shape[0])]
    if kl_ref is not None:
        chunks += [(kl_ref, vl_ref, r0, kchunk) for r0 in range(0, kl_ref.shape[0], kchunk)]
    for j in range(q_ref.shape[0] // tq):
        q = q_ref[j * tq:(j + 1) * tq, :].astype(F32)
        lane = lax.broadcasted_iota(jnp.int32, q.shape, 1)
        q2 = jnp.concatenate([jnp.where(lane < half, q, 0.0), jnp.where(lane >= half, q, 0.0)],
                             axis=0).astype(BF16)
        ps, ms, ls = [], [], []
        for k_ref, _, r0, n in chunks:
            s = lax.dot_general(q2, k_ref[r0:r0 + n, :], nt, preferred_element_type=F32)
            m_c = jnp.max(s, axis=-1, keepdims=True)
            p = jnp.exp2(s - m_c)
            ps.append(p)
            ms.append(m_c)
            ls.append(jnp.sum(p, axis=-1, keepdims=True))
        mx = functools.reduce(jnp.maximum, ms)
        scale = [jnp.exp2(m_c - mx) for m_c in ms]
        r = 1.0 / functools.reduce(jnp.add, [sc * l_c for sc, l_c in zip(scale, ls)])
        pv = None
        for p, sc, (_, v_ref, r0, n) in zip(ps, scale, chunks):
            w = sc * r
            pd = (p[:tq] * w[:tq] - p[tq:] * (lam * w[tq:])).astype(BF16)
            part = jnp.dot(pd, v_ref[r0:r0 + n, :], preferred_element_type=F32)
            pv = part if pv is None else pv + part
        o = pv * lax.rsqrt(jnp.mean(pv * pv, axis=-1, keepdims=True) + RMS_EPS)
        o_ref[j * tq:(j + 1) * tq, :] = ((o * g_ref[...]) * (1.0 - lam_init)).astype(o_ref.dtype)


def _attn_lat_kernel(lvec_ref, q_ref, kc_ref, kl_ref, vc_ref, vl_ref, g_ref, o_ref, **kw):
    _attn_body(lvec_ref, q_ref, kc_ref, kl_ref, vc_ref, vl_ref, g_ref, o_ref, **kw)


def _attn_ctx_kernel(lvec_ref, q_ref, kc_ref, vc_ref, g_ref, o_ref, **kw):
    _attn_body(lvec_ref, q_ref, kc_ref, None, vc_ref, None, g_ref, o_ref, **kw)


def _attention(cfg, q, k, v, lvec, subln_g, lam_init):
    tq, nsub = cfg.tq, cfg.attn_subtiles
    tstep = tq * nsub
    nqb = cfg.n_lat // tstep
    ctx_blk0 = cfg.t_lat // cfg.n_ctx
    hw = DIFF_V_DIM
    ctx_spec = pl.BlockSpec((cfg.n_ctx, hw), lambda b, h, *_: (ctx_blk0 + b, h))
    lat_spec = pl.BlockSpec((cfg.n_lat, hw), lambda b, h, *_: (b, h))
    q_spec = pl.BlockSpec((tstep, hw), lambda b, h, i: (b * nqb + i, h))
    kw = dict(tq=tq, lam_init=lam_init, kchunk=min(cfg.kchunk, cfg.n_lat))
    o_lat = pl.pallas_call(
        functools.partial(_attn_lat_kernel, **kw),
        grid=(cfg.batch, N_DIFF_HEADS, nqb),
        in_specs=[_const_spec(lvec.shape), q_spec, ctx_spec, lat_spec, ctx_spec, lat_spec, _const_spec((1, hw))],
        out_specs=q_spec,
        out_shape=jax.ShapeDtypeStruct((cfg.t_lat, DIFF_WIDTH), BF16),
        compiler_params=_cparams(("arbitrary", "arbitrary", "arbitrary")),
        name="diff_attn",
    )(lvec, q, k, k, v, v, subln_g)
    tqc = min(tq, cfg.n_ctx)
    o_ctx = pl.pallas_call(
        functools.partial(_attn_ctx_kernel, tq=tqc, lam_init=lam_init, kchunk=cfg.n_ctx),
        grid=(cfg.batch, N_DIFF_HEADS),
        in_specs=[_const_spec(lvec.shape), ctx_spec, ctx_spec, ctx_spec, _const_spec((1, hw))],
        out_specs=pl.BlockSpec((cfg.n_ctx, hw), lambda b, h: (b, h)),
        out_shape=jax.ShapeDtypeStruct((cfg.t_ctx, DIFF_WIDTH), BF16),
        compiler_params=_cparams(("arbitrary", "arbitrary")),
        name="diff_attn_ctx",
    )(lvec, q, k, v, subln_g)
    return jnp.concatenate([o_lat, o_ctx], axis=0)


def _route_from_logits(lg):
    lane = lax.broadcasted_iota(jnp.int32, lg.shape, 1)
    neg = -jnp.inf
    big = jnp.int32(ROUTE_LANES)
    gl = jnp.where(lane < MOE_GROUPS, lg, neg)
    gmax = jnp.max(gl, axis=-1, keepdims=True)
    gidx = jnp.min(jnp.where(gl == gmax, lane, big), axis=-1, keepdims=True)
    gw = 1.0 / jnp.sum(jnp.exp(gl - gmax), axis=-1, keepdims=True)
    lo = MOE_GROUPS + gidx * MOE_EXPERTS_PER_GROUP
    el = jnp.where((lane >= lo) & (lane < lo + MOE_EXPERTS_PER_GROUP), lg, neg)
    m1 = jnp.max(el, axis=-1, keepdims=True)
    i1 = jnp.min(jnp.where(el == m1, lane, big), axis=-1, keepdims=True)
    el2 = jnp.where(lane == i1, neg, el)
    m2 = jnp.max(el2, axis=-1, keepdims=True)
    i2 = jnp.min(jnp.where(el2 == m2, lane, big), axis=-1, keepdims=True)
    w1 = gw / (1.0 + jnp.exp(m2 - m1))
    w2 = gw - w1
    e1 = (i1 - MOE_GROUPS).astype(F32)
    e2 = (i2 - MOE_GROUPS).astype(F32)
    return jnp.where(lane == 0, e1, jnp.where(lane == 1, e2, jnp.where(lane == 2, w1, jnp.where(lane == 3, w2, 0.0))))


def _pack_bf16_pairs(v):
    w = v.shape[1] // 2
    lo = pltpu.bitcast(v[:, :w].astype(BF16).astype(F32), jnp.uint32)
    hi = pltpu.bitcast(v[:, w:].astype(BF16).astype(F32), jnp.uint32)
    return pltpu.bitcast((hi & jnp.uint32(0xFFFF0000)) | (lo >> 16), F32)


def _unpack_bf16_pairs(words):
    u = pltpu.bitcast(words, jnp.uint32)
    lo = pltpu.bitcast(u << 16, F32)
    hi = pltpu.bitcast(u & jnp.uint32(0xFFFF0000), F32)
    return lo, hi


def _router_weights(wg, bg, we, be):
    d = wg.shape[0]
    pad = ROUTE_LANES - MOE_GROUPS - MOE_EXPERTS
    wr = jnp.concatenate([wg, we, jnp.zeros((d, pad), F32)], axis=1)
    hi = wr.astype(BF16)
    lo = (wr - hi.astype(F32)).astype(BF16)
    br = jnp.concatenate([bg, be, jnp.zeros((pad,), F32)]).reshape(1, ROUTE_LANES)
    return jnp.concatenate([hi, lo], axis=1), hi, br


def _expert_onehots(rt):
    lane = lax.broadcasted_iota(jnp.int32, rt.shape, 1)
    return (jnp.where(lane == rt[:, 0:1].astype(jnp.int32), 1.0, 0.0),
            jnp.where(lane == rt[:, 1:2].astype(jnp.int32), 1.0, 0.0))


def _residual_norm_route(x, mix, m, g2, wr2_ref, wrh_ref, br_ref, xo_ref, h2_ref, rt_ref, cnt_ref):
    xn = x + m[2:3] * mix
    xo_ref[...] = xn
    h2 = _norm_mod(xn, g2, m[3:4], m[4:5])
    h2_ref[...] = _pack_bf16_pairs(h2)
    hh = h2.astype(BF16)
    hl = (h2 - hh.astype(F32)).astype(BF16)
    a = jnp.dot(hh, wr2_ref[...], preferred_element_type=F32)
    lg = (a[:, :ROUTE_LANES] + a[:, ROUTE_LANES:]) + jnp.dot(hl, wrh_ref[...], preferred_element_type=F32)
    rt = _route_from_logits(lg + br_ref[...])
    rt_ref[...] = rt

    @pl.when(pl.program_id(0) == 0)
    def _():
        cnt_ref[...] = jnp.zeros_like(cnt_ref)

    o1, o2 = _expert_onehots(rt)
    cnt_ref[...] += jnp.sum(o1 + o2, axis=0, keepdims=True)


def _epilogue_specs(cfg):
    d, tm = cfg.d_model, cfg.tm
    row = lambda w: pl.BlockSpec((tm, w), lambda i: (i, 0))
    in_specs = [row(d), _mod_spec(cfg), _const_spec((1, d)), _const_spec((d, 2 * ROUTE_LANES)),
                _const_spec((d, ROUTE_LANES)), _const_spec((1, ROUTE_LANES))]
    out_specs = [row(d), row(d // 2), row(ROUTE_LANES), _const_spec((SUBLANES_V7X, ROUTE_LANES))]
    out_shape = [jax.ShapeDtypeStruct((cfg.t_all, d), F32), jax.ShapeDtypeStruct((cfg.t_all, d // 2), F32),
                 jax.ShapeDtypeStruct((cfg.t_all, ROUTE_LANES), F32),
                 jax.ShapeDtypeStruct((SUBLANES_V7X, ROUTE_LANES), F32)]
    return in_specs, out_specs, out_shape


def _halo_specs(cfg, width, col_block=0):
    per = cfg.tm // SUBLANES_V7X
    last = cfg.t_all // SUBLANES_V7X - 1
    prev = pl.BlockSpec((SUBLANES_V7X, width), lambda i, *_: (jnp.maximum(i * per - 1, 0), col_block))
    nxt = pl.BlockSpec((SUBLANES_V7X, width), lambda i, *_: (jnp.minimum((i + 1) * per, last), col_block))
    return prev, nxt


def _shifted_rows(pad_ref, u, prev_blk, next_blk, pos, seqlen):
    tm = u.shape[0]
    s = SUBLANES_V7X
    pad_ref[s:s + tm, :] = u
    pad_ref[0:s, :] = prev_blk
    pad_ref[s + tm:2 * s + tm, :] = next_blk
    um1 = jnp.where(pos == 0, 0.0, pad_ref[s - 1:s - 1 + tm, :])
    up1 = jnp.where(pos == seqlen - 1, 0.0, pad_ref[s + 1:s + 1 + tm, :])
    return um1, up1


def _ab_out_kernel(o_ref, gb_ref, u_ref, up_ref, un_ref, cw_ref, wo_ref,
                   x_ref, mod_ref, g2_ref, wr2_ref, wrh_ref, br_ref,
                   xo_ref, h2_ref, rt_ref, cnt_ref, pad_ref, *, cfg):
    pos, seqlen = _seq_pos(cfg, cfg.tm)
    u = u_ref[...]
    um1, up1 = _shifted_rows(pad_ref, u, up_ref[...], un_ref[...], pos, seqlen)
    cw = cw_ref[...]
    conv = gb_ref[...] * (um1 * cw[0:1] + u * cw[1:2] + up1 * cw[2:3])
    mix = (jnp.dot(o_ref[...], wo_ref[:DIFF_WIDTH, :], preferred_element_type=F32)
           + jnp.dot(conv.astype(BF16), wo_ref[DIFF_WIDTH:, :], preferred_element_type=F32))
    _residual_norm_route(x_ref[...], mix, mod_ref[0], g2_ref[...], wr2_ref, wrh_ref, br_ref,
                         xo_ref, h2_ref, rt_ref, cnt_ref)


def _ab_out(cfg, o, gb, u, conv_w, wo, x, mod_l, g2, router):
    tm = cfg.tm
    row = lambda w: pl.BlockSpec((tm, w), lambda i: (i, 0))
    prev, nxt = _halo_specs(cfg, CONV_WIDTH)
    ep_in, ep_out, ep_shape = _epilogue_specs(cfg)
    return pl.pallas_call(
        functools.partial(_ab_out_kernel, cfg=cfg),
        grid=(cfg.t_all // tm,),
        in_specs=[row(DIFF_WIDTH), row(CONV_WIDTH), row(CONV_WIDTH), prev, nxt, _const_spec(conv_w.shape),
                  _const_spec(wo.shape)] + ep_in,
        out_specs=ep_out,
        out_shape=ep_shape,
        scratch_shapes=[pltpu.VMEM((tm + 2 * SUBLANES_V7X, CONV_WIDTH), F32)],
        compiler_params=_cparams(("arbitrary",)),
        name="ab_out",
    )(o, gb, u, u, u, conv_w, wo, x, mod_l, g2, *router)


def _ssd_in_kernel(x_ref, xp_ref, xn_ref, mod_ref, g_ref, w_ref, cw_ref, cb_ref,
                   z_ref, xbc_ref, dt_ref, pad_ref, *, cfg):
    tm = cfg.tm
    s = SUBLANES_V7X
    m = mod_ref[0]
    g = g_ref[...]
    h = _norm_mod(x_ref[...], g, m[0:1], m[1:2])
    hb = h.astype(BF16)
    h_ext = jnp.concatenate([_norm_mod(xp_ref[...], g, m[0:1], m[1:2]), h,
                             _norm_mod(xn_ref[...], g, m[0:1], m[1:2])], axis=0).astype(BF16)
    pos, seqlen = _seq_pos(cfg, tm)
    cw = 512
    for j in range(SSD_D_INNER // cw):
        z_ref[:, j * cw:(j + 1) * cw] = jnp.dot(hb, w_ref[:, j * cw:(j + 1) * cw],
                                                preferred_element_type=F32).astype(z_ref.dtype)
    for j in range(SSD_CONV_CH // cw):
        c0 = SSD_D_INNER + j * cw
        pad = pad_ref.at[j]
        pad[...] = jnp.dot(h_ext, w_ref[:, c0:c0 + cw], preferred_element_type=F32)
        taps = cw_ref[:, j * cw:(j + 1) * cw]
        um1 = jnp.where(pos == 0, 0.0, pad[s - 1:s - 1 + tm, :])
        up1 = jnp.where(pos == seqlen - 1, 0.0, pad[s + 1:s + 1 + tm, :])
        conv = um1 * taps[0:1] + pad[s:s + tm, :] * taps[1:2] + up1 * taps[2:3] + cb_ref[:, j * cw:(j + 1) * cw]
        xbc_ref[:, j * cw:(j + 1) * cw] = _silu(conv).astype(xbc_ref.dtype)
    dt_ref[...] = jnp.dot(hb, w_ref[:, SSD_D_INNER + SSD_CONV_CH:], preferred_element_type=F32)


def _ssd_in(cfg, x, mod_l, g1, w_in, conv_w, conv_b):
    t, d, tm = cfg.t_all, cfg.d_model, cfg.tm
    row = lambda w: pl.BlockSpec((tm, w), lambda i: (i, 0))
    prev, nxt = _halo_specs(cfg, d)
    return pl.pallas_call(
        functools.partial(_ssd_in_kernel, cfg=cfg),
        grid=(t // tm,),
        in_specs=[row(d), prev, nxt, _mod_spec(cfg), _const_spec((1, d)), _const_spec(w_in.shape),
                  _const_spec(conv_w.shape), _const_spec(conv_b.shape)],
        out_specs=[row(SSD_D_INNER), row(SSD_CONV_CH), row(LANES_V7X)],
        out_shape=[jax.ShapeDtypeStruct((t, SSD_D_INNER), BF16), jax.ShapeDtypeStruct((t, SSD_CONV_CH), BF16),
                   jax.ShapeDtypeStruct((t, LANES_V7X), F32)],
        scratch_shapes=[pltpu.VMEM((SSD_CONV_CH // 512, tm + 2 * SUBLANES_V7X, 512), F32)],
        compiler_params=_cparams(("arbitrary",)),
        name="ssd_in",
    )(x, x, x, mod_l, g1, w_in, conv_w, conv_b)


def _split3(v):
    b1 = v.astype(BF16)
    r1 = v - b1.astype(F32)
    b2 = r1.astype(BF16)
    b3 = (r1 - b2.astype(F32)).astype(BF16)
    return jnp.concatenate([b1, b2, b3], axis=1)


def _softplus(v):
    return jnp.maximum(v, 0.0) + jnp.log(1.0 + jnp.exp(-jnp.abs(v)))


def _ssd_scan_kernel(xs_ref, b_ref, c_ref, dt_ref, dtt_ref, bias_ref, biast_ref, alog_ref, alogt_ref,
                     ewide_ref, y_ref, state_ref):
    q = SSD_CHUNK
    d = pl.program_id(1)
    fwd = d == 0

    @pl.when(pl.program_id(2) == 0)
    def _():
        state_ref[...] = jnp.zeros_like(state_ref)

    dt = _softplus(dt_ref[0] + bias_ref[0])
    dtt = _softplus(dtt_ref[0] + biast_ref[0])
    a = dt * (-jnp.exp(alog_ref[0]))
    at = dtt * (-jnp.exp(alogt_ref[0]))

    ri = lax.broadcasted_iota(jnp.int32, (q, q), 0)
    ci = lax.broadcasted_iota(jnp.int32, (q, q), 1)
    keep = (ri - ci) * jnp.where(fwd, 1, -1) >= 0
    tri = jnp.where(keep, 1.0, 0.0)
    cs = jnp.dot(tri, a, precision=HIGHEST, preferred_element_type=F32)
    cst = lax.dot_general(at, tri, (((1,), (1,)), ((), ())), precision=HIGHEST,
                          preferred_element_type=F32)
    tot = jnp.sum(a, axis=0, keepdims=True)

    ewide = ewide_ref[...]
    w_wide = jnp.dot(_split3(dt * jnp.exp(tot - cs)), ewide, preferred_element_type=F32)
    ecs_wide = jnp.dot(_split3(jnp.exp(cs)), ewide, preferred_element_type=F32)
    dec_wide = jnp.dot(_split3(jnp.broadcast_to(jnp.exp(tot), (SUBLANES_V7X, SSD_HEADS))), ewide,
                       preferred_element_type=F32)[0:1]

    lane = lax.broadcasted_iota(jnp.int32, (q, LANES_V7X), 1)
    lo = lane < SSD_HEAD_DIM
    hpg = SSD_HEADS // SSD_GROUPS
    gw = hpg * SSD_HEAD_DIM
    nt = (((1,), (1,)), ((), ()))
    tn = (((0,), (0,)), ((), ()))
    for g in range(SSD_GROUPS):
        bm = b_ref[:, g * SSD_STATE:(g + 1) * SSD_STATE]
        cm = c_ref[:, g * SSD_STATE:(g + 1) * SSD_STATE]
        cb = lax.dot_general(cm, bm, nt, preferred_element_type=F32)
        sl = slice(g * gw, (g + 1) * gw)
        y_off = jnp.dot(cm, state_ref[:, sl].astype(BF16), preferred_element_type=F32) * ecs_wide[:, sl]
        for pr in range(hpg // 2):
            lhs, rhs = [], []
            c0 = g * gw + pr * LANES_V7X
            xs_pair = xs_ref[:, c0:c0 + LANES_V7X]
            zero = jnp.zeros_like(xs_pair)
            for k in range(2):
                h = g * hpg + pr * 2 + k
                seg = cs[:, h:h + 1] - cst[h:h + 1, :]
                lmat = jnp.exp(jnp.where(keep, seg, -jnp.inf))
                lhs.append((cb * lmat * dtt[h:h + 1, :]).astype(BF16))
                rhs.append(jnp.where(lo if k == 0 else jnp.logical_not(lo), xs_pair, zero))
            y = jnp.dot(jnp.concatenate(lhs, axis=1), jnp.concatenate(rhs, axis=0), preferred_element_type=F32)
            y_ref[0, :, c0:c0 + LANES_V7X] = (y + y_off[:, pr * LANES_V7X:(pr + 1) * LANES_V7X]).astype(y_ref.dtype)
        x2 = (xs_ref[:, sl].astype(F32) * w_wide[:, sl]).astype(BF16)
        upd = lax.dot_general(bm, x2, tn, preferred_element_type=F32)
        state_ref[:, sl] = state_ref[:, sl] * dec_wide[:, sl] + upd


def _ssd_scan(cfg, xbc_act, dt_raw, dt_bias, a_log):
    q = SSD_CHUNK
    n_cc, n_lc = cfg.n_ctx // q, cfg.n_lat // q
    ctx0 = cfg.t_lat // q
    h = SSD_HEADS
    dt2 = dt_raw[:, :2 * h].reshape(cfg.t_all, 2, h).transpose(1, 0, 2)
    dt2t = dt2.transpose(0, 2, 1)
    bias = dt_bias.reshape(2, 1, h)
    biast = dt_bias.reshape(2, h, 1)
    alog = a_log.reshape(2, 1, h)
    alogt = a_log.reshape(2, h, 1)
    head_of_row = jnp.tile(jnp.arange(h), 3)[:, None]
    ewide = (head_of_row == (jnp.arange(SSD_D_INNER) // SSD_HEAD_DIM)[None, :]).astype(BF16)

    def blk(b, d, s):
        cs = jnp.where(d == 0, s, n_cc - 1 - s)
        ls = jnp.where(d == 0, s - n_cc, n_lc - 1 - (s - n_cc))
        return jnp.where(s < n_cc, ctx0 + b * n_cc + cs, b * n_lc + ls)

    gs = SSD_GROUPS * SSD_STATE
    col = lambda w, cblk: pl.BlockSpec((q, w), lambda b, d, s: (blk(b, d, s), cblk))
    per_dir = lambda shape: pl.BlockSpec((1,) + shape, lambda b, d, s: (d, 0, 0))
    return pl.pallas_call(
        _ssd_scan_kernel,
        grid=(cfg.batch, 2, n_cc + n_lc),
        in_specs=[col(SSD_D_INNER, 0), col(gs, SSD_D_INNER // gs), col(gs, SSD_D_INNER // gs + 1),
                  pl.BlockSpec((1, q, h), lambda b, d, s: (d, blk(b, d, s), 0)),
                  pl.BlockSpec((1, h, q), lambda b, d, s: (d, 0, blk(b, d, s))),
                  per_dir((1, h)), per_dir((h, 1)), per_dir((1, h)), per_dir((h, 1)),
                  _const_spec(ewide.shape)],
        out_specs=pl.BlockSpec((1, q, SSD_D_INNER), lambda b, d, s: (d, blk(b, d, s), 0)),
        out_shape=jax.ShapeDtypeStruct((2, cfg.t_all, SSD_D_INNER), BF16),
        scratch_shapes=[pltpu.VMEM((SSD_STATE, SSD_D_INNER), F32)],
        compiler_params=_cparams(("arbitrary", "arbitrary", "arbitrary")),
        name="ssd_scan",
    )(xbc_act, xbc_act, xbc_act, dt2, dt2t, bias, biast, alog, alogt, ewide)


def _ssd_out_kernel(yf_ref, yb_ref, xs_ref, z_ref, dw_ref, ng_ref, wo_ref,
                    x_ref, mod_ref, g2_ref, wr2_ref, wrh_ref, br_ref, xo_ref, h2_ref, rt_ref, cnt_ref):
    gw = SSD_D_INNER // SSD_GROUPS
    mix = None
    for g in range(SSD_GROUPS):
        sl = slice(g * gw, (g + 1) * gw)
        y = yf_ref[0, :, sl].astype(F32) + yb_ref[0, :, sl].astype(F32) + xs_ref[:, sl].astype(F32) * dw_ref[:, sl]
        y = y * _silu(z_ref[:, sl].astype(F32))
        y = (y * lax.rsqrt(jnp.mean(y * y, axis=-1, keepdims=True) + RMS_EPS)) * ng_ref[:, sl]
        part = jnp.dot(y.astype(BF16), wo_ref[sl, :], preferred_element_type=F32)
        mix = part if mix is None else mix + part
    _residual_norm_route(x_ref[...], mix, mod_ref[0], g2_ref[...], wr2_ref, wrh_ref, br_ref,
                         xo_ref, h2_ref, rt_ref, cnt_ref)


def _ssd_out(cfg, y2, xbc_act, z, d_wide, norm_g, wo, x, mod_l, g2, router):
    tm = cfg.tm
    row = lambda w: pl.BlockSpec((tm, w), lambda i: (i, 0))
    ydir = lambda d: pl.BlockSpec((1, tm, SSD_D_INNER), lambda i: (d, i, 0))
    ep_in, ep_out, ep_shape = _epilogue_specs(cfg)
    return pl.pallas_call(
        _ssd_out_kernel,
        grid=(cfg.t_all // tm,),
        in_specs=[ydir(0), ydir(1), row(SSD_D_INNER), row(SSD_D_INNER), _const_spec((1, SSD_D_INNER)),
                  _const_spec((1, SSD_D_INNER)), _const_spec(wo.shape)] + ep_in,
        out_specs=ep_out,
        out_shape=ep_shape,
        compiler_params=_cparams(("arbitrary",)),
        name="ssd_out",
    )(y2, y2, xbc_act, z, d_wide, norm_g, wo, x, mod_l, g2, *router)


def _expert_kernel(eid_ref, nused_ref, xb_ref, wg_ref, wu_ref, wd_ref, y_ref, wg_s, wu_s, wd_s):
    i = pl.program_id(0)
    changed = jnp.logical_or(i == 0, eid_ref[i] != eid_ref[jnp.maximum(i - 1, 0)])

    @pl.when(jnp.logical_and(changed, i < nused_ref[0]))
    def _():
        wg_s[...] = wg_ref[0, 0].astype(BF16)
        wu_s[...] = wu_ref[0, 0].astype(BF16)
        wd_s[...] = wd_ref[0, 0].astype(BF16)

    @pl.when(i < nused_ref[0])
    def _():
        half = wg_s.shape[0] // 2
        lo, hi = _unpack_bf16_pairs(xb_ref[...])
        lo, hi = lo.astype(BF16), hi.astype(BF16)
        hg = (jnp.dot(lo, wg_s[:half, :], preferred_element_type=F32)
              + jnp.dot(hi, wg_s[half:, :], preferred_element_type=F32))
        hu = (jnp.dot(lo, wu_s[:half, :], preferred_element_type=F32)
              + jnp.dot(hi, wu_s[half:, :], preferred_element_type=F32))
        y = jnp.dot((_silu(hg) * hu).astype(BF16), wd_s[...], preferred_element_type=F32)
        y_ref[...] = _pack_bf16_pairs(y)

    @pl.when(i >= nused_ref[0])
    def _():
        y_ref[...] = jnp.zeros_like(y_ref)


def _experts(cfg, layer, block_eid, n_used, xb, w_gate, w_up, w_down):
    d, tile = cfg.d_model, cfg.moe_tile
    n_blocks = xb.shape[0] // tile
    grid_spec = pltpu.PrefetchScalarGridSpec(
        num_scalar_prefetch=2,
        grid=(n_blocks,),
        in_specs=[pl.BlockSpec((tile, d // 2), lambda i, e, n: (i, 0)),
                  pl.BlockSpec((1, 1, d, MOE_HIDDEN), lambda i, e, n: (layer, e[i], 0, 0)),
                  pl.BlockSpec((1, 1, d, MOE_HIDDEN), lambda i, e, n: (layer, e[i], 0, 0)),
                  pl.BlockSpec((1, 1, MOE_HIDDEN, d), lambda i, e, n: (layer, e[i], 0, 0))],
        out_specs=pl.BlockSpec((tile, d // 2), lambda i, e, n: (i, 0)),
        scratch_shapes=[pltpu.VMEM((d, MOE_HIDDEN), BF16), pltpu.VMEM((d, MOE_HIDDEN), BF16),
                        pltpu.VMEM((MOE_HIDDEN, d), BF16)],
    )
    return pl.pallas_call(
        _expert_kernel,
        grid_spec=grid_spec,
        out_shape=jax.ShapeDtypeStruct((n_blocks * tile, d // 2), F32),
        compiler_params=_cparams(("arbitrary",)),
        name="moe_experts",
    )(block_eid, n_used, xb, w_gate, w_up, w_down)


def _dispatch_kernel(rt_ref, cnt_ref, pos_ref, carry_ref, start_ref, *, tile):
    tm = rt_ref.shape[0]
    rt = rt_ref[...]
    lane = lax.broadcasted_iota(jnp.int32, rt.shape, 1)
    o1, o2 = _expert_onehots(rt)
    cnt1 = jnp.sum(o1, axis=0, keepdims=True)
    cnt2 = jnp.sum(o2, axis=0, keepdims=True)

    @pl.when(pl.program_id(0) == 0)
    def _():
        padded = jnp.floor((cnt_ref[...] + (tile - 1)) * (1.0 / tile)) * tile
        r = lax.broadcasted_iota(jnp.int32, (ROUTE_LANES, ROUTE_LANES), 0)
        c = lax.broadcasted_iota(jnp.int32, (ROUTE_LANES, ROUTE_LANES), 1)
        excl = jnp.where(r < c, 1.0, 0.0)
        start_ref[...] = jnp.dot(padded, excl, precision=HIGHEST, preferred_element_type=F32)
        carry_ref[...] = jnp.zeros_like(carry_ref)

    ri = lax.broadcasted_iota(jnp.int32, (tm, tm), 0)
    ci = lax.broadcasted_iota(jnp.int32, (tm, tm), 1)
    earlier = jnp.where(ci < ri, 1.0, 0.0).astype(BF16)
    p1 = jnp.dot(earlier, o1.astype(BF16), preferred_element_type=F32)
    p2 = jnp.dot(earlier, o2.astype(BF16), preferred_element_type=F32)
    base = start_ref[0:1] + carry_ref[0:1]
    pos1 = jnp.sum(o1 * (base + p1), axis=-1, keepdims=True)
    pos2 = jnp.sum(o2 * (base + cnt1 + p2), axis=-1, keepdims=True)
    pos_ref[...] = jnp.where(lane == 0, pos1, jnp.where(lane == 1, pos2, 0.0)).astype(jnp.int32)
    carry_ref[...] += cnt1 + cnt2


def _dispatch(cfg, route, cnt):
    tile, tm = cfg.moe_tile, cfg.tm
    t = cfg.t_all
    a_total = 2 * t
    pos = pl.pallas_call(
        functools.partial(_dispatch_kernel, tile=tile),
        grid=(t // tm,),
        in_specs=[pl.BlockSpec((tm, ROUTE_LANES), lambda i: (i, 0)), _const_spec((SUBLANES_V7X, ROUTE_LANES))],
        out_specs=pl.BlockSpec((tm, ROUTE_LANES), lambda i: (i, 0)),
        out_shape=jax.ShapeDtypeStruct((t, ROUTE_LANES), jnp.int32),
        scratch_shapes=[pltpu.VMEM((SUBLANES_V7X, ROUTE_LANES), F32), pltpu.VMEM((SUBLANES_V7X, ROUTE_LANES), F32)],
        compiler_params=_cparams(("arbitrary",)),
        name="moe_dispatch",
    )(route, cnt)
    counts = cnt[0, :MOE_EXPERTS].astype(jnp.int32)
    pend = jnp.cumsum((counts + tile - 1) // tile * tile)
    n_blocks = (a_total + MOE_EXPERTS * (tile - 1)) // tile
    block_eid = jnp.minimum(jnp.sum(pend[None, :] <= (jnp.arange(n_blocks, dtype=jnp.int32) * tile)[:, None], axis=1),
                            MOE_EXPERTS - 1).astype(jnp.int32)
    n_used = (pend[-1] // tile).astype(jnp.int32).reshape(1)
    pos2 = pos[:, 0:2]
    slot_tok = (jnp.arange(n_blocks * tile, dtype=jnp.int32) % t).at[pos2.reshape(-1)].set(
        jnp.arange(a_total, dtype=jnp.int32) // 2, unique_indices=True, mode="promise_in_bounds")
    return slot_tok, block_eid, n_used, pos2


def _combine_kernel(x_ref, y1_ref, y2_ref, rt_ref, mod_ref, g_ref, o_ref, *, final):
    rt = rt_ref[...]
    a_lo, a_hi = _unpack_bf16_pairs(y1_ref[...])
    b_lo, b_hi = _unpack_bf16_pairs(y2_ref[...])
    w1, w2 = rt[:, 2:3], rt[:, 3:4]
    f = jnp.concatenate([w1 * a_lo + w2 * b_lo, w1 * a_hi + w2 * b_hi], axis=1)
    xn = x_ref[...] + mod_ref[0][5:6] * f
    if final:
        xn = (xn * lax.rsqrt(jnp.mean(xn * xn, axis=-1, keepdims=True) + RMS_EPS)) * g_ref[...]
    o_ref[...] = xn


def _combine(cfg, x, y12, route, mod_l, g, final):
    d, tm = cfg.d_model, cfg.tm
    n_tiles = cfg.t_all // tm
    row = lambda w: pl.BlockSpec((tm, w), lambda i: (i, 0))
    return pl.pallas_call(
        functools.partial(_combine_kernel, final=final),
        grid=(n_tiles,),
        in_specs=[row(d), row(d // 2), pl.BlockSpec((tm, d // 2), lambda i: (n_tiles + i, 0)),
                  row(ROUTE_LANES), _mod_spec(cfg), _const_spec((1, d))],
        out_specs=row(d),
        out_shape=jax.ShapeDtypeStruct((cfg.t_all, d), F32),
        compiler_params=_cparams(("arbitrary",)),
        name="moe_combine",
    )(x, y12, y12, route, mod_l, g)


def _forward(cfg, x, c, ctx, c_ctx, ada_w, ada_b, norm1_g, norm2_g, ab_w_in, ab_w_out,
             diff_lq1, diff_lk1, diff_lq2, diff_lk2, diff_subln_g, bconv_w,
             ssd_w_in, ssd_conv_w, ssd_conv_b, ssd_A_log, ssd_dt_bias, ssd_D, ssd_norm_g, ssd_w_out,
             moe_wg, moe_bg, moe_we, moe_be, moe_w_gate, moe_w_up, moe_w_down, final_norm_g):
    d = cfg.d_model
    b = cfg.batch
    xa = jnp.concatenate([x.reshape(cfg.t_lat, d), ctx.reshape(cfg.t_ctx, d)], axis=0)
    c_all = jnp.zeros((SUBLANES_V7X, d), F32).at[:b].set(c).at[b].set(c_ctx)
    mod = _adaln(cfg, c_all, ada_w, ada_b)
    tables = _rope_tables(cfg)

    for layer in range(cfg.depth):
        i = layer // 2
        mod_l = mod[layer]
        g1 = norm1_g[layer].reshape(1, d)
        g2 = norm2_g[layer].reshape(1, d)
        router = _router_weights(moe_wg[layer], moe_bg[layer], moe_we[layer], moe_be[layer])
        if layer % 2 == 0:
            lam_init = 0.8 - 0.6 * math.exp(-0.3 * layer)
            q, k, v, gb, u = _ab_in(cfg, xa, mod_l, g1, ab_w_in[i].astype(BF16), tables)
            lvec = jnp.stack([diff_lq1[i], diff_lk1[i], diff_lq2[i], diff_lk2[i]])
            o = _attention(cfg, q, k, v, lvec, diff_subln_g[i].reshape(1, DIFF_V_DIM), lam_init)
            w_out = ab_w_out[i].astype(BF16)
            xa, h2, route, cnt = _ab_out(cfg, o, gb, u, bconv_w[i], w_out, xa, mod_l, g2, router)
        else:
            w_in = jnp.pad(ssd_w_in[i].astype(BF16), ((0, 0), (0, LANES_V7X - 2 * SSD_HEADS)))
            z, xbc_act, dt_raw = _ssd_in(cfg, xa, mod_l, g1, w_in, ssd_conv_w[i],
                                         ssd_conv_b[i].reshape(1, SSD_CONV_CH))
            y2 = _ssd_scan(cfg, xbc_act, dt_raw, ssd_dt_bias[i], ssd_A_log[i])
            d_wide = jnp.repeat(ssd_D[i], SSD_HEAD_DIM).reshape(1, SSD_D_INNER)
            xa, h2, route, cnt = _ssd_out(cfg, y2, xbc_act, z, d_wide, ssd_norm_g[i].reshape(1, SSD_D_INNER),
                                          ssd_w_out[i].astype(BF16), xa, mod_l, g2, router)
        slot_tok, block_eid, n_used, pos = _dispatch(cfg, route, cnt)
        yb = _experts(cfg, layer, block_eid, n_used, h2[slot_tok], moe_w_gate, moe_w_up, moe_w_down)
        last = layer == cfg.depth - 1
        y12 = yb[jnp.concatenate([pos[:, 0], pos[:, 1]])]
        xa = _combine(cfg, xa, y12, route, mod_l, final_norm_g.reshape(1, d), last)
    return xa[:cfg.t_lat].reshape(b, cfg.n_lat, d)


def kernel(x, c, ctx, c_ctx, ada_w, ada_b, norm1_g, norm2_g, ab_w_in, ab_w_out, diff_lq1, diff_lk1, diff_lq2, diff_lk2, diff_subln_g, bconv_w, ssd_w_in, ssd_conv_w, ssd_conv_b, ssd_A_log, ssd_dt_bias, ssd_D, ssd_norm_g, ssd_w_out, moe_wg, moe_bg, moe_we, moe_be, moe_w_gate, moe_w_up, moe_w_down, final_norm_g):
    cfg = Cfg(batch=x.shape[0], n_lat=x.shape[1], n_ctx=ctx.shape[1], d_model=x.shape[2], depth=ada_w.shape[0],
              tm=512, tq=256, moe_tile=512, attn_subtiles=2, kchunk=1024)
    return _forward(cfg, x, c, ctx, c_ctx, ada_w, ada_b, norm1_g, norm2_g, ab_w_in, ab_w_out,
                    diff_lq1, diff_lk1, diff_lq2, diff_lk2, diff_subln_g, bconv_w,
                    ssd_w_in, ssd_conv_w, ssd_conv_b, ssd_A_log, ssd_dt_bias, ssd_D, ssd_norm_g, ssd_w_out,
                    moe_wg, moe_bg, moe_we, moe_be, moe_w_gate, moe_w_up, moe_w_down, final_norm_g)
```

```python
import functools
import math
from typing import NamedTuple

import jax
import jax.numpy as jnp
from jax import lax
from jax.experimental import pallas as pl
from jax.experimental.pallas import tpu as pltpu

F32 = jnp.float32
BF16 = jnp.bfloat16
HIGHEST = lax.Precision.HIGHEST

LANES_V7X = 128
SUBLANES_V7X = 8
VMEM_LIMIT_BYTES_V7X = 56 * 1024 * 1024

RMS_EPS = 1e-6
GRID_W = 64
N_DIFF_HEADS = 4
DIFF_QK_DIM = 64
DIFF_V_DIM = 128
DIFF_WIDTH = 512
CONV_WIDTH = 512
ROPE_BASE = 10000.0
SSD_D_INNER = 2048
SSD_HEAD_DIM = 64
SSD_HEADS = 32
SSD_GROUPS = 4
SSD_STATE = 128
SSD_CHUNK = 128
SSD_CONV_CH = SSD_D_INNER + 2 * SSD_GROUPS * SSD_STATE
MOE_GROUPS = 4
MOE_EXPERTS_PER_GROUP = 8
MOE_EXPERTS = 32
MOE_HIDDEN = 512
ROUTE_LANES = LANES_V7X
MOE_GATHER_PARTS = 3


class Cfg(NamedTuple):
    batch: int
    n_lat: int
    n_ctx: int
    d_model: int
    depth: int
    tm: int
    tq: int
    moe_tile: int
    attn_subtiles: int
    kchunk: int

    @property
    def t_lat(self):
        return self.batch * self.n_lat

    @property
    def t_ctx(self):
        return self.batch * self.n_ctx

    @property
    def t_all(self):
        return self.t_lat + self.t_ctx


def _cparams(sem):
    return pltpu.CompilerParams(dimension_semantics=sem, vmem_limit_bytes=VMEM_LIMIT_BYTES_V7X)


def _silu(v):
    return v * (1.0 / (1.0 + jnp.exp(-v)))


def _const_spec(shape):
    nd = len(shape)
    return pl.BlockSpec(shape, lambda *_: (0,) * nd)


def _mod_spec(cfg):
    return pl.BlockSpec((1, 6, cfg.d_model),
                        lambda i: (jnp.minimum((i * cfg.tm) // cfg.n_lat, cfg.batch), 0, 0))


def _seq_pos(cfg, tm):
    r0 = pl.program_id(0) * tm
    row = r0 + lax.broadcasted_iota(jnp.int32, (tm, 1), 0)
    seqlen = jnp.where(r0 >= cfg.t_lat, cfg.n_ctx, cfg.n_lat)
    return row & (seqlen - 1), seqlen


def _norm_mod(x, g, shift, scale):
    ms = jnp.mean(x * x, axis=-1, keepdims=True)
    return (x * lax.rsqrt(ms + RMS_EPS) * g) * (1.0 + scale) + shift


def _adaln_kernel(c_ref, w_ref, b_ref, o_ref):
    sc = _silu(c_ref[...])
    o_ref[0] = jnp.dot(sc, w_ref[0], precision=HIGHEST, preferred_element_type=F32) + b_ref[0]


def _adaln(cfg, c_all, ada_w, ada_b):
    d = cfg.d_model
    out = pl.pallas_call(
        _adaln_kernel,
        grid=(cfg.depth, 6),
        in_specs=[_const_spec((SUBLANES_V7X, d)),
                  pl.BlockSpec((1, d, d), lambda l, j: (l, 0, j)),
                  pl.BlockSpec((1, 1, d), lambda l, j: (l, 0, j))],
        out_specs=pl.BlockSpec((1, SUBLANES_V7X, d), lambda l, j: (l, 0, j)),
        out_shape=jax.ShapeDtypeStruct((cfg.depth, SUBLANES_V7X, 6 * d), F32),
        compiler_params=_cparams(("arbitrary", "arbitrary")),
        name="adaln",
    )(c_all, ada_w, ada_b.reshape(cfg.depth, 1, 6 * d))
    return out.reshape(cfg.depth, SUBLANES_V7X, 6, d)


def _rope_tables(cfg):
    n = cfg.n_lat
    rows = n // GRID_W
    row = jnp.broadcast_to(jnp.arange(rows, dtype=F32)[:, None], (rows, GRID_W)).reshape(n)
    col = jnp.broadcast_to(jnp.arange(GRID_W, dtype=F32)[None, :], (rows, GRID_W)).reshape(n)
    axis_dim = DIFF_QK_DIM // 2
    inv_freq = ROPE_BASE ** (-jnp.arange(0, axis_dim, 2, dtype=F32) / axis_dim)
    ang_r = row[:, None] * inv_freq
    ang_c = col[:, None] * inv_freq
    zeros = jnp.zeros_like(ang_r)
    cos64 = jnp.concatenate([jnp.cos(ang_r), jnp.cos(ang_r), jnp.cos(ang_c), jnp.cos(ang_c)], axis=1)
    sa64 = jnp.concatenate([zeros, jnp.sin(ang_r), zeros, jnp.sin(ang_c)], axis=1)
    sb64 = jnp.concatenate([-jnp.sin(ang_r), zeros, -jnp.sin(ang_c), zeros], axis=1)
    ident = cfg.tm
    cos = jnp.concatenate([jnp.tile(cos64, (1, 2)), jnp.ones((ident, LANES_V7X), F32)], axis=0)
    sa = jnp.concatenate([jnp.tile(sa64, (1, 2)), jnp.zeros((ident, LANES_V7X), F32)], axis=0)
    sb = jnp.concatenate([jnp.tile(sb64, (1, 2)), jnp.zeros((ident, LANES_V7X), F32)], axis=0)
    return cos, sa, sb


def _dual_row_specs(cfg, width, pair):
    lat_arr, ctx_arr = pair
    nl = cfg.t_lat // cfg.tm
    c0 = nl if ctx_arr is lat_arr else 0
    return (pl.BlockSpec((cfg.tm, width), lambda i: (jnp.minimum(i, nl - 1), 0)),
            pl.BlockSpec((cfg.tm, width), lambda i: (jnp.maximum(i - nl, 0) + c0, 0)))


def _dual_rows(cfg, lat_ref, ctx_ref):
    return jnp.where(pl.program_id(0) >= cfg.t_lat // cfg.tm, ctx_ref[...], lat_ref[...])


def _ab_in_kernel(xl_ref, xc_ref, mod_ref, g_ref, w_ref, cos_ref, sa_ref, sb_ref,
                  q_ref, k_ref, v_ref, gb_ref, u_ref, *, cfg):
    m = mod_ref[0]
    hb = _norm_mod(_dual_rows(cfg, xl_ref, xc_ref), g_ref[...], m[0:1], m[1:2]).astype(BF16)
    cos, sa, sb = cos_ref[...], sa_ref[...], sb_ref[...]

    def proj(j):
        return jnp.dot(hb, w_ref[:, j * DIFF_WIDTH:(j + 1) * DIFF_WIDTH], preferred_element_type=F32)

    def rope(t, out_ref, scale):
        for gi in range(DIFF_WIDTH // LANES_V7X):
            xg = t[:, gi * LANES_V7X:(gi + 1) * LANES_V7X]
            r = xg * cos + pltpu.roll(xg, 16, 1) * sa + pltpu.roll(xg, LANES_V7X - 16, 1) * sb
            out_ref[:, gi * LANES_V7X:(gi + 1) * LANES_V7X] = (r * scale).astype(out_ref.dtype)

    rope(proj(0), q_ref, math.log2(math.e) * DIFF_QK_DIM ** -0.5)
    rope(proj(1), k_ref, 1.0)
    v_ref[...] = proj(2).astype(v_ref.dtype)
    gb_ref[...] = proj(3)
    u_ref[...] = proj(4) * proj(5)


def _ab_in(cfg, x_pair, mod_l, g1, w_in_bf16, tables):
    t, d, tm = cfg.t_all, cfg.d_model, cfg.tm
    n_lat_tiles, per_seq = cfg.t_lat // tm, cfg.n_lat // tm
    tab_spec = pl.BlockSpec((tm, LANES_V7X), lambda i: (jnp.where(i < n_lat_tiles, i % per_seq, per_seq), 0))
    row = lambda w: pl.BlockSpec((tm, w), lambda i: (i, 0))
    return pl.pallas_call(
        functools.partial(_ab_in_kernel, cfg=cfg),
        grid=(t // tm,),
        in_specs=[*_dual_row_specs(cfg, d, x_pair), _mod_spec(cfg), _const_spec((1, d)),
                  _const_spec(w_in_bf16.shape), tab_spec, tab_spec, tab_spec],
        out_specs=[row(DIFF_WIDTH)] * 5,
        out_shape=[jax.ShapeDtypeStruct((t, DIFF_WIDTH), BF16)] * 3
        + [jax.ShapeDtypeStruct((t, CONV_WIDTH), F32)] * 2,
        compiler_params=_cparams(("arbitrary",)),
        name="ab_in",
    )(*x_pair, mod_l, g1, w_in_bf16, *tables)


def _attn_body(lvec_ref, q_ref, kc_ref, kl_ref, vc_ref, vl_ref, g_ref, o_ref, *, tq, lam_init, kchunk):
    lv = lvec_ref[...]
    lam = (jnp.exp(jnp.sum(lv[0:1] * lv[1:2], axis=-1, keepdims=True))
           - jnp.exp(jnp.sum(lv[2:3] * lv[3:4], axis=-1, keepdims=True)) + lam_init)
    nt = (((1,), (1,)), ((), ()))
    half = DIFF_QK_DIM
    chunks = [(kc_ref, vc_ref, 0, kc_ref.shape[0])]
    if kl_ref is not None:
        chunks += [(kl_ref, vl_ref, r0, kchunk) for r0 in range(0, kl_ref.shape[0], kchunk)]
    for j in range(q_ref.shape[0] // tq):
        q = q_ref[j * tq:(j + 1) * tq, :].astype(F32)
        lane = lax.broadcasted_iota(jnp.int32, q.shape, 1)
        q2 = jnp.concatenate([jnp.where(lane < half, q, 0.0), jnp.where(lane >= half, q, 0.0)],
                             axis=0).astype(BF16)
        ps, ms, ls = [], [], []
        for k_ref, _, r0, n in chunks:
            s = lax.dot_general(q2, k_ref[r0:r0 + n, :], nt, preferred_element_type=F32)
            m_c = jnp.max(s, axis=-1, keepdims=True)
            p = jnp.exp2(s - m_c)
            ps.append(p)
            ms.append(m_c)
            ls.append(jnp.sum(p, axis=-1, keepdims=True))
        mx = functools.reduce(jnp.maximum, ms)
        scale = [jnp.exp2(m_c - mx) for m_c in ms]
        r = 1.0 / functools.reduce(jnp.add, [sc * l_c for sc, l_c in zip(scale, ls)])
        pv = None
        for p, sc, (_, v_ref, r0, n) in zip(ps, scale, chunks):
            w = sc * r
            pd = (p[:tq] * w[:tq] - p[tq:] * (lam * w[tq:])).astype(BF16)
            part = jnp.dot(pd, v_ref[r0:r0 + n, :], preferred_element_type=F32)
            pv = part if pv is None else pv + part
        o = pv * lax.rsqrt(jnp.mean(pv * pv, axis=-1, keepdims=True) + RMS_EPS)
        o_ref[j * tq:(j + 1) * tq, :] = ((o * g_ref[...]) * (1.0 - lam_init)).astype(o_ref.dtype)


def _attn_lat_kernel(lvec_ref, q_ref, kc_ref, kl_ref, vc_ref, vl_ref, g_ref, o_ref, **kw):
    _attn_body(lvec_ref, q_ref, kc_ref, kl_ref, vc_ref, vl_ref, g_ref, o_ref, **kw)


def _attn_ctx_kernel(lvec_ref, q_ref, kc_ref, vc_ref, g_ref, o_ref, **kw):
    _attn_body(lvec_ref, q_ref, kc_ref, None, vc_ref, None, g_ref, o_ref, **kw)


def _attention(cfg, q, k, v, lvec, subln_g, lam_init):
    tq, nsub = cfg.tq, cfg.attn_subtiles
    tstep = tq * nsub
    nqb = cfg.n_lat // tstep
    ctx_blk0 = cfg.t_lat // cfg.n_ctx
    hw = DIFF_V_DIM
    ctx_spec = pl.BlockSpec((cfg.n_ctx, hw), lambda b, h, *_: (ctx_blk0 + b, h))
    lat_spec = pl.BlockSpec((cfg.n_lat, hw), lambda b, h, *_: (b, h))
    q_spec = pl.BlockSpec((tstep, hw), lambda b, h, i: (b * nqb + i, h))
    kw = dict(tq=tq, lam_init=lam_init, kchunk=min(cfg.kchunk, cfg.n_lat))
    o_lat = pl.pallas_call(
        functools.partial(_attn_lat_kernel, **kw),
        grid=(cfg.batch, N_DIFF_HEADS, nqb),
        in_specs=[_const_spec(lvec.shape), q_spec, ctx_spec, lat_spec, ctx_spec, lat_spec, _const_spec((1, hw))],
        out_specs=q_spec,
        out_shape=jax.ShapeDtypeStruct((cfg.t_lat, DIFF_WIDTH), BF16),
        compiler_params=_cparams(("arbitrary", "arbitrary", "arbitrary")),
        name="diff_attn",
    )(lvec, q, k, k, v, v, subln_g)
    tqc = min(tq, cfg.n_ctx)
    o_ctx = pl.pallas_call(
        functools.partial(_attn_ctx_kernel, tq=tqc, lam_init=lam_init, kchunk=cfg.n_ctx),
        grid=(cfg.batch, N_DIFF_HEADS),
        in_specs=[_const_spec(lvec.shape), ctx_spec, ctx_spec, ctx_spec, _const_spec((1, hw))],
        out_specs=pl.BlockSpec((cfg.n_ctx, hw), lambda b, h: (b, h)),
        out_shape=jax.ShapeDtypeStruct((cfg.t_ctx, DIFF_WIDTH), BF16),
        compiler_params=_cparams(("arbitrary", "arbitrary")),
        name="diff_attn_ctx",
    )(lvec, q, k, v, subln_g)
    return o_lat, o_ctx


def _route_from_logits(lg):
    lane = lax.broadcasted_iota(jnp.int32, lg.shape, 1)
    neg = -jnp.inf
    big = jnp.int32(ROUTE_LANES)
    gl = jnp.where(lane < MOE_GROUPS, lg, neg)
    gmax = jnp.max(gl, axis=-1, keepdims=True)
    gidx = jnp.min(jnp.where(gl == gmax, lane, big), axis=-1, keepdims=True)
    gw = 1.0 / jnp.sum(jnp.exp(gl - gmax), axis=-1, keepdims=True)
    lo = MOE_GROUPS + gidx * MOE_EXPERTS_PER_GROUP
    el = jnp.where((lane >= lo) & (lane < lo + MOE_EXPERTS_PER_GROUP), lg, neg)
    m1 = jnp.max(el, axis=-1, keepdims=True)
    i1 = jnp.min(jnp.where(el == m1, lane, big), axis=-1, keepdims=True)
    el2 = jnp.where(lane == i1, neg, el)
    m2 = jnp.max(el2, axis=-1, keepdims=True)
    i2 = jnp.min(jnp.where(el2 == m2, lane, big), axis=-1, keepdims=True)
    w1 = gw / (1.0 + jnp.exp(m2 - m1))
    w2 = gw - w1
    e1 = (i1 - MOE_GROUPS).astype(F32)
    e2 = (i2 - MOE_GROUPS).astype(F32)
    return jnp.where(lane == 0, e1, jnp.where(lane == 1, e2, jnp.where(lane == 2, w1, jnp.where(lane == 3, w2, 0.0))))


def _pack_bf16_pairs(v):
    w = v.shape[1] // 2
    lo = pltpu.bitcast(v[:, :w].astype(BF16).astype(F32), jnp.uint32)
    hi = pltpu.bitcast(v[:, w:].astype(BF16).astype(F32), jnp.uint32)
    return pltpu.bitcast((hi & jnp.uint32(0xFFFF0000)) | (lo >> 16), F32)


def _unpack_bf16_pairs(words):
    u = pltpu.bitcast(words, jnp.uint32)
    lo = pltpu.bitcast(u << 16, F32)
    hi = pltpu.bitcast(u & jnp.uint32(0xFFFF0000), F32)
    return lo, hi


def _router_weights(wg, bg, we, be):
    d = wg.shape[0]
    pad = ROUTE_LANES - MOE_GROUPS - MOE_EXPERTS
    wr = jnp.concatenate([wg, we, jnp.zeros((d, pad), F32)], axis=1)
    hi = wr.astype(BF16)
    lo = (wr - hi.astype(F32)).astype(BF16)
    br = jnp.concatenate([bg, be, jnp.zeros((pad,), F32)]).reshape(1, ROUTE_LANES)
    return jnp.concatenate([hi, lo], axis=1), hi, br


def _expert_onehots(rt):
    lane = lax.broadcasted_iota(jnp.int32, rt.shape, 1)
    return (jnp.where(lane == rt[:, 0:1].astype(jnp.int32), 1.0, 0.0),
            jnp.where(lane == rt[:, 1:2].astype(jnp.int32), 1.0, 0.0))


def _residual_norm_route(x, mix, m, g2, wr2_ref, wrh_ref, br_ref, xo_ref, h2_ref, rt_ref, cnt_ref):
    xn = x + m[2:3] * mix
    xo_ref[...] = xn
    h2 = _norm_mod(xn, g2, m[3:4], m[4:5])
    h2_ref[...] = _pack_bf16_pairs(h2)
    hh = h2.astype(BF16)
    hl = (h2 - hh.astype(F32)).astype(BF16)
    a = jnp.dot(hh, wr2_ref[...], preferred_element_type=F32)
    lg = (a[:, :ROUTE_LANES] + a[:, ROUTE_LANES:]) + jnp.dot(hl, wrh_ref[...], preferred_element_type=F32)
    rt = _route_from_logits(lg + br_ref[...])
    rt_ref[...] = rt

    @pl.when(pl.program_id(0) == 0)
    def _():
        cnt_ref[...] = jnp.zeros_like(cnt_ref)

    o1, o2 = _expert_onehots(rt)
    cnt_ref[...] += jnp.sum(o1 + o2, axis=0, keepdims=True)


def _epilogue_specs(cfg, x_pair):
    d, tm = cfg.d_model, cfg.tm
    row = lambda w: pl.BlockSpec((tm, w), lambda i: (i, 0))
    in_specs = [*_dual_row_specs(cfg, d, x_pair), _mod_spec(cfg), _const_spec((1, d)),
                _const_spec((d, 2 * ROUTE_LANES)),
                _const_spec((d, ROUTE_LANES)), _const_spec((1, ROUTE_LANES))]
    out_specs = [row(d), row(d // 2), row(ROUTE_LANES), _const_spec((SUBLANES_V7X, ROUTE_LANES))]
    out_shape = [jax.ShapeDtypeStruct((cfg.t_all, d), F32), jax.ShapeDtypeStruct((cfg.t_all, d // 2), F32),
                 jax.ShapeDtypeStruct((cfg.t_all, ROUTE_LANES), F32),
                 jax.ShapeDtypeStruct((SUBLANES_V7X, ROUTE_LANES), F32)]
    return in_specs, out_specs, out_shape


def _halo_specs(cfg, width, col_block=0):
    per = cfg.tm // SUBLANES_V7X
    last = cfg.t_all // SUBLANES_V7X - 1
    prev = pl.BlockSpec((SUBLANES_V7X, width), lambda i, *_: (jnp.maximum(i * per - 1, 0), col_block))
    nxt = pl.BlockSpec((SUBLANES_V7X, width), lambda i, *_: (jnp.minimum((i + 1) * per, last), col_block))
    return prev, nxt


def _shifted_rows(pad_ref, u, prev_blk, next_blk, pos, seqlen):
    tm = u.shape[0]
    s = SUBLANES_V7X
    pad_ref[s:s + tm, :] = u
    pad_ref[0:s, :] = prev_blk
    pad_ref[s + tm:2 * s + tm, :] = next_blk
    um1 = jnp.where(pos == 0, 0.0, pad_ref[s - 1:s - 1 + tm, :])
    up1 = jnp.where(pos == seqlen - 1, 0.0, pad_ref[s + 1:s + 1 + tm, :])
    return um1, up1


def _ab_out_kernel(ol_ref, oc_ref, gb_ref, u_ref, up_ref, un_ref, cw_ref, wo_ref,
                   xl_ref, xc_ref, mod_ref, g2_ref, wr2_ref, wrh_ref, br_ref,
                   xo_ref, h2_ref, rt_ref, cnt_ref, pad_ref, *, cfg):
    pos, seqlen = _seq_pos(cfg, cfg.tm)
    u = u_ref[...]
    um1, up1 = _shifted_rows(pad_ref, u, up_ref[...], un_ref[...], pos, seqlen)
    cw = cw_ref[...]
    conv = gb_ref[...] * (um1 * cw[0:1] + u * cw[1:2] + up1 * cw[2:3])
    mix = (jnp.dot(_dual_rows(cfg, ol_ref, oc_ref), wo_ref[:DIFF_WIDTH, :], preferred_element_type=F32)
           + jnp.dot(conv.astype(BF16), wo_ref[DIFF_WIDTH:, :], preferred_element_type=F32))
    _residual_norm_route(_dual_rows(cfg, xl_ref, xc_ref), mix, mod_ref[0], g2_ref[...], wr2_ref, wrh_ref, br_ref,
                         xo_ref, h2_ref, rt_ref, cnt_ref)


def _ab_out(cfg, o_pair, gb, u, conv_w, wo, x_pair, mod_l, g2, router):
    tm = cfg.tm
    row = lambda w: pl.BlockSpec((tm, w), lambda i: (i, 0))
    prev, nxt = _halo_specs(cfg, CONV_WIDTH)
    ep_in, ep_out, ep_shape = _epilogue_specs(cfg, x_pair)
    return pl.pallas_call(
        functools.partial(_ab_out_kernel, cfg=cfg),
        grid=(cfg.t_all // tm,),
        in_specs=[*_dual_row_specs(cfg, DIFF_WIDTH, o_pair), row(CONV_WIDTH), row(CONV_WIDTH), prev, nxt,
                  _const_spec(conv_w.shape),
                  _const_spec(wo.shape)] + ep_in,
        out_specs=ep_out,
        out_shape=ep_shape,
        scratch_shapes=[pltpu.VMEM((tm + 2 * SUBLANES_V7X, CONV_WIDTH), F32)],
        compiler_params=_cparams(("arbitrary",)),
        name="ab_out",
    )(*o_pair, gb, u, u, u, conv_w, wo, *x_pair, mod_l, g2, *router)


def _ssd_in_kernel(x_ref, xp_ref, xn_ref, mod_ref, g_ref, w_ref, cw_ref, cb_ref,
                   z_ref, xbc_ref, dt_ref, pad_ref, *, cfg):
    tm = cfg.tm
    s = SUBLANES_V7X
    m = mod_ref[0]
    g = g_ref[...]
    h = _norm_mod(x_ref[...], g, m[0:1], m[1:2])
    hb = h.astype(BF16)
    h_ext = jnp.concatenate([_norm_mod(xp_ref[...], g, m[0:1], m[1:2]), h,
                             _norm_mod(xn_ref[...], g, m[0:1], m[1:2])], axis=0).astype(BF16)
    pos, seqlen = _seq_pos(cfg, tm)
    cw = 512
    for j in range(SSD_D_INNER // cw):
        z_ref[:, j * cw:(j + 1) * cw] = jnp.dot(hb, w_ref[:, j * cw:(j + 1) * cw],
                                                preferred_element_type=F32).astype(z_ref.dtype)
    for j in range(SSD_CONV_CH // cw):
        c0 = SSD_D_INNER + j * cw
        pad = pad_ref.at[j]
        pad[...] = jnp.dot(h_ext, w_ref[:, c0:c0 + cw], preferred_element_type=F32)
        taps = cw_ref[:, j * cw:(j + 1) * cw]
        um1 = jnp.where(pos == 0, 0.0, pad[s - 1:s - 1 + tm, :])
        up1 = jnp.where(pos == seqlen - 1, 0.0, pad[s + 1:s + 1 + tm, :])
        conv = um1 * taps[0:1] + pad[s:s + tm, :] * taps[1:2] + up1 * taps[2:3] + cb_ref[:, j * cw:(j + 1) * cw]
        xbc_ref[:, j * cw:(j + 1) * cw] = _silu(conv).astype(xbc_ref.dtype)
    dt_ref[...] = jnp.dot(hb, w_ref[:, SSD_D_INNER + SSD_CONV_CH:], preferred_element_type=F32)


def _ssd_in(cfg, x, mod_l, g1, w_in, conv_w, conv_b):
    t, d, tm = cfg.t_all, cfg.d_model, cfg.tm
    row = lambda w: pl.BlockSpec((tm, w), lambda i: (i, 0))
    prev, nxt = _halo_specs(cfg, d)
    return pl.pallas_call(
        functools.partial(_ssd_in_kernel, cfg=cfg),
        grid=(t // tm,),
        in_specs=[row(d), prev, nxt, _mod_spec(cfg), _const_spec((1, d)), _const_spec(w_in.shape),
                  _const_spec(conv_w.shape), _const_spec(conv_b.shape)],
        out_specs=[row(SSD_D_INNER), row(SSD_CONV_CH), row(LANES_V7X)],
        out_shape=[jax.ShapeDtypeStruct((t, SSD_D_INNER), BF16), jax.ShapeDtypeStruct((t, SSD_CONV_CH), BF16),
                   jax.ShapeDtypeStruct((t, LANES_V7X), F32)],
        scratch_shapes=[pltpu.VMEM((SSD_CONV_CH // 512, tm + 2 * SUBLANES_V7X, 512), F32)],
        compiler_params=_cparams(("arbitrary",)),
        name="ssd_in",
    )(x, x, x, mod_l, g1, w_in, conv_w, conv_b)


def _split3(v):
    b1 = v.astype(BF16)
    r1 = v - b1.astype(F32)
    b2 = r1.astype(BF16)
    b3 = (r1 - b2.astype(F32)).astype(BF16)
    return jnp.concatenate([b1, b2, b3], axis=1)


def _softplus(v):
    return jnp.maximum(v, 0.0) + jnp.log(1.0 + jnp.exp(-jnp.abs(v)))


def _ssd_scan_kernel(xs_ref, b_ref, c_ref, dt_ref, dtt_ref, bias_ref, biast_ref, alog_ref, alogt_ref,
                     ewide_ref, y_ref, state_ref):
    q = SSD_CHUNK
    d = pl.program_id(1)
    fwd = d == 0

    @pl.when(pl.program_id(2) == 0)
    def _():
        state_ref[...] = jnp.zeros_like(state_ref)

    dt = _softplus(dt_ref[0] + bias_ref[0])
    dtt = _softplus(dtt_ref[0] + biast_ref[0])
    a = dt * (-jnp.exp(alog_ref[0]))
    at = dtt * (-jnp.exp(alogt_ref[0]))

    ri = lax.broadcasted_iota(jnp.int32, (q, q), 0)
    ci = lax.broadcasted_iota(jnp.int32, (q, q), 1)
    keep = (ri - ci) * jnp.where(fwd, 1, -1) >= 0
    tri = jnp.where(keep, 1.0, 0.0)
    cs = jnp.dot(tri, a, precision=HIGHEST, preferred_element_type=F32)
    cst = lax.dot_general(at, tri, (((1,), (1,)), ((), ())), precision=HIGHEST,
                          preferred_element_type=F32)
    tot = jnp.sum(a, axis=0, keepdims=True)

    ewide = ewide_ref[...]
    w_wide = jnp.dot(_split3(dt * jnp.exp(tot - cs)), ewide, preferred_element_type=F32)
    ecs_wide = jnp.dot(_split3(jnp.exp(cs)), ewide, preferred_element_type=F32)
    dec_wide = jnp.dot(_split3(jnp.broadcast_to(jnp.exp(tot), (SUBLANES_V7X, SSD_HEADS))), ewide,
                       preferred_element_type=F32)[0:1]

    lane = lax.broadcasted_iota(jnp.int32, (q, LANES_V7X), 1)
    lo = lane < SSD_HEAD_DIM
    hpg = SSD_HEADS // SSD_GROUPS
    gw = hpg * SSD_HEAD_DIM
    nt = (((1,), (1,)), ((), ()))
    tn = (((0,), (0,)), ((), ()))
    for g in range(SSD_GROUPS):
        bm = b_ref[:, g * SSD_STATE:(g + 1) * SSD_STATE]
        cm = c_ref[:, g * SSD_STATE:(g + 1) * SSD_STATE]
        cb = lax.dot_general(cm, bm, nt, preferred_element_type=F32)
        sl = slice(g * gw, (g + 1) * gw)
        y_off = jnp.dot(cm, state_ref[:, sl].astype(BF16), preferred_element_type=F32) * ecs_wide[:, sl]
        for pr in range(hpg // 2):
            lhs, rhs = [], []
            c0 = g * gw + pr * LANES_V7X
            xs_pair = xs_ref[:, c0:c0 + LANES_V7X]
            zero = jnp.zeros_like(xs_pair)
            for k in range(2):
                h = g * hpg + pr * 2 + k
                seg = cs[:, h:h + 1] - cst[h:h + 1, :]
                lmat = jnp.exp(jnp.where(keep, seg, -jnp.inf))
                lhs.append((cb * lmat * dtt[h:h + 1, :]).astype(BF16))
                rhs.append(jnp.where(lo if k == 0 else jnp.logical_not(lo), xs_pair, zero))
            y = jnp.dot(jnp.concatenate(lhs, axis=1), jnp.concatenate(rhs, axis=0), preferred_element_type=F32)
            y_ref[0, :, c0:c0 + LANES_V7X] = (y + y_off[:, pr * LANES_V7X:(pr + 1) * LANES_V7X]).astype(y_ref.dtype)
        x2 = (xs_ref[:, sl].astype(F32) * w_wide[:, sl]).astype(BF16)
        upd = lax.dot_general(bm, x2, tn, preferred_element_type=F32)
        state_ref[:, sl] = state_ref[:, sl] * dec_wide[:, sl] + upd


def _ssd_scan(cfg, xbc_act, dt_raw, dt_bias, a_log):
    q = SSD_CHUNK
    n_cc, n_lc = cfg.n_ctx // q, cfg.n_lat // q
    ctx0 = cfg.t_lat // q
    h = SSD_HEADS
    dt2 = dt_raw[:, :2 * h].reshape(cfg.t_all, 2, h).transpose(1, 0, 2)
    dt2t = dt2.transpose(0, 2, 1)
    bias = dt_bias.reshape(2, 1, h)
    biast = dt_bias.reshape(2, h, 1)
    alog = a_log.reshape(2, 1, h)
    alogt = a_log.reshape(2, h, 1)
    head_of_row = jnp.tile(jnp.arange(h), 3)[:, None]
    ewide = (head_of_row == (jnp.arange(SSD_D_INNER) // SSD_HEAD_DIM)[None, :]).astype(BF16)

    def blk(b, d, s):
        cs = jnp.where(d == 0, s, n_cc - 1 - s)
        ls = jnp.where(d == 0, s - n_cc, n_lc - 1 - (s - n_cc))
        return jnp.where(s < n_cc, ctx0 + b * n_cc + cs, b * n_lc + ls)

    gs = SSD_GROUPS * SSD_STATE
    col = lambda w, cblk: pl.BlockSpec((q, w), lambda b, d, s: (blk(b, d, s), cblk))
    per_dir = lambda shape: pl.BlockSpec((1,) + shape, lambda b, d, s: (d, 0, 0))
    return pl.pallas_call(
        _ssd_scan_kernel,
        grid=(cfg.batch, 2, n_cc + n_lc),
        in_specs=[col(SSD_D_INNER, 0), col(gs, SSD_D_INNER // gs), col(gs, SSD_D_INNER // gs + 1),
                  pl.BlockSpec((1, q, h), lambda b, d, s: (d, blk(b, d, s), 0)),
                  pl.BlockSpec((1, h, q), lambda b, d, s: (d, 0, blk(b, d, s))),
                  per_dir((1, h)), per_dir((h, 1)), per_dir((1, h)), per_dir((h, 1)),
                  _const_spec(ewide.shape)],
        out_specs=pl.BlockSpec((1, q, SSD_D_INNER), lambda b, d, s: (d, blk(b, d, s), 0)),
        out_shape=jax.ShapeDtypeStruct((2, cfg.t_all, SSD_D_INNER), BF16),
        scratch_shapes=[pltpu.VMEM((SSD_STATE, SSD_D_INNER), F32)],
        compiler_params=_cparams(("arbitrary", "arbitrary", "arbitrary")),
        name="ssd_scan",
    )(xbc_act, xbc_act, xbc_act, dt2, dt2t, bias, biast, alog, alogt, ewide)


def _ssd_out_kernel(yf_ref, yb_ref, xs_ref, z_ref, dw_ref, ng_ref, wo_ref,
                    xl_ref, xc_ref, mod_ref, g2_ref, wr2_ref, wrh_ref, br_ref, xo_ref, h2_ref, rt_ref, cnt_ref,
                    *, cfg):
    gw = SSD_D_INNER // SSD_GROUPS
    mix = None
    for g in range(SSD_GROUPS):
        sl = slice(g * gw, (g + 1) * gw)
        y = yf_ref[0, :, sl].astype(F32) + yb_ref[0, :, sl].astype(F32) + xs_ref[:, sl].astype(F32) * dw_ref[:, sl]
        y = y * _silu(z_ref[:, sl].astype(F32))
        y = (y * lax.rsqrt(jnp.mean(y * y, axis=-1, keepdims=True) + RMS_EPS)) * ng_ref[:, sl]
        part = jnp.dot(y.astype(BF16), wo_ref[sl, :], preferred_element_type=F32)
        mix = part if mix is None else mix + part
    _residual_norm_route(_dual_rows(cfg, xl_ref, xc_ref), mix, mod_ref[0], g2_ref[...], wr2_ref, wrh_ref, br_ref,
                         xo_ref, h2_ref, rt_ref, cnt_ref)


def _ssd_out(cfg, y2, xbc_act, z, d_wide, norm_g, wo, x, mod_l, g2, router):
    tm = cfg.tm
    row = lambda w: pl.BlockSpec((tm, w), lambda i: (i, 0))
    ydir = lambda d: pl.BlockSpec((1, tm, SSD_D_INNER), lambda i: (d, i, 0))
    ep_in, ep_out, ep_shape = _epilogue_specs(cfg, (x, x))
    return pl.pallas_call(
        functools.partial(_ssd_out_kernel, cfg=cfg),
        grid=(cfg.t_all // tm,),
        in_specs=[ydir(0), ydir(1), row(SSD_D_INNER), row(SSD_D_INNER), _const_spec((1, SSD_D_INNER)),
                  _const_spec((1, SSD_D_INNER)), _const_spec(wo.shape)] + ep_in,
        out_specs=ep_out,
        out_shape=ep_shape,
        compiler_params=_cparams(("arbitrary",)),
        name="ssd_out",
    )(y2, y2, xbc_act, z, d_wide, norm_g, wo, x, x, mod_l, g2, *router)


def _expert_kernel(eid_ref, nused_ref, *refs, blocks_per_part):
    xb_refs = refs[:MOE_GATHER_PARTS]
    wg_ref, wu_ref, wd_ref, y_ref, wg_s, wu_s, wd_s = refs[MOE_GATHER_PARTS:]
    i = pl.program_id(0)
    changed = jnp.logical_or(i == 0, eid_ref[i] != eid_ref[jnp.maximum(i - 1, 0)])

    @pl.when(jnp.logical_and(changed, i < nused_ref[0]))
    def _():
        wg_s[...] = wg_ref[0, 0].astype(BF16)
        wu_s[...] = wu_ref[0, 0].astype(BF16)
        wd_s[...] = wd_ref[0, 0].astype(BF16)

    @pl.when(i < nused_ref[0])
    def _():
        half = wg_s.shape[0] // 2
        words = xb_refs[0][...]
        for k in range(1, MOE_GATHER_PARTS):
            words = jnp.where(i >= k * blocks_per_part, xb_refs[k][...], words)
        lo, hi = _unpack_bf16_pairs(words)
        lo, hi = lo.astype(BF16), hi.astype(BF16)
        hg = (jnp.dot(lo, wg_s[:half, :], preferred_element_type=F32)
              + jnp.dot(hi, wg_s[half:, :], preferred_element_type=F32))
        hu = (jnp.dot(lo, wu_s[:half, :], preferred_element_type=F32)
              + jnp.dot(hi, wu_s[half:, :], preferred_element_type=F32))
        y = jnp.dot((_silu(hg) * hu).astype(BF16), wd_s[...], preferred_element_type=F32)
        y_ref[...] = _pack_bf16_pairs(y)

    @pl.when(i >= nused_ref[0])
    def _():
        y_ref[...] = jnp.zeros_like(y_ref)


def _experts(cfg, layer, block_eid, n_used, xb_parts, w_gate, w_up, w_down):
    d, tile = cfg.d_model, cfg.moe_tile
    per = xb_parts[0].shape[0] // tile
    n_blocks = per * MOE_GATHER_PARTS

    def part_spec(k):
        return pl.BlockSpec((tile, d // 2), lambda i, e, n: (jnp.clip(i - k * per, 0, per - 1), 0))

    grid_spec = pltpu.PrefetchScalarGridSpec(
        num_scalar_prefetch=2,
        grid=(n_blocks,),
        in_specs=[part_spec(k) for k in range(MOE_GATHER_PARTS)] + [
                  pl.BlockSpec((1, 1, d, MOE_HIDDEN), lambda i, e, n: (layer, e[i], 0, 0)),
                  pl.BlockSpec((1, 1, d, MOE_HIDDEN), lambda i, e, n: (layer, e[i], 0, 0)),
                  pl.BlockSpec((1, 1, MOE_HIDDEN, d), lambda i, e, n: (layer, e[i], 0, 0))],
        out_specs=pl.BlockSpec((tile, d // 2), lambda i, e, n: (i, 0)),
        scratch_shapes=[pltpu.VMEM((d, MOE_HIDDEN), BF16), pltpu.VMEM((d, MOE_HIDDEN), BF16),
                        pltpu.VMEM((MOE_HIDDEN, d), BF16)],
    )
    return pl.pallas_call(
        functools.partial(_expert_kernel, blocks_per_part=per),
        grid_spec=grid_spec,
        out_shape=jax.ShapeDtypeStruct((n_blocks * tile, d // 2), F32),
        compiler_params=_cparams(("arbitrary",)),
        name="moe_experts",
    )(block_eid, n_used, *xb_parts, w_gate, w_up, w_down)


def _dispatch_kernel(rt_ref, cnt_ref, pos_ref, carry_ref, start_ref, *, tile):
    tm = rt_ref.shape[0]
    rt = rt_ref[...]
    lane = lax.broadcasted_iota(jnp.int32, rt.shape, 1)
    o1, o2 = _expert_onehots(rt)
    cnt1 = jnp.sum(o1, axis=0, keepdims=True)
    cnt2 = jnp.sum(o2, axis=0, keepdims=True)

    @pl.when(pl.program_id(0) == 0)
    def _():
        padded = jnp.floor((cnt_ref[...] + (tile - 1)) * (1.0 / tile)) * tile
        r = lax.broadcasted_iota(jnp.int32, (ROUTE_LANES, ROUTE_LANES), 0)
        c = lax.broadcasted_iota(jnp.int32, (ROUTE_LANES, ROUTE_LANES), 1)
        excl = jnp.where(r < c, 1.0, 0.0)
        start_ref[...] = jnp.dot(padded, excl, precision=HIGHEST, preferred_element_type=F32)
        carry_ref[...] = jnp.zeros_like(carry_ref)

    ri = lax.broadcasted_iota(jnp.int32, (tm, tm), 0)
    ci = lax.broadcasted_iota(jnp.int32, (tm, tm), 1)
    earlier = jnp.where(ci < ri, 1.0, 0.0).astype(BF16)
    p1 = jnp.dot(earlier, o1.astype(BF16), preferred_element_type=F32)
    p2 = jnp.dot(earlier, o2.astype(BF16), preferred_element_type=F32)
    base = start_ref[0:1] + carry_ref[0:1]
    pos1 = jnp.sum(o1 * (base + p1), axis=-1, keepdims=True)
    pos2 = jnp.sum(o2 * (base + cnt1 + p2), axis=-1, keepdims=True)
    pos_ref[...] = jnp.where(lane == 0, pos1, jnp.where(lane == 1, pos2, 0.0)).astype(jnp.int32)
    carry_ref[...] += cnt1 + cnt2


def _dispatch(cfg, route, cnt):
    tile, tm = cfg.moe_tile, cfg.tm
    t = cfg.t_all
    a_total = 2 * t
    pos = pl.pallas_call(
        functools.partial(_dispatch_kernel, tile=tile),
        grid=(t // tm,),
        in_specs=[pl.BlockSpec((tm, ROUTE_LANES), lambda i: (i, 0)), _const_spec((SUBLANES_V7X, ROUTE_LANES))],
        out_specs=pl.BlockSpec((tm, ROUTE_LANES), lambda i: (i, 0)),
        out_shape=jax.ShapeDtypeStruct((t, ROUTE_LANES), jnp.int32),
        scratch_shapes=[pltpu.VMEM((SUBLANES_V7X, ROUTE_LANES), F32), pltpu.VMEM((SUBLANES_V7X, ROUTE_LANES), F32)],
        compiler_params=_cparams(("arbitrary",)),
        name="moe_dispatch",
    )(route, cnt)
    counts = cnt[0, :MOE_EXPERTS].astype(jnp.int32)
    pend = jnp.cumsum((counts + tile - 1) // tile * tile)
    n_blocks = (a_total + MOE_EXPERTS * (tile - 1)) // tile
    n_blocks = -(-n_blocks // MOE_GATHER_PARTS) * MOE_GATHER_PARTS
    block_eid = jnp.minimum(jnp.sum(pend[None, :] <= (jnp.arange(n_blocks, dtype=jnp.int32) * tile)[:, None], axis=1),
                            MOE_EXPERTS - 1).astype(jnp.int32)
    n_used = (pend[-1] // tile).astype(jnp.int32).reshape(1)
    pos2 = pos[:, 0:2]
    slot_tok = (jnp.arange(n_blocks * tile, dtype=jnp.int32) % t).at[pos2.reshape(-1)].set(
        jnp.arange(a_total, dtype=jnp.int32) // 2, unique_indices=True, mode="promise_in_bounds")
    return slot_tok, block_eid, n_used, pos2


def _combine_kernel(x_ref, y1_ref, y2_ref, rt_ref, mod_ref, g_ref, o_ref, *, final):
    rt = rt_ref[...]
    a_lo, a_hi = _unpack_bf16_pairs(y1_ref[...])
    b_lo, b_hi = _unpack_bf16_pairs(y2_ref[...])
    w1, w2 = rt[:, 2:3], rt[:, 3:4]
    f = jnp.concatenate([w1 * a_lo + w2 * b_lo, w1 * a_hi + w2 * b_hi], axis=1)
    xn = x_ref[...] + mod_ref[0][5:6] * f
    if final:
        xn = (xn * lax.rsqrt(jnp.mean(xn * xn, axis=-1, keepdims=True) + RMS_EPS)) * g_ref[...]
    o_ref[...] = xn


def _combine(cfg, x, y12, route, mod_l, g, final):
    d, tm = cfg.d_model, cfg.tm
    n_tiles = cfg.t_all // tm
    row = lambda w: pl.BlockSpec((tm, w), lambda i: (i, 0))
    t_out = cfg.t_lat if final else cfg.t_all
    return pl.pallas_call(
        functools.partial(_combine_kernel, final=final),
        grid=(t_out // tm,),
        in_specs=[row(d), row(d // 2), pl.BlockSpec((tm, d // 2), lambda i: (n_tiles + i, 0)),
                  row(ROUTE_LANES), _mod_spec(cfg), _const_spec((1, d))],
        out_specs=row(d),
        out_shape=jax.ShapeDtypeStruct((t_out, d), F32),
        compiler_params=_cparams(("arbitrary",)),
        name="moe_combine",
    )(x, y12, y12, route, mod_l, g)


def _forward(cfg, x, c, ctx, c_ctx, ada_w, ada_b, norm1_g, norm2_g, ab_w_in, ab_w_out,
             diff_lq1, diff_lk1, diff_lq2, diff_lk2, diff_subln_g, bconv_w,
             ssd_w_in, ssd_conv_w, ssd_conv_b, ssd_A_log, ssd_dt_bias, ssd_D, ssd_norm_g, ssd_w_out,
             moe_wg, moe_bg, moe_we, moe_be, moe_w_gate, moe_w_up, moe_w_down, final_norm_g):
    d = cfg.d_model
    b = cfg.batch
    x_pair = (x.reshape(cfg.t_lat, d), ctx.reshape(cfg.t_ctx, d))
    c_all = jnp.zeros((SUBLANES_V7X, d), F32).at[:b].set(c).at[b].set(c_ctx)
    mod = _adaln(cfg, c_all, ada_w, ada_b)
    tables = _rope_tables(cfg)

    for layer in range(cfg.depth):
        i = layer // 2
        mod_l = mod[layer]
        g1 = norm1_g[layer].reshape(1, d)
        g2 = norm2_g[layer].reshape(1, d)
        router = _router_weights(moe_wg[layer], moe_bg[layer], moe_we[layer], moe_be[layer])
        if layer % 2 == 0:
            lam_init = 0.8 - 0.6 * math.exp(-0.3 * layer)
            q, k, v, gb, u = _ab_in(cfg, x_pair, mod_l, g1, ab_w_in[i].astype(BF16), tables)
            lvec = jnp.stack([diff_lq1[i], diff_lk1[i], diff_lq2[i], diff_lk2[i]])
            o_pair = _attention(cfg, q, k, v, lvec, diff_subln_g[i].reshape(1, DIFF_V_DIM), lam_init)
            w_out = ab_w_out[i].astype(BF16)
            xa, h2, route, cnt = _ab_out(cfg, o_pair, gb, u, bconv_w[i], w_out, x_pair, mod_l, g2, router)
        else:
            w_in = jnp.pad(ssd_w_in[i].astype(BF16), ((0, 0), (0, LANES_V7X - 2 * SSD_HEADS)))
            z, xbc_act, dt_raw = _ssd_in(cfg, xa, mod_l, g1, w_in, ssd_conv_w[i],
                                         ssd_conv_b[i].reshape(1, SSD_CONV_CH))
            y2 = _ssd_scan(cfg, xbc_act, dt_raw, ssd_dt_bias[i], ssd_A_log[i])
            d_wide = jnp.repeat(ssd_D[i], SSD_HEAD_DIM).reshape(1, SSD_D_INNER)
            xa, h2, route, cnt = _ssd_out(cfg, y2, xbc_act, z, d_wide, ssd_norm_g[i].reshape(1, SSD_D_INNER),
                                          ssd_w_out[i].astype(BF16), xa, mod_l, g2, router)
        slot_tok, block_eid, n_used, pos = _dispatch(cfg, route, cnt)
        xb_parts = [h2[p_] for p_ in jnp.split(slot_tok, MOE_GATHER_PARTS)]
        yb = _experts(cfg, layer, block_eid, n_used, xb_parts, moe_w_gate, moe_w_up, moe_w_down)
        last = layer == cfg.depth - 1
        y12 = yb[jnp.concatenate([pos[:, 0], pos[:, 1]])]
        xa = _combine(cfg, xa, y12, route, mod_l, final_norm_g.reshape(1, d), last)
        x_pair = (xa, xa)
    return xa.reshape(b, cfg.n_lat, d)


def kernel(x, c, ctx, c_ctx, ada_w, ada_b, norm1_g, norm2_g, ab_w_in, ab_w_out, diff_lq1, diff_lk1, diff_lq2, diff_lk2, diff_subln_g, bconv_w, ssd_w_in, ssd_conv_w, ssd_conv_b, ssd_A_log, ssd_dt_bias, ssd_D, ssd_norm_g, ssd_w_out, moe_wg, moe_bg, moe_we, moe_be, moe_w_gate, moe_w_up, moe_w_down, final_norm_g):
    cfg = Cfg(batch=x.shape[0], n_lat=x.shape[1], n_ctx=ctx.shape[1], d_model=x.shape[2], depth=ada_w.shape[0],
              tm=512, tq=256, moe_tile=512, attn_subtiles=2, kchunk=1024)
    return _forward(cfg, x, c, ctx, c_ctx, ada_w, ada_b, norm1_g, norm2_g, ab_w_in, ab_w_out,
                    diff_lq1, diff_lk1, diff_lq2, diff_lk2, diff_subln_g, bconv_w,
                    ssd_w_in, ssd_conv_w, ssd_conv_b, ssd_A_log, ssd_dt_bias, ssd_D, ssd_norm_g, ssd_w_out,
                    moe_wg, moe_bg, moe_we, moe_be, moe_w_gate, moe_w_up, moe_w_down, final_norm_g)
```

```python
import functools
import math
from typing import NamedTuple

import jax
import jax.numpy as jnp
from jax import lax
from jax.experimental import pallas as pl
from jax.experimental.pallas import tpu as pltpu

F32 = jnp.float32
BF16 = jnp.bfloat16
HIGHEST = lax.Precision.HIGHEST

LANES_V7X = 128
SUBLANES_V7X = 8
VMEM_LIMIT_BYTES_V7X = 56 * 1024 * 1024

RMS_EPS = 1e-6
GRID_W = 64
N_DIFF_HEADS = 4
DIFF_QK_DIM = 64
DIFF_V_DIM = 128
DIFF_WIDTH = 512
CONV_WIDTH = 512
ROPE_BASE = 10000.0
SSD_D_INNER = 2048
SSD_HEAD_DIM = 64
SSD_HEADS = 32
SSD_GROUPS = 4
SSD_STATE = 128
SSD_CHUNK = 128
SSD_CONV_CH = SSD_D_INNER + 2 * SSD_GROUPS * SSD_STATE
MOE_GROUPS = 4
MOE_EXPERTS_PER_GROUP = 8
MOE_EXPERTS = 32
MOE_HIDDEN = 512
ROUTE_LANES = LANES_V7X
MOE_GATHER_PARTS = 3


class Cfg(NamedTuple):
    batch: int
    n_lat: int
    n_ctx: int
    d_model: int
    depth: int
    tm: int
    tq: int
    moe_tile: int
    attn_subtiles: int
    kchunk: int

    @property
    def t_lat(self):
        return self.batch * self.n_lat

    @property
    def t_ctx(self):
        return self.batch * self.n_ctx

    @property
    def t_all(self):
        return self.t_lat + self.t_ctx


def _cparams(sem):
    return pltpu.CompilerParams(dimension_semantics=sem, vmem_limit_bytes=VMEM_LIMIT_BYTES_V7X)


def _silu(v):
    return v * (1.0 / (1.0 + jnp.exp(-v)))


def _const_spec(shape):
    nd = len(shape)
    return pl.BlockSpec(shape, lambda *_: (0,) * nd)


def _mod_spec(cfg):
    return pl.BlockSpec((1, 6, cfg.d_model),
                        lambda i: (jnp.minimum((i * cfg.tm) // cfg.n_lat, cfg.batch), 0, 0))


def _seq_pos(cfg, tm):
    r0 = pl.program_id(0) * tm
    row = r0 + lax.broadcasted_iota(jnp.int32, (tm, 1), 0)
    seqlen = jnp.where(r0 >= cfg.t_lat, cfg.n_ctx, cfg.n_lat)
    return row & (seqlen - 1), seqlen


def _norm_mod(x, g, shift, scale):
    ms = jnp.mean(x * x, axis=-1, keepdims=True)
    return (x * lax.rsqrt(ms + RMS_EPS) * g) * (1.0 + scale) + shift


def _adaln_kernel(c_ref, w_ref, b_ref, o_ref):
    sc = _silu(c_ref[...])
    o_ref[0] = jnp.dot(sc, w_ref[0], precision=HIGHEST, preferred_element_type=F32) + b_ref[0]


def _adaln(cfg, c_all, ada_w, ada_b):
    d = cfg.d_model
    out = pl.pallas_call(
        _adaln_kernel,
        grid=(cfg.depth, 6),
        in_specs=[_const_spec((SUBLANES_V7X, d)),
                  pl.BlockSpec((1, d, d), lambda l, j: (l, 0, j)),
                  pl.BlockSpec((1, 1, d), lambda l, j: (l, 0, j))],
        out_specs=pl.BlockSpec((1, SUBLANES_V7X, d), lambda l, j: (l, 0, j)),
        out_shape=jax.ShapeDtypeStruct((cfg.depth, SUBLANES_V7X, 6 * d), F32),
        compiler_params=_cparams(("arbitrary", "arbitrary")),
        name="adaln",
    )(c_all, ada_w, ada_b.reshape(cfg.depth, 1, 6 * d))
    return out.reshape(cfg.depth, SUBLANES_V7X, 6, d)


def _rope_tables(cfg):
    n = cfg.n_lat
    rows = n // GRID_W
    row = jnp.broadcast_to(jnp.arange(rows, dtype=F32)[:, None], (rows, GRID_W)).reshape(n)
    col = jnp.broadcast_to(jnp.arange(GRID_W, dtype=F32)[None, :], (rows, GRID_W)).reshape(n)
    axis_dim = DIFF_QK_DIM // 2
    inv_freq = ROPE_BASE ** (-jnp.arange(0, axis_dim, 2, dtype=F32) / axis_dim)
    ang_r = row[:, None] * inv_freq
    ang_c = col[:, None] * inv_freq
    zeros = jnp.zeros_like(ang_r)
    cos64 = jnp.concatenate([jnp.cos(ang_r), jnp.cos(ang_r), jnp.cos(ang_c), jnp.cos(ang_c)], axis=1)
    sa64 = jnp.concatenate([zeros, jnp.sin(ang_r), zeros, jnp.sin(ang_c)], axis=1)
    sb64 = jnp.concatenate([-jnp.sin(ang_r), zeros, -jnp.sin(ang_c), zeros], axis=1)
    ident = cfg.tm
    cos = jnp.concatenate([jnp.tile(cos64, (1, 2)), jnp.ones((ident, LANES_V7X), F32)], axis=0)
    sa = jnp.concatenate([jnp.tile(sa64, (1, 2)), jnp.zeros((ident, LANES_V7X), F32)], axis=0)
    sb = jnp.concatenate([jnp.tile(sb64, (1, 2)), jnp.zeros((ident, LANES_V7X), F32)], axis=0)
    return cos, sa, sb


def _dual_row_specs(cfg, width, pair):
    lat_arr, ctx_arr = pair
    nl = cfg.t_lat // cfg.tm
    c0 = nl if ctx_arr is lat_arr else 0
    return (pl.BlockSpec((cfg.tm, width), lambda i: (jnp.minimum(i, nl - 1), 0)),
            pl.BlockSpec((cfg.tm, width), lambda i: (jnp.maximum(i - nl, 0) + c0, 0)))


def _dual_rows(cfg, lat_ref, ctx_ref):
    return jnp.where(pl.program_id(0) >= cfg.t_lat // cfg.tm, ctx_ref[...], lat_ref[...])


def _ab_in_kernel(xl_ref, xc_ref, mod_ref, g_ref, w_ref, cos_ref, sa_ref, sb_ref,
                  q_ref, k_ref, v_ref, gb_ref, u_ref, *, cfg):
    m = mod_ref[0]
    hb = _norm_mod(_dual_rows(cfg, xl_ref, xc_ref), g_ref[...], m[0:1], m[1:2]).astype(BF16)
    cos, sa, sb = cos_ref[...], sa_ref[...], sb_ref[...]

    def proj(j):
        return jnp.dot(hb, w_ref[:, j * DIFF_WIDTH:(j + 1) * DIFF_WIDTH], preferred_element_type=F32)

    def rope(t, out_ref, scale):
        for gi in range(DIFF_WIDTH // LANES_V7X):
            xg = t[:, gi * LANES_V7X:(gi + 1) * LANES_V7X]
            r = xg * cos + pltpu.roll(xg, 16, 1) * sa + pltpu.roll(xg, LANES_V7X - 16, 1) * sb
            out_ref[:, gi * LANES_V7X:(gi + 1) * LANES_V7X] = (r * scale).astype(out_ref.dtype)

    rope(proj(0), q_ref, math.log2(math.e) * DIFF_QK_DIM ** -0.5)
    rope(proj(1), k_ref, 1.0)
    v_ref[...] = proj(2).astype(v_ref.dtype)
    gb_ref[...] = proj(3)
    u_ref[...] = proj(4) * proj(5)


def _ab_in(cfg, x_pair, mod_l, g1, w_in_bf16, tables):
    t, d, tm = cfg.t_all, cfg.d_model, cfg.tm
    n_lat_tiles, per_seq = cfg.t_lat // tm, cfg.n_lat // tm
    tab_spec = pl.BlockSpec((tm, LANES_V7X), lambda i: (jnp.where(i < n_lat_tiles, i % per_seq, per_seq), 0))
    row = lambda w: pl.BlockSpec((tm, w), lambda i: (i, 0))
    return pl.pallas_call(
        functools.partial(_ab_in_kernel, cfg=cfg),
        grid=(t // tm,),
        in_specs=[*_dual_row_specs(cfg, d, x_pair), _mod_spec(cfg), _const_spec((1, d)),
                  _const_spec(w_in_bf16.shape), tab_spec, tab_spec, tab_spec],
        out_specs=[row(DIFF_WIDTH)] * 5,
        out_shape=[jax.ShapeDtypeStruct((t, DIFF_WIDTH), BF16)] * 3
        + [jax.ShapeDtypeStruct((t, CONV_WIDTH), F32)] * 2,
        compiler_params=_cparams(("arbitrary",)),
        name="ab_in",
    )(*x_pair, mod_l, g1, w_in_bf16, *tables)


def _attn_body(lvec_ref, q_ref, kc_ref, kl_ref, vc_ref, vl_ref, g_ref, o_ref, *, tq, lam_init, kchunk):
    lv = lvec_ref[...]
    lam = (jnp.exp(jnp.sum(lv[0:1] * lv[1:2], axis=-1, keepdims=True))
           - jnp.exp(jnp.sum(lv[2:3] * lv[3:4], axis=-1, keepdims=True)) + lam_init)
    nt = (((1,), (1,)), ((), ()))
    half = DIFF_QK_DIM
    chunks = [(kc_ref, vc_ref, 0, kc_ref.shape[0])]
    if kl_ref is not None:
        chunks += [(kl_ref, vl_ref, r0, kchunk) for r0 in range(0, kl_ref.shape[0], kchunk)]
    for j in range(q_ref.shape[0] // tq):
        q = q_ref[j * tq:(j + 1) * tq, :].astype(F32)
        lane = lax.broadcasted_iota(jnp.int32, q.shape, 1)
        q2 = jnp.concatenate([jnp.where(lane < half, q, 0.0), jnp.where(lane >= half, q, 0.0)],
                             axis=0).astype(BF16)
        ps, ms, ls = [], [], []
        for k_ref, _, r0, n in chunks:
            s = lax.dot_general(q2, k_ref[r0:r0 + n, :], nt, preferred_element_type=F32)
            m_c = jnp.max(s, axis=-1, keepdims=True)
            p = jnp.exp2(s - m_c)
            ps.append(p)
            ms.append(m_c)
            ls.append(jnp.sum(p, axis=-1, keepdims=True))
        mx = functools.reduce(jnp.maximum, ms)
        scale = [jnp.exp2(m_c - mx) for m_c in ms]
        r = 1.0 / functools.reduce(jnp.add, [sc * l_c for sc, l_c in zip(scale, ls)])
        pv = None
        for p, sc, (_, v_ref, r0, n) in zip(ps, scale, chunks):
            w = sc * r
            pd = (p[:tq] * w[:tq] - p[tq:] * (lam * w[tq:])).astype(BF16)
            part = jnp.dot(pd, v_ref[r0:r0 + n, :], preferred_element_type=F32)
            pv = part if pv is None else pv + part
        o = pv * lax.rsqrt(jnp.mean(pv * pv, axis=-1, keepdims=True) + RMS_EPS)
        o_ref[j * tq:(j + 1) * tq, :] = ((o * g_ref[...]) * (1.0 - lam_init)).astype(o_ref.dtype)


def _attn_lat_kernel(lvec_ref, q_ref, kc_ref, kl_ref, vc_ref, vl_ref, g_ref, o_ref, **kw):
    _attn_body(lvec_ref, q_ref, kc_ref, kl_ref, vc_ref, vl_ref, g_ref, o_ref, **kw)


def _attn_ctx_kernel(lvec_ref, q_ref, kc_ref, vc_ref, g_ref, o_ref, **kw):
    _attn_body(lvec_ref, q_ref, kc_ref, None, vc_ref, None, g_ref, o_ref, **kw)


def _attention(cfg, q, k, v, lvec, subln_g, lam_init):
    tq, nsub = cfg.tq, cfg.attn_subtiles
    tstep = tq * nsub
    nqb = cfg.n_lat // tstep
    ctx_blk0 = cfg.t_lat // cfg.n_ctx
    hw = DIFF_V_DIM
    ctx_spec = pl.BlockSpec((cfg.n_ctx, hw), lambda b, h, *_: (ctx_blk0 + b, h))
    lat_spec = pl.BlockSpec((cfg.n_lat, hw), lambda b, h, *_: (b, h))
    q_spec = pl.BlockSpec((tstep, hw), lambda b, h, i: (b * nqb + i, h))
    kw = dict(tq=tq, lam_init=lam_init, kchunk=min(cfg.kchunk, cfg.n_lat))
    o_lat = pl.pallas_call(
        functools.partial(_attn_lat_kernel, **kw),
        grid=(cfg.batch, N_DIFF_HEADS, nqb),
        in_specs=[_const_spec(lvec.shape), q_spec, ctx_spec, lat_spec, ctx_spec, lat_spec, _const_spec((1, hw))],
        out_specs=q_spec,
        out_shape=jax.ShapeDtypeStruct((cfg.t_lat, DIFF_WIDTH), BF16),
        compiler_params=_cparams(("arbitrary", "arbitrary", "arbitrary")),
        name="diff_attn",
    )(lvec, q, k, k, v, v, subln_g)
    tqc = min(tq, cfg.n_ctx)
    o_ctx = pl.pallas_call(
        functools.partial(_attn_ctx_kernel, tq=tqc, lam_init=lam_init, kchunk=cfg.n_ctx),
        grid=(cfg.batch, N_DIFF_HEADS),
        in_specs=[_const_spec(lvec.shape), ctx_spec, ctx_spec, ctx_spec, _const_spec((1, hw))],
        out_specs=pl.BlockSpec((cfg.n_ctx, hw), lambda b, h: (b, h)),
        out_shape=jax.ShapeDtypeStruct((cfg.t_ctx, DIFF_WIDTH), BF16),
        compiler_params=_cparams(("arbitrary", "arbitrary")),
        name="diff_attn_ctx",
    )(lvec, q, k, v, subln_g)
    return o_lat, o_ctx


def _route_from_logits(lg):
    lane = lax.broadcasted_iota(jnp.int32, lg.shape, 1)
    neg = -jnp.inf
    big = jnp.int32(ROUTE_LANES)
    gl = jnp.where(lane < MOE_GROUPS, lg, neg)
    gmax = jnp.max(gl, axis=-1, keepdims=True)
    gidx = jnp.min(jnp.where(gl == gmax, lane, big), axis=-1, keepdims=True)
    gw = 1.0 / jnp.sum(jnp.exp(gl - gmax), axis=-1, keepdims=True)
    lo = MOE_GROUPS + gidx * MOE_EXPERTS_PER_GROUP
    el = jnp.where((lane >= lo) & (lane < lo + MOE_EXPERTS_PER_GROUP), lg, neg)
    m1 = jnp.max(el, axis=-1, keepdims=True)
    i1 = jnp.min(jnp.where(el == m1, lane, big), axis=-1, keepdims=True)
    el2 = jnp.where(lane == i1, neg, el)
    m2 = jnp.max(el2, axis=-1, keepdims=True)
    i2 = jnp.min(jnp.where(el2 == m2, lane, big), axis=-1, keepdims=True)
    w1 = gw / (1.0 + jnp.exp(m2 - m1))
    w2 = gw - w1
    e1 = (i1 - MOE_GROUPS).astype(F32)
    e2 = (i2 - MOE_GROUPS).astype(F32)
    return jnp.where(lane == 0, e1, jnp.where(lane == 1, e2, jnp.where(lane == 2, w1, jnp.where(lane == 3, w2, 0.0))))


def _pack_bf16_pairs(v):
    w = v.shape[1] // 2
    lo = pltpu.bitcast(v[:, :w].astype(BF16).astype(F32), jnp.uint32)
    hi = pltpu.bitcast(v[:, w:].astype(BF16).astype(F32), jnp.uint32)
    return pltpu.bitcast((hi & jnp.uint32(0xFFFF0000)) | (lo >> 16), F32)


def _unpack_bf16_pairs(words):
    u = pltpu.bitcast(words, jnp.uint32)
    lo = pltpu.bitcast(u << 16, F32)
    hi = pltpu.bitcast(u & jnp.uint32(0xFFFF0000), F32)
    return lo, hi


def _router_weights(wg, bg, we, be):
    d = wg.shape[0]
    pad = ROUTE_LANES - MOE_GROUPS - MOE_EXPERTS
    wr = jnp.concatenate([wg, we, jnp.zeros((d, pad), F32)], axis=1)
    hi = wr.astype(BF16)
    lo = (wr - hi.astype(F32)).astype(BF16)
    br = jnp.concatenate([bg, be, jnp.zeros((pad,), F32)]).reshape(1, ROUTE_LANES)
    return jnp.concatenate([hi, lo], axis=1), hi, br


def _expert_onehots(rt):
    lane = lax.broadcasted_iota(jnp.int32, rt.shape, 1)
    return (jnp.where(lane == rt[:, 0:1].astype(jnp.int32), 1.0, 0.0),
            jnp.where(lane == rt[:, 1:2].astype(jnp.int32), 1.0, 0.0))


def _residual_norm_route(x, mix, m, g2, wr2_ref, wrh_ref, br_ref, xo_ref, h2_ref, rt_ref, cnt_ref):
    xn = x + m[2:3] * mix
    xo_ref[...] = xn
    h2 = _norm_mod(xn, g2, m[3:4], m[4:5])
    h2_ref[...] = _pack_bf16_pairs(h2)
    hh = h2.astype(BF16)
    hl = (h2 - hh.astype(F32)).astype(BF16)
    a = jnp.dot(hh, wr2_ref[...], preferred_element_type=F32)
    lg = (a[:, :ROUTE_LANES] + a[:, ROUTE_LANES:]) + jnp.dot(hl, wrh_ref[...], preferred_element_type=F32)
    rt = _route_from_logits(lg + br_ref[...])
    rt_ref[...] = rt

    @pl.when(pl.program_id(0) == 0)
    def _():
        cnt_ref[...] = jnp.zeros_like(cnt_ref)

    o1, o2 = _expert_onehots(rt)
    cnt_ref[...] += jnp.sum(o1 + o2, axis=0, keepdims=True)


def _epilogue_specs(cfg, x_pair):
    d, tm = cfg.d_model, cfg.tm
    row = lambda w: pl.BlockSpec((tm, w), lambda i: (i, 0))
    in_specs = [*_dual_row_specs(cfg, d, x_pair), _mod_spec(cfg), _const_spec((1, d)),
                _const_spec((d, 2 * ROUTE_LANES)),
                _const_spec((d, ROUTE_LANES)), _const_spec((1, ROUTE_LANES))]
    out_specs = [row(d), row(d // 2), row(ROUTE_LANES), _const_spec((SUBLANES_V7X, ROUTE_LANES))]
    out_shape = [jax.ShapeDtypeStruct((cfg.t_all, d), F32), jax.ShapeDtypeStruct((cfg.t_all, d // 2), F32),
                 jax.ShapeDtypeStruct((cfg.t_all, ROUTE_LANES), F32),
                 jax.ShapeDtypeStruct((SUBLANES_V7X, ROUTE_LANES), F32)]
    return in_specs, out_specs, out_shape


def _halo_specs(cfg, width, col_block=0):
    per = cfg.tm // SUBLANES_V7X
    last = cfg.t_all // SUBLANES_V7X - 1
    prev = pl.BlockSpec((SUBLANES_V7X, width), lambda i, *_: (jnp.maximum(i * per - 1, 0), col_block))
    nxt = pl.BlockSpec((SUBLANES_V7X, width), lambda i, *_: (jnp.minimum((i + 1) * per, last), col_block))
    return prev, nxt


def _shifted_rows(pad_ref, u, prev_blk, next_blk, pos, seqlen):
    tm = u.shape[0]
    s = SUBLANES_V7X
    pad_ref[s:s + tm, :] = u
    pad_ref[0:s, :] = prev_blk
    pad_ref[s + tm:2 * s + tm, :] = next_blk
    um1 = jnp.where(pos == 0, 0.0, pad_ref[s - 1:s - 1 + tm, :])
    up1 = jnp.where(pos == seqlen - 1, 0.0, pad_ref[s + 1:s + 1 + tm, :])
    return um1, up1


def _ab_out_kernel(ol_ref, oc_ref, gb_ref, u_ref, up_ref, un_ref, cw_ref, wo_ref,
                   xl_ref, xc_ref, mod_ref, g2_ref, wr2_ref, wrh_ref, br_ref,
                   xo_ref, h2_ref, rt_ref, cnt_ref, pad_ref, *, cfg):
    pos, seqlen = _seq_pos(cfg, cfg.tm)
    u = u_ref[...]
    um1, up1 = _shifted_rows(pad_ref, u, up_ref[...], un_ref[...], pos, seqlen)
    cw = cw_ref[...]
    conv = gb_ref[...] * (um1 * cw[0:1] + u * cw[1:2] + up1 * cw[2:3])
    mix = (jnp.dot(_dual_rows(cfg, ol_ref, oc_ref), wo_ref[:DIFF_WIDTH, :], preferred_element_type=F32)
           + jnp.dot(conv.astype(BF16), wo_ref[DIFF_WIDTH:, :], preferred_element_type=F32))
    _residual_norm_route(_dual_rows(cfg, xl_ref, xc_ref), mix, mod_ref[0], g2_ref[...], wr2_ref, wrh_ref, br_ref,
                         xo_ref, h2_ref, rt_ref, cnt_ref)


def _ab_out(cfg, o_pair, gb, u, conv_w, wo, x_pair, mod_l, g2, router):
    tm = cfg.tm
    row = lambda w: pl.BlockSpec((tm, w), lambda i: (i, 0))
    prev, nxt = _halo_specs(cfg, CONV_WIDTH)
    ep_in, ep_out, ep_shape = _epilogue_specs(cfg, x_pair)
    return pl.pallas_call(
        functools.partial(_ab_out_kernel, cfg=cfg),
        grid=(cfg.t_all // tm,),
        in_specs=[*_dual_row_specs(cfg, DIFF_WIDTH, o_pair), row(CONV_WIDTH), row(CONV_WIDTH), prev, nxt,
                  _const_spec(conv_w.shape),
                  _const_spec(wo.shape)] + ep_in,
        out_specs=ep_out,
        out_shape=ep_shape,
        scratch_shapes=[pltpu.VMEM((tm + 2 * SUBLANES_V7X, CONV_WIDTH), F32)],
        compiler_params=_cparams(("arbitrary",)),
        name="ab_out",
    )(*o_pair, gb, u, u, u, conv_w, wo, *x_pair, mod_l, g2, *router)


def _ssd_in_kernel(x_ref, xp_ref, xn_ref, mod_ref, g_ref, w_ref, cw_ref, cb_ref,
                   z_ref, xbc_ref, dt_ref, pad_ref, *, cfg):
    tm = cfg.tm
    s = SUBLANES_V7X
    m = mod_ref[0]
    g = g_ref[...]
    h = _norm_mod(x_ref[...], g, m[0:1], m[1:2])
    hb = h.astype(BF16)
    h_ext = jnp.concatenate([_norm_mod(xp_ref[...], g, m[0:1], m[1:2]), h,
                             _norm_mod(xn_ref[...], g, m[0:1], m[1:2])], axis=0).astype(BF16)
    pos, seqlen = _seq_pos(cfg, tm)
    cw = 512
    for j in range(SSD_D_INNER // cw):
        z_ref[:, j * cw:(j + 1) * cw] = jnp.dot(hb, w_ref[:, j * cw:(j + 1) * cw],
                                                preferred_element_type=F32).astype(z_ref.dtype)
    for j in range(SSD_CONV_CH // cw):
        c0 = SSD_D_INNER + j * cw
        pad = pad_ref.at[j]
        pad[...] = jnp.dot(h_ext, w_ref[:, c0:c0 + cw], preferred_element_type=F32)
        taps = cw_ref[:, j * cw:(j + 1) * cw]
        um1 = jnp.where(pos == 0, 0.0, pad[s - 1:s - 1 + tm, :])
        up1 = jnp.where(pos == seqlen - 1, 0.0, pad[s + 1:s + 1 + tm, :])
        conv = um1 * taps[0:1] + pad[s:s + tm, :] * taps[1:2] + up1 * taps[2:3] + cb_ref[:, j * cw:(j + 1) * cw]
        xbc_ref[:, j * cw:(j + 1) * cw] = _silu(conv).astype(xbc_ref.dtype)
    dt_ref[...] = jnp.dot(hb, w_ref[:, SSD_D_INNER + SSD_CONV_CH:], preferred_element_type=F32)


def _ssd_in(cfg, x, mod_l, g1, w_in, conv_w, conv_b):
    t, d, tm = cfg.t_all, cfg.d_model, cfg.tm
    row = lambda w: pl.BlockSpec((tm, w), lambda i: (i, 0))
    prev, nxt = _halo_specs(cfg, d)
    return pl.pallas_call(
        functools.partial(_ssd_in_kernel, cfg=cfg),
        grid=(t // tm,),
        in_specs=[row(d), prev, nxt, _mod_spec(cfg), _const_spec((1, d)), _const_spec(w_in.shape),
                  _const_spec(conv_w.shape), _const_spec(conv_b.shape)],
        out_specs=[row(SSD_D_INNER), row(SSD_CONV_CH), row(LANES_V7X)],
        out_shape=[jax.ShapeDtypeStruct((t, SSD_D_INNER), BF16), jax.ShapeDtypeStruct((t, SSD_CONV_CH), BF16),
                   jax.ShapeDtypeStruct((t, LANES_V7X), F32)],
        scratch_shapes=[pltpu.VMEM((SSD_CONV_CH // 512, tm + 2 * SUBLANES_V7X, 512), F32)],
        compiler_params=_cparams(("arbitrary",)),
        name="ssd_in",
    )(x, x, x, mod_l, g1, w_in, conv_w, conv_b)


def _split3(v):
    b1 = v.astype(BF16)
    r1 = v - b1.astype(F32)
    b2 = r1.astype(BF16)
    b3 = (r1 - b2.astype(F32)).astype(BF16)
    return jnp.concatenate([b1, b2, b3], axis=1)


def _softplus(v):
    return jnp.maximum(v, 0.0) + jnp.log(1.0 + jnp.exp(-jnp.abs(v)))


def _ssd_scan_kernel(xs_ref, b_ref, c_ref, dt_ref, dtt_ref, bias_ref, biast_ref, alog_ref, alogt_ref,
                     ewide_ref, y_ref, state_ref):
    q = SSD_CHUNK
    d = pl.program_id(1)
    fwd = d == 0

    @pl.when(pl.program_id(2) == 0)
    def _():
        state_ref[...] = jnp.zeros_like(state_ref)

    dt = _softplus(dt_ref[0] + bias_ref[0])
    dtt = _softplus(dtt_ref[0] + biast_ref[0])
    a = dt * (-jnp.exp(alog_ref[0]))
    at = dtt * (-jnp.exp(alogt_ref[0]))

    ri = lax.broadcasted_iota(jnp.int32, (q, q), 0)
    ci = lax.broadcasted_iota(jnp.int32, (q, q), 1)
    keep = (ri - ci) * jnp.where(fwd, 1, -1) >= 0
    tri = jnp.where(keep, 1.0, 0.0)
    cs = jnp.dot(tri, a, precision=HIGHEST, preferred_element_type=F32)
    cst = lax.dot_general(at, tri, (((1,), (1,)), ((), ())), precision=HIGHEST,
                          preferred_element_type=F32)
    tot = jnp.sum(a, axis=0, keepdims=True)

    ewide = ewide_ref[...]
    w_wide = jnp.dot(_split3(dt * jnp.exp(tot - cs)), ewide, preferred_element_type=F32)
    ecs_wide = jnp.dot(_split3(jnp.exp(cs)), ewide, preferred_element_type=F32)
    dec_wide = jnp.dot(_split3(jnp.broadcast_to(jnp.exp(tot), (SUBLANES_V7X, SSD_HEADS))), ewide,
                       preferred_element_type=F32)[0:1]

    lane = lax.broadcasted_iota(jnp.int32, (q, LANES_V7X), 1)
    lo = lane < SSD_HEAD_DIM
    hpg = SSD_HEADS // SSD_GROUPS
    gw = hpg * SSD_HEAD_DIM
    nt = (((1,), (1,)), ((), ()))
    tn = (((0,), (0,)), ((), ()))
    for g in range(SSD_GROUPS):
        bm = b_ref[:, g * SSD_STATE:(g + 1) * SSD_STATE]
        cm = c_ref[:, g * SSD_STATE:(g + 1) * SSD_STATE]
        cb = lax.dot_general(cm, bm, nt, preferred_element_type=F32)
        sl = slice(g * gw, (g + 1) * gw)
        y_off = jnp.dot(cm, state_ref[:, sl].astype(BF16), preferred_element_type=F32) * ecs_wide[:, sl]
        for pr in range(hpg // 2):
            lhs, rhs = [], []
            c0 = g * gw + pr * LANES_V7X
            xs_pair = xs_ref[:, c0:c0 + LANES_V7X]
            zero = jnp.zeros_like(xs_pair)
            for k in range(2):
                h = g * hpg + pr * 2 + k
                seg = cs[:, h:h + 1] - cst[h:h + 1, :]
                lmat = jnp.exp(jnp.where(keep, seg, -jnp.inf))
                lhs.append((cb * lmat * dtt[h:h + 1, :]).astype(BF16))
                rhs.append(jnp.where(lo if k == 0 else jnp.logical_not(lo), xs_pair, zero))
            y = jnp.dot(jnp.concatenate(lhs, axis=1), jnp.concatenate(rhs, axis=0), preferred_element_type=F32)
            y_ref[0, :, c0:c0 + LANES_V7X] = (y + y_off[:, pr * LANES_V7X:(pr + 1) * LANES_V7X]).astype(y_ref.dtype)
        x2 = (xs_ref[:, sl].astype(F32) * w_wide[:, sl]).astype(BF16)
        upd = lax.dot_general(bm, x2, tn, preferred_element_type=F32)
        state_ref[:, sl] = state_ref[:, sl] * dec_wide[:, sl] + upd


def _ssd_scan(cfg, xbc_act, dt_raw, dt_bias, a_log):
    q = SSD_CHUNK
    n_cc, n_lc = cfg.n_ctx // q, cfg.n_lat // q
    ctx0 = cfg.t_lat // q
    h = SSD_HEADS
    dt2 = dt_raw[:, :2 * h].reshape(cfg.t_all, 2, h).transpose(1, 0, 2)
    dt2t = dt2.transpose(0, 2, 1)
    bias = dt_bias.reshape(2, 1, h)
    biast = dt_bias.reshape(2, h, 1)
    alog = a_log.reshape(2, 1, h)
    alogt = a_log.reshape(2, h, 1)
    head_of_row = jnp.tile(jnp.arange(h), 3)[:, None]
    ewide = (head_of_row == (jnp.arange(SSD_D_INNER) // SSD_HEAD_DIM)[None, :]).astype(BF16)

    def blk(b, d, s):
        cs = jnp.where(d == 0, s, n_cc - 1 - s)
        ls = jnp.where(d == 0, s - n_cc, n_lc - 1 - (s - n_cc))
        return jnp.where(s < n_cc, ctx0 + b * n_cc + cs, b * n_lc + ls)

    gs = SSD_GROUPS * SSD_STATE
    col = lambda w, cblk: pl.BlockSpec((q, w), lambda b, d, s: (blk(b, d, s), cblk))
    per_dir = lambda shape: pl.BlockSpec((1,) + shape, lambda b, d, s: (d, 0, 0))
    return pl.pallas_call(
        _ssd_scan_kernel,
        grid=(cfg.batch, 2, n_cc + n_lc),
        in_specs=[col(SSD_D_INNER, 0), col(gs, SSD_D_INNER // gs), col(gs, SSD_D_INNER // gs + 1),
                  pl.BlockSpec((1, q, h), lambda b, d, s: (d, blk(b, d, s), 0)),
                  pl.BlockSpec((1, h, q), lambda b, d, s: (d, 0, blk(b, d, s))),
                  per_dir((1, h)), per_dir((h, 1)), per_dir((1, h)), per_dir((h, 1)),
                  _const_spec(ewide.shape)],
        out_specs=pl.BlockSpec((1, q, SSD_D_INNER), lambda b, d, s: (d, blk(b, d, s), 0)),
        out_shape=jax.ShapeDtypeStruct((2, cfg.t_all, SSD_D_INNER), BF16),
        scratch_shapes=[pltpu.VMEM((SSD_STATE, SSD_D_INNER), F32)],
        compiler_params=_cparams(("arbitrary", "arbitrary", "arbitrary")),
        name="ssd_scan",
    )(xbc_act, xbc_act, xbc_act, dt2, dt2t, bias, biast, alog, alogt, ewide)


def _ssd_out_kernel(yf_ref, yb_ref, xs_ref, z_ref, dw_ref, ng_ref, wo_ref,
                    xl_ref, xc_ref, mod_ref, g2_ref, wr2_ref, wrh_ref, br_ref, xo_ref, h2_ref, rt_ref, cnt_ref,
                    *, cfg):
    gw = SSD_D_INNER // SSD_GROUPS
    mix = None
    for g in range(SSD_GROUPS):
        sl = slice(g * gw, (g + 1) * gw)
        y = yf_ref[0, :, sl].astype(F32) + yb_ref[0, :, sl].astype(F32) + xs_ref[:, sl].astype(F32) * dw_ref[:, sl]
        y = y * _silu(z_ref[:, sl].astype(F32))
        y = (y * lax.rsqrt(jnp.mean(y * y, axis=-1, keepdims=True) + RMS_EPS)) * ng_ref[:, sl]
        part = jnp.dot(y.astype(BF16), wo_ref[sl, :], preferred_element_type=F32)
        mix = part if mix is None else mix + part
    _residual_norm_route(_dual_rows(cfg, xl_ref, xc_ref), mix, mod_ref[0], g2_ref[...], wr2_ref, wrh_ref, br_ref,
                         xo_ref, h2_ref, rt_ref, cnt_ref)


def _ssd_out(cfg, y2, xbc_act, z, d_wide, norm_g, wo, x, mod_l, g2, router):
    tm = cfg.tm
    row = lambda w: pl.BlockSpec((tm, w), lambda i: (i, 0))
    ydir = lambda d: pl.BlockSpec((1, tm, SSD_D_INNER), lambda i: (d, i, 0))
    ep_in, ep_out, ep_shape = _epilogue_specs(cfg, (x, x))
    return pl.pallas_call(
        functools.partial(_ssd_out_kernel, cfg=cfg),
        grid=(cfg.t_all // tm,),
        in_specs=[ydir(0), ydir(1), row(SSD_D_INNER), row(SSD_D_INNER), _const_spec((1, SSD_D_INNER)),
                  _const_spec((1, SSD_D_INNER)), _const_spec(wo.shape)] + ep_in,
        out_specs=ep_out,
        out_shape=ep_shape,
        compiler_params=_cparams(("arbitrary",)),
        name="ssd_out",
    )(y2, y2, xbc_act, z, d_wide, norm_g, wo, x, x, mod_l, g2, *router)


def _expert_kernel(eid_ref, nused_ref, *refs, blocks_per_part):
    xb_refs = refs[:MOE_GATHER_PARTS]
    wg_ref, wu_ref, wd_ref, y_ref, wg_s, wu_s, wd_s = refs[MOE_GATHER_PARTS:]
    i = pl.program_id(0)
    changed = jnp.logical_or(i == 0, eid_ref[i] != eid_ref[jnp.maximum(i - 1, 0)])

    @pl.when(jnp.logical_and(changed, i < nused_ref[0]))
    def _():
        wg_s[...] = wg_ref[0, 0].astype(BF16)
        wu_s[...] = wu_ref[0, 0].astype(BF16)
        wd_s[...] = wd_ref[0, 0].astype(BF16)

    @pl.when(i < nused_ref[0])
    def _():
        half = wg_s.shape[0] // 2
        words = xb_refs[0][...]
        for k in range(1, MOE_GATHER_PARTS):
            words = jnp.where(i >= k * blocks_per_part, xb_refs[k][...], words)
        lo, hi = _unpack_bf16_pairs(words)
        lo, hi = lo.astype(BF16), hi.astype(BF16)
        hg = (jnp.dot(lo, wg_s[:half, :], preferred_element_type=F32)
              + jnp.dot(hi, wg_s[half:, :], preferred_element_type=F32))
        hu = (jnp.dot(lo, wu_s[:half, :], preferred_element_type=F32)
              + jnp.dot(hi, wu_s[half:, :], preferred_element_type=F32))
        y = jnp.dot((_silu(hg) * hu).astype(BF16), wd_s[...], preferred_element_type=F32)
        y_ref[...] = _pack_bf16_pairs(y)

    @pl.when(i >= nused_ref[0])
    def _():
        y_ref[...] = jnp.zeros_like(y_ref)


def _experts(cfg, layer, block_eid, n_used, xb_parts, w_gate, w_up, w_down):
    d, tile = cfg.d_model, cfg.moe_tile
    per = xb_parts[0].shape[0] // tile
    n_blocks = per * MOE_GATHER_PARTS

    def part_spec(k):
        return pl.BlockSpec((tile, d // 2), lambda i, e, n: (jnp.clip(i - k * per, 0, per - 1), 0))

    grid_spec = pltpu.PrefetchScalarGridSpec(
        num_scalar_prefetch=2,
        grid=(n_blocks,),
        in_specs=[part_spec(k) for k in range(MOE_GATHER_PARTS)] + [
                  pl.BlockSpec((1, 1, d, MOE_HIDDEN), lambda i, e, n: (layer, e[i], 0, 0)),
                  pl.BlockSpec((1, 1, d, MOE_HIDDEN), lambda i, e, n: (layer, e[i], 0, 0)),
                  pl.BlockSpec((1, 1, MOE_HIDDEN, d), lambda i, e, n: (layer, e[i], 0, 0))],
        out_specs=pl.BlockSpec((tile, d // 2), lambda i, e, n: (i, 0)),
        scratch_shapes=[pltpu.VMEM((d, MOE_HIDDEN), BF16), pltpu.VMEM((d, MOE_HIDDEN), BF16),
                        pltpu.VMEM((MOE_HIDDEN, d), BF16)],
    )
    return pl.pallas_call(
        functools.partial(_expert_kernel, blocks_per_part=per),
        grid_spec=grid_spec,
        out_shape=jax.ShapeDtypeStruct((n_blocks * tile, d // 2), F32),
        compiler_params=_cparams(("arbitrary",)),
        name="moe_experts",
    )(block_eid, n_used, *xb_parts, w_gate, w_up, w_down)


def _dispatch_kernel(rt_ref, cnt_ref, pos_ref, carry_ref, start_ref, *, tile):
    tm = rt_ref.shape[0]
    rt = rt_ref[...]
    lane = lax.broadcasted_iota(jnp.int32, rt.shape, 1)
    o1, o2 = _expert_onehots(rt)
    cnt1 = jnp.sum(o1, axis=0, keepdims=True)
    cnt2 = jnp.sum(o2, axis=0, keepdims=True)

    @pl.when(pl.program_id(0) == 0)
    def _():
        padded = jnp.floor((cnt_ref[...] + (tile - 1)) * (1.0 / tile)) * tile
        r = lax.broadcasted_iota(jnp.int32, (ROUTE_LANES, ROUTE_LANES), 0)
        c = lax.broadcasted_iota(jnp.int32, (ROUTE_LANES, ROUTE_LANES), 1)
        excl = jnp.where(r < c, 1.0, 0.0)
        start_ref[...] = jnp.dot(padded, excl, precision=HIGHEST, preferred_element_type=F32)
        carry_ref[...] = jnp.zeros_like(carry_ref)

    ri = lax.broadcasted_iota(jnp.int32, (tm, tm), 0)
    ci = lax.broadcasted_iota(jnp.int32, (tm, tm), 1)
    earlier = jnp.where(ci < ri, 1.0, 0.0).astype(BF16)
    p1 = jnp.dot(earlier, o1.astype(BF16), preferred_element_type=F32)
    p2 = jnp.dot(earlier, o2.astype(BF16), preferred_element_type=F32)
    base = start_ref[0:1] + carry_ref[0:1]
    pos1 = jnp.sum(o1 * (base + p1), axis=-1, keepdims=True)
    pos2 = jnp.sum(o2 * (base + cnt1 + p2), axis=-1, keepdims=True)
    cols = jnp.where(lane == 0, pos1, jnp.where(lane == 1, pos2, 0.0))
    r = lax.broadcasted_iota(jnp.int32, (SUBLANES_V7X, ROUTE_LANES), 0)
    c = lax.broadcasted_iota(jnp.int32, (SUBLANES_V7X, ROUTE_LANES), 1)
    pick = jnp.where(r == c, 1.0, 0.0)
    rows = lax.dot_general(pick, cols, (((1,), (1,)), ((), ())), precision=HIGHEST, preferred_element_type=F32)
    pos_ref[...] = rows.astype(jnp.int32)
    carry_ref[...] += cnt1 + cnt2


def _dispatch(cfg, route, cnt):
    tile, tm = cfg.moe_tile, cfg.tm
    t = cfg.t_all
    a_total = 2 * t
    pos = pl.pallas_call(
        functools.partial(_dispatch_kernel, tile=tile),
        grid=(t // tm,),
        in_specs=[pl.BlockSpec((tm, ROUTE_LANES), lambda i: (i, 0)), _const_spec((SUBLANES_V7X, ROUTE_LANES))],
        out_specs=pl.BlockSpec((SUBLANES_V7X, tm), lambda i: (0, i)),
        out_shape=jax.ShapeDtypeStruct((SUBLANES_V7X, t), jnp.int32),
        scratch_shapes=[pltpu.VMEM((SUBLANES_V7X, ROUTE_LANES), F32), pltpu.VMEM((SUBLANES_V7X, ROUTE_LANES), F32)],
        compiler_params=_cparams(("arbitrary",)),
        name="moe_dispatch",
    )(route, cnt)
    counts = cnt[0, :MOE_EXPERTS].astype(jnp.int32)
    pend = jnp.cumsum((counts + tile - 1) // tile * tile)
    n_blocks = (a_total + MOE_EXPERTS * (tile - 1)) // tile
    n_blocks = -(-n_blocks // MOE_GATHER_PARTS) * MOE_GATHER_PARTS
    block_eid = jnp.minimum(jnp.sum(pend[None, :] <= (jnp.arange(n_blocks, dtype=jnp.int32) * tile)[:, None], axis=1),
                            MOE_EXPERTS - 1).astype(jnp.int32)
    n_used = (pend[-1] // tile).astype(jnp.int32).reshape(1)
    slots = pos[0:2].reshape(-1)
    slot_tok = (jnp.arange(n_blocks * tile, dtype=jnp.int32) % t).at[slots].set(
        jnp.arange(a_total, dtype=jnp.int32) % t, unique_indices=True, mode="promise_in_bounds")
    return slot_tok, block_eid, n_used, slots


def _combine_kernel(x_ref, y1_ref, y2_ref, rt_ref, mod_ref, g_ref, o_ref, *, final):
    rt = rt_ref[...]
    a_lo, a_hi = _unpack_bf16_pairs(y1_ref[...])
    b_lo, b_hi = _unpack_bf16_pairs(y2_ref[...])
    w1, w2 = rt[:, 2:3], rt[:, 3:4]
    f = jnp.concatenate([w1 * a_lo + w2 * b_lo, w1 * a_hi + w2 * b_hi], axis=1)
    xn = x_ref[...] + mod_ref[0][5:6] * f
    if final:
        xn = (xn * lax.rsqrt(jnp.mean(xn * xn, axis=-1, keepdims=True) + RMS_EPS)) * g_ref[...]
    o_ref[...] = xn


def _combine(cfg, x, y12, route, mod_l, g, final):
    d, tm = cfg.d_model, cfg.tm
    n_tiles = cfg.t_all // tm
    row = lambda w: pl.BlockSpec((tm, w), lambda i: (i, 0))
    t_out = cfg.t_lat if final else cfg.t_all
    return pl.pallas_call(
        functools.partial(_combine_kernel, final=final),
        grid=(t_out // tm,),
        in_specs=[row(d), row(d // 2), pl.BlockSpec((tm, d // 2), lambda i: (n_tiles + i, 0)),
                  row(ROUTE_LANES), _mod_spec(cfg), _const_spec((1, d))],
        out_specs=row(d),
        out_shape=jax.ShapeDtypeStruct((t_out, d), F32),
        compiler_params=_cparams(("arbitrary",)),
        name="moe_combine",
    )(x, y12, y12, route, mod_l, g)


def _forward(cfg, x, c, ctx, c_ctx, ada_w, ada_b, norm1_g, norm2_g, ab_w_in, ab_w_out,
             diff_lq1, diff_lk1, diff_lq2, diff_lk2, diff_subln_g, bconv_w,
             ssd_w_in, ssd_conv_w, ssd_conv_b, ssd_A_log, ssd_dt_bias, ssd_D, ssd_norm_g, ssd_w_out,
             moe_wg, moe_bg, moe_we, moe_be, moe_w_gate, moe_w_up, moe_w_down, final_norm_g):
    d = cfg.d_model
    b = cfg.batch
    x_pair = (x.reshape(cfg.t_lat, d), ctx.reshape(cfg.t_ctx, d))
    c_all = jnp.zeros((SUBLANES_V7X, d), F32).at[:b].set(c).at[b].set(c_ctx)
    mod = _adaln(cfg, c_all, ada_w, ada_b)
    tables = _rope_tables(cfg)

    for layer in range(cfg.depth):
        i = layer // 2
        mod_l = mod[layer]
        g1 = norm1_g[layer].reshape(1, d)
        g2 = norm2_g[layer].reshape(1, d)
        router = _router_weights(moe_wg[layer], moe_bg[layer], moe_we[layer], moe_be[layer])
        if layer % 2 == 0:
            lam_init = 0.8 - 0.6 * math.exp(-0.3 * layer)
            q, k, v, gb, u = _ab_in(cfg, x_pair, mod_l, g1, ab_w_in[i].astype(BF16), tables)
            lvec = jnp.stack([diff_lq1[i], diff_lk1[i], diff_lq2[i], diff_lk2[i]])
            o_pair = _attention(cfg, q, k, v, lvec, diff_subln_g[i].reshape(1, DIFF_V_DIM), lam_init)
            w_out = ab_w_out[i].astype(BF16)
            xa, h2, route, cnt = _ab_out(cfg, o_pair, gb, u, bconv_w[i], w_out, x_pair, mod_l, g2, router)
        else:
            w_in = jnp.pad(ssd_w_in[i].astype(BF16), ((0, 0), (0, LANES_V7X - 2 * SSD_HEADS)))
            z, xbc_act, dt_raw = _ssd_in(cfg, xa, mod_l, g1, w_in, ssd_conv_w[i],
                                         ssd_conv_b[i].reshape(1, SSD_CONV_CH))
            y2 = _ssd_scan(cfg, xbc_act, dt_raw, ssd_dt_bias[i], ssd_A_log[i])
            d_wide = jnp.repeat(ssd_D[i], SSD_HEAD_DIM).reshape(1, SSD_D_INNER)
            xa, h2, route, cnt = _ssd_out(cfg, y2, xbc_act, z, d_wide, ssd_norm_g[i].reshape(1, SSD_D_INNER),
                                          ssd_w_out[i].astype(BF16), xa, mod_l, g2, router)
        slot_tok, block_eid, n_used, slots = _dispatch(cfg, route, cnt)
        xb_parts = [h2[p_] for p_ in jnp.split(slot_tok, MOE_GATHER_PARTS)]
        yb = _experts(cfg, layer, block_eid, n_used, xb_parts, moe_w_gate, moe_w_up, moe_w_down)
        last = layer == cfg.depth - 1
        y12 = yb[slots]
        xa = _combine(cfg, xa, y12, route, mod_l, final_norm_g.reshape(1, d), last)
        x_pair = (xa, xa)
    return xa.reshape(b, cfg.n_lat, d)


def kernel(x, c, ctx, c_ctx, ada_w, ada_b, norm1_g, norm2_g, ab_w_in, ab_w_out, diff_lq1, diff_lk1, diff_lq2, diff_lk2, diff_subln_g, bconv_w, ssd_w_in, ssd_conv_w, ssd_conv_b, ssd_A_log, ssd_dt_bias, ssd_D, ssd_norm_g, ssd_w_out, moe_wg, moe_bg, moe_we, moe_be, moe_w_gate, moe_w_up, moe_w_down, final_norm_g):
    cfg = Cfg(batch=x.shape[0], n_lat=x.shape[1], n_ctx=ctx.shape[1], d_model=x.shape[2], depth=ada_w.shape[0],
              tm=512, tq=256, moe_tile=512, attn_subtiles=2, kchunk=1024)
    return _forward(cfg, x, c, ctx, c_ctx, ada_w, ada_b, norm1_g, norm2_g, ab_w_in, ab_w_out,
                    diff_lq1, diff_lk1, diff_lq2, diff_lk2, diff_subln_g, bconv_w,
                    ssd_w_in, ssd_conv_w, ssd_conv_b, ssd_A_log, ssd_dt_bias, ssd_D, ssd_norm_g, ssd_w_out,
                    moe_wg, moe_bg, moe_we, moe_be, moe_w_gate, moe_w_up, moe_w_down, final_norm_g)
```

```python
import functools
import math
from typing import NamedTuple

import jax
import jax.numpy as jnp
from jax import lax
from jax.experimental import pallas as pl
from jax.experimental.pallas import tpu as pltpu

F32 = jnp.float32
BF16 = jnp.bfloat16
HIGHEST = lax.Precision.HIGHEST

LANES_V7X = 128
SUBLANES_V7X = 8
VMEM_LIMIT_BYTES_V7X = 56 * 1024 * 1024

RMS_EPS = 1e-6
GRID_W = 64
N_DIFF_HEADS = 4
DIFF_QK_DIM = 64
DIFF_V_DIM = 128
DIFF_WIDTH = 512
CONV_WIDTH = 512
ROPE_BASE = 10000.0
SSD_D_INNER = 2048
SSD_HEAD_DIM = 64
SSD_HEADS = 32
SSD_GROUPS = 4
SSD_STATE = 128
SSD_CHUNK = 128
SSD_CONV_CH = SSD_D_INNER + 2 * SSD_GROUPS * SSD_STATE
MOE_GROUPS = 4
MOE_EXPERTS_PER_GROUP = 8
MOE_EXPERTS = 32
MOE_HIDDEN = 512
ROUTE_LANES = LANES_V7X
MOE_GATHER_PARTS = 3


class Cfg(NamedTuple):
    batch: int
    n_lat: int
    n_ctx: int
    d_model: int
    depth: int
    tm: int
    tq: int
    moe_tile: int
    attn_subtiles: int
    kchunk: int

    @property
    def t_lat(self):
        return self.batch * self.n_lat

    @property
    def t_ctx(self):
        return self.batch * self.n_ctx

    @property
    def t_all(self):
        return self.t_lat + self.t_ctx


def _cparams(sem):
    return pltpu.CompilerParams(dimension_semantics=sem, vmem_limit_bytes=VMEM_LIMIT_BYTES_V7X)


def _silu(v):
    return v * (1.0 / (1.0 + jnp.exp(-v)))


def _const_spec(shape):
    nd = len(shape)
    return pl.BlockSpec(shape, lambda *_: (0,) * nd)


def _mod_spec(cfg):
    return pl.BlockSpec((1, 6, cfg.d_model),
                        lambda i: (jnp.minimum((i * cfg.tm) // cfg.n_lat, cfg.batch), 0, 0))


def _seq_pos(cfg, tm):
    r0 = pl.program_id(0) * tm
    row = r0 + lax.broadcasted_iota(jnp.int32, (tm, 1), 0)
    seqlen = jnp.where(r0 >= cfg.t_lat, cfg.n_ctx, cfg.n_lat)
    return row & (seqlen - 1), seqlen


def _norm_mod(x, g, shift, scale):
    ms = jnp.mean(x * x, axis=-1, keepdims=True)
    return (x * lax.rsqrt(ms + RMS_EPS) * g) * (1.0 + scale) + shift


def _adaln_kernel(c_ref, w_ref, b_ref, o_ref):
    sc = _silu(c_ref[...])
    o_ref[0] = jnp.dot(sc, w_ref[0], precision=HIGHEST, preferred_element_type=F32) + b_ref[0]


def _adaln(cfg, c_all, ada_w, ada_b):
    d = cfg.d_model
    out = pl.pallas_call(
        _adaln_kernel,
        grid=(cfg.depth, 6),
        in_specs=[_const_spec((SUBLANES_V7X, d)),
                  pl.BlockSpec((1, d, d), lambda l, j: (l, 0, j)),
                  pl.BlockSpec((1, 1, d), lambda l, j: (l, 0, j))],
        out_specs=pl.BlockSpec((1, SUBLANES_V7X, d), lambda l, j: (l, 0, j)),
        out_shape=jax.ShapeDtypeStruct((cfg.depth, SUBLANES_V7X, 6 * d), F32),
        compiler_params=_cparams(("arbitrary", "arbitrary")),
        name="adaln",
    )(c_all, ada_w, ada_b.reshape(cfg.depth, 1, 6 * d))
    return out.reshape(cfg.depth, SUBLANES_V7X, 6, d)


def _rope_tables(cfg):
    n = cfg.n_lat
    rows = n // GRID_W
    row = jnp.broadcast_to(jnp.arange(rows, dtype=F32)[:, None], (rows, GRID_W)).reshape(n)
    col = jnp.broadcast_to(jnp.arange(GRID_W, dtype=F32)[None, :], (rows, GRID_W)).reshape(n)
    axis_dim = DIFF_QK_DIM // 2
    inv_freq = ROPE_BASE ** (-jnp.arange(0, axis_dim, 2, dtype=F32) / axis_dim)
    ang_r = row[:, None] * inv_freq
    ang_c = col[:, None] * inv_freq
    zeros = jnp.zeros_like(ang_r)
    cos64 = jnp.concatenate([jnp.cos(ang_r), jnp.cos(ang_r), jnp.cos(ang_c), jnp.cos(ang_c)], axis=1)
    sa64 = jnp.concatenate([zeros, jnp.sin(ang_r), zeros, jnp.sin(ang_c)], axis=1)
    sb64 = jnp.concatenate([-jnp.sin(ang_r), zeros, -jnp.sin(ang_c), zeros], axis=1)
    ident = cfg.tm
    cos = jnp.concatenate([jnp.tile(cos64, (1, 2)), jnp.ones((ident, LANES_V7X), F32)], axis=0)
    sa = jnp.concatenate([jnp.tile(sa64, (1, 2)), jnp.zeros((ident, LANES_V7X), F32)], axis=0)
    sb = jnp.concatenate([jnp.tile(sb64, (1, 2)), jnp.zeros((ident, LANES_V7X), F32)], axis=0)
    return cos, sa, sb


def _dual_row_specs(cfg, width, pair):
    lat_arr, ctx_arr = pair
    nl = cfg.t_lat // cfg.tm
    c0 = nl if ctx_arr is lat_arr else 0
    return (pl.BlockSpec((cfg.tm, width), lambda i: (jnp.minimum(i, nl - 1), 0)),
            pl.BlockSpec((cfg.tm, width), lambda i: (jnp.maximum(i - nl, 0) + c0, 0)))


def _dual_rows(cfg, lat_ref, ctx_ref):
    return jnp.where(pl.program_id(0) >= cfg.t_lat // cfg.tm, ctx_ref[...], lat_ref[...])


def _ab_in_kernel(xl_ref, xc_ref, mod_ref, g_ref, w_ref, cos_ref, sa_ref, sb_ref,
                  q_ref, k_ref, v_ref, gb_ref, u_ref, *, cfg):
    m = mod_ref[0]
    hb = _norm_mod(_dual_rows(cfg, xl_ref, xc_ref), g_ref[...], m[0:1], m[1:2]).astype(BF16)
    cos, sa, sb = cos_ref[...], sa_ref[...], sb_ref[...]

    def proj(j):
        return jnp.dot(hb, w_ref[:, j * DIFF_WIDTH:(j + 1) * DIFF_WIDTH], preferred_element_type=F32)

    def rope(t, out_ref, scale):
        for gi in range(DIFF_WIDTH // LANES_V7X):
            xg = t[:, gi * LANES_V7X:(gi + 1) * LANES_V7X]
            r = xg * cos + pltpu.roll(xg, 16, 1) * sa + pltpu.roll(xg, LANES_V7X - 16, 1) * sb
            out_ref[:, gi * LANES_V7X:(gi + 1) * LANES_V7X] = (r * scale).astype(out_ref.dtype)

    rope(proj(0), q_ref, math.log2(math.e) * DIFF_QK_DIM ** -0.5)
    rope(proj(1), k_ref, 1.0)
    v_ref[...] = proj(2).astype(v_ref.dtype)
    gb_ref[...] = proj(3)
    u_ref[...] = proj(4) * proj(5)


def _ab_in(cfg, x_pair, mod_l, g1, w_in_bf16, tables):
    t, d, tm = cfg.t_all, cfg.d_model, cfg.tm
    n_lat_tiles, per_seq = cfg.t_lat // tm, cfg.n_lat // tm
    tab_spec = pl.BlockSpec((tm, LANES_V7X), lambda i: (jnp.where(i < n_lat_tiles, i % per_seq, per_seq), 0))
    row = lambda w: pl.BlockSpec((tm, w), lambda i: (i, 0))
    return pl.pallas_call(
        functools.partial(_ab_in_kernel, cfg=cfg),
        grid=(t // tm,),
        in_specs=[*_dual_row_specs(cfg, d, x_pair), _mod_spec(cfg), _const_spec((1, d)),
                  _const_spec(w_in_bf16.shape), tab_spec, tab_spec, tab_spec],
        out_specs=[row(DIFF_WIDTH)] * 5,
        out_shape=[jax.ShapeDtypeStruct((t, DIFF_WIDTH), BF16)] * 3
        + [jax.ShapeDtypeStruct((t, CONV_WIDTH), F32)] * 2,
        compiler_params=_cparams(("arbitrary",)),
        name="ab_in",
    )(*x_pair, mod_l, g1, w_in_bf16, *tables)


def _attn_body(lvec_ref, q_ref, kc_ref, kl_ref, vc_ref, vl_ref, g_ref, o_ref, *, tq, lam_init, kchunk):
    lv = lvec_ref[...]
    lam = (jnp.exp(jnp.sum(lv[0:1] * lv[1:2], axis=-1, keepdims=True))
           - jnp.exp(jnp.sum(lv[2:3] * lv[3:4], axis=-1, keepdims=True)) + lam_init)
    nt = (((1,), (1,)), ((), ()))
    half = DIFF_QK_DIM
    chunks = [(kc_ref, vc_ref, 0, kc_ref.shape[0])]
    if kl_ref is not None:
        chunks += [(kl_ref, vl_ref, r0, kchunk) for r0 in range(0, kl_ref.shape[0], kchunk)]
    for j in range(q_ref.shape[0] // tq):
        q = q_ref[j * tq:(j + 1) * tq, :].astype(F32)
        lane = lax.broadcasted_iota(jnp.int32, q.shape, 1)
        q2 = jnp.concatenate([jnp.where(lane < half, q, 0.0), jnp.where(lane >= half, q, 0.0)],
                             axis=0).astype(BF16)
        ps, ms, ls = [], [], []
        for k_ref, _, r0, n in chunks:
            s = lax.dot_general(q2, k_ref[r0:r0 + n, :], nt, preferred_element_type=F32)
            m_c = jnp.max(s, axis=-1, keepdims=True)
            p = jnp.exp2(s - m_c)
            ps.append(p)
            ms.append(m_c)
            ls.append(jnp.sum(p, axis=-1, keepdims=True))
        mx = functools.reduce(jnp.maximum, ms)
        scale = [jnp.exp2(m_c - mx) for m_c in ms]
        r = 1.0 / functools.reduce(jnp.add, [sc * l_c for sc, l_c in zip(scale, ls)])
        pv = None
        for p, sc, (_, v_ref, r0, n) in zip(ps, scale, chunks):
            w = sc * r
            pd = (p[:tq] * w[:tq] - p[tq:] * (lam * w[tq:])).astype(BF16)
            part = jnp.dot(pd, v_ref[r0:r0 + n, :], preferred_element_type=F32)
            pv = part if pv is None else pv + part
        o = pv * lax.rsqrt(jnp.mean(pv * pv, axis=-1, keepdims=True) + RMS_EPS)
        o_ref[j * tq:(j + 1) * tq, :] = ((o * g_ref[...]) * (1.0 - lam_init)).astype(o_ref.dtype)


def _attn_lat_kernel(lvec_ref, q_ref, kc_ref, kl_ref, vc_ref, vl_ref, g_ref, o_ref, **kw):
    _attn_body(lvec_ref, q_ref, kc_ref, kl_ref, vc_ref, vl_ref, g_ref, o_ref, **kw)


def _attn_ctx_kernel(lvec_ref, q_ref, kc_ref, vc_ref, g_ref, o_ref, **kw):
    _attn_body(lvec_ref, q_ref, kc_ref, None, vc_ref, None, g_ref, o_ref, **kw)


def _attention(cfg, q, k, v, lvec, subln_g, lam_init):
    tq, nsub = cfg.tq, cfg.attn_subtiles
    tstep = tq * nsub
    nqb = cfg.n_lat // tstep
    ctx_blk0 = cfg.t_lat // cfg.n_ctx
    hw = DIFF_V_DIM
    ctx_spec = pl.BlockSpec((cfg.n_ctx, hw), lambda b, h, *_: (ctx_blk0 + b, h))
    lat_spec = pl.BlockSpec((cfg.n_lat, hw), lambda b, h, *_: (b, h))
    q_spec = pl.BlockSpec((tstep, hw), lambda b, h, i: (b * nqb + i, h))
    kw = dict(tq=tq, lam_init=lam_init, kchunk=min(cfg.kchunk, cfg.n_lat))
    o_lat = pl.pallas_call(
        functools.partial(_attn_lat_kernel, **kw),
        grid=(cfg.batch, N_DIFF_HEADS, nqb),
        in_specs=[_const_spec(lvec.shape), q_spec, ctx_spec, lat_spec, ctx_spec, lat_spec, _const_spec((1, hw))],
        out_specs=q_spec,
        out_shape=jax.ShapeDtypeStruct((cfg.t_lat, DIFF_WIDTH), BF16),
        compiler_params=_cparams(("arbitrary", "arbitrary", "arbitrary")),
        name="diff_attn",
    )(lvec, q, k, k, v, v, subln_g)
    tqc = min(tq, cfg.n_ctx)
    o_ctx = pl.pallas_call(
        functools.partial(_attn_ctx_kernel, tq=tqc, lam_init=lam_init, kchunk=cfg.n_ctx),
        grid=(cfg.batch, N_DIFF_HEADS),
        in_specs=[_const_spec(lvec.shape), ctx_spec, ctx_spec, ctx_spec, _const_spec((1, hw))],
        out_specs=pl.BlockSpec((cfg.n_ctx, hw), lambda b, h: (b, h)),
        out_shape=jax.ShapeDtypeStruct((cfg.t_ctx, DIFF_WIDTH), BF16),
        compiler_params=_cparams(("arbitrary", "arbitrary")),
        name="diff_attn_ctx",
    )(lvec, q, k, v, subln_g)
    return o_lat, o_ctx


def _route_from_logits(lg):
    lane = lax.broadcasted_iota(jnp.int32, lg.shape, 1)
    neg = -jnp.inf
    big = jnp.int32(ROUTE_LANES)
    gl = jnp.where(lane < MOE_GROUPS, lg, neg)
    gmax = jnp.max(gl, axis=-1, keepdims=True)
    gidx = jnp.min(jnp.where(gl == gmax, lane, big), axis=-1, keepdims=True)
    gw = 1.0 / jnp.sum(jnp.exp(gl - gmax), axis=-1, keepdims=True)
    lo = MOE_GROUPS + gidx * MOE_EXPERTS_PER_GROUP
    el = jnp.where((lane >= lo) & (lane < lo + MOE_EXPERTS_PER_GROUP), lg, neg)
    m1 = jnp.max(el, axis=-1, keepdims=True)
    i1 = jnp.min(jnp.where(el == m1, lane, big), axis=-1, keepdims=True)
    el2 = jnp.where(lane == i1, neg, el)
    m2 = jnp.max(el2, axis=-1, keepdims=True)
    i2 = jnp.min(jnp.where(el2 == m2, lane, big), axis=-1, keepdims=True)
    w1 = gw / (1.0 + jnp.exp(m2 - m1))
    w2 = gw - w1
    e1 = (i1 - MOE_GROUPS).astype(F32)
    e2 = (i2 - MOE_GROUPS).astype(F32)
    return jnp.where(lane == 0, e1, jnp.where(lane == 1, e2, jnp.where(lane == 2, w1, jnp.where(lane == 3, w2, 0.0))))


def _pack_bf16_pairs(v):
    w = v.shape[1] // 2
    lo = pltpu.bitcast(v[:, :w].astype(BF16).astype(F32), jnp.uint32)
    hi = pltpu.bitcast(v[:, w:].astype(BF16).astype(F32), jnp.uint32)
    return pltpu.bitcast((hi & jnp.uint32(0xFFFF0000)) | (lo >> 16), F32)


def _unpack_bf16_pairs(words):
    u = pltpu.bitcast(words, jnp.uint32)
    lo = pltpu.bitcast(u << 16, F32)
    hi = pltpu.bitcast(u & jnp.uint32(0xFFFF0000), F32)
    return lo, hi


def _router_weights(wg, bg, we, be):
    d = wg.shape[0]
    pad = ROUTE_LANES - MOE_GROUPS - MOE_EXPERTS
    wr = jnp.concatenate([wg, we, jnp.zeros((d, pad), F32)], axis=1)
    hi = wr.astype(BF16)
    lo = (wr - hi.astype(F32)).astype(BF16)
    br = jnp.concatenate([bg, be, jnp.zeros((pad,), F32)]).reshape(1, ROUTE_LANES)
    return jnp.concatenate([hi, lo], axis=1), hi, br


def _expert_onehots(rt):
    lane = lax.broadcasted_iota(jnp.int32, rt.shape, 1)
    return (jnp.where(lane == rt[:, 0:1].astype(jnp.int32), 1.0, 0.0),
            jnp.where(lane == rt[:, 1:2].astype(jnp.int32), 1.0, 0.0))


def _residual_norm_route(x, mix, m, g2, wr2_ref, wrh_ref, br_ref, xo_ref, h2_ref, rt_ref, cnt_ref):
    xn = x + m[2:3] * mix
    xo_ref[...] = xn
    h2 = _norm_mod(xn, g2, m[3:4], m[4:5])
    h2_ref[...] = _pack_bf16_pairs(h2)
    hh = h2.astype(BF16)
    hl = (h2 - hh.astype(F32)).astype(BF16)
    a = jnp.dot(hh, wr2_ref[...], preferred_element_type=F32)
    lg = (a[:, :ROUTE_LANES] + a[:, ROUTE_LANES:]) + jnp.dot(hl, wrh_ref[...], preferred_element_type=F32)
    rt = _route_from_logits(lg + br_ref[...])
    rt_ref[...] = rt

    @pl.when(pl.program_id(0) == 0)
    def _():
        cnt_ref[...] = jnp.zeros_like(cnt_ref)

    o1, o2 = _expert_onehots(rt)
    cnt_ref[...] += jnp.sum(o1 + o2, axis=0, keepdims=True)


def _epilogue_specs(cfg, x_pair):
    d, tm = cfg.d_model, cfg.tm
    row = lambda w: pl.BlockSpec((tm, w), lambda i: (i, 0))
    in_specs = [*_dual_row_specs(cfg, d, x_pair), _mod_spec(cfg), _const_spec((1, d)),
                _const_spec((d, 2 * ROUTE_LANES)),
                _const_spec((d, ROUTE_LANES)), _const_spec((1, ROUTE_LANES))]
    out_specs = [row(d), row(d // 2), row(ROUTE_LANES), _const_spec((SUBLANES_V7X, ROUTE_LANES))]
    out_shape = [jax.ShapeDtypeStruct((cfg.t_all, d), F32), jax.ShapeDtypeStruct((cfg.t_all, d // 2), F32),
                 jax.ShapeDtypeStruct((cfg.t_all, ROUTE_LANES), F32),
                 jax.ShapeDtypeStruct((SUBLANES_V7X, ROUTE_LANES), F32)]
    return in_specs, out_specs, out_shape


def _halo_specs(cfg, width, col_block=0):
    per = cfg.tm // SUBLANES_V7X
    last = cfg.t_all // SUBLANES_V7X - 1
    prev = pl.BlockSpec((SUBLANES_V7X, width), lambda i, *_: (jnp.maximum(i * per - 1, 0), col_block))
    nxt = pl.BlockSpec((SUBLANES_V7X, width), lambda i, *_: (jnp.minimum((i + 1) * per, last), col_block))
    return prev, nxt


def _shifted_rows(pad_ref, u, prev_blk, next_blk, pos, seqlen):
    tm = u.shape[0]
    s = SUBLANES_V7X
    pad_ref[s:s + tm, :] = u
    pad_ref[0:s, :] = prev_blk
    pad_ref[s + tm:2 * s + tm, :] = next_blk
    um1 = jnp.where(pos == 0, 0.0, pad_ref[s - 1:s - 1 + tm, :])
    up1 = jnp.where(pos == seqlen - 1, 0.0, pad_ref[s + 1:s + 1 + tm, :])
    return um1, up1


def _ab_out_kernel(ol_ref, oc_ref, gb_ref, u_ref, up_ref, un_ref, cw_ref, wo_ref,
                   xl_ref, xc_ref, mod_ref, g2_ref, wr2_ref, wrh_ref, br_ref,
                   xo_ref, h2_ref, rt_ref, cnt_ref, pad_ref, *, cfg):
    pos, seqlen = _seq_pos(cfg, cfg.tm)
    u = u_ref[...]
    um1, up1 = _shifted_rows(pad_ref, u, up_ref[...], un_ref[...], pos, seqlen)
    cw = cw_ref[...]
    conv = gb_ref[...] * (um1 * cw[0:1] + u * cw[1:2] + up1 * cw[2:3])
    mix = (jnp.dot(_dual_rows(cfg, ol_ref, oc_ref), wo_ref[:DIFF_WIDTH, :], preferred_element_type=F32)
           + jnp.dot(conv.astype(BF16), wo_ref[DIFF_WIDTH:, :], preferred_element_type=F32))
    _residual_norm_route(_dual_rows(cfg, xl_ref, xc_ref), mix, mod_ref[0], g2_ref[...], wr2_ref, wrh_ref, br_ref,
                         xo_ref, h2_ref, rt_ref, cnt_ref)


def _ab_out(cfg, o_pair, gb, u, conv_w, wo, x_pair, mod_l, g2, router):
    tm = cfg.tm
    row = lambda w: pl.BlockSpec((tm, w), lambda i: (i, 0))
    prev, nxt = _halo_specs(cfg, CONV_WIDTH)
    ep_in, ep_out, ep_shape = _epilogue_specs(cfg, x_pair)
    return pl.pallas_call(
        functools.partial(_ab_out_kernel, cfg=cfg),
        grid=(cfg.t_all // tm,),
        in_specs=[*_dual_row_specs(cfg, DIFF_WIDTH, o_pair), row(CONV_WIDTH), row(CONV_WIDTH), prev, nxt,
                  _const_spec(conv_w.shape),
                  _const_spec(wo.shape)] + ep_in,
        out_specs=ep_out,
        out_shape=ep_shape,
        scratch_shapes=[pltpu.VMEM((tm + 2 * SUBLANES_V7X, CONV_WIDTH), F32)],
        compiler_params=_cparams(("arbitrary",)),
        name="ab_out",
    )(*o_pair, gb, u, u, u, conv_w, wo, *x_pair, mod_l, g2, *router)


def _ssd_in_kernel(x_ref, xp_ref, xn_ref, mod_ref, g_ref, w_ref, cw_ref, cb_ref,
                   z_ref, xbc_ref, dt_ref, pad_ref, *, cfg):
    tm = cfg.tm
    s = SUBLANES_V7X
    m = mod_ref[0]
    g = g_ref[...]
    h = _norm_mod(x_ref[...], g, m[0:1], m[1:2])
    hb = h.astype(BF16)
    h_ext = jnp.concatenate([_norm_mod(xp_ref[...], g, m[0:1], m[1:2]), h,
                             _norm_mod(xn_ref[...], g, m[0:1], m[1:2])], axis=0).astype(BF16)
    pos, seqlen = _seq_pos(cfg, tm)
    cw = 512
    for j in range(SSD_D_INNER // cw):
        z_ref[:, j * cw:(j + 1) * cw] = jnp.dot(hb, w_ref[:, j * cw:(j + 1) * cw],
                                                preferred_element_type=F32).astype(z_ref.dtype)
    for j in range(SSD_CONV_CH // cw):
        c0 = SSD_D_INNER + j * cw
        pad = pad_ref.at[j]
        pad[...] = jnp.dot(h_ext, w_ref[:, c0:c0 + cw], preferred_element_type=F32)
        taps = cw_ref[:, j * cw:(j + 1) * cw]
        um1 = jnp.where(pos == 0, 0.0, pad[s - 1:s - 1 + tm, :])
        up1 = jnp.where(pos == seqlen - 1, 0.0, pad[s + 1:s + 1 + tm, :])
        conv = um1 * taps[0:1] + pad[s:s + tm, :] * taps[1:2] + up1 * taps[2:3] + cb_ref[:, j * cw:(j + 1) * cw]
        xbc_ref[:, j * cw:(j + 1) * cw] = _silu(conv).astype(xbc_ref.dtype)
    dt_ref[...] = jnp.dot(hb, w_ref[:, SSD_D_INNER + SSD_CONV_CH:], preferred_element_type=F32)


def _ssd_in(cfg, x, mod_l, g1, w_in, conv_w, conv_b):
    t, d, tm = cfg.t_all, cfg.d_model, cfg.tm
    row = lambda w: pl.BlockSpec((tm, w), lambda i: (i, 0))
    prev, nxt = _halo_specs(cfg, d)
    return pl.pallas_call(
        functools.partial(_ssd_in_kernel, cfg=cfg),
        grid=(t // tm,),
        in_specs=[row(d), prev, nxt, _mod_spec(cfg), _const_spec((1, d)), _const_spec(w_in.shape),
                  _const_spec(conv_w.shape), _const_spec(conv_b.shape)],
        out_specs=[row(SSD_D_INNER), row(SSD_CONV_CH), row(LANES_V7X)],
        out_shape=[jax.ShapeDtypeStruct((t, SSD_D_INNER), BF16), jax.ShapeDtypeStruct((t, SSD_CONV_CH), BF16),
                   jax.ShapeDtypeStruct((t, LANES_V7X), F32)],
        scratch_shapes=[pltpu.VMEM((SSD_CONV_CH // 512, tm + 2 * SUBLANES_V7X, 512), F32)],
        compiler_params=_cparams(("arbitrary",)),
        name="ssd_in",
    )(x, x, x, mod_l, g1, w_in, conv_w, conv_b)


def _split3(v):
    b1 = v.astype(BF16)
    r1 = v - b1.astype(F32)
    b2 = r1.astype(BF16)
    b3 = (r1 - b2.astype(F32)).astype(BF16)
    return jnp.concatenate([b1, b2, b3], axis=1)


def _softplus(v):
    return jnp.maximum(v, 0.0) + jnp.log(1.0 + jnp.exp(-jnp.abs(v)))


def _ssd_scan_kernel(*refs):
    n_in = 5
    fwd_in, bwd_in = refs[:n_in], refs[n_in:2 * n_in]
    bias_ref, biast_ref, alog_ref, alogt_ref, ewide_ref, yf_ref, yb_ref, sf_ref, sb_ref = refs[2 * n_in:]

    @pl.when(pl.program_id(1) == 0)
    def _():
        sf_ref[...] = jnp.zeros_like(sf_ref)
        sb_ref[...] = jnp.zeros_like(sb_ref)

    for d, (ins, y_ref, state_ref) in enumerate(((fwd_in, yf_ref, sf_ref), (bwd_in, yb_ref, sb_ref))):
        _ssd_chunk(d == 0, *ins, bias_ref[d], biast_ref[d], alog_ref[d], alogt_ref[d], ewide_ref, y_ref, state_ref)


def _ssd_chunk(fwd, xs_ref, b_ref, c_ref, dt_ref, dtt_ref, bias, biast, alog, alogt, ewide_ref, y_ref, state_ref):
    q = SSD_CHUNK
    dt = _softplus(dt_ref[0] + bias)
    dtt = _softplus(dtt_ref[0] + biast)
    a = dt * (-jnp.exp(alog))
    at = dtt * (-jnp.exp(alogt))

    ri = lax.broadcasted_iota(jnp.int32, (q, q), 0)
    ci = lax.broadcasted_iota(jnp.int32, (q, q), 1)
    keep = (ci <= ri) if fwd else (ci >= ri)
    tri = jnp.where(keep, 1.0, 0.0)
    cs = jnp.dot(tri, a, precision=HIGHEST, preferred_element_type=F32)
    cst = lax.dot_general(at, tri, (((1,), (1,)), ((), ())), precision=HIGHEST,
                          preferred_element_type=F32)
    tot = jnp.sum(a, axis=0, keepdims=True)

    ewide = ewide_ref[...]
    w_wide = jnp.dot(_split3(dt * jnp.exp(tot - cs)), ewide, preferred_element_type=F32)
    ecs_wide = jnp.dot(_split3(jnp.exp(cs)), ewide, preferred_element_type=F32)
    dec_wide = jnp.dot(_split3(jnp.broadcast_to(jnp.exp(tot), (SUBLANES_V7X, SSD_HEADS))), ewide,
                       preferred_element_type=F32)[0:1]

    lane = lax.broadcasted_iota(jnp.int32, (q, LANES_V7X), 1)
    lo = lane < SSD_HEAD_DIM
    hpg = SSD_HEADS // SSD_GROUPS
    gw = hpg * SSD_HEAD_DIM
    nt = (((1,), (1,)), ((), ()))
    tn = (((0,), (0,)), ((), ()))
    for g in range(SSD_GROUPS):
        bm = b_ref[:, g * SSD_STATE:(g + 1) * SSD_STATE]
        cm = c_ref[:, g * SSD_STATE:(g + 1) * SSD_STATE]
        cb = lax.dot_general(cm, bm, nt, preferred_element_type=F32)
        sl = slice(g * gw, (g + 1) * gw)
        y_off = jnp.dot(cm, state_ref[:, sl].astype(BF16), preferred_element_type=F32) * ecs_wide[:, sl]
        for pr in range(hpg // 2):
            lhs, rhs = [], []
            c0 = g * gw + pr * LANES_V7X
            xs_pair = xs_ref[:, c0:c0 + LANES_V7X]
            zero = jnp.zeros_like(xs_pair)
            for k in range(2):
                h = g * hpg + pr * 2 + k
                seg = cs[:, h:h + 1] - cst[h:h + 1, :]
                lmat = jnp.exp(jnp.where(keep, seg, -jnp.inf))
                lhs.append((cb * lmat * dtt[h:h + 1, :]).astype(BF16))
                rhs.append(jnp.where(lo if k == 0 else jnp.logical_not(lo), xs_pair, zero))
            y = jnp.dot(jnp.concatenate(lhs, axis=1), jnp.concatenate(rhs, axis=0), preferred_element_type=F32)
            y_ref[:, c0:c0 + LANES_V7X] = (y + y_off[:, pr * LANES_V7X:(pr + 1) * LANES_V7X]).astype(y_ref.dtype)
        x2 = (xs_ref[:, sl].astype(F32) * w_wide[:, sl]).astype(BF16)
        upd = lax.dot_general(bm, x2, tn, preferred_element_type=F32)
        state_ref[:, sl] = state_ref[:, sl] * dec_wide[:, sl] + upd


def _ssd_scan(cfg, xbc_act, dt_raw, dt_bias, a_log):
    q = SSD_CHUNK
    n_cc, n_lc = cfg.n_ctx // q, cfg.n_lat // q
    ctx0 = cfg.t_lat // q
    h = SSD_HEADS
    dt2 = dt_raw[:, :2 * h].reshape(cfg.t_all, 2, h).transpose(1, 0, 2)
    dt2t = dt2.transpose(0, 2, 1)
    bias = dt_bias.reshape(2, 1, h)
    biast = dt_bias.reshape(2, h, 1)
    alog = a_log.reshape(2, 1, h)
    alogt = a_log.reshape(2, h, 1)
    head_of_row = jnp.tile(jnp.arange(h), 3)[:, None]
    ewide = (head_of_row == (jnp.arange(SSD_D_INNER) // SSD_HEAD_DIM)[None, :]).astype(BF16)

    def blk(d):
        def index(b, s):
            cs = s if d == 0 else n_cc - 1 - s
            ls = s - n_cc if d == 0 else n_lc - 1 - (s - n_cc)
            return jnp.where(s < n_cc, ctx0 + b * n_cc + cs, b * n_lc + ls)
        return index

    gs = SSD_GROUPS * SSD_STATE

    def dir_specs(d):
        at = blk(d)
        col = lambda w, cblk: pl.BlockSpec((q, w), lambda b, s: (at(b, s), cblk))
        return [col(SSD_D_INNER, 0), col(gs, SSD_D_INNER // gs), col(gs, SSD_D_INNER // gs + 1),
                pl.BlockSpec((1, q, h), lambda b, s: (d, at(b, s), 0)),
                pl.BlockSpec((1, h, q), lambda b, s: (d, 0, at(b, s)))]

    dir_args = (xbc_act, xbc_act, xbc_act, dt2, dt2t)
    y_spec = lambda d: pl.BlockSpec((q, SSD_D_INNER), lambda b, s: (blk(d)(b, s), 0))
    return pl.pallas_call(
        _ssd_scan_kernel,
        grid=(cfg.batch, n_cc + n_lc),
        in_specs=dir_specs(0) + dir_specs(1) + [_const_spec(bias.shape), _const_spec(biast.shape),
                                                _const_spec(alog.shape), _const_spec(alogt.shape),
                                                _const_spec(ewide.shape)],
        out_specs=[y_spec(0), y_spec(1)],
        out_shape=[jax.ShapeDtypeStruct((cfg.t_all, SSD_D_INNER), BF16)] * 2,
        scratch_shapes=[pltpu.VMEM((SSD_STATE, SSD_D_INNER), F32)] * 2,
        compiler_params=_cparams(("arbitrary", "arbitrary")),
        name="ssd_scan",
    )(*dir_args, *dir_args, bias, biast, alog, alogt, ewide)


def _ssd_out_kernel(yf_ref, yb_ref, xs_ref, z_ref, dw_ref, ng_ref, wo_ref,
                    xl_ref, xc_ref, mod_ref, g2_ref, wr2_ref, wrh_ref, br_ref, xo_ref, h2_ref, rt_ref, cnt_ref,
                    *, cfg):
    gw = SSD_D_INNER // SSD_GROUPS
    mix = None
    for g in range(SSD_GROUPS):
        sl = slice(g * gw, (g + 1) * gw)
        y = yf_ref[:, sl].astype(F32) + yb_ref[:, sl].astype(F32) + xs_ref[:, sl].astype(F32) * dw_ref[:, sl]
        y = y * _silu(z_ref[:, sl].astype(F32))
        y = (y * lax.rsqrt(jnp.mean(y * y, axis=-1, keepdims=True) + RMS_EPS)) * ng_ref[:, sl]
        part = jnp.dot(y.astype(BF16), wo_ref[sl, :], preferred_element_type=F32)
        mix = part if mix is None else mix + part
    _residual_norm_route(_dual_rows(cfg, xl_ref, xc_ref), mix, mod_ref[0], g2_ref[...], wr2_ref, wrh_ref, br_ref,
                         xo_ref, h2_ref, rt_ref, cnt_ref)


def _ssd_out(cfg, y2, xbc_act, z, d_wide, norm_g, wo, x, mod_l, g2, router):
    tm = cfg.tm
    row = lambda w: pl.BlockSpec((tm, w), lambda i: (i, 0))
    ep_in, ep_out, ep_shape = _epilogue_specs(cfg, (x, x))
    return pl.pallas_call(
        functools.partial(_ssd_out_kernel, cfg=cfg),
        grid=(cfg.t_all // tm,),
        in_specs=[row(SSD_D_INNER)] * 4 + [_const_spec((1, SSD_D_INNER)),
                  _const_spec((1, SSD_D_INNER)), _const_spec(wo.shape)] + ep_in,
        out_specs=ep_out,
        out_shape=ep_shape,
        compiler_params=_cparams(("arbitrary",)),
        name="ssd_out",
    )(*y2, xbc_act, z, d_wide, norm_g, wo, x, x, mod_l, g2, *router)


def _expert_kernel(eid_ref, nused_ref, *refs, blocks_per_part):
    xb_refs = refs[:MOE_GATHER_PARTS]
    wg_ref, wu_ref, wd_ref, y_ref, wg_s, wu_s, wd_s = refs[MOE_GATHER_PARTS:]
    i = pl.program_id(0)
    changed = jnp.logical_or(i == 0, eid_ref[i] != eid_ref[jnp.maximum(i - 1, 0)])

    @pl.when(jnp.logical_and(changed, i < nused_ref[0]))
    def _():
        wg_s[...] = wg_ref[0, 0].astype(BF16)
        wu_s[...] = wu_ref[0, 0].astype(BF16)
        wd_s[...] = wd_ref[0, 0].astype(BF16)

    @pl.when(i < nused_ref[0])
    def _():
        half = wg_s.shape[0] // 2
        words = xb_refs[0][...]
        for k in range(1, MOE_GATHER_PARTS):
            words = jnp.where(i >= k * blocks_per_part, xb_refs[k][...], words)
        lo, hi = _unpack_bf16_pairs(words)
        lo, hi = lo.astype(BF16), hi.astype(BF16)
        hg = (jnp.dot(lo, wg_s[:half, :], preferred_element_type=F32)
              + jnp.dot(hi, wg_s[half:, :], preferred_element_type=F32))
        hu = (jnp.dot(lo, wu_s[:half, :], preferred_element_type=F32)
              + jnp.dot(hi, wu_s[half:, :], preferred_element_type=F32))
        y = jnp.dot((_silu(hg) * hu).astype(BF16), wd_s[...], preferred_element_type=F32)
        y_ref[...] = _pack_bf16_pairs(y)

    @pl.when(i >= nused_ref[0])
    def _():
        y_ref[...] = jnp.zeros_like(y_ref)


def _experts(cfg, layer, block_eid, n_used, xb_parts, w_gate, w_up, w_down):
    d, tile = cfg.d_model, cfg.moe_tile
    per = xb_parts[0].shape[0] // tile
    n_blocks = per * MOE_GATHER_PARTS

    def part_spec(k):
        return pl.BlockSpec((tile, d // 2), lambda i, e, n: (jnp.clip(i - k * per, 0, per - 1), 0))

    grid_spec = pltpu.PrefetchScalarGridSpec(
        num_scalar_prefetch=2,
        grid=(n_blocks,),
        in_specs=[part_spec(k) for k in range(MOE_GATHER_PARTS)] + [
                  pl.BlockSpec((1, 1, d, MOE_HIDDEN), lambda i, e, n: (layer, e[i], 0, 0)),
                  pl.BlockSpec((1, 1, d, MOE_HIDDEN), lambda i, e, n: (layer, e[i], 0, 0)),
                  pl.BlockSpec((1, 1, MOE_HIDDEN, d), lambda i, e, n: (layer, e[i], 0, 0))],
        out_specs=pl.BlockSpec((tile, d // 2), lambda i, e, n: (i, 0)),
        scratch_shapes=[pltpu.VMEM((d, MOE_HIDDEN), BF16), pltpu.VMEM((d, MOE_HIDDEN), BF16),
                        pltpu.VMEM((MOE_HIDDEN, d), BF16)],
    )
    return pl.pallas_call(
        functools.partial(_expert_kernel, blocks_per_part=per),
        grid_spec=grid_spec,
        out_shape=jax.ShapeDtypeStruct((n_blocks * tile, d // 2), F32),
        compiler_params=_cparams(("arbitrary",)),
        name="moe_experts",
    )(block_eid, n_used, *xb_parts, w_gate, w_up, w_down)


def _dispatch_kernel(rt_ref, cnt_ref, pos_ref, carry_ref, start_ref, *, tile):
    tm = rt_ref.shape[0]
    rt = rt_ref[...]
    lane = lax.broadcasted_iota(jnp.int32, rt.shape, 1)
    o1, o2 = _expert_onehots(rt)
    cnt1 = jnp.sum(o1, axis=0, keepdims=True)
    cnt2 = jnp.sum(o2, axis=0, keepdims=True)

    @pl.when(pl.program_id(0) == 0)
    def _():
        padded = jnp.floor((cnt_ref[...] + (tile - 1)) * (1.0 / tile)) * tile
        r = lax.broadcasted_iota(jnp.int32, (ROUTE_LANES, ROUTE_LANES), 0)
        c = lax.broadcasted_iota(jnp.int32, (ROUTE_LANES, ROUTE_LANES), 1)
        excl = jnp.where(r < c, 1.0, 0.0)
        start_ref[...] = jnp.dot(padded, excl, precision=HIGHEST, preferred_element_type=F32)
        carry_ref[...] = jnp.zeros_like(carry_ref)

    ri = lax.broadcasted_iota(jnp.int32, (tm, tm), 0)
    ci = lax.broadcasted_iota(jnp.int32, (tm, tm), 1)
    earlier = jnp.where(ci < ri, 1.0, 0.0).astype(BF16)
    p1 = jnp.dot(earlier, o1.astype(BF16), preferred_element_type=F32)
    p2 = jnp.dot(earlier, o2.astype(BF16), preferred_element_type=F32)
    base = start_ref[0:1] + carry_ref[0:1]
    pos1 = jnp.sum(o1 * (base + p1), axis=-1, keepdims=True)
    pos2 = jnp.sum(o2 * (base + cnt1 + p2), axis=-1, keepdims=True)
    cols = jnp.where(lane == 0, pos1, jnp.where(lane == 1, pos2, 0.0))
    r = lax.broadcasted_iota(jnp.int32, (SUBLANES_V7X, ROUTE_LANES), 0)
    c = lax.broadcasted_iota(jnp.int32, (SUBLANES_V7X, ROUTE_LANES), 1)
    pick = jnp.where(r == c, 1.0, 0.0)
    rows = lax.dot_general(pick, cols, (((1,), (1,)), ((), ())), precision=HIGHEST, preferred_element_type=F32)
    pos_ref[...] = rows.astype(jnp.int32)
    carry_ref[...] += cnt1 + cnt2


def _dispatch(cfg, route, cnt):
    tile, tm = cfg.moe_tile, cfg.tm
    t = cfg.t_all
    a_total = 2 * t
    pos = pl.pallas_call(
        functools.partial(_dispatch_kernel, tile=tile),
        grid=(t // tm,),
        in_specs=[pl.BlockSpec((tm, ROUTE_LANES), lambda i: (i, 0)), _const_spec((SUBLANES_V7X, ROUTE_LANES))],
        out_specs=pl.BlockSpec((SUBLANES_V7X, tm), lambda i: (0, i)),
        out_shape=jax.ShapeDtypeStruct((SUBLANES_V7X, t), jnp.int32),
        scratch_shapes=[pltpu.VMEM((SUBLANES_V7X, ROUTE_LANES), F32), pltpu.VMEM((SUBLANES_V7X, ROUTE_LANES), F32)],
        compiler_params=_cparams(("arbitrary",)),
        name="moe_dispatch",
    )(route, cnt)
    counts = cnt[0, :MOE_EXPERTS].astype(jnp.int32)
    pend = jnp.cumsum((counts + tile - 1) // tile * tile)
    n_blocks = (a_total + MOE_EXPERTS * (tile - 1)) // tile
    n_blocks = -(-n_blocks // MOE_GATHER_PARTS) * MOE_GATHER_PARTS
    block_eid = jnp.minimum(jnp.sum(pend[None, :] <= (jnp.arange(n_blocks, dtype=jnp.int32) * tile)[:, None], axis=1),
                            MOE_EXPERTS - 1).astype(jnp.int32)
    n_used = (pend[-1] // tile).astype(jnp.int32).reshape(1)
    slots = pos[0:2].reshape(-1)
    slot_tok = (jnp.arange(n_blocks * tile, dtype=jnp.int32) % t).at[slots].set(
        jnp.arange(a_total, dtype=jnp.int32) % t, unique_indices=True, mode="promise_in_bounds")
    return slot_tok, block_eid, n_used, slots


def _combine_kernel(x_ref, y1_ref, y2_ref, rt_ref, mod_ref, g_ref, o_ref, *, final):
    rt = rt_ref[...]
    a_lo, a_hi = _unpack_bf16_pairs(y1_ref[...])
    b_lo, b_hi = _unpack_bf16_pairs(y2_ref[...])
    w1, w2 = rt[:, 2:3], rt[:, 3:4]
    f = jnp.concatenate([w1 * a_lo + w2 * b_lo, w1 * a_hi + w2 * b_hi], axis=1)
    xn = x_ref[...] + mod_ref[0][5:6] * f
    if final:
        xn = (xn * lax.rsqrt(jnp.mean(xn * xn, axis=-1, keepdims=True) + RMS_EPS)) * g_ref[...]
    o_ref[...] = xn


def _combine(cfg, x, y12, route, mod_l, g, final):
    d, tm = cfg.d_model, cfg.tm
    n_tiles = cfg.t_all // tm
    row = lambda w: pl.BlockSpec((tm, w), lambda i: (i, 0))
    t_out = cfg.t_lat if final else cfg.t_all
    return pl.pallas_call(
        functools.partial(_combine_kernel, final=final),
        grid=(t_out // tm,),
        in_specs=[row(d), row(d // 2), pl.BlockSpec((tm, d // 2), lambda i: (n_tiles + i, 0)),
                  row(ROUTE_LANES), _mod_spec(cfg), _const_spec((1, d))],
        out_specs=row(d),
        out_shape=jax.ShapeDtypeStruct((t_out, d), F32),
        compiler_params=_cparams(("arbitrary",)),
        name="moe_combine",
    )(x, y12, y12, route, mod_l, g)


def _forward(cfg, x, c, ctx, c_ctx, ada_w, ada_b, norm1_g, norm2_g, ab_w_in, ab_w_out,
             diff_lq1, diff_lk1, diff_lq2, diff_lk2, diff_subln_g, bconv_w,
             ssd_w_in, ssd_conv_w, ssd_conv_b, ssd_A_log, ssd_dt_bias, ssd_D, ssd_norm_g, ssd_w_out,
             moe_wg, moe_bg, moe_we, moe_be, moe_w_gate, moe_w_up, moe_w_down, final_norm_g):
    d = cfg.d_model
    b = cfg.batch
    x_pair = (x.reshape(cfg.t_lat, d), ctx.reshape(cfg.t_ctx, d))
    c_all = jnp.zeros((SUBLANES_V7X, d), F32).at[:b].set(c).at[b].set(c_ctx)
    mod = _adaln(cfg, c_all, ada_w, ada_b)
    tables = _rope_tables(cfg)

    for layer in range(cfg.depth):
        i = layer // 2
        mod_l = mod[layer]
        g1 = norm1_g[layer].reshape(1, d)
        g2 = norm2_g[layer].reshape(1, d)
        router = _router_weights(moe_wg[layer], moe_bg[layer], moe_we[layer], moe_be[layer])
        if layer % 2 == 0:
            lam_init = 0.8 - 0.6 * math.exp(-0.3 * layer)
            q, k, v, gb, u = _ab_in(cfg, x_pair, mod_l, g1, ab_w_in[i].astype(BF16), tables)
            lvec = jnp.stack([diff_lq1[i], diff_lk1[i], diff_lq2[i], diff_lk2[i]])
            o_pair = _attention(cfg, q, k, v, lvec, diff_subln_g[i].reshape(1, DIFF_V_DIM), lam_init)
            w_out = ab_w_out[i].astype(BF16)
            xa, h2, route, cnt = _ab_out(cfg, o_pair, gb, u, bconv_w[i], w_out, x_pair, mod_l, g2, router)
        else:
            w_in = jnp.pad(ssd_w_in[i].astype(BF16), ((0, 0), (0, LANES_V7X - 2 * SSD_HEADS)))
            z, xbc_act, dt_raw = _ssd_in(cfg, xa, mod_l, g1, w_in, ssd_conv_w[i],
                                         ssd_conv_b[i].reshape(1, SSD_CONV_CH))
            y2 = _ssd_scan(cfg, xbc_act, dt_raw, ssd_dt_bias[i], ssd_A_log[i])
            d_wide = jnp.repeat(ssd_D[i], SSD_HEAD_DIM).reshape(1, SSD_D_INNER)
            xa, h2, route, cnt = _ssd_out(cfg, y2, xbc_act, z, d_wide, ssd_norm_g[i].reshape(1, SSD_D_INNER),
                                          ssd_w_out[i].astype(BF16), xa, mod_l, g2, router)
        slot_tok, block_eid, n_used, slots = _dispatch(cfg, route, cnt)
        xb_parts = [h2[p_] for p_ in jnp.split(slot_tok, MOE_GATHER_PARTS)]
        yb = _experts(cfg, layer, block_eid, n_used, xb_parts, moe_w_gate, moe_w_up, moe_w_down)
        last = layer == cfg.depth - 1
        y12 = yb[slots]
        xa = _combine(cfg, xa, y12, route, mod_l, final_norm_g.reshape(1, d), last)
        x_pair = (xa, xa)
    return xa.reshape(b, cfg.n_lat, d)


def kernel(x, c, ctx, c_ctx, ada_w, ada_b, norm1_g, norm2_g, ab_w_in, ab_w_out, diff_lq1, diff_lk1, diff_lq2, diff_lk2, diff_subln_g, bconv_w, ssd_w_in, ssd_conv_w, ssd_conv_b, ssd_A_log, ssd_dt_bias, ssd_D, ssd_norm_g, ssd_w_out, moe_wg, moe_bg, moe_we, moe_be, moe_w_gate, moe_w_up, moe_w_down, final_norm_g):
    cfg = Cfg(batch=x.shape[0], n_lat=x.shape[1], n_ctx=ctx.shape[1], d_model=x.shape[2], depth=ada_w.shape[0],
              tm=512, tq=256, moe_tile=512, attn_subtiles=2, kchunk=1024)
    return _forward(cfg, x, c, ctx, c_ctx, ada_w, ada_b, norm1_g, norm2_g, ab_w_in, ab_w_out,
                    diff_lq1, diff_lk1, diff_lq2, diff_lk2, diff_subln_g, bconv_w,
                    ssd_w_in, ssd_conv_w, ssd_conv_b, ssd_A_log, ssd_dt_bias, ssd_D, ssd_norm_g, ssd_w_out,
                    moe_wg, moe_bg, moe_we, moe_be, moe_w_gate, moe_w_up, moe_w_down, final_norm_g)
```

```python
import functools
import math
from typing import NamedTuple

import jax
import jax.numpy as jnp
from jax import lax
from jax.experimental import pallas as pl
from jax.experimental.pallas import tpu as pltpu

F32 = jnp.float32
BF16 = jnp.bfloat16
HIGHEST = lax.Precision.HIGHEST

LANES_V7X = 128
SUBLANES_V7X = 8
VMEM_LIMIT_BYTES_V7X = 56 * 1024 * 1024

RMS_EPS = 1e-6
GRID_W = 64
N_DIFF_HEADS = 4
DIFF_QK_DIM = 64
DIFF_V_DIM = 128
DIFF_WIDTH = 512
CONV_WIDTH = 512
ROPE_BASE = 10000.0
SSD_D_INNER = 2048
SSD_HEAD_DIM = 64
SSD_HEADS = 32
SSD_GROUPS = 4
SSD_STATE = 128
SSD_CHUNK = 128
SSD_CONV_CH = SSD_D_INNER + 2 * SSD_GROUPS * SSD_STATE
MOE_GROUPS = 4
MOE_EXPERTS_PER_GROUP = 8
MOE_EXPERTS = 32
MOE_HIDDEN = 512
ROUTE_LANES = LANES_V7X
MOE_GATHER_PARTS = 3


class Cfg(NamedTuple):
    batch: int
    n_lat: int
    n_ctx: int
    d_model: int
    depth: int
    tm: int
    tq: int
    moe_tile: int
    attn_subtiles: int
    kchunk: int

    @property
    def t_lat(self):
        return self.batch * self.n_lat

    @property
    def t_ctx(self):
        return self.batch * self.n_ctx

    @property
    def t_all(self):
        return self.t_lat + self.t_ctx


def _cparams(sem):
    return pltpu.CompilerParams(dimension_semantics=sem, vmem_limit_bytes=VMEM_LIMIT_BYTES_V7X)


def _silu(v):
    return v * (1.0 / (1.0 + jnp.exp(-v)))


def _const_spec(shape):
    nd = len(shape)
    return pl.BlockSpec(shape, lambda *_: (0,) * nd)


def _layer_spec(stacked, index):
    rest = stacked.shape[1:]
    return pl.BlockSpec((None,) + rest, lambda *_: (index,) + (0,) * len(rest))


def _mod_spec(cfg):
    return pl.BlockSpec((1, 6, cfg.d_model),
                        lambda i: (jnp.minimum((i * cfg.tm) // cfg.n_lat, cfg.batch), 0, 0))


def _seq_pos(cfg, tm):
    r0 = pl.program_id(0) * tm
    row = r0 + lax.broadcasted_iota(jnp.int32, (tm, 1), 0)
    seqlen = jnp.where(r0 >= cfg.t_lat, cfg.n_ctx, cfg.n_lat)
    return row & (seqlen - 1), seqlen


def _norm_mod(x, g, shift, scale):
    ms = jnp.mean(x * x, axis=-1, keepdims=True)
    return (x * lax.rsqrt(ms + RMS_EPS) * g) * (1.0 + scale) + shift


def _cast_kernel(w_ref, o_ref):
    n, n_out = w_ref.shape[1], o_ref.shape[1]
    full = n // LANES_V7X * LANES_V7X
    o_ref[:, :full] = w_ref[:, :full].astype(BF16)
    if n_out > full:
        o_ref[:, full:] = jnp.pad(w_ref[:, full:], ((0, 0), (0, n_out - n))).astype(BF16)


def _to_bf16(w):
    rows, n = math.prod(w.shape[:-1]), w.shape[-1]
    n_out = -(-n // LANES_V7X) * LANES_V7X
    rb = 256
    out = pl.pallas_call(
        _cast_kernel,
        grid=(rows // rb,),
        in_specs=[pl.BlockSpec((rb, n), lambda i: (i, 0))],
        out_specs=pl.BlockSpec((rb, n_out), lambda i: (i, 0)),
        out_shape=jax.ShapeDtypeStruct((rows, n_out), BF16),
        compiler_params=_cparams(("arbitrary",)),
        name="cast_bf16",
    )(w.reshape(rows, n))
    return out.reshape(w.shape[:-1] + (n_out,))


def _hi_lo(v):
    hi = v.astype(BF16)
    return hi, (v - hi.astype(F32)).astype(BF16)


def _adaln_kernel(c_ref, w_ref, b_ref, o_ref):
    s_hi, s_lo = _hi_lo(_silu(c_ref[...]))
    w_hi, w_lo = _hi_lo(w_ref[0])
    o_ref[0] = (jnp.dot(s_hi, w_hi, preferred_element_type=F32) + jnp.dot(s_hi, w_lo, preferred_element_type=F32)
                + jnp.dot(s_lo, w_hi, preferred_element_type=F32)) + b_ref[0]


def _adaln(cfg, c_all, ada_w, ada_b):
    d = cfg.d_model
    out = pl.pallas_call(
        _adaln_kernel,
        grid=(cfg.depth, 6),
        in_specs=[_const_spec((SUBLANES_V7X, d)),
                  pl.BlockSpec((1, d, d), lambda l, j: (l, 0, j)),
                  pl.BlockSpec((1, 1, d), lambda l, j: (l, 0, j))],
        out_specs=pl.BlockSpec((1, SUBLANES_V7X, d), lambda l, j: (l, 0, j)),
        out_shape=jax.ShapeDtypeStruct((cfg.depth, SUBLANES_V7X, 6 * d), F32),
        compiler_params=_cparams(("arbitrary", "arbitrary")),
        name="adaln",
    )(c_all, ada_w, ada_b.reshape(cfg.depth, 1, 6 * d))
    return out.reshape(cfg.depth, SUBLANES_V7X, 6, d)


def _rope_tables(cfg):
    n = cfg.n_lat
    rows = n // GRID_W
    row = jnp.broadcast_to(jnp.arange(rows, dtype=F32)[:, None], (rows, GRID_W)).reshape(n)
    col = jnp.broadcast_to(jnp.arange(GRID_W, dtype=F32)[None, :], (rows, GRID_W)).reshape(n)
    axis_dim = DIFF_QK_DIM // 2
    inv_freq = ROPE_BASE ** (-jnp.arange(0, axis_dim, 2, dtype=F32) / axis_dim)
    ang_r = row[:, None] * inv_freq
    ang_c = col[:, None] * inv_freq
    zeros = jnp.zeros_like(ang_r)
    cos64 = jnp.concatenate([jnp.cos(ang_r), jnp.cos(ang_r), jnp.cos(ang_c), jnp.cos(ang_c)], axis=1)
    sa64 = jnp.concatenate([zeros, jnp.sin(ang_r), zeros, jnp.sin(ang_c)], axis=1)
    sb64 = jnp.concatenate([-jnp.sin(ang_r), zeros, -jnp.sin(ang_c), zeros], axis=1)
    ident = cfg.tm
    cos = jnp.concatenate([jnp.tile(cos64, (1, 2)), jnp.ones((ident, LANES_V7X), F32)], axis=0)
    sa = jnp.concatenate([jnp.tile(sa64, (1, 2)), jnp.zeros((ident, LANES_V7X), F32)], axis=0)
    sb = jnp.concatenate([jnp.tile(sb64, (1, 2)), jnp.zeros((ident, LANES_V7X), F32)], axis=0)
    return cos, sa, sb


def _dual_row_specs(cfg, width, pair):
    lat_arr, ctx_arr = pair
    nl = cfg.t_lat // cfg.tm
    c0 = nl if ctx_arr is lat_arr else 0
    return (pl.BlockSpec((cfg.tm, width), lambda i: (jnp.minimum(i, nl - 1), 0)),
            pl.BlockSpec((cfg.tm, width), lambda i: (jnp.maximum(i - nl, 0) + c0, 0)))


def _dual_rows(cfg, lat_ref, ctx_ref):
    return jnp.where(pl.program_id(0) >= cfg.t_lat // cfg.tm, ctx_ref[...], lat_ref[...])


def _ab_in_kernel(xl_ref, xc_ref, mod_ref, g_ref, w_ref, cos_ref, sa_ref, sb_ref,
                  q_ref, k_ref, v_ref, gb_ref, u_ref, *, cfg):
    m = mod_ref[0]
    hb = _norm_mod(_dual_rows(cfg, xl_ref, xc_ref), g_ref[...], m[0:1], m[1:2]).astype(BF16)
    cos, sa, sb = cos_ref[...], sa_ref[...], sb_ref[...]

    def proj(j):
        return jnp.dot(hb, w_ref[:, j * DIFF_WIDTH:(j + 1) * DIFF_WIDTH], preferred_element_type=F32)

    def rope(t, out_ref, scale):
        for gi in range(DIFF_WIDTH // LANES_V7X):
            xg = t[:, gi * LANES_V7X:(gi + 1) * LANES_V7X]
            r = xg * cos + pltpu.roll(xg, 16, 1) * sa + pltpu.roll(xg, LANES_V7X - 16, 1) * sb
            out_ref[:, gi * LANES_V7X:(gi + 1) * LANES_V7X] = (r * scale).astype(out_ref.dtype)

    rope(proj(0), q_ref, math.log2(math.e) * DIFF_QK_DIM ** -0.5)
    rope(proj(1), k_ref, 1.0)
    v_ref[...] = proj(2).astype(v_ref.dtype)
    gb_ref[...] = proj(3)
    u_ref[...] = proj(4) * proj(5)


def _ab_in(cfg, x_pair, mod_l, g1, w_stack, w_index, tables):
    t, d, tm = cfg.t_all, cfg.d_model, cfg.tm
    n_lat_tiles, per_seq = cfg.t_lat // tm, cfg.n_lat // tm
    tab_spec = pl.BlockSpec((tm, LANES_V7X), lambda i: (jnp.where(i < n_lat_tiles, i % per_seq, per_seq), 0))
    row = lambda w: pl.BlockSpec((tm, w), lambda i: (i, 0))
    return pl.pallas_call(
        functools.partial(_ab_in_kernel, cfg=cfg),
        grid=(t // tm,),
        in_specs=[*_dual_row_specs(cfg, d, x_pair), _mod_spec(cfg), _const_spec((1, d)),
                  _layer_spec(w_stack, w_index), tab_spec, tab_spec, tab_spec],
        out_specs=[row(DIFF_WIDTH)] * 5,
        out_shape=[jax.ShapeDtypeStruct((t, DIFF_WIDTH), BF16)] * 3
        + [jax.ShapeDtypeStruct((t, CONV_WIDTH), F32)] * 2,
        compiler_params=_cparams(("arbitrary",)),
        name="ab_in",
    )(*x_pair, mod_l, g1, w_stack, *tables)


def _attn_body(lvec_ref, q_ref, kc_ref, kl_ref, vc_ref, vl_ref, g_ref, o_ref, *, tq, lam_init, kchunk):
    lv = lvec_ref[...]
    lam = (jnp.exp(jnp.sum(lv[0:1] * lv[1:2], axis=-1, keepdims=True))
           - jnp.exp(jnp.sum(lv[2:3] * lv[3:4], axis=-1, keepdims=True)) + lam_init)
    nt = (((1,), (1,)), ((), ()))
    half = DIFF_QK_DIM
    chunks = [(kc_ref, vc_ref, 0, kc_ref.shape[0])]
    if kl_ref is not None:
        chunks += [(kl_ref, vl_ref, r0, kchunk) for r0 in range(0, kl_ref.shape[0], kchunk)]
    for j in range(q_ref.shape[0] // tq):
        q = q_ref[j * tq:(j + 1) * tq, :].astype(F32)
        lane = lax.broadcasted_iota(jnp.int32, q.shape, 1)
        q2 = jnp.concatenate([jnp.where(lane < half, q, 0.0), jnp.where(lane >= half, q, 0.0)],
                             axis=0).astype(BF16)
        ps, ms, ls = [], [], []
        for k_ref, _, r0, n in chunks:
            s = lax.dot_general(q2, k_ref[r0:r0 + n, :], nt, preferred_element_type=F32)
            m_c = jnp.max(s, axis=-1, keepdims=True)
            p = jnp.exp2(s - m_c)
            ps.append(p)
            ms.append(m_c)
            ls.append(jnp.sum(p, axis=-1, keepdims=True))
        mx = functools.reduce(jnp.maximum, ms)
        scale = [jnp.exp2(m_c - mx) for m_c in ms]
        r = 1.0 / functools.reduce(jnp.add, [sc * l_c for sc, l_c in zip(scale, ls)])
        pv = None
        for p, sc, (_, v_ref, r0, n) in zip(ps, scale, chunks):
            w = sc * r
            pd = (p[:tq] * w[:tq] - p[tq:] * (lam * w[tq:])).astype(BF16)
            part = jnp.dot(pd, v_ref[r0:r0 + n, :], preferred_element_type=F32)
            pv = part if pv is None else pv + part
        o = pv * lax.rsqrt(jnp.mean(pv * pv, axis=-1, keepdims=True) + RMS_EPS)
        o_ref[j * tq:(j + 1) * tq, :] = ((o * g_ref[...]) * (1.0 - lam_init)).astype(o_ref.dtype)


def _attn_lat_kernel(lvec_ref, q_ref, kc_ref, kl_ref, vc_ref, vl_ref, g_ref, o_ref, **kw):
    _attn_body(lvec_ref, q_ref, kc_ref, kl_ref, vc_ref, vl_ref, g_ref, o_ref, **kw)


def _attn_ctx_kernel(lvec_ref, q_ref, kc_ref, vc_ref, g_ref, o_ref, **kw):
    _attn_body(lvec_ref, q_ref, kc_ref, None, vc_ref, None, g_ref, o_ref, **kw)


def _attention(cfg, q, k, v, lvec, subln_g, lam_init):
    tq, nsub = cfg.tq, cfg.attn_subtiles
    tstep = tq * nsub
    nqb = cfg.n_lat // tstep
    ctx_blk0 = cfg.t_lat // cfg.n_ctx
    hw = DIFF_V_DIM
    ctx_spec = pl.BlockSpec((cfg.n_ctx, hw), lambda b, h, *_: (ctx_blk0 + b, h))
    lat_spec = pl.BlockSpec((cfg.n_lat, hw), lambda b, h, *_: (b, h))
    q_spec = pl.BlockSpec((tstep, hw), lambda b, h, i: (b * nqb + i, h))
    kw = dict(tq=tq, lam_init=lam_init, kchunk=min(cfg.kchunk, cfg.n_lat))
    o_lat = pl.pallas_call(
        functools.partial(_attn_lat_kernel, **kw),
        grid=(cfg.batch, N_DIFF_HEADS, nqb),
        in_specs=[_const_spec(lvec.shape), q_spec, ctx_spec, lat_spec, ctx_spec, lat_spec, _const_spec((1, hw))],
        out_specs=q_spec,
        out_shape=jax.ShapeDtypeStruct((cfg.t_lat, DIFF_WIDTH), BF16),
        compiler_params=_cparams(("arbitrary", "arbitrary", "arbitrary")),
        name="diff_attn",
    )(lvec, q, k, k, v, v, subln_g)
    tqc = min(tq, cfg.n_ctx)
    o_ctx = pl.pallas_call(
        functools.partial(_attn_ctx_kernel, tq=tqc, lam_init=lam_init, kchunk=cfg.n_ctx),
        grid=(cfg.batch, N_DIFF_HEADS),
        in_specs=[_const_spec(lvec.shape), ctx_spec, ctx_spec, ctx_spec, _const_spec((1, hw))],
        out_specs=pl.BlockSpec((cfg.n_ctx, hw), lambda b, h: (b, h)),
        out_shape=jax.ShapeDtypeStruct((cfg.t_ctx, DIFF_WIDTH), BF16),
        compiler_params=_cparams(("arbitrary", "arbitrary")),
        name="diff_attn_ctx",
    )(lvec, q, k, v, subln_g)
    return o_lat, o_ctx


def _route_from_logits(lg):
    lane = lax.broadcasted_iota(jnp.int32, lg.shape, 1)
    neg = -jnp.inf
    big = jnp.int32(ROUTE_LANES)
    gl = jnp.where(lane < MOE_GROUPS, lg, neg)
    gmax = jnp.max(gl, axis=-1, keepdims=True)
    gidx = jnp.min(jnp.where(gl == gmax, lane, big), axis=-1, keepdims=True)
    gw = 1.0 / jnp.sum(jnp.exp(gl - gmax), axis=-1, keepdims=True)
    lo = MOE_GROUPS + gidx * MOE_EXPERTS_PER_GROUP
    el = jnp.where((lane >= lo) & (lane < lo + MOE_EXPERTS_PER_GROUP), lg, neg)
    m1 = jnp.max(el, axis=-1, keepdims=True)
    i1 = jnp.min(jnp.where(el == m1, lane, big), axis=-1, keepdims=True)
    el2 = jnp.where(lane == i1, neg, el)
    m2 = jnp.max(el2, axis=-1, keepdims=True)
    i2 = jnp.min(jnp.where(el2 == m2, lane, big), axis=-1, keepdims=True)
    w1 = gw / (1.0 + jnp.exp(m2 - m1))
    w2 = gw - w1
    e1 = (i1 - MOE_GROUPS).astype(F32)
    e2 = (i2 - MOE_GROUPS).astype(F32)
    return jnp.where(lane == 0, e1, jnp.where(lane == 1, e2, jnp.where(lane == 2, w1, jnp.where(lane == 3, w2, 0.0))))


def _pack_bf16_pairs(v):
    w = v.shape[1] // 2
    lo = pltpu.bitcast(v[:, :w].astype(BF16).astype(F32), jnp.uint32)
    hi = pltpu.bitcast(v[:, w:].astype(BF16).astype(F32), jnp.uint32)
    return pltpu.bitcast((hi & jnp.uint32(0xFFFF0000)) | (lo >> 16), F32)


def _unpack_bf16_pairs(words):
    u = pltpu.bitcast(words, jnp.uint32)
    lo = pltpu.bitcast(u << 16, F32)
    hi = pltpu.bitcast(u & jnp.uint32(0xFFFF0000), F32)
    return lo, hi


def _router_weights(wg, bg, we, be):
    d = wg.shape[0]
    pad = ROUTE_LANES - MOE_GROUPS - MOE_EXPERTS
    wr = jnp.concatenate([wg, we, jnp.zeros((d, pad), F32)], axis=1)
    hi, lo = _hi_lo(wr)
    br = jnp.concatenate([bg, be, jnp.zeros((pad,), F32)]).reshape(1, ROUTE_LANES)
    return jnp.concatenate([hi, lo], axis=1), hi, br


def _expert_onehots(rt):
    lane = lax.broadcasted_iota(jnp.int32, rt.shape, 1)
    return (jnp.where(lane == rt[:, 0:1].astype(jnp.int32), 1.0, 0.0),
            jnp.where(lane == rt[:, 1:2].astype(jnp.int32), 1.0, 0.0))


def _residual_norm_route(x, mix, m, g2, wr2_ref, wrh_ref, br_ref, xo_ref, h2_ref, rt_ref, cnt_ref):
    xn = x + m[2:3] * mix
    xo_ref[...] = xn
    h2 = _norm_mod(xn, g2, m[3:4], m[4:5])
    h2_ref[...] = _pack_bf16_pairs(h2)
    hh, hl = _hi_lo(h2)
    a = jnp.dot(hh, wr2_ref[...], preferred_element_type=F32)
    lg = (a[:, :ROUTE_LANES] + a[:, ROUTE_LANES:]) + jnp.dot(hl, wrh_ref[...], preferred_element_type=F32)
    rt = _route_from_logits(lg + br_ref[...])
    rt_ref[...] = rt

    @pl.when(pl.program_id(0) == 0)
    def _():
        cnt_ref[...] = jnp.zeros_like(cnt_ref)

    o1, o2 = _expert_onehots(rt)
    cnt_ref[...] += jnp.sum(o1 + o2, axis=0, keepdims=True)


def _epilogue_specs(cfg, x_pair):
    d, tm = cfg.d_model, cfg.tm
    row = lambda w: pl.BlockSpec((tm, w), lambda i: (i, 0))
    in_specs = [*_dual_row_specs(cfg, d, x_pair), _mod_spec(cfg), _const_spec((1, d)),
                _const_spec((d, 2 * ROUTE_LANES)),
                _const_spec((d, ROUTE_LANES)), _const_spec((1, ROUTE_LANES))]
    out_specs = [row(d), row(d // 2), row(ROUTE_LANES), _const_spec((SUBLANES_V7X, ROUTE_LANES))]
    out_shape = [jax.ShapeDtypeStruct((cfg.t_all, d), F32), jax.ShapeDtypeStruct((cfg.t_all, d // 2), F32),
                 jax.ShapeDtypeStruct((cfg.t_all, ROUTE_LANES), F32),
                 jax.ShapeDtypeStruct((SUBLANES_V7X, ROUTE_LANES), F32)]
    return in_specs, out_specs, out_shape


def _halo_specs(cfg, width, col_block=0):
    per = cfg.tm // SUBLANES_V7X
    last = cfg.t_all // SUBLANES_V7X - 1
    prev = pl.BlockSpec((SUBLANES_V7X, width), lambda i, *_: (jnp.maximum(i * per - 1, 0), col_block))
    nxt = pl.BlockSpec((SUBLANES_V7X, width), lambda i, *_: (jnp.minimum((i + 1) * per, last), col_block))
    return prev, nxt


def _shifted_rows(pad_ref, u, prev_blk, next_blk, pos, seqlen):
    tm = u.shape[0]
    s = SUBLANES_V7X
    pad_ref[s:s + tm, :] = u
    pad_ref[0:s, :] = prev_blk
    pad_ref[s + tm:2 * s + tm, :] = next_blk
    um1 = jnp.where(pos == 0, 0.0, pad_ref[s - 1:s - 1 + tm, :])
    up1 = jnp.where(pos == seqlen - 1, 0.0, pad_ref[s + 1:s + 1 + tm, :])
    return um1, up1


def _ab_out_kernel(ol_ref, oc_ref, gb_ref, u_ref, up_ref, un_ref, cw_ref, wo_ref,
                   xl_ref, xc_ref, mod_ref, g2_ref, wr2_ref, wrh_ref, br_ref,
                   xo_ref, h2_ref, rt_ref, cnt_ref, pad_ref, *, cfg):
    pos, seqlen = _seq_pos(cfg, cfg.tm)
    u = u_ref[...]
    um1, up1 = _shifted_rows(pad_ref, u, up_ref[...], un_ref[...], pos, seqlen)
    cw = cw_ref[...]
    conv = gb_ref[...] * (um1 * cw[0:1] + u * cw[1:2] + up1 * cw[2:3])
    mix = (jnp.dot(_dual_rows(cfg, ol_ref, oc_ref), wo_ref[:DIFF_WIDTH, :], preferred_element_type=F32)
           + jnp.dot(conv.astype(BF16), wo_ref[DIFF_WIDTH:, :], preferred_element_type=F32))
    _residual_norm_route(_dual_rows(cfg, xl_ref, xc_ref), mix, mod_ref[0], g2_ref[...], wr2_ref, wrh_ref, br_ref,
                         xo_ref, h2_ref, rt_ref, cnt_ref)


def _ab_out(cfg, o_pair, gb, u, conv_w, wo_stack, w_index, x_pair, mod_l, g2, router):
    tm = cfg.tm
    row = lambda w: pl.BlockSpec((tm, w), lambda i: (i, 0))
    prev, nxt = _halo_specs(cfg, CONV_WIDTH)
    ep_in, ep_out, ep_shape = _epilogue_specs(cfg, x_pair)
    return pl.pallas_call(
        functools.partial(_ab_out_kernel, cfg=cfg),
        grid=(cfg.t_all // tm,),
        in_specs=[*_dual_row_specs(cfg, DIFF_WIDTH, o_pair), row(CONV_WIDTH), row(CONV_WIDTH), prev, nxt,
                  _const_spec(conv_w.shape),
                  _layer_spec(wo_stack, w_index)] + ep_in,
        out_specs=ep_out,
        out_shape=ep_shape,
        scratch_shapes=[pltpu.VMEM((tm + 2 * SUBLANES_V7X, CONV_WIDTH), F32)],
        compiler_params=_cparams(("arbitrary",)),
        name="ab_out",
    )(*o_pair, gb, u, u, u, conv_w, wo_stack, *x_pair, mod_l, g2, *router)


def _ssd_in_kernel(x_ref, xp_ref, xn_ref, mod_ref, g_ref, w_ref, cw_ref, cb_ref,
                   z_ref, xbc_ref, dt_ref, pad_ref, *, cfg):
    tm = cfg.tm
    s = SUBLANES_V7X
    m = mod_ref[0]
    g = g_ref[...]
    h = _norm_mod(x_ref[...], g, m[0:1], m[1:2])
    hb = h.astype(BF16)
    h_ext = jnp.concatenate([_norm_mod(xp_ref[...], g, m[0:1], m[1:2]), h,
                             _norm_mod(xn_ref[...], g, m[0:1], m[1:2])], axis=0).astype(BF16)
    pos, seqlen = _seq_pos(cfg, tm)
    cw = 512
    for j in range(SSD_D_INNER // cw):
        z_ref[:, j * cw:(j + 1) * cw] = jnp.dot(hb, w_ref[:, j * cw:(j + 1) * cw],
                                                preferred_element_type=F32).astype(z_ref.dtype)
    for j in range(SSD_CONV_CH // cw):
        c0 = SSD_D_INNER + j * cw
        pad = pad_ref.at[j]
        pad[...] = jnp.dot(h_ext, w_ref[:, c0:c0 + cw], preferred_element_type=F32)
        taps = cw_ref[:, j * cw:(j + 1) * cw]
        um1 = jnp.where(pos == 0, 0.0, pad[s - 1:s - 1 + tm, :])
        up1 = jnp.where(pos == seqlen - 1, 0.0, pad[s + 1:s + 1 + tm, :])
        conv = um1 * taps[0:1] + pad[s:s + tm, :] * taps[1:2] + up1 * taps[2:3] + cb_ref[:, j * cw:(j + 1) * cw]
        xbc_ref[:, j * cw:(j + 1) * cw] = _silu(conv).astype(xbc_ref.dtype)
    dt_ref[...] = jnp.dot(hb, w_ref[:, SSD_D_INNER + SSD_CONV_CH:], preferred_element_type=F32)


def _ssd_in(cfg, x, mod_l, g1, w_stack, w_index, conv_w, conv_b):
    t, d, tm = cfg.t_all, cfg.d_model, cfg.tm
    row = lambda w: pl.BlockSpec((tm, w), lambda i: (i, 0))
    prev, nxt = _halo_specs(cfg, d)
    return pl.pallas_call(
        functools.partial(_ssd_in_kernel, cfg=cfg),
        grid=(t // tm,),
        in_specs=[row(d), prev, nxt, _mod_spec(cfg), _const_spec((1, d)), _layer_spec(w_stack, w_index),
                  _const_spec(conv_w.shape), _const_spec(conv_b.shape)],
        out_specs=[row(SSD_D_INNER), row(SSD_CONV_CH), row(LANES_V7X)],
        out_shape=[jax.ShapeDtypeStruct((t, SSD_D_INNER), BF16), jax.ShapeDtypeStruct((t, SSD_CONV_CH), BF16),
                   jax.ShapeDtypeStruct((t, LANES_V7X), F32)],
        scratch_shapes=[pltpu.VMEM((SSD_CONV_CH // 512, tm + 2 * SUBLANES_V7X, 512), F32)],
        compiler_params=_cparams(("arbitrary",)),
        name="ssd_in",
    )(x, x, x, mod_l, g1, w_stack, conv_w, conv_b)


def _split3(v):
    b1 = v.astype(BF16)
    r1 = v - b1.astype(F32)
    b2 = r1.astype(BF16)
    b3 = (r1 - b2.astype(F32)).astype(BF16)
    return jnp.concatenate([b1, b2, b3], axis=1)


def _softplus(v):
    return jnp.maximum(v, 0.0) + jnp.log(1.0 + jnp.exp(-jnp.abs(v)))


def _ssd_scan_kernel(*refs):
    n_in = 5
    fwd_in, bwd_in = refs[:n_in], refs[n_in:2 * n_in]
    bias_ref, biast_ref, alog_ref, alogt_ref, ewide_ref, yf_ref, yb_ref, sf_ref, sb_ref = refs[2 * n_in:]

    @pl.when(pl.program_id(1) == 0)
    def _():
        sf_ref[...] = jnp.zeros_like(sf_ref)
        sb_ref[...] = jnp.zeros_like(sb_ref)

    for d, (ins, y_ref, state_ref) in enumerate(((fwd_in, yf_ref, sf_ref), (bwd_in, yb_ref, sb_ref))):
        _ssd_chunk(d == 0, *ins, bias_ref[d], biast_ref[d], alog_ref[d], alogt_ref[d], ewide_ref, y_ref, state_ref)


def _ssd_chunk(fwd, xs_ref, b_ref, c_ref, dt_ref, dtt_ref, bias, biast, alog, alogt, ewide_ref, y_ref, state_ref):
    q = SSD_CHUNK
    dt = _softplus(dt_ref[0] + bias)
    dtt = _softplus(dtt_ref[0] + biast)
    a = dt * (-jnp.exp(alog))
    at = dtt * (-jnp.exp(alogt))

    ri = lax.broadcasted_iota(jnp.int32, (q, q), 0)
    ci = lax.broadcasted_iota(jnp.int32, (q, q), 1)
    keep = (ci <= ri) if fwd else (ci >= ri)
    tri = jnp.where(keep, 1.0, 0.0)
    cs = jnp.dot(tri, a, precision=HIGHEST, preferred_element_type=F32)
    cst = lax.dot_general(at, tri, (((1,), (1,)), ((), ())), precision=HIGHEST,
                          preferred_element_type=F32)
    tot = jnp.sum(a, axis=0, keepdims=True)

    ewide = ewide_ref[...]
    w_wide = jnp.dot(_split3(dt * jnp.exp(tot - cs)), ewide, preferred_element_type=F32)
    ecs_wide = jnp.dot(_split3(jnp.exp(cs)), ewide, preferred_element_type=F32)
    dec_wide = jnp.dot(_split3(jnp.broadcast_to(jnp.exp(tot), (SUBLANES_V7X, SSD_HEADS))), ewide,
                       preferred_element_type=F32)[0:1]

    lane = lax.broadcasted_iota(jnp.int32, (q, LANES_V7X), 1)
    lo = lane < SSD_HEAD_DIM
    hpg = SSD_HEADS // SSD_GROUPS
    gw = hpg * SSD_HEAD_DIM
    nt = (((1,), (1,)), ((), ()))
    tn = (((0,), (0,)), ((), ()))
    for g in range(SSD_GROUPS):
        bm = b_ref[:, g * SSD_STATE:(g + 1) * SSD_STATE]
        cm = c_ref[:, g * SSD_STATE:(g + 1) * SSD_STATE]
        cb = lax.dot_general(cm, bm, nt, preferred_element_type=F32)
        sl = slice(g * gw, (g + 1) * gw)
        y_off = jnp.dot(cm, state_ref[:, sl].astype(BF16), preferred_element_type=F32) * ecs_wide[:, sl]
        for pr in range(hpg // 2):
            lhs, rhs = [], []
            c0 = g * gw + pr * LANES_V7X
            xs_pair = xs_ref[:, c0:c0 + LANES_V7X]
            zero = jnp.zeros_like(xs_pair)
            for k in range(2):
                h = g * hpg + pr * 2 + k
                seg = cs[:, h:h + 1] - cst[h:h + 1, :]
                lmat = jnp.exp(jnp.where(keep, seg, -jnp.inf))
                lhs.append((cb * lmat * dtt[h:h + 1, :]).astype(BF16))
                rhs.append(jnp.where(lo if k == 0 else jnp.logical_not(lo), xs_pair, zero))
            y = jnp.dot(jnp.concatenate(lhs, axis=1), jnp.concatenate(rhs, axis=0), preferred_element_type=F32)
            y_ref[:, c0:c0 + LANES_V7X] = (y + y_off[:, pr * LANES_V7X:(pr + 1) * LANES_V7X]).astype(y_ref.dtype)
        x2 = (xs_ref[:, sl].astype(F32) * w_wide[:, sl]).astype(BF16)
        upd = lax.dot_general(bm, x2, tn, preferred_element_type=F32)
        state_ref[:, sl] = state_ref[:, sl] * dec_wide[:, sl] + upd


def _ssd_scan(cfg, xbc_act, dt_raw, dt_bias, a_log):
    q = SSD_CHUNK
    n_cc, n_lc = cfg.n_ctx // q, cfg.n_lat // q
    ctx0 = cfg.t_lat // q
    h = SSD_HEADS
    dt2 = dt_raw[:, :2 * h].reshape(cfg.t_all, 2, h).transpose(1, 0, 2)
    dt2t = dt2.transpose(0, 2, 1)
    bias = dt_bias.reshape(2, 1, h)
    biast = dt_bias.reshape(2, h, 1)
    alog = a_log.reshape(2, 1, h)
    alogt = a_log.reshape(2, h, 1)
    head_of_row = jnp.tile(jnp.arange(h), 3)[:, None]
    ewide = (head_of_row == (jnp.arange(SSD_D_INNER) // SSD_HEAD_DIM)[None, :]).astype(BF16)

    def blk(d):
        def index(b, s):
            cs = s if d == 0 else n_cc - 1 - s
            ls = s - n_cc if d == 0 else n_lc - 1 - (s - n_cc)
            return jnp.where(s < n_cc, ctx0 + b * n_cc + cs, b * n_lc + ls)
        return index

    gs = SSD_GROUPS * SSD_STATE

    def dir_specs(d):
        at = blk(d)
        col = lambda w, cblk: pl.BlockSpec((q, w), lambda b, s: (at(b, s), cblk))
        return [col(SSD_D_INNER, 0), col(gs, SSD_D_INNER // gs), col(gs, SSD_D_INNER // gs + 1),
                pl.BlockSpec((1, q, h), lambda b, s: (d, at(b, s), 0)),
                pl.BlockSpec((1, h, q), lambda b, s: (d, 0, at(b, s)))]

    dir_args = (xbc_act, xbc_act, xbc_act, dt2, dt2t)
    y_spec = lambda d: pl.BlockSpec((q, SSD_D_INNER), lambda b, s: (blk(d)(b, s), 0))
    return pl.pallas_call(
        _ssd_scan_kernel,
        grid=(cfg.batch, n_cc + n_lc),
        in_specs=dir_specs(0) + dir_specs(1) + [_const_spec(bias.shape), _const_spec(biast.shape),
                                                _const_spec(alog.shape), _const_spec(alogt.shape),
                                                _const_spec(ewide.shape)],
        out_specs=[y_spec(0), y_spec(1)],
        out_shape=[jax.ShapeDtypeStruct((cfg.t_all, SSD_D_INNER), BF16)] * 2,
        scratch_shapes=[pltpu.VMEM((SSD_STATE, SSD_D_INNER), F32)] * 2,
        compiler_params=_cparams(("arbitrary", "arbitrary")),
        name="ssd_scan",
    )(*dir_args, *dir_args, bias, biast, alog, alogt, ewide)


def _ssd_out_kernel(yf_ref, yb_ref, xs_ref, z_ref, dw_ref, ng_ref, wo_ref,
                    xl_ref, xc_ref, mod_ref, g2_ref, wr2_ref, wrh_ref, br_ref, xo_ref, h2_ref, rt_ref, cnt_ref,
                    *, cfg):
    gw = SSD_D_INNER // SSD_GROUPS
    mix = None
    for g in range(SSD_GROUPS):
        sl = slice(g * gw, (g + 1) * gw)
        y = yf_ref[:, sl].astype(F32) + yb_ref[:, sl].astype(F32) + xs_ref[:, sl].astype(F32) * dw_ref[:, sl]
        y = y * _silu(z_ref[:, sl].astype(F32))
        y = (y * lax.rsqrt(jnp.mean(y * y, axis=-1, keepdims=True) + RMS_EPS)) * ng_ref[:, sl]
        part = jnp.dot(y.astype(BF16), wo_ref[sl, :], preferred_element_type=F32)
        mix = part if mix is None else mix + part
    _residual_norm_route(_dual_rows(cfg, xl_ref, xc_ref), mix, mod_ref[0], g2_ref[...], wr2_ref, wrh_ref, br_ref,
                         xo_ref, h2_ref, rt_ref, cnt_ref)


def _ssd_out(cfg, y2, xbc_act, z, d_wide, norm_g, wo_stack, w_index, x, mod_l, g2, router):
    tm = cfg.tm
    row = lambda w: pl.BlockSpec((tm, w), lambda i: (i, 0))
    ep_in, ep_out, ep_shape = _epilogue_specs(cfg, (x, x))
    return pl.pallas_call(
        functools.partial(_ssd_out_kernel, cfg=cfg),
        grid=(cfg.t_all // tm,),
        in_specs=[row(SSD_D_INNER)] * 4 + [_const_spec((1, SSD_D_INNER)),
                  _const_spec((1, SSD_D_INNER)), _layer_spec(wo_stack, w_index)] + ep_in,
        out_specs=ep_out,
        out_shape=ep_shape,
        compiler_params=_cparams(("arbitrary",)),
        name="ssd_out",
    )(*y2, xbc_act, z, d_wide, norm_g, wo_stack, x, x, mod_l, g2, *router)


def _expert_kernel(eid_ref, nused_ref, *refs, blocks_per_part):
    xb_refs = refs[:MOE_GATHER_PARTS]
    wg_ref, wu_ref, wd_ref, y_ref, wg_s, wu_s, wd_s = refs[MOE_GATHER_PARTS:]
    i = pl.program_id(0)
    changed = jnp.logical_or(i == 0, eid_ref[i] != eid_ref[jnp.maximum(i - 1, 0)])

    @pl.when(jnp.logical_and(changed, i < nused_ref[0]))
    def _():
        wg_s[...] = wg_ref[0, 0].astype(BF16)
        wu_s[...] = wu_ref[0, 0].astype(BF16)
        wd_s[...] = wd_ref[0, 0].astype(BF16)

    @pl.when(i < nused_ref[0])
    def _():
        half = wg_s.shape[0] // 2
        words = xb_refs[0][...]
        for k in range(1, MOE_GATHER_PARTS):
            words = jnp.where(i >= k * blocks_per_part, xb_refs[k][...], words)
        lo, hi = _unpack_bf16_pairs(words)
        lo, hi = lo.astype(BF16), hi.astype(BF16)
        hg = (jnp.dot(lo, wg_s[:half, :], preferred_element_type=F32)
              + jnp.dot(hi, wg_s[half:, :], preferred_element_type=F32))
        hu = (jnp.dot(lo, wu_s[:half, :], preferred_element_type=F32)
              + jnp.dot(hi, wu_s[half:, :], preferred_element_type=F32))
        y = jnp.dot((_silu(hg) * hu).astype(BF16), wd_s[...], preferred_element_type=F32)
        y_ref[...] = _pack_bf16_pairs(y)

    @pl.when(i >= nused_ref[0])
    def _():
        y_ref[...] = jnp.zeros_like(y_ref)


def _experts(cfg, layer, block_eid, n_used, xb_parts, w_gate, w_up, w_down):
    d, tile = cfg.d_model, cfg.moe_tile
    per = xb_parts[0].shape[0] // tile
    n_blocks = per * MOE_GATHER_PARTS

    def part_spec(k):
        return pl.BlockSpec((tile, d // 2), lambda i, e, n: (jnp.clip(i - k * per, 0, per - 1), 0))

    grid_spec = pltpu.PrefetchScalarGridSpec(
        num_scalar_prefetch=2,
        grid=(n_blocks,),
        in_specs=[part_spec(k) for k in range(MOE_GATHER_PARTS)] + [
                  pl.BlockSpec((1, 1, d, MOE_HIDDEN), lambda i, e, n: (layer, e[i], 0, 0)),
                  pl.BlockSpec((1, 1, d, MOE_HIDDEN), lambda i, e, n: (layer, e[i], 0, 0)),
                  pl.BlockSpec((1, 1, MOE_HIDDEN, d), lambda i, e, n: (layer, e[i], 0, 0))],
        out_specs=pl.BlockSpec((tile, d // 2), lambda i, e, n: (i, 0)),
        scratch_shapes=[pltpu.VMEM((d, MOE_HIDDEN), BF16), pltpu.VMEM((d, MOE_HIDDEN), BF16),
                        pltpu.VMEM((MOE_HIDDEN, d), BF16)],
    )
    return pl.pallas_call(
        functools.partial(_expert_kernel, blocks_per_part=per),
        grid_spec=grid_spec,
        out_shape=jax.ShapeDtypeStruct((n_blocks * tile, d // 2), F32),
        compiler_params=_cparams(("arbitrary",)),
        name="moe_experts",
    )(block_eid, n_used, *xb_parts, w_gate, w_up, w_down)


def _dispatch_kernel(rt_ref, cnt_ref, pos_ref, carry_ref, start_ref, *, tile):
    tm = rt_ref.shape[0]
    rt = rt_ref[...]
    lane = lax.broadcasted_iota(jnp.int32, rt.shape, 1)
    o1, o2 = _expert_onehots(rt)
    cnt1 = jnp.sum(o1, axis=0, keepdims=True)
    cnt2 = jnp.sum(o2, axis=0, keepdims=True)

    @pl.when(pl.program_id(0) == 0)
    def _():
        padded = jnp.floor((cnt_ref[...] + (tile - 1)) * (1.0 / tile)) * tile
        r = lax.broadcasted_iota(jnp.int32, (ROUTE_LANES, ROUTE_LANES), 0)
        c = lax.broadcasted_iota(jnp.int32, (ROUTE_LANES, ROUTE_LANES), 1)
        excl = jnp.where(r < c, 1.0, 0.0)
        start_ref[...] = jnp.dot(padded, excl, precision=HIGHEST, preferred_element_type=F32)
        carry_ref[...] = jnp.zeros_like(carry_ref)

    ri = lax.broadcasted_iota(jnp.int32, (tm, tm), 0)
    ci = lax.broadcasted_iota(jnp.int32, (tm, tm), 1)
    earlier = jnp.where(ci < ri, 1.0, 0.0).astype(BF16)
    p1 = jnp.dot(earlier, o1.astype(BF16), preferred_element_type=F32)
    p2 = jnp.dot(earlier, o2.astype(BF16), preferred_element_type=F32)
    base = start_ref[0:1] + carry_ref[0:1]
    pos1 = jnp.sum(o1 * (base + p1), axis=-1, keepdims=True)
    pos2 = jnp.sum(o2 * (base + cnt1 + p2), axis=-1, keepdims=True)
    cols = jnp.where(lane == 0, pos1, jnp.where(lane == 1, pos2, 0.0))
    r = lax.broadcasted_iota(jnp.int32, (SUBLANES_V7X, ROUTE_LANES), 0)
    c = lax.broadcasted_iota(jnp.int32, (SUBLANES_V7X, ROUTE_LANES), 1)
    pick = jnp.where(r == c, 1.0, 0.0)
    rows = lax.dot_general(pick, cols, (((1,), (1,)), ((), ())), precision=HIGHEST, preferred_element_type=F32)
    pos_ref[...] = rows.astype(jnp.int32)
    carry_ref[...] += cnt1 + cnt2


def _dispatch(cfg, route, cnt):
    tile, tm = cfg.moe_tile, cfg.tm
    t = cfg.t_all
    a_total = 2 * t
    pos = pl.pallas_call(
        functools.partial(_dispatch_kernel, tile=tile),
        grid=(t // tm,),
        in_specs=[pl.BlockSpec((tm, ROUTE_LANES), lambda i: (i, 0)), _const_spec((SUBLANES_V7X, ROUTE_LANES))],
        out_specs=pl.BlockSpec((SUBLANES_V7X, tm), lambda i: (0, i)),
        out_shape=jax.ShapeDtypeStruct((SUBLANES_V7X, t), jnp.int32),
        scratch_shapes=[pltpu.VMEM((SUBLANES_V7X, ROUTE_LANES), F32), pltpu.VMEM((SUBLANES_V7X, ROUTE_LANES), F32)],
        compiler_params=_cparams(("arbitrary",)),
        name="moe_dispatch",
    )(route, cnt)
    counts = cnt[0, :MOE_EXPERTS].astype(jnp.int32)
    pend = jnp.cumsum((counts + tile - 1) // tile * tile)
    n_blocks = (a_total + MOE_EXPERTS * (tile - 1)) // tile
    n_blocks = -(-n_blocks // MOE_GATHER_PARTS) * MOE_GATHER_PARTS
    block_eid = jnp.minimum(jnp.sum(pend[None, :] <= (jnp.arange(n_blocks, dtype=jnp.int32) * tile)[:, None], axis=1),
                            MOE_EXPERTS - 1).astype(jnp.int32)
    n_used = (pend[-1] // tile).astype(jnp.int32).reshape(1)
    slots = pos[0:2].reshape(-1)
    slot_tok = (jnp.arange(n_blocks * tile, dtype=jnp.int32) % t).at[slots].set(
        jnp.arange(a_total, dtype=jnp.int32) % t, unique_indices=True, mode="promise_in_bounds")
    return slot_tok, block_eid, n_used, slots


def _combine_kernel(x_ref, y1_ref, y2_ref, rt_ref, mod_ref, g_ref, o_ref, *, final):
    rt = rt_ref[...]
    a_lo, a_hi = _unpack_bf16_pairs(y1_ref[...])
    b_lo, b_hi = _unpack_bf16_pairs(y2_ref[...])
    w1, w2 = rt[:, 2:3], rt[:, 3:4]
    f = jnp.concatenate([w1 * a_lo + w2 * b_lo, w1 * a_hi + w2 * b_hi], axis=1)
    xn = x_ref[...] + mod_ref[0][5:6] * f
    if final:
        xn = (xn * lax.rsqrt(jnp.mean(xn * xn, axis=-1, keepdims=True) + RMS_EPS)) * g_ref[...]
    o_ref[...] = xn


def _combine(cfg, x, y12, route, mod_l, g, final):
    d, tm = cfg.d_model, cfg.tm
    n_tiles = cfg.t_all // tm
    row = lambda w: pl.BlockSpec((tm, w), lambda i: (i, 0))
    t_out = cfg.t_lat if final else cfg.t_all
    return pl.pallas_call(
        functools.partial(_combine_kernel, final=final),
        grid=(t_out // tm,),
        in_specs=[row(d), row(d // 2), pl.BlockSpec((tm, d // 2), lambda i: (n_tiles + i, 0)),
                  row(ROUTE_LANES), _mod_spec(cfg), _const_spec((1, d))],
        out_specs=row(d),
        out_shape=jax.ShapeDtypeStruct((t_out, d), F32),
        compiler_params=_cparams(("arbitrary",)),
        name="moe_combine",
    )(x, y12, y12, route, mod_l, g)


def _forward(cfg, x, c, ctx, c_ctx, ada_w, ada_b, norm1_g, norm2_g, ab_w_in, ab_w_out,
             diff_lq1, diff_lk1, diff_lq2, diff_lk2, diff_subln_g, bconv_w,
             ssd_w_in, ssd_conv_w, ssd_conv_b, ssd_A_log, ssd_dt_bias, ssd_D, ssd_norm_g, ssd_w_out,
             moe_wg, moe_bg, moe_we, moe_be, moe_w_gate, moe_w_up, moe_w_down, final_norm_g):
    d = cfg.d_model
    b = cfg.batch
    pow2 = lambda n: n & (n - 1) == 0
    assert pow2(cfg.n_lat) and pow2(cfg.n_ctx), "sequence positions are taken with a bit mask"
    assert cfg.n_lat % cfg.tm == 0 and cfg.t_ctx % cfg.tm == 0, "row tiles must not straddle latent / context rows"
    assert cfg.tm % cfg.n_ctx == 0 or cfg.n_ctx % cfg.tm == 0, "row tiles hold whole context sequences or parts of one"
    assert cfg.n_lat % (cfg.tq * cfg.attn_subtiles) == 0 and cfg.n_lat % min(cfg.kchunk, cfg.n_lat) == 0
    assert cfg.n_lat % SSD_CHUNK == 0 and cfg.n_ctx % SSD_CHUNK == 0 and cfg.n_lat % GRID_W == 0
    x_pair = (x.reshape(cfg.t_lat, d), ctx.reshape(cfg.t_ctx, d))
    c_all = jnp.zeros((SUBLANES_V7X, d), F32).at[:b].set(c).at[b].set(c_ctx)
    mod = _adaln(cfg, c_all, ada_w, ada_b)
    tables = _rope_tables(cfg)
    ab_w_in_b, ab_w_out_b = _to_bf16(ab_w_in), _to_bf16(ab_w_out)
    ssd_w_in_b, ssd_w_out_b = _to_bf16(ssd_w_in), _to_bf16(ssd_w_out)

    for layer in range(cfg.depth):
        i = layer // 2
        mod_l = mod[layer]
        g1 = norm1_g[layer].reshape(1, d)
        g2 = norm2_g[layer].reshape(1, d)
        router = _router_weights(moe_wg[layer], moe_bg[layer], moe_we[layer], moe_be[layer])
        if layer % 2 == 0:
            lam_init = 0.8 - 0.6 * math.exp(-0.3 * layer)
            q, k, v, gb, u = _ab_in(cfg, x_pair, mod_l, g1, ab_w_in_b, i, tables)
            lvec = jnp.stack([diff_lq1[i], diff_lk1[i], diff_lq2[i], diff_lk2[i]])
            o_pair = _attention(cfg, q, k, v, lvec, diff_subln_g[i].reshape(1, DIFF_V_DIM), lam_init)
            xa, h2, route, cnt = _ab_out(cfg, o_pair, gb, u, bconv_w[i], ab_w_out_b, i, x_pair, mod_l, g2, router)
        else:
            z, xbc_act, dt_raw = _ssd_in(cfg, xa, mod_l, g1, ssd_w_in_b, i, ssd_conv_w[i],
                                         ssd_conv_b[i].reshape(1, SSD_CONV_CH))
            y2 = _ssd_scan(cfg, xbc_act, dt_raw, ssd_dt_bias[i], ssd_A_log[i])
            d_wide = jnp.repeat(ssd_D[i], SSD_HEAD_DIM).reshape(1, SSD_D_INNER)
            xa, h2, route, cnt = _ssd_out(cfg, y2, xbc_act, z, d_wide, ssd_norm_g[i].reshape(1, SSD_D_INNER),
                                          ssd_w_out_b, i, xa, mod_l, g2, router)
        slot_tok, block_eid, n_used, slots = _dispatch(cfg, route, cnt)
        xb_parts = [h2[p_] for p_ in jnp.split(slot_tok, MOE_GATHER_PARTS)]
        yb = _experts(cfg, layer, block_eid, n_used, xb_parts, moe_w_gate, moe_w_up, moe_w_down)
        last = layer == cfg.depth - 1
        y12 = yb[slots]
        xa = _combine(cfg, xa, y12, route, mod_l, final_norm_g.reshape(1, d), last)
        x_pair = (xa, xa)
    return xa.reshape(b, cfg.n_lat, d)


def kernel(x, c, ctx, c_ctx, ada_w, ada_b, norm1_g, norm2_g, ab_w_in, ab_w_out, diff_lq1, diff_lk1, diff_lq2, diff_lk2, diff_subln_g, bconv_w, ssd_w_in, ssd_conv_w, ssd_conv_b, ssd_A_log, ssd_dt_bias, ssd_D, ssd_norm_g, ssd_w_out, moe_wg, moe_bg, moe_we, moe_be, moe_w_gate, moe_w_up, moe_w_down, final_norm_g):
    cfg = Cfg(batch=x.shape[0], n_lat=x.shape[1], n_ctx=ctx.shape[1], d_model=x.shape[2], depth=ada_w.shape[0],
              tm=512, tq=256, moe_tile=512, attn_subtiles=2, kchunk=1024)
    return _forward(cfg, x, c, ctx, c_ctx, ada_w, ada_b, norm1_g, norm2_g, ab_w_in, ab_w_out,
                    diff_lq1, diff_lk1, diff_lq2, diff_lk2, diff_subln_g, bconv_w,
                    ssd_w_in, ssd_conv_w, ssd_conv_b, ssd_A_log, ssd_dt_bias, ssd_D, ssd_norm_g, ssd_w_out,
                    moe_wg, moe_bg, moe_we, moe_be, moe_w_gate, moe_w_up, moe_w_down, final_norm_g)
```

```python
import functools
import math
from typing import NamedTuple

import jax
import jax.numpy as jnp
from jax import lax
from jax.experimental import pallas as pl
from jax.experimental.pallas import tpu as pltpu

F32 = jnp.float32
BF16 = jnp.bfloat16
HIGHEST = lax.Precision.HIGHEST

LANES_V7X = 128
SUBLANES_V7X = 8
VMEM_LIMIT_BYTES_V7X = 56 * 1024 * 1024

RMS_EPS = 1e-6
GRID_W = 64
N_DIFF_HEADS = 4
DIFF_QK_DIM = 64
DIFF_V_DIM = 128
DIFF_WIDTH = 512
CONV_WIDTH = 512
ROPE_BASE = 10000.0
SSD_D_INNER = 2048
SSD_HEAD_DIM = 64
SSD_HEADS = 32
SSD_GROUPS = 4
SSD_STATE = 128
SSD_CHUNK = 128
SSD_CONV_CH = SSD_D_INNER + 2 * SSD_GROUPS * SSD_STATE
MOE_GROUPS = 4
MOE_EXPERTS_PER_GROUP = 8
MOE_EXPERTS = 32
MOE_HIDDEN = 512
ROUTE_LANES = LANES_V7X
MOE_GATHER_PARTS = 3


class Cfg(NamedTuple):
    batch: int
    n_lat: int
    n_ctx: int
    d_model: int
    depth: int
    tm: int
    tq: int
    moe_tile: int
    attn_subtiles: int
    kchunk: int

    @property
    def t_lat(self):
        return self.batch * self.n_lat

    @property
    def t_ctx(self):
        return self.batch * self.n_ctx

    @property
    def t_all(self):
        return self.t_lat + self.t_ctx


def _cparams(sem):
    return pltpu.CompilerParams(dimension_semantics=sem, vmem_limit_bytes=VMEM_LIMIT_BYTES_V7X)


def _silu(v):
    return v * (1.0 / (1.0 + jnp.exp(-v)))


def _const_spec(shape):
    nd = len(shape)
    return pl.BlockSpec(shape, lambda *_: (0,) * nd)


def _layer_spec(stacked, index):
    rest = stacked.shape[1:]
    return pl.BlockSpec((None,) + rest, lambda *_: (index,) + (0,) * len(rest))


def _mod_spec(cfg):
    return pl.BlockSpec((1, 6, cfg.d_model),
                        lambda i: (jnp.minimum((i * cfg.tm) // cfg.n_lat, cfg.batch), 0, 0))


def _seq_pos(cfg, tm):
    r0 = pl.program_id(0) * tm
    row = r0 + lax.broadcasted_iota(jnp.int32, (tm, 1), 0)
    seqlen = jnp.where(r0 >= cfg.t_lat, cfg.n_ctx, cfg.n_lat)
    return row & (seqlen - 1), seqlen


def _norm_mod(x, g, shift, scale):
    ms = jnp.mean(x * x, axis=-1, keepdims=True)
    return (x * lax.rsqrt(ms + RMS_EPS) * g) * (1.0 + scale) + shift


def _cast_kernel(w_ref, o_ref):
    n, n_out = w_ref.shape[1], o_ref.shape[1]
    full = n // LANES_V7X * LANES_V7X
    o_ref[:, :full] = w_ref[:, :full].astype(BF16)
    if n_out > full:
        o_ref[:, full:] = jnp.pad(w_ref[:, full:], ((0, 0), (0, n_out - n))).astype(BF16)


def _to_bf16(w):
    layers, rows, n = w.shape
    n_out = -(-n // LANES_V7X) * LANES_V7X
    rb = 256
    return pl.pallas_call(
        _cast_kernel,
        grid=(layers, rows // rb),
        in_specs=[pl.BlockSpec((None, rb, n), lambda l, i: (l, i, 0))],
        out_specs=pl.BlockSpec((None, rb, n_out), lambda l, i: (l, i, 0)),
        out_shape=jax.ShapeDtypeStruct((layers, rows, n_out), BF16),
        compiler_params=_cparams(("arbitrary", "arbitrary")),
        name="cast_bf16",
    )(w)


def _hi_lo(v):
    hi = v.astype(BF16)
    return hi, (v - hi.astype(F32)).astype(BF16)


def _adaln_kernel(c_ref, w_ref, b_ref, o_ref):
    s_hi, s_lo = _hi_lo(_silu(c_ref[...]))
    w_hi, w_lo = _hi_lo(w_ref[0])
    o_ref[0] = (jnp.dot(s_hi, w_hi, preferred_element_type=F32) + jnp.dot(s_hi, w_lo, preferred_element_type=F32)
                + jnp.dot(s_lo, w_hi, preferred_element_type=F32)) + b_ref[0]


def _adaln(cfg, c_all, ada_w, ada_b):
    d = cfg.d_model
    out = pl.pallas_call(
        _adaln_kernel,
        grid=(cfg.depth, 6),
        in_specs=[_const_spec((SUBLANES_V7X, d)),
                  pl.BlockSpec((1, d, d), lambda l, j: (l, 0, j)),
                  pl.BlockSpec((1, 1, d), lambda l, j: (l, 0, j))],
        out_specs=pl.BlockSpec((1, SUBLANES_V7X, d), lambda l, j: (l, 0, j)),
        out_shape=jax.ShapeDtypeStruct((cfg.depth, SUBLANES_V7X, 6 * d), F32),
        compiler_params=_cparams(("arbitrary", "arbitrary")),
        name="adaln",
    )(c_all, ada_w, ada_b.reshape(cfg.depth, 1, 6 * d))
    return out.reshape(cfg.depth, SUBLANES_V7X, 6, d)


def _rope_tables(cfg):
    n = cfg.n_lat
    rows = n // GRID_W
    row = jnp.broadcast_to(jnp.arange(rows, dtype=F32)[:, None], (rows, GRID_W)).reshape(n)
    col = jnp.broadcast_to(jnp.arange(GRID_W, dtype=F32)[None, :], (rows, GRID_W)).reshape(n)
    axis_dim = DIFF_QK_DIM // 2
    inv_freq = ROPE_BASE ** (-jnp.arange(0, axis_dim, 2, dtype=F32) / axis_dim)
    ang_r = row[:, None] * inv_freq
    ang_c = col[:, None] * inv_freq
    zeros = jnp.zeros_like(ang_r)
    cos64 = jnp.concatenate([jnp.cos(ang_r), jnp.cos(ang_r), jnp.cos(ang_c), jnp.cos(ang_c)], axis=1)
    sa64 = jnp.concatenate([zeros, jnp.sin(ang_r), zeros, jnp.sin(ang_c)], axis=1)
    sb64 = jnp.concatenate([-jnp.sin(ang_r), zeros, -jnp.sin(ang_c), zeros], axis=1)
    ident = cfg.tm
    cos = jnp.concatenate([jnp.tile(cos64, (1, 2)), jnp.ones((ident, LANES_V7X), F32)], axis=0)
    sa = jnp.concatenate([jnp.tile(sa64, (1, 2)), jnp.zeros((ident, LANES_V7X), F32)], axis=0)
    sb = jnp.concatenate([jnp.tile(sb64, (1, 2)), jnp.zeros((ident, LANES_V7X), F32)], axis=0)
    return cos, sa, sb


def _dual_row_specs(cfg, width, pair):
    lat_arr, ctx_arr = pair
    nl = cfg.t_lat // cfg.tm
    c0 = nl if ctx_arr is lat_arr else 0
    return (pl.BlockSpec((cfg.tm, width), lambda i: (jnp.minimum(i, nl - 1), 0)),
            pl.BlockSpec((cfg.tm, width), lambda i: (jnp.maximum(i - nl, 0) + c0, 0)))


def _dual_rows(cfg, lat_ref, ctx_ref):
    return jnp.where(pl.program_id(0) >= cfg.t_lat // cfg.tm, ctx_ref[...], lat_ref[...])


def _ab_in_kernel(xl_ref, xc_ref, mod_ref, g_ref, w_ref, cos_ref, sa_ref, sb_ref,
                  q_ref, k_ref, v_ref, gb_ref, u_ref, *, cfg):
    m = mod_ref[0]
    hb = _norm_mod(_dual_rows(cfg, xl_ref, xc_ref), g_ref[...], m[0:1], m[1:2]).astype(BF16)
    cos, sa, sb = cos_ref[...], sa_ref[...], sb_ref[...]

    def proj(j):
        return jnp.dot(hb, w_ref[:, j * DIFF_WIDTH:(j + 1) * DIFF_WIDTH], preferred_element_type=F32)

    def rope(t, out_ref, scale):
        for gi in range(DIFF_WIDTH // LANES_V7X):
            xg = t[:, gi * LANES_V7X:(gi + 1) * LANES_V7X]
            r = xg * cos + pltpu.roll(xg, 16, 1) * sa + pltpu.roll(xg, LANES_V7X - 16, 1) * sb
            out_ref[:, gi * LANES_V7X:(gi + 1) * LANES_V7X] = (r * scale).astype(out_ref.dtype)

    rope(proj(0), q_ref, math.log2(math.e) * DIFF_QK_DIM ** -0.5)
    rope(proj(1), k_ref, 1.0)
    v_ref[...] = proj(2).astype(v_ref.dtype)
    gb_ref[...] = proj(3)
    u_ref[...] = proj(4) * proj(5)


def _ab_in(cfg, x_pair, mod_l, g1, w_stack, w_index, tables):
    t, d, tm = cfg.t_all, cfg.d_model, cfg.tm
    n_lat_tiles, per_seq = cfg.t_lat // tm, cfg.n_lat // tm
    tab_spec = pl.BlockSpec((tm, LANES_V7X), lambda i: (jnp.where(i < n_lat_tiles, i % per_seq, per_seq), 0))
    row = lambda w: pl.BlockSpec((tm, w), lambda i: (i, 0))
    return pl.pallas_call(
        functools.partial(_ab_in_kernel, cfg=cfg),
        grid=(t // tm,),
        in_specs=[*_dual_row_specs(cfg, d, x_pair), _mod_spec(cfg), _const_spec((1, d)),
                  _layer_spec(w_stack, w_index), tab_spec, tab_spec, tab_spec],
        out_specs=[row(DIFF_WIDTH)] * 5,
        out_shape=[jax.ShapeDtypeStruct((t, DIFF_WIDTH), BF16)] * 3
        + [jax.ShapeDtypeStruct((t, CONV_WIDTH), F32)] * 2,
        compiler_params=_cparams(("arbitrary",)),
        name="ab_in",
    )(*x_pair, mod_l, g1, w_stack, *tables)


def _attn_body(lvec_ref, q_ref, kc_ref, kl_ref, vc_ref, vl_ref, g_ref, o_ref, *, tq, lam_init, kchunk):
    lv = lvec_ref[...]
    lam = (jnp.exp(jnp.sum(lv[0:1] * lv[1:2], axis=-1, keepdims=True))
           - jnp.exp(jnp.sum(lv[2:3] * lv[3:4], axis=-1, keepdims=True)) + lam_init)
    nt = (((1,), (1,)), ((), ()))
    half = DIFF_QK_DIM
    chunks = [(kc_ref, vc_ref, 0, kc_ref.shape[0])]
    if kl_ref is not None:
        chunks += [(kl_ref, vl_ref, r0, kchunk) for r0 in range(0, kl_ref.shape[0], kchunk)]
    for j in range(q_ref.shape[0] // tq):
        q = q_ref[j * tq:(j + 1) * tq, :].astype(F32)
        lane = lax.broadcasted_iota(jnp.int32, q.shape, 1)
        q2 = jnp.concatenate([jnp.where(lane < half, q, 0.0), jnp.where(lane >= half, q, 0.0)],
                             axis=0).astype(BF16)
        ps, ms, ls = [], [], []
        for k_ref, _, r0, n in chunks:
            s = lax.dot_general(q2, k_ref[r0:r0 + n, :], nt, preferred_element_type=F32)
            m_c = jnp.max(s, axis=-1, keepdims=True)
            p = jnp.exp2(s - m_c)
            ps.append(p)
            ms.append(m_c)
            ls.append(jnp.sum(p, axis=-1, keepdims=True))
        mx = functools.reduce(jnp.maximum, ms)
        scale = [jnp.exp2(m_c - mx) for m_c in ms]
        r = 1.0 / functools.reduce(jnp.add, [sc * l_c for sc, l_c in zip(scale, ls)])
        pv = None
        for p, sc, (_, v_ref, r0, n) in zip(ps, scale, chunks):
            w = sc * r
            pd = (p[:tq] * w[:tq] - p[tq:] * (lam * w[tq:])).astype(BF16)
            part = jnp.dot(pd, v_ref[r0:r0 + n, :], preferred_element_type=F32)
            pv = part if pv is None else pv + part
        o = pv * lax.rsqrt(jnp.mean(pv * pv, axis=-1, keepdims=True) + RMS_EPS)
        o_ref[j * tq:(j + 1) * tq, :] = ((o * g_ref[...]) * (1.0 - lam_init)).astype(o_ref.dtype)


def _attn_lat_kernel(lvec_ref, q_ref, kc_ref, kl_ref, vc_ref, vl_ref, g_ref, o_ref, **kw):
    _attn_body(lvec_ref, q_ref, kc_ref, kl_ref, vc_ref, vl_ref, g_ref, o_ref, **kw)


def _attn_ctx_kernel(lvec_ref, q_ref, kc_ref, vc_ref, g_ref, o_ref, **kw):
    _attn_body(lvec_ref, q_ref, kc_ref, None, vc_ref, None, g_ref, o_ref, **kw)


def _attention(cfg, q, k, v, lvec, subln_g, lam_init):
    tq, nsub = cfg.tq, cfg.attn_subtiles
    tstep = tq * nsub
    nqb = cfg.n_lat // tstep
    ctx_blk0 = cfg.t_lat // cfg.n_ctx
    hw = DIFF_V_DIM
    ctx_spec = pl.BlockSpec((cfg.n_ctx, hw), lambda b, h, *_: (ctx_blk0 + b, h))
    lat_spec = pl.BlockSpec((cfg.n_lat, hw), lambda b, h, *_: (b, h))
    q_spec = pl.BlockSpec((tstep, hw), lambda b, h, i: (b * nqb + i, h))
    kw = dict(tq=tq, lam_init=lam_init, kchunk=min(cfg.kchunk, cfg.n_lat))
    o_lat = pl.pallas_call(
        functools.partial(_attn_lat_kernel, **kw),
        grid=(cfg.batch, N_DIFF_HEADS, nqb),
        in_specs=[_const_spec(lvec.shape), q_spec, ctx_spec, lat_spec, ctx_spec, lat_spec, _const_spec((1, hw))],
        out_specs=q_spec,
        out_shape=jax.ShapeDtypeStruct((cfg.t_lat, DIFF_WIDTH), BF16),
        compiler_params=_cparams(("arbitrary", "arbitrary", "arbitrary")),
        name="diff_attn",
    )(lvec, q, k, k, v, v, subln_g)
    tqc = min(tq, cfg.n_ctx)
    o_ctx = pl.pallas_call(
        functools.partial(_attn_ctx_kernel, tq=tqc, lam_init=lam_init, kchunk=cfg.n_ctx),
        grid=(cfg.batch, N_DIFF_HEADS),
        in_specs=[_const_spec(lvec.shape), ctx_spec, ctx_spec, ctx_spec, _const_spec((1, hw))],
        out_specs=pl.BlockSpec((cfg.n_ctx, hw), lambda b, h: (b, h)),
        out_shape=jax.ShapeDtypeStruct((cfg.t_ctx, DIFF_WIDTH), BF16),
        compiler_params=_cparams(("arbitrary", "arbitrary")),
        name="diff_attn_ctx",
    )(lvec, q, k, v, subln_g)
    return o_lat, o_ctx


def _route_from_logits(lg):
    lane = lax.broadcasted_iota(jnp.int32, lg.shape, 1)
    neg = -jnp.inf
    big = jnp.int32(ROUTE_LANES)
    gl = jnp.where(lane < MOE_GROUPS, lg, neg)
    gmax = jnp.max(gl, axis=-1, keepdims=True)
    gidx = jnp.min(jnp.where(gl == gmax, lane, big), axis=-1, keepdims=True)
    gw = 1.0 / jnp.sum(jnp.exp(gl - gmax), axis=-1, keepdims=True)
    lo = MOE_GROUPS + gidx * MOE_EXPERTS_PER_GROUP
    el = jnp.where((lane >= lo) & (lane < lo + MOE_EXPERTS_PER_GROUP), lg, neg)
    m1 = jnp.max(el, axis=-1, keepdims=True)
    i1 = jnp.min(jnp.where(el == m1, lane, big), axis=-1, keepdims=True)
    el2 = jnp.where(lane == i1, neg, el)
    m2 = jnp.max(el2, axis=-1, keepdims=True)
    i2 = jnp.min(jnp.where(el2 == m2, lane, big), axis=-1, keepdims=True)
    w1 = gw / (1.0 + jnp.exp(m2 - m1))
    w2 = gw - w1
    e1 = (i1 - MOE_GROUPS).astype(F32)
    e2 = (i2 - MOE_GROUPS).astype(F32)
    return jnp.where(lane == 0, e1, jnp.where(lane == 1, e2, jnp.where(lane == 2, w1, jnp.where(lane == 3, w2, 0.0))))


def _pack_bf16_pairs(v):
    w = v.shape[1] // 2
    lo = pltpu.bitcast(v[:, :w].astype(BF16).astype(F32), jnp.uint32)
    hi = pltpu.bitcast(v[:, w:].astype(BF16).astype(F32), jnp.uint32)
    return pltpu.bitcast((hi & jnp.uint32(0xFFFF0000)) | (lo >> 16), F32)


def _unpack_bf16_pairs(words):
    u = pltpu.bitcast(words, jnp.uint32)
    lo = pltpu.bitcast(u << 16, F32)
    hi = pltpu.bitcast(u & jnp.uint32(0xFFFF0000), F32)
    return lo, hi


def _router_weights(wg, bg, we, be):
    d = wg.shape[0]
    pad = ROUTE_LANES - MOE_GROUPS - MOE_EXPERTS
    wr = jnp.concatenate([wg, we, jnp.zeros((d, pad), F32)], axis=1)
    hi, lo = _hi_lo(wr)
    br = jnp.concatenate([bg, be, jnp.zeros((pad,), F32)]).reshape(1, ROUTE_LANES)
    return jnp.concatenate([hi, lo], axis=1), hi, br


def _expert_onehots(rt):
    lane = lax.broadcasted_iota(jnp.int32, rt.shape, 1)
    return (jnp.where(lane == rt[:, 0:1].astype(jnp.int32), 1.0, 0.0),
            jnp.where(lane == rt[:, 1:2].astype(jnp.int32), 1.0, 0.0))


def _residual_norm_route(x, mix, m, g2, wr2_ref, wrh_ref, br_ref, xo_ref, h2_ref, rt_ref, cnt_ref):
    xn = x + m[2:3] * mix
    xo_ref[...] = xn
    h2 = _norm_mod(xn, g2, m[3:4], m[4:5])
    h2_ref[...] = _pack_bf16_pairs(h2)
    hh, hl = _hi_lo(h2)
    a = jnp.dot(hh, wr2_ref[...], preferred_element_type=F32)
    lg = (a[:, :ROUTE_LANES] + a[:, ROUTE_LANES:]) + jnp.dot(hl, wrh_ref[...], preferred_element_type=F32)
    rt = _route_from_logits(lg + br_ref[...])
    rt_ref[...] = rt

    @pl.when(pl.program_id(0) == 0)
    def _():
        cnt_ref[...] = jnp.zeros_like(cnt_ref)

    o1, o2 = _expert_onehots(rt)
    cnt_ref[...] += jnp.sum(o1 + o2, axis=0, keepdims=True)


def _epilogue_specs(cfg, x_pair):
    d, tm = cfg.d_model, cfg.tm
    row = lambda w: pl.BlockSpec((tm, w), lambda i: (i, 0))
    in_specs = [*_dual_row_specs(cfg, d, x_pair), _mod_spec(cfg), _const_spec((1, d)),
                _const_spec((d, 2 * ROUTE_LANES)),
                _const_spec((d, ROUTE_LANES)), _const_spec((1, ROUTE_LANES))]
    out_specs = [row(d), row(d // 2), row(ROUTE_LANES), _const_spec((SUBLANES_V7X, ROUTE_LANES))]
    out_shape = [jax.ShapeDtypeStruct((cfg.t_all, d), F32), jax.ShapeDtypeStruct((cfg.t_all, d // 2), F32),
                 jax.ShapeDtypeStruct((cfg.t_all, ROUTE_LANES), F32),
                 jax.ShapeDtypeStruct((SUBLANES_V7X, ROUTE_LANES), F32)]
    return in_specs, out_specs, out_shape


def _halo_specs(cfg, width, col_block=0):
    per = cfg.tm // SUBLANES_V7X
    last = cfg.t_all // SUBLANES_V7X - 1
    prev = pl.BlockSpec((SUBLANES_V7X, width), lambda i, *_: (jnp.maximum(i * per - 1, 0), col_block))
    nxt = pl.BlockSpec((SUBLANES_V7X, width), lambda i, *_: (jnp.minimum((i + 1) * per, last), col_block))
    return prev, nxt


def _shifted_rows(pad_ref, u, prev_blk, next_blk, pos, seqlen):
    tm = u.shape[0]
    s = SUBLANES_V7X
    pad_ref[s:s + tm, :] = u
    pad_ref[0:s, :] = prev_blk
    pad_ref[s + tm:2 * s + tm, :] = next_blk
    um1 = jnp.where(pos == 0, 0.0, pad_ref[s - 1:s - 1 + tm, :])
    up1 = jnp.where(pos == seqlen - 1, 0.0, pad_ref[s + 1:s + 1 + tm, :])
    return um1, up1


def _ab_out_kernel(ol_ref, oc_ref, gb_ref, u_ref, up_ref, un_ref, cw_ref, wo_ref,
                   xl_ref, xc_ref, mod_ref, g2_ref, wr2_ref, wrh_ref, br_ref,
                   xo_ref, h2_ref, rt_ref, cnt_ref, pad_ref, *, cfg):
    pos, seqlen = _seq_pos(cfg, cfg.tm)
    u = u_ref[...]
    um1, up1 = _shifted_rows(pad_ref, u, up_ref[...], un_ref[...], pos, seqlen)
    cw = cw_ref[...]
    conv = gb_ref[...] * (um1 * cw[0:1] + u * cw[1:2] + up1 * cw[2:3])
    mix = (jnp.dot(_dual_rows(cfg, ol_ref, oc_ref), wo_ref[:DIFF_WIDTH, :], preferred_element_type=F32)
           + jnp.dot(conv.astype(BF16), wo_ref[DIFF_WIDTH:, :], preferred_element_type=F32))
    _residual_norm_route(_dual_rows(cfg, xl_ref, xc_ref), mix, mod_ref[0], g2_ref[...], wr2_ref, wrh_ref, br_ref,
                         xo_ref, h2_ref, rt_ref, cnt_ref)


def _ab_out(cfg, o_pair, gb, u, conv_w, wo_stack, w_index, x_pair, mod_l, g2, router):
    tm = cfg.tm
    row = lambda w: pl.BlockSpec((tm, w), lambda i: (i, 0))
    prev, nxt = _halo_specs(cfg, CONV_WIDTH)
    ep_in, ep_out, ep_shape = _epilogue_specs(cfg, x_pair)
    return pl.pallas_call(
        functools.partial(_ab_out_kernel, cfg=cfg),
        grid=(cfg.t_all // tm,),
        in_specs=[*_dual_row_specs(cfg, DIFF_WIDTH, o_pair), row(CONV_WIDTH), row(CONV_WIDTH), prev, nxt,
                  _const_spec(conv_w.shape),
                  _layer_spec(wo_stack, w_index)] + ep_in,
        out_specs=ep_out,
        out_shape=ep_shape,
        scratch_shapes=[pltpu.VMEM((tm + 2 * SUBLANES_V7X, CONV_WIDTH), F32)],
        compiler_params=_cparams(("arbitrary",)),
        name="ab_out",
    )(*o_pair, gb, u, u, u, conv_w, wo_stack, *x_pair, mod_l, g2, *router)


def _ssd_in_kernel(x_ref, xp_ref, xn_ref, mod_ref, g_ref, w_ref, cw_ref, cb_ref,
                   z_ref, xbc_ref, dt_ref, pad_ref, *, cfg):
    tm = cfg.tm
    s = SUBLANES_V7X
    m = mod_ref[0]
    g = g_ref[...]
    h = _norm_mod(x_ref[...], g, m[0:1], m[1:2])
    hb = h.astype(BF16)
    h_ext = jnp.concatenate([_norm_mod(xp_ref[...], g, m[0:1], m[1:2]), h,
                             _norm_mod(xn_ref[...], g, m[0:1], m[1:2])], axis=0).astype(BF16)
    pos, seqlen = _seq_pos(cfg, tm)
    cw = 512
    for j in range(SSD_D_INNER // cw):
        z_ref[:, j * cw:(j + 1) * cw] = jnp.dot(hb, w_ref[:, j * cw:(j + 1) * cw],
                                                preferred_element_type=F32).astype(z_ref.dtype)
    for j in range(SSD_CONV_CH // cw):
        c0 = SSD_D_INNER + j * cw
        pad = pad_ref.at[j]
        pad[...] = jnp.dot(h_ext, w_ref[:, c0:c0 + cw], preferred_element_type=F32)
        taps = cw_ref[:, j * cw:(j + 1) * cw]
        um1 = jnp.where(pos == 0, 0.0, pad[s - 1:s - 1 + tm, :])
        up1 = jnp.where(pos == seqlen - 1, 0.0, pad[s + 1:s + 1 + tm, :])
        conv = um1 * taps[0:1] + pad[s:s + tm, :] * taps[1:2] + up1 * taps[2:3] + cb_ref[:, j * cw:(j + 1) * cw]
        xbc_ref[:, j * cw:(j + 1) * cw] = _silu(conv).astype(xbc_ref.dtype)
    dt_ref[...] = jnp.dot(hb, w_ref[:, SSD_D_INNER + SSD_CONV_CH:], preferred_element_type=F32)


def _ssd_in(cfg, x, mod_l, g1, w_stack, w_index, conv_w, conv_b):
    t, d, tm = cfg.t_all, cfg.d_model, cfg.tm
    row = lambda w: pl.BlockSpec((tm, w), lambda i: (i, 0))
    prev, nxt = _halo_specs(cfg, d)
    return pl.pallas_call(
        functools.partial(_ssd_in_kernel, cfg=cfg),
        grid=(t // tm,),
        in_specs=[row(d), prev, nxt, _mod_spec(cfg), _const_spec((1, d)), _layer_spec(w_stack, w_index),
                  _const_spec(conv_w.shape), _const_spec(conv_b.shape)],
        out_specs=[row(SSD_D_INNER), row(SSD_CONV_CH), row(LANES_V7X)],
        out_shape=[jax.ShapeDtypeStruct((t, SSD_D_INNER), BF16), jax.ShapeDtypeStruct((t, SSD_CONV_CH), BF16),
                   jax.ShapeDtypeStruct((t, LANES_V7X), F32)],
        scratch_shapes=[pltpu.VMEM((SSD_CONV_CH // 512, tm + 2 * SUBLANES_V7X, 512), F32)],
        compiler_params=_cparams(("arbitrary",)),
        name="ssd_in",
    )(x, x, x, mod_l, g1, w_stack, conv_w, conv_b)


def _split3(v):
    b1 = v.astype(BF16)
    r1 = v - b1.astype(F32)
    b2 = r1.astype(BF16)
    b3 = (r1 - b2.astype(F32)).astype(BF16)
    return jnp.concatenate([b1, b2, b3], axis=1)


def _softplus(v):
    return jnp.maximum(v, 0.0) + jnp.log(1.0 + jnp.exp(-jnp.abs(v)))


def _ssd_scan_kernel(*refs):
    n_in = 5
    fwd_in, bwd_in = refs[:n_in], refs[n_in:2 * n_in]
    bias_ref, biast_ref, alog_ref, alogt_ref, ewide_ref, yf_ref, yb_ref, sf_ref, sb_ref = refs[2 * n_in:]

    @pl.when(pl.program_id(1) == 0)
    def _():
        sf_ref[...] = jnp.zeros_like(sf_ref)
        sb_ref[...] = jnp.zeros_like(sb_ref)

    for d, (ins, y_ref, state_ref) in enumerate(((fwd_in, yf_ref, sf_ref), (bwd_in, yb_ref, sb_ref))):
        _ssd_chunk(d == 0, *ins, bias_ref[d], biast_ref[d], alog_ref[d], alogt_ref[d], ewide_ref, y_ref, state_ref)


def _ssd_chunk(fwd, xs_ref, b_ref, c_ref, dt_ref, dtt_ref, bias, biast, alog, alogt, ewide_ref, y_ref, state_ref):
    q = SSD_CHUNK
    dt = _softplus(dt_ref[0] + bias)
    dtt = _softplus(dtt_ref[0] + biast)
    a = dt * (-jnp.exp(alog))
    at = dtt * (-jnp.exp(alogt))

    ri = lax.broadcasted_iota(jnp.int32, (q, q), 0)
    ci = lax.broadcasted_iota(jnp.int32, (q, q), 1)
    keep = (ci <= ri) if fwd else (ci >= ri)
    tri = jnp.where(keep, 1.0, 0.0)
    cs = jnp.dot(tri, a, precision=HIGHEST, preferred_element_type=F32)
    cst = lax.dot_general(at, tri, (((1,), (1,)), ((), ())), precision=HIGHEST,
                          preferred_element_type=F32)
    tot = jnp.sum(a, axis=0, keepdims=True)

    ewide = ewide_ref[...]
    w_wide = jnp.dot(_split3(dt * jnp.exp(tot - cs)), ewide, preferred_element_type=F32)
    ecs_wide = jnp.dot(_split3(jnp.exp(cs)), ewide, preferred_element_type=F32)
    dec_wide = jnp.dot(_split3(jnp.broadcast_to(jnp.exp(tot), (SUBLANES_V7X, SSD_HEADS))), ewide,
                       preferred_element_type=F32)[0:1]

    lane = lax.broadcasted_iota(jnp.int32, (q, LANES_V7X), 1)
    lo = lane < SSD_HEAD_DIM
    hpg = SSD_HEADS // SSD_GROUPS
    gw = hpg * SSD_HEAD_DIM
    nt = (((1,), (1,)), ((), ()))
    tn = (((0,), (0,)), ((), ()))
    for g in range(SSD_GROUPS):
        bm = b_ref[:, g * SSD_STATE:(g + 1) * SSD_STATE]
        cm = c_ref[:, g * SSD_STATE:(g + 1) * SSD_STATE]
        cb = lax.dot_general(cm, bm, nt, preferred_element_type=F32)
        sl = slice(g * gw, (g + 1) * gw)
        y_off = jnp.dot(cm, state_ref[:, sl].astype(BF16), preferred_element_type=F32) * ecs_wide[:, sl]
        for pr in range(hpg // 2):
            lhs, rhs = [], []
            c0 = g * gw + pr * LANES_V7X
            xs_pair = xs_ref[:, c0:c0 + LANES_V7X]
            zero = jnp.zeros_like(xs_pair)
            for k in range(2):
                h = g * hpg + pr * 2 + k
                seg = cs[:, h:h + 1] - cst[h:h + 1, :]
                lmat = jnp.exp(jnp.where(keep, seg, -jnp.inf))
                lhs.append((cb * lmat * dtt[h:h + 1, :]).astype(BF16))
                rhs.append(jnp.where(lo if k == 0 else jnp.logical_not(lo), xs_pair, zero))
            y = jnp.dot(jnp.concatenate(lhs, axis=1), jnp.concatenate(rhs, axis=0), preferred_element_type=F32)
            y_ref[:, c0:c0 + LANES_V7X] = (y + y_off[:, pr * LANES_V7X:(pr + 1) * LANES_V7X]).astype(y_ref.dtype)
        x2 = (xs_ref[:, sl].astype(F32) * w_wide[:, sl]).astype(BF16)
        upd = lax.dot_general(bm, x2, tn, preferred_element_type=F32)
        state_ref[:, sl] = state_ref[:, sl] * dec_wide[:, sl] + upd


def _ssd_scan(cfg, xbc_act, dt_raw, dt_bias, a_log):
    q = SSD_CHUNK
    n_cc, n_lc = cfg.n_ctx // q, cfg.n_lat // q
    ctx0 = cfg.t_lat // q
    h = SSD_HEADS
    dt2 = dt_raw[:, :2 * h].reshape(cfg.t_all, 2, h).transpose(1, 0, 2)
    dt2t = dt2.transpose(0, 2, 1)
    bias = dt_bias.reshape(2, 1, h)
    biast = dt_bias.reshape(2, h, 1)
    alog = a_log.reshape(2, 1, h)
    alogt = a_log.reshape(2, h, 1)
    head_of_row = jnp.tile(jnp.arange(h), 3)[:, None]
    ewide = (head_of_row == (jnp.arange(SSD_D_INNER) // SSD_HEAD_DIM)[None, :]).astype(BF16)

    def blk(d):
        def index(b, s):
            cs = s if d == 0 else n_cc - 1 - s
            ls = s - n_cc if d == 0 else n_lc - 1 - (s - n_cc)
            return jnp.where(s < n_cc, ctx0 + b * n_cc + cs, b * n_lc + ls)
        return index

    gs = SSD_GROUPS * SSD_STATE

    def dir_specs(d):
        at = blk(d)
        col = lambda w, cblk: pl.BlockSpec((q, w), lambda b, s: (at(b, s), cblk))
        return [col(SSD_D_INNER, 0), col(gs, SSD_D_INNER // gs), col(gs, SSD_D_INNER // gs + 1),
                pl.BlockSpec((1, q, h), lambda b, s: (d, at(b, s), 0)),
                pl.BlockSpec((1, h, q), lambda b, s: (d, 0, at(b, s)))]

    dir_args = (xbc_act, xbc_act, xbc_act, dt2, dt2t)
    y_spec = lambda d: pl.BlockSpec((q, SSD_D_INNER), lambda b, s: (blk(d)(b, s), 0))
    return pl.pallas_call(
        _ssd_scan_kernel,
        grid=(cfg.batch, n_cc + n_lc),
        in_specs=dir_specs(0) + dir_specs(1) + [_const_spec(bias.shape), _const_spec(biast.shape),
                                                _const_spec(alog.shape), _const_spec(alogt.shape),
                                                _const_spec(ewide.shape)],
        out_specs=[y_spec(0), y_spec(1)],
        out_shape=[jax.ShapeDtypeStruct((cfg.t_all, SSD_D_INNER), BF16)] * 2,
        scratch_shapes=[pltpu.VMEM((SSD_STATE, SSD_D_INNER), F32)] * 2,
        compiler_params=_cparams(("arbitrary", "arbitrary")),
        name="ssd_scan",
    )(*dir_args, *dir_args, bias, biast, alog, alogt, ewide)


def _ssd_out_kernel(yf_ref, yb_ref, xs_ref, z_ref, dw_ref, ng_ref, wo_ref,
                    xl_ref, xc_ref, mod_ref, g2_ref, wr2_ref, wrh_ref, br_ref, xo_ref, h2_ref, rt_ref, cnt_ref,
                    *, cfg):
    gw = SSD_D_INNER // SSD_GROUPS
    mix = None
    for g in range(SSD_GROUPS):
        sl = slice(g * gw, (g + 1) * gw)
        y = yf_ref[:, sl].astype(F32) + yb_ref[:, sl].astype(F32) + xs_ref[:, sl].astype(F32) * dw_ref[:, sl]
        y = y * _silu(z_ref[:, sl].astype(F32))
        y = (y * lax.rsqrt(jnp.mean(y * y, axis=-1, keepdims=True) + RMS_EPS)) * ng_ref[:, sl]
        part = jnp.dot(y.astype(BF16), wo_ref[sl, :], preferred_element_type=F32)
        mix = part if mix is None else mix + part
    _residual_norm_route(_dual_rows(cfg, xl_ref, xc_ref), mix, mod_ref[0], g2_ref[...], wr2_ref, wrh_ref, br_ref,
                         xo_ref, h2_ref, rt_ref, cnt_ref)


def _ssd_out(cfg, y2, xbc_act, z, d_wide, norm_g, wo_stack, w_index, x, mod_l, g2, router):
    tm = cfg.tm
    row = lambda w: pl.BlockSpec((tm, w), lambda i: (i, 0))
    ep_in, ep_out, ep_shape = _epilogue_specs(cfg, (x, x))
    return pl.pallas_call(
        functools.partial(_ssd_out_kernel, cfg=cfg),
        grid=(cfg.t_all // tm,),
        in_specs=[row(SSD_D_INNER)] * 4 + [_const_spec((1, SSD_D_INNER)),
                  _const_spec((1, SSD_D_INNER)), _layer_spec(wo_stack, w_index)] + ep_in,
        out_specs=ep_out,
        out_shape=ep_shape,
        compiler_params=_cparams(("arbitrary",)),
        name="ssd_out",
    )(*y2, xbc_act, z, d_wide, norm_g, wo_stack, x, x, mod_l, g2, *router)


def _expert_kernel(eid_ref, nused_ref, *refs, blocks_per_part):
    xb_refs = refs[:MOE_GATHER_PARTS]
    wg_ref, wu_ref, wd_ref, y_ref, wg_s, wu_s, wd_s = refs[MOE_GATHER_PARTS:]
    i = pl.program_id(0)
    changed = jnp.logical_or(i == 0, eid_ref[i] != eid_ref[jnp.maximum(i - 1, 0)])

    @pl.when(jnp.logical_and(changed, i < nused_ref[0]))
    def _():
        wg_s[...] = wg_ref[0, 0].astype(BF16)
        wu_s[...] = wu_ref[0, 0].astype(BF16)
        wd_s[...] = wd_ref[0, 0].astype(BF16)

    @pl.when(i < nused_ref[0])
    def _():
        half = wg_s.shape[0] // 2
        words = xb_refs[0][...]
        for k in range(1, MOE_GATHER_PARTS):
            words = jnp.where(i >= k * blocks_per_part, xb_refs[k][...], words)
        lo, hi = _unpack_bf16_pairs(words)
        lo, hi = lo.astype(BF16), hi.astype(BF16)
        hg = (jnp.dot(lo, wg_s[:half, :], preferred_element_type=F32)
              + jnp.dot(hi, wg_s[half:, :], preferred_element_type=F32))
        hu = (jnp.dot(lo, wu_s[:half, :], preferred_element_type=F32)
              + jnp.dot(hi, wu_s[half:, :], preferred_element_type=F32))
        y = jnp.dot((_silu(hg) * hu).astype(BF16), wd_s[...], preferred_element_type=F32)
        y_ref[...] = _pack_bf16_pairs(y)

    @pl.when(i >= nused_ref[0])
    def _():
        y_ref[...] = jnp.zeros_like(y_ref)


def _experts(cfg, layer, block_eid, n_used, xb_parts, w_gate, w_up, w_down):
    d, tile = cfg.d_model, cfg.moe_tile
    per = xb_parts[0].shape[0] // tile
    n_blocks = per * MOE_GATHER_PARTS

    def part_spec(k):
        return pl.BlockSpec((tile, d // 2), lambda i, e, n: (jnp.clip(i - k * per, 0, per - 1), 0))

    grid_spec = pltpu.PrefetchScalarGridSpec(
        num_scalar_prefetch=2,
        grid=(n_blocks,),
        in_specs=[part_spec(k) for k in range(MOE_GATHER_PARTS)] + [
                  pl.BlockSpec((1, 1, d, MOE_HIDDEN), lambda i, e, n: (layer, e[i], 0, 0)),
                  pl.BlockSpec((1, 1, d, MOE_HIDDEN), lambda i, e, n: (layer, e[i], 0, 0)),
                  pl.BlockSpec((1, 1, MOE_HIDDEN, d), lambda i, e, n: (layer, e[i], 0, 0))],
        out_specs=pl.BlockSpec((tile, d // 2), lambda i, e, n: (i, 0)),
        scratch_shapes=[pltpu.VMEM((d, MOE_HIDDEN), BF16), pltpu.VMEM((d, MOE_HIDDEN), BF16),
                        pltpu.VMEM((MOE_HIDDEN, d), BF16)],
    )
    return pl.pallas_call(
        functools.partial(_expert_kernel, blocks_per_part=per),
        grid_spec=grid_spec,
        out_shape=jax.ShapeDtypeStruct((n_blocks * tile, d // 2), F32),
        compiler_params=_cparams(("arbitrary",)),
        name="moe_experts",
    )(block_eid, n_used, *xb_parts, w_gate, w_up, w_down)


def _dispatch_kernel(rt_ref, cnt_ref, pos_ref, carry_ref, start_ref, *, tile):
    tm = rt_ref.shape[0]
    rt = rt_ref[...]
    lane = lax.broadcasted_iota(jnp.int32, rt.shape, 1)
    o1, o2 = _expert_onehots(rt)
    cnt1 = jnp.sum(o1, axis=0, keepdims=True)
    cnt2 = jnp.sum(o2, axis=0, keepdims=True)

    @pl.when(pl.program_id(0) == 0)
    def _():
        padded = jnp.floor((cnt_ref[...] + (tile - 1)) * (1.0 / tile)) * tile
        r = lax.broadcasted_iota(jnp.int32, (ROUTE_LANES, ROUTE_LANES), 0)
        c = lax.broadcasted_iota(jnp.int32, (ROUTE_LANES, ROUTE_LANES), 1)
        excl = jnp.where(r < c, 1.0, 0.0)
        start_ref[...] = jnp.dot(padded, excl, precision=HIGHEST, preferred_element_type=F32)
        carry_ref[...] = jnp.zeros_like(carry_ref)

    ri = lax.broadcasted_iota(jnp.int32, (tm, tm), 0)
    ci = lax.broadcasted_iota(jnp.int32, (tm, tm), 1)
    earlier = jnp.where(ci < ri, 1.0, 0.0).astype(BF16)
    p1 = jnp.dot(earlier, o1.astype(BF16), preferred_element_type=F32)
    p2 = jnp.dot(earlier, o2.astype(BF16), preferred_element_type=F32)
    base = start_ref[0:1] + carry_ref[0:1]
    pos1 = jnp.sum(o1 * (base + p1), axis=-1, keepdims=True)
    pos2 = jnp.sum(o2 * (base + cnt1 + p2), axis=-1, keepdims=True)
    cols = jnp.where(lane == 0, pos1, jnp.where(lane == 1, pos2, 0.0))
    r = lax.broadcasted_iota(jnp.int32, (SUBLANES_V7X, ROUTE_LANES), 0)
    c = lax.broadcasted_iota(jnp.int32, (SUBLANES_V7X, ROUTE_LANES), 1)
    pick = jnp.where(r == c, 1.0, 0.0)
    rows = lax.dot_general(pick, cols, (((1,), (1,)), ((), ())), precision=HIGHEST, preferred_element_type=F32)
    pos_ref[...] = rows.astype(jnp.int32)
    carry_ref[...] += cnt1 + cnt2


def _dispatch(cfg, route, cnt):
    tile, tm = cfg.moe_tile, cfg.tm
    t = cfg.t_all
    a_total = 2 * t
    pos = pl.pallas_call(
        functools.partial(_dispatch_kernel, tile=tile),
        grid=(t // tm,),
        in_specs=[pl.BlockSpec((tm, ROUTE_LANES), lambda i: (i, 0)), _const_spec((SUBLANES_V7X, ROUTE_LANES))],
        out_specs=pl.BlockSpec((SUBLANES_V7X, tm), lambda i: (0, i)),
        out_shape=jax.ShapeDtypeStruct((SUBLANES_V7X, t), jnp.int32),
        scratch_shapes=[pltpu.VMEM((SUBLANES_V7X, ROUTE_LANES), F32), pltpu.VMEM((SUBLANES_V7X, ROUTE_LANES), F32)],
        compiler_params=_cparams(("arbitrary",)),
        name="moe_dispatch",
    )(route, cnt)
    counts = cnt[0, :MOE_EXPERTS].astype(jnp.int32)
    pend = jnp.cumsum((counts + tile - 1) // tile * tile)
    n_blocks = (a_total + MOE_EXPERTS * (tile - 1)) // tile
    n_blocks = -(-n_blocks // MOE_GATHER_PARTS) * MOE_GATHER_PARTS
    block_eid = jnp.minimum(jnp.sum(pend[None, :] <= (jnp.arange(n_blocks, dtype=jnp.int32) * tile)[:, None], axis=1),
                            MOE_EXPERTS - 1).astype(jnp.int32)
    n_used = (pend[-1] // tile).astype(jnp.int32).reshape(1)
    slots = pos[0:2].reshape(-1)
    slot_tok = (jnp.arange(n_blocks * tile, dtype=jnp.int32) % t).at[slots].set(
        jnp.arange(a_total, dtype=jnp.int32) % t, unique_indices=True, mode="promise_in_bounds")
    return slot_tok, block_eid, n_used, slots


def _combine_kernel(x_ref, y1_ref, y2_ref, rt_ref, mod_ref, g_ref, o_ref, *, final):
    rt = rt_ref[...]
    a_lo, a_hi = _unpack_bf16_pairs(y1_ref[...])
    b_lo, b_hi = _unpack_bf16_pairs(y2_ref[...])
    w1, w2 = rt[:, 2:3], rt[:, 3:4]
    f = jnp.concatenate([w1 * a_lo + w2 * b_lo, w1 * a_hi + w2 * b_hi], axis=1)
    xn = x_ref[...] + mod_ref[0][5:6] * f
    if final:
        xn = (xn * lax.rsqrt(jnp.mean(xn * xn, axis=-1, keepdims=True) + RMS_EPS)) * g_ref[...]
    o_ref[...] = xn


def _combine(cfg, x, y12, route, mod_l, g, final):
    d, tm = cfg.d_model, cfg.tm
    n_tiles = cfg.t_all // tm
    row = lambda w: pl.BlockSpec((tm, w), lambda i: (i, 0))
    t_out = cfg.t_lat if final else cfg.t_all
    return pl.pallas_call(
        functools.partial(_combine_kernel, final=final),
        grid=(t_out // tm,),
        in_specs=[row(d), row(d // 2), pl.BlockSpec((tm, d // 2), lambda i: (n_tiles + i, 0)),
                  row(ROUTE_LANES), _mod_spec(cfg), _const_spec((1, d))],
        out_specs=row(d),
        out_shape=jax.ShapeDtypeStruct((t_out, d), F32),
        compiler_params=_cparams(("arbitrary",)),
        name="moe_combine",
    )(x, y12, y12, route, mod_l, g)


def _forward(cfg, x, c, ctx, c_ctx, ada_w, ada_b, norm1_g, norm2_g, ab_w_in, ab_w_out,
             diff_lq1, diff_lk1, diff_lq2, diff_lk2, diff_subln_g, bconv_w,
             ssd_w_in, ssd_conv_w, ssd_conv_b, ssd_A_log, ssd_dt_bias, ssd_D, ssd_norm_g, ssd_w_out,
             moe_wg, moe_bg, moe_we, moe_be, moe_w_gate, moe_w_up, moe_w_down, final_norm_g):
    d = cfg.d_model
    b = cfg.batch
    pow2 = lambda n: n & (n - 1) == 0
    assert pow2(cfg.n_lat) and pow2(cfg.n_ctx), "sequence positions are taken with a bit mask"
    assert cfg.n_lat % cfg.tm == 0 and cfg.t_ctx % cfg.tm == 0, "row tiles must not straddle latent / context rows"
    assert cfg.tm % cfg.n_ctx == 0 or cfg.n_ctx % cfg.tm == 0, "row tiles hold whole context sequences or parts of one"
    assert cfg.n_lat % (cfg.tq * cfg.attn_subtiles) == 0 and cfg.n_lat % min(cfg.kchunk, cfg.n_lat) == 0
    assert cfg.n_lat % SSD_CHUNK == 0 and cfg.n_ctx % SSD_CHUNK == 0 and cfg.n_lat % GRID_W == 0
    x_pair = (x.reshape(cfg.t_lat, d), ctx.reshape(cfg.t_ctx, d))
    c_all = jnp.zeros((SUBLANES_V7X, d), F32).at[:b].set(c).at[b].set(c_ctx)
    mod = _adaln(cfg, c_all, ada_w, ada_b)
    tables = _rope_tables(cfg)
    ab_w_in_b, ab_w_out_b = _to_bf16(ab_w_in), _to_bf16(ab_w_out)
    ssd_w_in_b, ssd_w_out_b = _to_bf16(ssd_w_in), _to_bf16(ssd_w_out)

    for layer in range(cfg.depth):
        i = layer // 2
        mod_l = mod[layer]
        g1 = norm1_g[layer].reshape(1, d)
        g2 = norm2_g[layer].reshape(1, d)
        router = _router_weights(moe_wg[layer], moe_bg[layer], moe_we[layer], moe_be[layer])
        if layer % 2 == 0:
            lam_init = 0.8 - 0.6 * math.exp(-0.3 * layer)
            q, k, v, gb, u = _ab_in(cfg, x_pair, mod_l, g1, ab_w_in_b, i, tables)
            lvec = jnp.stack([diff_lq1[i], diff_lk1[i], diff_lq2[i], diff_lk2[i]])
            o_pair = _attention(cfg, q, k, v, lvec, diff_subln_g[i].reshape(1, DIFF_V_DIM), lam_init)
            xa, h2, route, cnt = _ab_out(cfg, o_pair, gb, u, bconv_w[i], ab_w_out_b, i, x_pair, mod_l, g2, router)
        else:
            z, xbc_act, dt_raw = _ssd_in(cfg, xa, mod_l, g1, ssd_w_in_b, i, ssd_conv_w[i],
                                         ssd_conv_b[i].reshape(1, SSD_CONV_CH))
            y2 = _ssd_scan(cfg, xbc_act, dt_raw, ssd_dt_bias[i], ssd_A_log[i])
            d_wide = jnp.repeat(ssd_D[i], SSD_HEAD_DIM).reshape(1, SSD_D_INNER)
            xa, h2, route, cnt = _ssd_out(cfg, y2, xbc_act, z, d_wide, ssd_norm_g[i].reshape(1, SSD_D_INNER),
                                          ssd_w_out_b, i, xa, mod_l, g2, router)
        slot_tok, block_eid, n_used, slots = _dispatch(cfg, route, cnt)
        xb_parts = [h2[p_] for p_ in jnp.split(slot_tok, MOE_GATHER_PARTS)]
        yb = _experts(cfg, layer, block_eid, n_used, xb_parts, moe_w_gate, moe_w_up, moe_w_down)
        last = layer == cfg.depth - 1
        y12 = yb[slots]
        xa = _combine(cfg, xa, y12, route, mod_l, final_norm_g.reshape(1, d), last)
        x_pair = (xa, xa)
    return xa.reshape(b, cfg.n_lat, d)


def kernel(x, c, ctx, c_ctx, ada_w, ada_b, norm1_g, norm2_g, ab_w_in, ab_w_out, diff_lq1, diff_lk1, diff_lq2, diff_lk2, diff_subln_g, bconv_w, ssd_w_in, ssd_conv_w, ssd_conv_b, ssd_A_log, ssd_dt_bias, ssd_D, ssd_norm_g, ssd_w_out, moe_wg, moe_bg, moe_we, moe_be, moe_w_gate, moe_w_up, moe_w_down, final_norm_g):
    cfg = Cfg(batch=x.shape[0], n_lat=x.shape[1], n_ctx=ctx.shape[1], d_model=x.shape[2], depth=ada_w.shape[0],
              tm=512, tq=256, moe_tile=512, attn_subtiles=2, kchunk=1024)
    return _forward(cfg, x, c, ctx, c_ctx, ada_w, ada_b, norm1_g, norm2_g, ab_w_in, ab_w_out,
                    diff_lq1, diff_lk1, diff_lq2, diff_lk2, diff_subln_g, bconv_w,
                    ssd_w_in, ssd_conv_w, ssd_conv_b, ssd_A_log, ssd_dt_bias, ssd_D, ssd_norm_g, ssd_w_out,
                    moe_wg, moe_bg, moe_we, moe_be, moe_w_gate, moe_w_up, moe_w_down, final_norm_g)
```

```python
import functools
import math
from typing import NamedTuple

import jax
import jax.numpy as jnp
from jax import lax
from jax.experimental import pallas as pl
from jax.experimental.pallas import tpu as pltpu

F32 = jnp.float32
BF16 = jnp.bfloat16
HIGHEST = lax.Precision.HIGHEST

LANES_V7X = 128
SUBLANES_V7X = 8
VMEM_LIMIT_BYTES_V7X = 56 * 1024 * 1024

RMS_EPS = 1e-6
GRID_W = 64
N_DIFF_HEADS = 4
DIFF_QK_DIM = 64
DIFF_V_DIM = 128
DIFF_WIDTH = 512
CONV_WIDTH = 512
ROPE_BASE = 10000.0
SSD_D_INNER = 2048
SSD_HEAD_DIM = 64
SSD_HEADS = 32
SSD_GROUPS = 4
SSD_STATE = 128
SSD_CHUNK = 128
SSD_CONV_CH = SSD_D_INNER + 2 * SSD_GROUPS * SSD_STATE
MOE_GROUPS = 4
MOE_EXPERTS_PER_GROUP = 8
MOE_EXPERTS = 32
MOE_HIDDEN = 512
ROUTE_LANES = LANES_V7X
MOE_GATHER_PARTS = 3


class Cfg(NamedTuple):
    batch: int
    n_lat: int
    n_ctx: int
    d_model: int
    depth: int
    tm: int
    tq: int
    moe_tile: int
    attn_subtiles: int
    kchunk: int

    @property
    def t_lat(self):
        return self.batch * self.n_lat

    @property
    def t_ctx(self):
        return self.batch * self.n_ctx

    @property
    def t_all(self):
        return self.t_lat + self.t_ctx


def _cparams(sem):
    return pltpu.CompilerParams(dimension_semantics=sem, vmem_limit_bytes=VMEM_LIMIT_BYTES_V7X)


def _silu(v):
    return v * (1.0 / (1.0 + jnp.exp(-v)))


def _const_spec(shape):
    nd = len(shape)
    return pl.BlockSpec(shape, lambda *_: (0,) * nd)


def _layer_spec(stacked, index):
    rest = stacked.shape[1:]
    return pl.BlockSpec((None,) + rest, lambda *_: (index,) + (0,) * len(rest))


def _mod_spec(cfg):
    return pl.BlockSpec((1, 6, cfg.d_model),
                        lambda i: (jnp.minimum((i * cfg.tm) // cfg.n_lat, cfg.batch), 0, 0))


def _seq_pos(cfg, tm):
    r0 = pl.program_id(0) * tm
    row = r0 + lax.broadcasted_iota(jnp.int32, (tm, 1), 0)
    seqlen = jnp.where(r0 >= cfg.t_lat, cfg.n_ctx, cfg.n_lat)
    return row & (seqlen - 1), seqlen


def _norm_mod(x, g, shift, scale):
    ms = jnp.mean(x * x, axis=-1, keepdims=True)
    return (x * lax.rsqrt(ms + RMS_EPS) * g) * (1.0 + scale) + shift


def _to_bf16(w):
    n = w.shape[-1]
    return jnp.pad(w.astype(BF16), ((0, 0), (0, 0), (0, -n % LANES_V7X)))


def _hi_lo(v):
    hi = v.astype(BF16)
    return hi, (v - hi.astype(F32)).astype(BF16)


def _adaln_kernel(c_ref, w_ref, b_ref, o_ref):
    s_hi, s_lo = _hi_lo(_silu(c_ref[...]))
    w_hi, w_lo = _hi_lo(w_ref[0])
    o_ref[0] = (jnp.dot(s_hi, w_hi, preferred_element_type=F32) + jnp.dot(s_hi, w_lo, preferred_element_type=F32)
                + jnp.dot(s_lo, w_hi, preferred_element_type=F32)) + b_ref[0]


def _adaln(cfg, c_all, ada_w, ada_b):
    d = cfg.d_model
    out = pl.pallas_call(
        _adaln_kernel,
        grid=(cfg.depth, 6),
        in_specs=[_const_spec((SUBLANES_V7X, d)),
                  pl.BlockSpec((1, d, d), lambda l, j: (l, 0, j)),
                  pl.BlockSpec((1, 1, d), lambda l, j: (l, 0, j))],
        out_specs=pl.BlockSpec((1, SUBLANES_V7X, d), lambda l, j: (l, 0, j)),
        out_shape=jax.ShapeDtypeStruct((cfg.depth, SUBLANES_V7X, 6 * d), F32),
        compiler_params=_cparams(("arbitrary", "arbitrary")),
        name="adaln",
    )(c_all, ada_w, ada_b.reshape(cfg.depth, 1, 6 * d))
    return out.reshape(cfg.depth, SUBLANES_V7X, 6, d)


def _rope_tables(cfg):
    n = cfg.n_lat
    rows = n // GRID_W
    row = jnp.broadcast_to(jnp.arange(rows, dtype=F32)[:, None], (rows, GRID_W)).reshape(n)
    col = jnp.broadcast_to(jnp.arange(GRID_W, dtype=F32)[None, :], (rows, GRID_W)).reshape(n)
    axis_dim = DIFF_QK_DIM // 2
    inv_freq = ROPE_BASE ** (-jnp.arange(0, axis_dim, 2, dtype=F32) / axis_dim)
    ang_r = row[:, None] * inv_freq
    ang_c = col[:, None] * inv_freq
    zeros = jnp.zeros_like(ang_r)
    cos64 = jnp.concatenate([jnp.cos(ang_r), jnp.cos(ang_r), jnp.cos(ang_c), jnp.cos(ang_c)], axis=1)
    sa64 = jnp.concatenate([zeros, jnp.sin(ang_r), zeros, jnp.sin(ang_c)], axis=1)
    sb64 = jnp.concatenate([-jnp.sin(ang_r), zeros, -jnp.sin(ang_c), zeros], axis=1)
    ident = cfg.tm
    cos = jnp.concatenate([jnp.tile(cos64, (1, 2)), jnp.ones((ident, LANES_V7X), F32)], axis=0)
    sa = jnp.concatenate([jnp.tile(sa64, (1, 2)), jnp.zeros((ident, LANES_V7X), F32)], axis=0)
    sb = jnp.concatenate([jnp.tile(sb64, (1, 2)), jnp.zeros((ident, LANES_V7X), F32)], axis=0)
    return cos, sa, sb


def _dual_row_specs(cfg, width, pair):
    lat_arr, ctx_arr = pair
    nl = cfg.t_lat // cfg.tm
    c0 = nl if ctx_arr is lat_arr else 0
    return (pl.BlockSpec((cfg.tm, width), lambda i: (jnp.minimum(i, nl - 1), 0)),
            pl.BlockSpec((cfg.tm, width), lambda i: (jnp.maximum(i - nl, 0) + c0, 0)))


def _dual_rows(cfg, lat_ref, ctx_ref):
    return jnp.where(pl.program_id(0) >= cfg.t_lat // cfg.tm, ctx_ref[...], lat_ref[...])


def _ab_in_kernel(xl_ref, xc_ref, mod_ref, g_ref, w_ref, cos_ref, sa_ref, sb_ref,
                  q_ref, k_ref, v_ref, gb_ref, u_ref, *, cfg):
    m = mod_ref[0]
    hb = _norm_mod(_dual_rows(cfg, xl_ref, xc_ref), g_ref[...], m[0:1], m[1:2]).astype(BF16)
    cos, sa, sb = cos_ref[...], sa_ref[...], sb_ref[...]

    def proj(j):
        return jnp.dot(hb, w_ref[:, j * DIFF_WIDTH:(j + 1) * DIFF_WIDTH], preferred_element_type=F32)

    def rope(t, out_ref, scale):
        for gi in range(DIFF_WIDTH // LANES_V7X):
            xg = t[:, gi * LANES_V7X:(gi + 1) * LANES_V7X]
            r = xg * cos + pltpu.roll(xg, 16, 1) * sa + pltpu.roll(xg, LANES_V7X - 16, 1) * sb
            out_ref[:, gi * LANES_V7X:(gi + 1) * LANES_V7X] = (r * scale).astype(out_ref.dtype)

    rope(proj(0), q_ref, math.log2(math.e) * DIFF_QK_DIM ** -0.5)
    rope(proj(1), k_ref, 1.0)
    v_ref[...] = proj(2).astype(v_ref.dtype)
    gb_ref[...] = proj(3)
    u_ref[...] = proj(4) * proj(5)


def _ab_in(cfg, x_pair, mod_l, g1, w_stack, w_index, tables):
    t, d, tm = cfg.t_all, cfg.d_model, cfg.tm
    n_lat_tiles, per_seq = cfg.t_lat // tm, cfg.n_lat // tm
    tab_spec = pl.BlockSpec((tm, LANES_V7X), lambda i: (jnp.where(i < n_lat_tiles, i % per_seq, per_seq), 0))
    row = lambda w: pl.BlockSpec((tm, w), lambda i: (i, 0))
    return pl.pallas_call(
        functools.partial(_ab_in_kernel, cfg=cfg),
        grid=(t // tm,),
        in_specs=[*_dual_row_specs(cfg, d, x_pair), _mod_spec(cfg), _const_spec((1, d)),
                  _layer_spec(w_stack, w_index), tab_spec, tab_spec, tab_spec],
        out_specs=[row(DIFF_WIDTH)] * 5,
        out_shape=[jax.ShapeDtypeStruct((t, DIFF_WIDTH), BF16)] * 3
        + [jax.ShapeDtypeStruct((t, CONV_WIDTH), F32)] * 2,
        compiler_params=_cparams(("arbitrary",)),
        name="ab_in",
    )(*x_pair, mod_l, g1, w_stack, *tables)


def _attn_body(lvec_ref, q_ref, kc_ref, kl_ref, vc_ref, vl_ref, g_ref, o_ref, *, tq, lam_init, kchunk):
    lv = lvec_ref[...]
    lam = (jnp.exp(jnp.sum(lv[0:1] * lv[1:2], axis=-1, keepdims=True))
           - jnp.exp(jnp.sum(lv[2:3] * lv[3:4], axis=-1, keepdims=True)) + lam_init)
    nt = (((1,), (1,)), ((), ()))
    half = DIFF_QK_DIM
    chunks = [(kc_ref, vc_ref, 0, kc_ref.shape[0])]
    if kl_ref is not None:
        chunks += [(kl_ref, vl_ref, r0, kchunk) for r0 in range(0, kl_ref.shape[0], kchunk)]
    for j in range(q_ref.shape[0] // tq):
        q = q_ref[j * tq:(j + 1) * tq, :].astype(F32)
        lane = lax.broadcasted_iota(jnp.int32, q.shape, 1)
        q2 = jnp.concatenate([jnp.where(lane < half, q, 0.0), jnp.where(lane >= half, q, 0.0)],
                             axis=0).astype(BF16)
        ps, ms, ls = [], [], []
        for k_ref, _, r0, n in chunks:
            s = lax.dot_general(q2, k_ref[r0:r0 + n, :], nt, preferred_element_type=F32)
            m_c = jnp.max(s, axis=-1, keepdims=True)
            p = jnp.exp2(s - m_c)
            ps.append(p)
            ms.append(m_c)
            ls.append(jnp.sum(p, axis=-1, keepdims=True))
        mx = functools.reduce(jnp.maximum, ms)
        scale = [jnp.exp2(m_c - mx) for m_c in ms]
        r = 1.0 / functools.reduce(jnp.add, [sc * l_c for sc, l_c in zip(scale, ls)])
        pv = None
        for p, sc, (_, v_ref, r0, n) in zip(ps, scale, chunks):
            w = sc * r
            pd = (p[:tq] * w[:tq] - p[tq:] * (lam * w[tq:])).astype(BF16)
            part = jnp.dot(pd, v_ref[r0:r0 + n, :], preferred_element_type=F32)
            pv = part if pv is None else pv + part
        o = pv * lax.rsqrt(jnp.mean(pv * pv, axis=-1, keepdims=True) + RMS_EPS)
        o_ref[j * tq:(j + 1) * tq, :] = ((o * g_ref[...]) * (1.0 - lam_init)).astype(o_ref.dtype)


def _attn_lat_kernel(lvec_ref, q_ref, kc_ref, kl_ref, vc_ref, vl_ref, g_ref, o_ref, **kw):
    _attn_body(lvec_ref, q_ref, kc_ref, kl_ref, vc_ref, vl_ref, g_ref, o_ref, **kw)


def _attn_ctx_kernel(lvec_ref, q_ref, kc_ref, vc_ref, g_ref, o_ref, **kw):
    _attn_body(lvec_ref, q_ref, kc_ref, None, vc_ref, None, g_ref, o_ref, **kw)


def _attention(cfg, q, k, v, lvec, subln_g, lam_init):
    tq, nsub = cfg.tq, cfg.attn_subtiles
    tstep = tq * nsub
    nqb = cfg.n_lat // tstep
    ctx_blk0 = cfg.t_lat // cfg.n_ctx
    hw = DIFF_V_DIM
    ctx_spec = pl.BlockSpec((cfg.n_ctx, hw), lambda b, h, *_: (ctx_blk0 + b, h))
    lat_spec = pl.BlockSpec((cfg.n_lat, hw), lambda b, h, *_: (b, h))
    q_spec = pl.BlockSpec((tstep, hw), lambda b, h, i: (b * nqb + i, h))
    kw = dict(tq=tq, lam_init=lam_init, kchunk=min(cfg.kchunk, cfg.n_lat))
    o_lat = pl.pallas_call(
        functools.partial(_attn_lat_kernel, **kw),
        grid=(cfg.batch, N_DIFF_HEADS, nqb),
        in_specs=[_const_spec(lvec.shape), q_spec, ctx_spec, lat_spec, ctx_spec, lat_spec, _const_spec((1, hw))],
        out_specs=q_spec,
        out_shape=jax.ShapeDtypeStruct((cfg.t_lat, DIFF_WIDTH), BF16),
        compiler_params=_cparams(("arbitrary", "arbitrary", "arbitrary")),
        name="diff_attn",
    )(lvec, q, k, k, v, v, subln_g)
    tqc = min(tq, cfg.n_ctx)
    o_ctx = pl.pallas_call(
        functools.partial(_attn_ctx_kernel, tq=tqc, lam_init=lam_init, kchunk=cfg.n_ctx),
        grid=(cfg.batch, N_DIFF_HEADS),
        in_specs=[_const_spec(lvec.shape), ctx_spec, ctx_spec, ctx_spec, _const_spec((1, hw))],
        out_specs=pl.BlockSpec((cfg.n_ctx, hw), lambda b, h: (b, h)),
        out_shape=jax.ShapeDtypeStruct((cfg.t_ctx, DIFF_WIDTH), BF16),
        compiler_params=_cparams(("arbitrary", "arbitrary")),
        name="diff_attn_ctx",
    )(lvec, q, k, v, subln_g)
    return o_lat, o_ctx


def _route_from_logits(lg):
    lane = lax.broadcasted_iota(jnp.int32, lg.shape, 1)
    neg = -jnp.inf
    big = jnp.int32(ROUTE_LANES)
    gl = jnp.where(lane < MOE_GROUPS, lg, neg)
    gmax = jnp.max(gl, axis=-1, keepdims=True)
    gidx = jnp.min(jnp.where(gl == gmax, lane, big), axis=-1, keepdims=True)
    gw = 1.0 / jnp.sum(jnp.exp(gl - gmax), axis=-1, keepdims=True)
    lo = MOE_GROUPS + gidx * MOE_EXPERTS_PER_GROUP
    el = jnp.where((lane >= lo) & (lane < lo + MOE_EXPERTS_PER_GROUP), lg, neg)
    m1 = jnp.max(el, axis=-1, keepdims=True)
    i1 = jnp.min(jnp.where(el == m1, lane, big), axis=-1, keepdims=True)
    el2 = jnp.where(lane == i1, neg, el)
    m2 = jnp.max(el2, axis=-1, keepdims=True)
    i2 = jnp.min(jnp.where(el2 == m2, lane, big), axis=-1, keepdims=True)
    w1 = gw / (1.0 + jnp.exp(m2 - m1))
    w2 = gw - w1
    e1 = (i1 - MOE_GROUPS).astype(F32)
    e2 = (i2 - MOE_GROUPS).astype(F32)
    return jnp.where(lane == 0, e1, jnp.where(lane == 1, e2, jnp.where(lane == 2, w1, jnp.where(lane == 3, w2, 0.0))))


def _pack_bf16_pairs(v):
    w = v.shape[1] // 2
    lo = pltpu.bitcast(v[:, :w].astype(BF16).astype(F32), jnp.uint32)
    hi = pltpu.bitcast(v[:, w:].astype(BF16).astype(F32), jnp.uint32)
    return pltpu.bitcast((hi & jnp.uint32(0xFFFF0000)) | (lo >> 16), F32)


def _unpack_bf16_pairs(words):
    u = pltpu.bitcast(words, jnp.uint32)
    lo = pltpu.bitcast(u << 16, F32)
    hi = pltpu.bitcast(u & jnp.uint32(0xFFFF0000), F32)
    return lo, hi


def _router_weights(wg, bg, we, be):
    d = wg.shape[0]
    pad = ROUTE_LANES - MOE_GROUPS - MOE_EXPERTS
    wr = jnp.concatenate([wg, we, jnp.zeros((d, pad), F32)], axis=1)
    hi, lo = _hi_lo(wr)
    br = jnp.concatenate([bg, be, jnp.zeros((pad,), F32)]).reshape(1, ROUTE_LANES)
    return jnp.concatenate([hi, lo], axis=1), hi, br


def _expert_onehots(rt):
    lane = lax.broadcasted_iota(jnp.int32, rt.shape, 1)
    return (jnp.where(lane == rt[:, 0:1].astype(jnp.int32), 1.0, 0.0),
            jnp.where(lane == rt[:, 1:2].astype(jnp.int32), 1.0, 0.0))


def _residual_norm_route(x, mix, m, g2, wr2_ref, wrh_ref, br_ref, xo_ref, h2_ref, rt_ref, cnt_ref):
    xn = x + m[2:3] * mix
    xo_ref[...] = xn
    h2 = _norm_mod(xn, g2, m[3:4], m[4:5])
    h2_ref[...] = _pack_bf16_pairs(h2)
    hh, hl = _hi_lo(h2)
    a = jnp.dot(hh, wr2_ref[...], preferred_element_type=F32)
    lg = (a[:, :ROUTE_LANES] + a[:, ROUTE_LANES:]) + jnp.dot(hl, wrh_ref[...], preferred_element_type=F32)
    rt = _route_from_logits(lg + br_ref[...])
    rt_ref[...] = rt

    @pl.when(pl.program_id(0) == 0)
    def _():
        cnt_ref[...] = jnp.zeros_like(cnt_ref)

    o1, o2 = _expert_onehots(rt)
    cnt_ref[...] += jnp.sum(o1 + o2, axis=0, keepdims=True)


def _epilogue_specs(cfg, x_pair):
    d, tm = cfg.d_model, cfg.tm
    row = lambda w: pl.BlockSpec((tm, w), lambda i: (i, 0))
    in_specs = [*_dual_row_specs(cfg, d, x_pair), _mod_spec(cfg), _const_spec((1, d)),
                _const_spec((d, 2 * ROUTE_LANES)),
                _const_spec((d, ROUTE_LANES)), _const_spec((1, ROUTE_LANES))]
    out_specs = [row(d), row(d // 2), row(ROUTE_LANES), _const_spec((SUBLANES_V7X, ROUTE_LANES))]
    out_shape = [jax.ShapeDtypeStruct((cfg.t_all, d), F32), jax.ShapeDtypeStruct((cfg.t_all, d // 2), F32),
                 jax.ShapeDtypeStruct((cfg.t_all, ROUTE_LANES), F32),
                 jax.ShapeDtypeStruct((SUBLANES_V7X, ROUTE_LANES), F32)]
    return in_specs, out_specs, out_shape


def _halo_specs(cfg, width, col_block=0):
    per = cfg.tm // SUBLANES_V7X
    last = cfg.t_all // SUBLANES_V7X - 1
    prev = pl.BlockSpec((SUBLANES_V7X, width), lambda i, *_: (jnp.maximum(i * per - 1, 0), col_block))
    nxt = pl.BlockSpec((SUBLANES_V7X, width), lambda i, *_: (jnp.minimum((i + 1) * per, last), col_block))
    return prev, nxt


def _shifted_rows(pad_ref, u, prev_blk, next_blk, pos, seqlen):
    tm = u.shape[0]
    s = SUBLANES_V7X
    pad_ref[s:s + tm, :] = u
    pad_ref[0:s, :] = prev_blk
    pad_ref[s + tm:2 * s + tm, :] = next_blk
    um1 = jnp.where(pos == 0, 0.0, pad_ref[s - 1:s - 1 + tm, :])
    up1 = jnp.where(pos == seqlen - 1, 0.0, pad_ref[s + 1:s + 1 + tm, :])
    return um1, up1


def _ab_out_kernel(ol_ref, oc_ref, gb_ref, u_ref, up_ref, un_ref, cw_ref, wo_ref,
                   xl_ref, xc_ref, mod_ref, g2_ref, wr2_ref, wrh_ref, br_ref,
                   xo_ref, h2_ref, rt_ref, cnt_ref, pad_ref, *, cfg):
    pos, seqlen = _seq_pos(cfg, cfg.tm)
    u = u_ref[...]
    um1, up1 = _shifted_rows(pad_ref, u, up_ref[...], un_ref[...], pos, seqlen)
    cw = cw_ref[...]
    conv = gb_ref[...] * (um1 * cw[0:1] + u * cw[1:2] + up1 * cw[2:3])
    mix = (jnp.dot(_dual_rows(cfg, ol_ref, oc_ref), wo_ref[:DIFF_WIDTH, :], preferred_element_type=F32)
           + jnp.dot(conv.astype(BF16), wo_ref[DIFF_WIDTH:, :], preferred_element_type=F32))
    _residual_norm_route(_dual_rows(cfg, xl_ref, xc_ref), mix, mod_ref[0], g2_ref[...], wr2_ref, wrh_ref, br_ref,
                         xo_ref, h2_ref, rt_ref, cnt_ref)


def _ab_out(cfg, o_pair, gb, u, conv_w, wo_stack, w_index, x_pair, mod_l, g2, router):
    tm = cfg.tm
    row = lambda w: pl.BlockSpec((tm, w), lambda i: (i, 0))
    prev, nxt = _halo_specs(cfg, CONV_WIDTH)
    ep_in, ep_out, ep_shape = _epilogue_specs(cfg, x_pair)
    return pl.pallas_call(
        functools.partial(_ab_out_kernel, cfg=cfg),
        grid=(cfg.t_all // tm,),
        in_specs=[*_dual_row_specs(cfg, DIFF_WIDTH, o_pair), row(CONV_WIDTH), row(CONV_WIDTH), prev, nxt,
                  _const_spec(conv_w.shape),
                  _layer_spec(wo_stack, w_index)] + ep_in,
        out_specs=ep_out,
        out_shape=ep_shape,
        scratch_shapes=[pltpu.VMEM((tm + 2 * SUBLANES_V7X, CONV_WIDTH), F32)],
        compiler_params=_cparams(("arbitrary",)),
        name="ab_out",
    )(*o_pair, gb, u, u, u, conv_w, wo_stack, *x_pair, mod_l, g2, *router)


def _ssd_in_kernel(x_ref, xp_ref, xn_ref, mod_ref, g_ref, w_ref, cw_ref, cb_ref,
                   z_ref, xbc_ref, dt_ref, pad_ref, *, cfg):
    tm = cfg.tm
    s = SUBLANES_V7X
    m = mod_ref[0]
    g = g_ref[...]
    h = _norm_mod(x_ref[...], g, m[0:1], m[1:2])
    hb = h.astype(BF16)
    h_ext = jnp.concatenate([_norm_mod(xp_ref[...], g, m[0:1], m[1:2]), h,
                             _norm_mod(xn_ref[...], g, m[0:1], m[1:2])], axis=0).astype(BF16)
    pos, seqlen = _seq_pos(cfg, tm)
    cw = 512
    for j in range(SSD_D_INNER // cw):
        z_ref[:, j * cw:(j + 1) * cw] = jnp.dot(hb, w_ref[:, j * cw:(j + 1) * cw],
                                                preferred_element_type=F32).astype(z_ref.dtype)
    for j in range(SSD_CONV_CH // cw):
        c0 = SSD_D_INNER + j * cw
        pad = pad_ref.at[j]
        pad[...] = jnp.dot(h_ext, w_ref[:, c0:c0 + cw], preferred_element_type=F32)
        taps = cw_ref[:, j * cw:(j + 1) * cw]
        um1 = jnp.where(pos == 0, 0.0, pad[s - 1:s - 1 + tm, :])
        up1 = jnp.where(pos == seqlen - 1, 0.0, pad[s + 1:s + 1 + tm, :])
        conv = um1 * taps[0:1] + pad[s:s + tm, :] * taps[1:2] + up1 * taps[2:3] + cb_ref[:, j * cw:(j + 1) * cw]
        xbc_ref[:, j * cw:(j + 1) * cw] = _silu(conv).astype(xbc_ref.dtype)
    dt_ref[...] = jnp.dot(hb, w_ref[:, SSD_D_INNER + SSD_CONV_CH:], preferred_element_type=F32)


def _ssd_in(cfg, x, mod_l, g1, w_stack, w_index, conv_w, conv_b):
    t, d, tm = cfg.t_all, cfg.d_model, cfg.tm
    row = lambda w: pl.BlockSpec((tm, w), lambda i: (i, 0))
    prev, nxt = _halo_specs(cfg, d)
    return pl.pallas_call(
        functools.partial(_ssd_in_kernel, cfg=cfg),
        grid=(t // tm,),
        in_specs=[row(d), prev, nxt, _mod_spec(cfg), _const_spec((1, d)), _layer_spec(w_stack, w_index),
                  _const_spec(conv_w.shape), _const_spec(conv_b.shape)],
        out_specs=[row(SSD_D_INNER), row(SSD_CONV_CH), row(LANES_V7X)],
        out_shape=[jax.ShapeDtypeStruct((t, SSD_D_INNER), BF16), jax.ShapeDtypeStruct((t, SSD_CONV_CH), BF16),
                   jax.ShapeDtypeStruct((t, LANES_V7X), F32)],
        scratch_shapes=[pltpu.VMEM((SSD_CONV_CH // 512, tm + 2 * SUBLANES_V7X, 512), F32)],
        compiler_params=_cparams(("arbitrary",)),
        name="ssd_in",
    )(x, x, x, mod_l, g1, w_stack, conv_w, conv_b)


def _split3(v):
    b1 = v.astype(BF16)
    r1 = v - b1.astype(F32)
    b2 = r1.astype(BF16)
    b3 = (r1 - b2.astype(F32)).astype(BF16)
    return jnp.concatenate([b1, b2, b3], axis=1)


def _softplus(v):
    return jnp.maximum(v, 0.0) + jnp.log(1.0 + jnp.exp(-jnp.abs(v)))


def _ssd_scan_kernel(*refs):
    n_in = 5
    fwd_in, bwd_in = refs[:n_in], refs[n_in:2 * n_in]
    bias_ref, biast_ref, alog_ref, alogt_ref, ewide_ref, yf_ref, yb_ref, sf_ref, sb_ref = refs[2 * n_in:]

    @pl.when(pl.program_id(1) == 0)
    def _():
        sf_ref[...] = jnp.zeros_like(sf_ref)
        sb_ref[...] = jnp.zeros_like(sb_ref)

    for d, (ins, y_ref, state_ref) in enumerate(((fwd_in, yf_ref, sf_ref), (bwd_in, yb_ref, sb_ref))):
        _ssd_chunk(d == 0, *ins, bias_ref[d], biast_ref[d], alog_ref[d], alogt_ref[d], ewide_ref, y_ref, state_ref)


def _ssd_chunk(fwd, xs_ref, b_ref, c_ref, dt_ref, dtt_ref, bias, biast, alog, alogt, ewide_ref, y_ref, state_ref):
    q = SSD_CHUNK
    dt = _softplus(dt_ref[0] + bias)
    dtt = _softplus(dtt_ref[0] + biast)
    a = dt * (-jnp.exp(alog))
    at = dtt * (-jnp.exp(alogt))

    ri = lax.broadcasted_iota(jnp.int32, (q, q), 0)
    ci = lax.broadcasted_iota(jnp.int32, (q, q), 1)
    keep = (ci <= ri) if fwd else (ci >= ri)
    tri = jnp.where(keep, 1.0, 0.0)
    cs = jnp.dot(tri, a, precision=HIGHEST, preferred_element_type=F32)
    cst = lax.dot_general(at, tri, (((1,), (1,)), ((), ())), precision=HIGHEST,
                          preferred_element_type=F32)
    tot = jnp.sum(a, axis=0, keepdims=True)

    ewide = ewide_ref[...]
    w_wide = jnp.dot(_split3(dt * jnp.exp(tot - cs)), ewide, preferred_element_type=F32)
    ecs_wide = jnp.dot(_split3(jnp.exp(cs)), ewide, preferred_element_type=F32)
    dec_wide = jnp.dot(_split3(jnp.broadcast_to(jnp.exp(tot), (SUBLANES_V7X, SSD_HEADS))), ewide,
                       preferred_element_type=F32)[0:1]

    lane = lax.broadcasted_iota(jnp.int32, (q, LANES_V7X), 1)
    lo = lane < SSD_HEAD_DIM
    hpg = SSD_HEADS // SSD_GROUPS
    gw = hpg * SSD_HEAD_DIM
    nt = (((1,), (1,)), ((), ()))
    tn = (((0,), (0,)), ((), ()))
    for g in range(SSD_GROUPS):
        bm = b_ref[:, g * SSD_STATE:(g + 1) * SSD_STATE]
        cm = c_ref[:, g * SSD_STATE:(g + 1) * SSD_STATE]
        cb = lax.dot_general(cm, bm, nt, preferred_element_type=F32)
        sl = slice(g * gw, (g + 1) * gw)
        y_off = jnp.dot(cm, state_ref[:, sl].astype(BF16), preferred_element_type=F32) * ecs_wide[:, sl]
        for pr in range(hpg // 2):
            lhs, rhs = [], []
            c0 = g * gw + pr * LANES_V7X
            xs_pair = xs_ref[:, c0:c0 + LANES_V7X]
            zero = jnp.zeros_like(xs_pair)
            for k in range(2):
                h = g * hpg + pr * 2 + k
                seg = cs[:, h:h + 1] - cst[h:h + 1, :]
                lmat = jnp.exp(jnp.where(keep, seg, -jnp.inf))
                lhs.append((cb * lmat * dtt[h:h + 1, :]).astype(BF16))
                rhs.append(jnp.where(lo if k == 0 else jnp.logical_not(lo), xs_pair, zero))
            y = jnp.dot(jnp.concatenate(lhs, axis=1), jnp.concatenate(rhs, axis=0), preferred_element_type=F32)
            y_ref[:, c0:c0 + LANES_V7X] = (y + y_off[:, pr * LANES_V7X:(pr + 1) * LANES_V7X]).astype(y_ref.dtype)
        x2 = (xs_ref[:, sl].astype(F32) * w_wide[:, sl]).astype(BF16)
        upd = lax.dot_general(bm, x2, tn, preferred_element_type=F32)
        state_ref[:, sl] = state_ref[:, sl] * dec_wide[:, sl] + upd


def _ssd_scan(cfg, xbc_act, dt_raw, dt_bias, a_log):
    q = SSD_CHUNK
    n_cc, n_lc = cfg.n_ctx // q, cfg.n_lat // q
    ctx0 = cfg.t_lat // q
    h = SSD_HEADS
    dt2 = dt_raw[:, :2 * h].reshape(cfg.t_all, 2, h).transpose(1, 0, 2)
    dt2t = dt2.transpose(0, 2, 1)
    bias = dt_bias.reshape(2, 1, h)
    biast = dt_bias.reshape(2, h, 1)
    alog = a_log.reshape(2, 1, h)
    alogt = a_log.reshape(2, h, 1)
    head_of_row = jnp.tile(jnp.arange(h), 3)[:, None]
    ewide = (head_of_row == (jnp.arange(SSD_D_INNER) // SSD_HEAD_DIM)[None, :]).astype(BF16)

    def blk(d):
        def index(b, s):
            cs = s if d == 0 else n_cc - 1 - s
            ls = s - n_cc if d == 0 else n_lc - 1 - (s - n_cc)
            return jnp.where(s < n_cc, ctx0 + b * n_cc + cs, b * n_lc + ls)
        return index

    gs = SSD_GROUPS * SSD_STATE

    def dir_specs(d):
        at = blk(d)
        col = lambda w, cblk: pl.BlockSpec((q, w), lambda b, s: (at(b, s), cblk))
        return [col(SSD_D_INNER, 0), col(gs, SSD_D_INNER // gs), col(gs, SSD_D_INNER // gs + 1),
                pl.BlockSpec((1, q, h), lambda b, s: (d, at(b, s), 0)),
                pl.BlockSpec((1, h, q), lambda b, s: (d, 0, at(b, s)))]

    dir_args = (xbc_act, xbc_act, xbc_act, dt2, dt2t)
    y_spec = lambda d: pl.BlockSpec((q, SSD_D_INNER), lambda b, s: (blk(d)(b, s), 0))
    return pl.pallas_call(
        _ssd_scan_kernel,
        grid=(cfg.batch, n_cc + n_lc),
        in_specs=dir_specs(0) + dir_specs(1) + [_const_spec(bias.shape), _const_spec(biast.shape),
                                                _const_spec(alog.shape), _const_spec(alogt.shape),
                                                _const_spec(ewide.shape)],
        out_specs=[y_spec(0), y_spec(1)],
        out_shape=[jax.ShapeDtypeStruct((cfg.t_all, SSD_D_INNER), BF16)] * 2,
        scratch_shapes=[pltpu.VMEM((SSD_STATE, SSD_D_INNER), F32)] * 2,
        compiler_params=_cparams(("arbitrary", "arbitrary")),
        name="ssd_scan",
    )(*dir_args, *dir_args, bias, biast, alog, alogt, ewide)


def _ssd_out_kernel(yf_ref, yb_ref, xs_ref, z_ref, dw_ref, ng_ref, wo_ref,
                    xl_ref, xc_ref, mod_ref, g2_ref, wr2_ref, wrh_ref, br_ref, xo_ref, h2_ref, rt_ref, cnt_ref,
                    *, cfg):
    gw = SSD_D_INNER // SSD_GROUPS
    mix = None
    for g in range(SSD_GROUPS):
        sl = slice(g * gw, (g + 1) * gw)
        y = yf_ref[:, sl].astype(F32) + yb_ref[:, sl].astype(F32) + xs_ref[:, sl].astype(F32) * dw_ref[:, sl]
        y = y * _silu(z_ref[:, sl].astype(F32))
        y = (y * lax.rsqrt(jnp.mean(y * y, axis=-1, keepdims=True) + RMS_EPS)) * ng_ref[:, sl]
        part = jnp.dot(y.astype(BF16), wo_ref[sl, :], preferred_element_type=F32)
        mix = part if mix is None else mix + part
    _residual_norm_route(_dual_rows(cfg, xl_ref, xc_ref), mix, mod_ref[0], g2_ref[...], wr2_ref, wrh_ref, br_ref,
                         xo_ref, h2_ref, rt_ref, cnt_ref)


def _ssd_out(cfg, y2, xbc_act, z, d_wide, norm_g, wo_stack, w_index, x, mod_l, g2, router):
    tm = cfg.tm
    row = lambda w: pl.BlockSpec((tm, w), lambda i: (i, 0))
    ep_in, ep_out, ep_shape = _epilogue_specs(cfg, (x, x))
    return pl.pallas_call(
        functools.partial(_ssd_out_kernel, cfg=cfg),
        grid=(cfg.t_all // tm,),
        in_specs=[row(SSD_D_INNER)] * 4 + [_const_spec((1, SSD_D_INNER)),
                  _const_spec((1, SSD_D_INNER)), _layer_spec(wo_stack, w_index)] + ep_in,
        out_specs=ep_out,
        out_shape=ep_shape,
        compiler_params=_cparams(("arbitrary",)),
        name="ssd_out",
    )(*y2, xbc_act, z, d_wide, norm_g, wo_stack, x, x, mod_l, g2, *router)


def _expert_kernel(eid_ref, nused_ref, *refs, blocks_per_part):
    xb_refs = refs[:MOE_GATHER_PARTS]
    wg_ref, wu_ref, wd_ref, y_ref, wg_s, wu_s, wd_s = refs[MOE_GATHER_PARTS:]
    i = pl.program_id(0)
    changed = jnp.logical_or(i == 0, eid_ref[i] != eid_ref[jnp.maximum(i - 1, 0)])

    @pl.when(jnp.logical_and(changed, i < nused_ref[0]))
    def _():
        wg_s[...] = wg_ref[0, 0].astype(BF16)
        wu_s[...] = wu_ref[0, 0].astype(BF16)
        wd_s[...] = wd_ref[0, 0].astype(BF16)

    @pl.when(i < nused_ref[0])
    def _():
        half = wg_s.shape[0] // 2
        words = xb_refs[0][...]
        for k in range(1, MOE_GATHER_PARTS):
            words = jnp.where(i >= k * blocks_per_part, xb_refs[k][...], words)
        lo, hi = _unpack_bf16_pairs(words)
        lo, hi = lo.astype(BF16), hi.astype(BF16)
        hg = (jnp.dot(lo, wg_s[:half, :], preferred_element_type=F32)
              + jnp.dot(hi, wg_s[half:, :], preferred_element_type=F32))
        hu = (jnp.dot(lo, wu_s[:half, :], preferred_element_type=F32)
              + jnp.dot(hi, wu_s[half:, :], preferred_element_type=F32))
        y = jnp.dot((_silu(hg) * hu).astype(BF16), wd_s[...], preferred_element_type=F32)
        y_ref[...] = _pack_bf16_pairs(y)

    @pl.when(i >= nused_ref[0])
    def _():
        y_ref[...] = jnp.zeros_like(y_ref)


def _experts(cfg, layer, block_eid, n_used, xb_parts, w_gate, w_up, w_down):
    d, tile = cfg.d_model, cfg.moe_tile
    per = xb_parts[0].shape[0] // tile
    n_blocks = per * MOE_GATHER_PARTS

    def part_spec(k):
        return pl.BlockSpec((tile, d // 2), lambda i, e, n: (jnp.clip(i - k * per, 0, per - 1), 0))

    grid_spec = pltpu.PrefetchScalarGridSpec(
        num_scalar_prefetch=2,
        grid=(n_blocks,),
        in_specs=[part_spec(k) for k in range(MOE_GATHER_PARTS)] + [
                  pl.BlockSpec((1, 1, d, MOE_HIDDEN), lambda i, e, n: (layer, e[i], 0, 0)),
                  pl.BlockSpec((1, 1, d, MOE_HIDDEN), lambda i, e, n: (layer, e[i], 0, 0)),
                  pl.BlockSpec((1, 1, MOE_HIDDEN, d), lambda i, e, n: (layer, e[i], 0, 0))],
        out_specs=pl.BlockSpec((tile, d // 2), lambda i, e, n: (i, 0)),
        scratch_shapes=[pltpu.VMEM((d, MOE_HIDDEN), BF16), pltpu.VMEM((d, MOE_HIDDEN), BF16),
                        pltpu.VMEM((MOE_HIDDEN, d), BF16)],
    )
    return pl.pallas_call(
        functools.partial(_expert_kernel, blocks_per_part=per),
        grid_spec=grid_spec,
        out_shape=jax.ShapeDtypeStruct((n_blocks * tile, d // 2), F32),
        compiler_params=_cparams(("arbitrary",)),
        name="moe_experts",
    )(block_eid, n_used, *xb_parts, w_gate, w_up, w_down)


def _dispatch_kernel(rt_ref, cnt_ref, pos_ref, carry_ref, start_ref, *, tile):
    tm = rt_ref.shape[0]
    rt = rt_ref[...]
    lane = lax.broadcasted_iota(jnp.int32, rt.shape, 1)
    o1, o2 = _expert_onehots(rt)
    cnt1 = jnp.sum(o1, axis=0, keepdims=True)
    cnt2 = jnp.sum(o2, axis=0, keepdims=True)

    @pl.when(pl.program_id(0) == 0)
    def _():
        padded = jnp.floor((cnt_ref[...] + (tile - 1)) * (1.0 / tile)) * tile
        r = lax.broadcasted_iota(jnp.int32, (ROUTE_LANES, ROUTE_LANES), 0)
        c = lax.broadcasted_iota(jnp.int32, (ROUTE_LANES, ROUTE_LANES), 1)
        excl = jnp.where(r < c, 1.0, 0.0)
        start_ref[...] = jnp.dot(padded, excl, precision=HIGHEST, preferred_element_type=F32)
        carry_ref[...] = jnp.zeros_like(carry_ref)

    ri = lax.broadcasted_iota(jnp.int32, (tm, tm), 0)
    ci = lax.broadcasted_iota(jnp.int32, (tm, tm), 1)
    earlier = jnp.where(ci < ri, 1.0, 0.0).astype(BF16)
    p1 = jnp.dot(earlier, o1.astype(BF16), preferred_element_type=F32)
    p2 = jnp.dot(earlier, o2.astype(BF16), preferred_element_type=F32)
    base = start_ref[0:1] + carry_ref[0:1]
    pos1 = jnp.sum(o1 * (base + p1), axis=-1, keepdims=True)
    pos2 = jnp.sum(o2 * (base + cnt1 + p2), axis=-1, keepdims=True)
    cols = jnp.where(lane == 0, pos1, jnp.where(lane == 1, pos2, 0.0))
    r = lax.broadcasted_iota(jnp.int32, (SUBLANES_V7X, ROUTE_LANES), 0)
    c = lax.broadcasted_iota(jnp.int32, (SUBLANES_V7X, ROUTE_LANES), 1)
    pick = jnp.where(r == c, 1.0, 0.0)
    rows = lax.dot_general(pick, cols, (((1,), (1,)), ((), ())), precision=HIGHEST, preferred_element_type=F32)
    pos_ref[...] = rows.astype(jnp.int32)
    carry_ref[...] += cnt1 + cnt2


def _dispatch(cfg, route, cnt):
    tile, tm = cfg.moe_tile, cfg.tm
    t = cfg.t_all
    a_total = 2 * t
    pos = pl.pallas_call(
        functools.partial(_dispatch_kernel, tile=tile),
        grid=(t // tm,),
        in_specs=[pl.BlockSpec((tm, ROUTE_LANES), lambda i: (i, 0)), _const_spec((SUBLANES_V7X, ROUTE_LANES))],
        out_specs=pl.BlockSpec((SUBLANES_V7X, tm), lambda i: (0, i)),
        out_shape=jax.ShapeDtypeStruct((SUBLANES_V7X, t), jnp.int32),
        scratch_shapes=[pltpu.VMEM((SUBLANES_V7X, ROUTE_LANES), F32), pltpu.VMEM((SUBLANES_V7X, ROUTE_LANES), F32)],
        compiler_params=_cparams(("arbitrary",)),
        name="moe_dispatch",
    )(route, cnt)
    counts = cnt[0, :MOE_EXPERTS].astype(jnp.int32)
    pend = jnp.cumsum((counts + tile - 1) // tile * tile)
    n_blocks = (a_total + MOE_EXPERTS * (tile - 1)) // tile
    n_blocks = -(-n_blocks // MOE_GATHER_PARTS) * MOE_GATHER_PARTS
    block_eid = jnp.minimum(jnp.sum(pend[None, :] <= (jnp.arange(n_blocks, dtype=jnp.int32) * tile)[:, None], axis=1),
                            MOE_EXPERTS - 1).astype(jnp.int32)
    n_used = (pend[-1] // tile).astype(jnp.int32).reshape(1)
    slots = pos[0:2].reshape(-1)
    slot_tok = (jnp.arange(n_blocks * tile, dtype=jnp.int32) % t).at[slots].set(
        jnp.arange(a_total, dtype=jnp.int32) % t, unique_indices=True, mode="promise_in_bounds")
    return slot_tok, block_eid, n_used, slots


def _combine_kernel(x_ref, y1_ref, y2_ref, rt_ref, mod_ref, g_ref, o_ref, *, final):
    rt = rt_ref[...]
    a_lo, a_hi = _unpack_bf16_pairs(y1_ref[...])
    b_lo, b_hi = _unpack_bf16_pairs(y2_ref[...])
    w1, w2 = rt[:, 2:3], rt[:, 3:4]
    f = jnp.concatenate([w1 * a_lo + w2 * b_lo, w1 * a_hi + w2 * b_hi], axis=1)
    xn = x_ref[...] + mod_ref[0][5:6] * f
    if final:
        xn = (xn * lax.rsqrt(jnp.mean(xn * xn, axis=-1, keepdims=True) + RMS_EPS)) * g_ref[...]
    o_ref[...] = xn


def _combine(cfg, x, y12, route, mod_l, g, final):
    d, tm = cfg.d_model, cfg.tm
    n_tiles = cfg.t_all // tm
    row = lambda w: pl.BlockSpec((tm, w), lambda i: (i, 0))
    t_out = cfg.t_lat if final else cfg.t_all
    return pl.pallas_call(
        functools.partial(_combine_kernel, final=final),
        grid=(t_out // tm,),
        in_specs=[row(d), row(d // 2), pl.BlockSpec((tm, d // 2), lambda i: (n_tiles + i, 0)),
                  row(ROUTE_LANES), _mod_spec(cfg), _const_spec((1, d))],
        out_specs=row(d),
        out_shape=jax.ShapeDtypeStruct((t_out, d), F32),
        compiler_params=_cparams(("arbitrary",)),
        name="moe_combine",
    )(x, y12, y12, route, mod_l, g)


def _forward(cfg, x, c, ctx, c_ctx, ada_w, ada_b, norm1_g, norm2_g, ab_w_in, ab_w_out,
             diff_lq1, diff_lk1, diff_lq2, diff_lk2, diff_subln_g, bconv_w,
             ssd_w_in, ssd_conv_w, ssd_conv_b, ssd_A_log, ssd_dt_bias, ssd_D, ssd_norm_g, ssd_w_out,
             moe_wg, moe_bg, moe_we, moe_be, moe_w_gate, moe_w_up, moe_w_down, final_norm_g):
    d = cfg.d_model
    b = cfg.batch
    pow2 = lambda n: n & (n - 1) == 0
    assert pow2(cfg.n_lat) and pow2(cfg.n_ctx), "sequence positions are taken with a bit mask"
    assert cfg.n_lat % cfg.tm == 0 and cfg.t_ctx % cfg.tm == 0, "row tiles must not straddle latent / context rows"
    assert cfg.tm % cfg.n_ctx == 0 or cfg.n_ctx % cfg.tm == 0, "row tiles hold whole context sequences or parts of one"
    assert cfg.n_lat % (cfg.tq * cfg.attn_subtiles) == 0 and cfg.n_lat % min(cfg.kchunk, cfg.n_lat) == 0
    assert cfg.n_lat % SSD_CHUNK == 0 and cfg.n_ctx % SSD_CHUNK == 0 and cfg.n_lat % GRID_W == 0
    x_pair = (x.reshape(cfg.t_lat, d), ctx.reshape(cfg.t_ctx, d))
    c_all = jnp.zeros((SUBLANES_V7X, d), F32).at[:b].set(c).at[b].set(c_ctx)
    mod = _adaln(cfg, c_all, ada_w, ada_b)
    tables = _rope_tables(cfg)
    ab_w_in_b, ab_w_out_b = _to_bf16(ab_w_in), _to_bf16(ab_w_out)
    ssd_w_in_b, ssd_w_out_b = _to_bf16(ssd_w_in), _to_bf16(ssd_w_out)

    for layer in range(cfg.depth):
        i = layer // 2
        mod_l = mod[layer]
        g1 = norm1_g[layer].reshape(1, d)
        g2 = norm2_g[layer].reshape(1, d)
        router = _router_weights(moe_wg[layer], moe_bg[layer], moe_we[layer], moe_be[layer])
        if layer % 2 == 0:
            lam_init = 0.8 - 0.6 * math.exp(-0.3 * layer)
            q, k, v, gb, u = _ab_in(cfg, x_pair, mod_l, g1, ab_w_in_b, i, tables)
            lvec = jnp.stack([diff_lq1[i], diff_lk1[i], diff_lq2[i], diff_lk2[i]])
            o_pair = _attention(cfg, q, k, v, lvec, diff_subln_g[i].reshape(1, DIFF_V_DIM), lam_init)
            xa, h2, route, cnt = _ab_out(cfg, o_pair, gb, u, bconv_w[i], ab_w_out_b, i, x_pair, mod_l, g2, router)
        else:
            z, xbc_act, dt_raw = _ssd_in(cfg, xa, mod_l, g1, ssd_w_in_b, i, ssd_conv_w[i],
                                         ssd_conv_b[i].reshape(1, SSD_CONV_CH))
            y2 = _ssd_scan(cfg, xbc_act, dt_raw, ssd_dt_bias[i], ssd_A_log[i])
            d_wide = jnp.repeat(ssd_D[i], SSD_HEAD_DIM).reshape(1, SSD_D_INNER)
            xa, h2, route, cnt = _ssd_out(cfg, y2, xbc_act, z, d_wide, ssd_norm_g[i].reshape(1, SSD_D_INNER),
                                          ssd_w_out_b, i, xa, mod_l, g2, router)
        slot_tok, block_eid, n_used, slots = _dispatch(cfg, route, cnt)
        xb_parts = [h2[p_] for p_ in jnp.split(slot_tok, MOE_GATHER_PARTS)]
        yb = _experts(cfg, layer, block_eid, n_used, xb_parts, moe_w_gate, moe_w_up, moe_w_down)
        last = layer == cfg.depth - 1
        y12 = yb[slots]
        xa = _combine(cfg, xa, y12, route, mod_l, final_norm_g.reshape(1, d), last)
        x_pair = (xa, xa)
    return xa.reshape(b, cfg.n_lat, d)


def kernel(x, c, ctx, c_ctx, ada_w, ada_b, norm1_g, norm2_g, ab_w_in, ab_w_out, diff_lq1, diff_lk1, diff_lq2, diff_lk2, diff_subln_g, bconv_w, ssd_w_in, ssd_conv_w, ssd_conv_b, ssd_A_log, ssd_dt_bias, ssd_D, ssd_norm_g, ssd_w_out, moe_wg, moe_bg, moe_we, moe_be, moe_w_gate, moe_w_up, moe_w_down, final_norm_g):
    cfg = Cfg(batch=x.shape[0], n_lat=x.shape[1], n_ctx=ctx.shape[1], d_model=x.shape[2], depth=ada_w.shape[0],
              tm=512, tq=256, moe_tile=512, attn_subtiles=2, kchunk=1024)
    return _forward(cfg, x, c, ctx, c_ctx, ada_w, ada_b, norm1_g, norm2_g, ab_w_in, ab_w_out,
                    diff_lq1, diff_lk1, diff_lq2, diff_lk2, diff_subln_g, bconv_w,
                    ssd_w_in, ssd_conv_w, ssd_conv_b, ssd_A_log, ssd_dt_bias, ssd_D, ssd_norm_g, ssd_w_out,
                    moe_wg, moe_bg, moe_we, moe_be, moe_w_gate, moe_w_up, moe_w_down, final_norm_g)
```

```python
import functools
import math
from typing import NamedTuple

import jax
import jax.numpy as jnp
from jax import lax
from jax.experimental import pallas as pl
from jax.experimental.pallas import tpu as pltpu

F32 = jnp.float32
BF16 = jnp.bfloat16
HIGHEST = lax.Precision.HIGHEST

LANES_V7X = 128
SUBLANES_V7X = 8
VMEM_LIMIT_BYTES_V7X = 56 * 1024 * 1024

RMS_EPS = 1e-6
SOFTMAX_MIN_DENOMINATOR = 2.0 ** -60
GRID_W = 64
N_DIFF_HEADS = 4
DIFF_QK_DIM = 64
DIFF_V_DIM = 128
DIFF_WIDTH = 512
CONV_WIDTH = 512
ROPE_BASE = 10000.0
SSD_D_INNER = 2048
SSD_HEAD_DIM = 64
SSD_HEADS = 32
SSD_GROUPS = 4
SSD_STATE = 128
SSD_CHUNK = 128
SSD_CONV_CH = SSD_D_INNER + 2 * SSD_GROUPS * SSD_STATE
MOE_GROUPS = 4
MOE_EXPERTS_PER_GROUP = 8
MOE_EXPERTS = 32
MOE_HIDDEN = 512
ROUTE_LANES = LANES_V7X
MOE_GATHER_PARTS = 3


class Cfg(NamedTuple):
    batch: int
    n_lat: int
    n_ctx: int
    d_model: int
    depth: int
    tm: int
    tq: int
    moe_tile: int
    attn_subtiles: int
    kchunk: int

    @property
    def t_lat(self):
        return self.batch * self.n_lat

    @property
    def t_ctx(self):
        return self.batch * self.n_ctx

    @property
    def t_all(self):
        return self.t_lat + self.t_ctx


def _cparams(sem):
    return pltpu.CompilerParams(dimension_semantics=sem, vmem_limit_bytes=VMEM_LIMIT_BYTES_V7X)


def _silu(v):
    return v * (1.0 / (1.0 + jnp.exp(-v)))


def _const_spec(shape):
    nd = len(shape)
    return pl.BlockSpec(shape, lambda *_: (0,) * nd)


def _layer_spec(stacked, index):
    rest = stacked.shape[1:]
    return pl.BlockSpec((None,) + rest, lambda *_: (index,) + (0,) * len(rest))


def _mod_spec(cfg):
    return pl.BlockSpec((1, 6, cfg.d_model),
                        lambda i: (jnp.minimum((i * cfg.tm) // cfg.n_lat, cfg.batch), 0, 0))


def _seq_pos(cfg, tm):
    r0 = pl.program_id(0) * tm
    row = r0 + lax.broadcasted_iota(jnp.int32, (tm, 1), 0)
    seqlen = jnp.where(r0 >= cfg.t_lat, cfg.n_ctx, cfg.n_lat)
    return row & (seqlen - 1), seqlen


def _norm_mod(x, g, shift, scale):
    ms = jnp.mean(x * x, axis=-1, keepdims=True)
    return (x * lax.rsqrt(ms + RMS_EPS) * g) * (1.0 + scale) + shift


def _to_bf16(w):
    n = w.shape[-1]
    return jnp.pad(w.astype(BF16), ((0, 0), (0, 0), (0, -n % LANES_V7X)))


def _hi_lo(v):
    hi = v.astype(BF16)
    return hi, (v - hi.astype(F32)).astype(BF16)


def _adaln_kernel(c_ref, w_ref, b_ref, o_ref):
    s_hi, s_lo = _hi_lo(_silu(c_ref[...]))
    w_hi, w_lo = _hi_lo(w_ref[0])
    o_ref[0] = (jnp.dot(s_hi, w_hi, preferred_element_type=F32) + jnp.dot(s_hi, w_lo, preferred_element_type=F32)
                + jnp.dot(s_lo, w_hi, preferred_element_type=F32)) + b_ref[0]


def _adaln(cfg, c_all, ada_w, ada_b):
    d = cfg.d_model
    out = pl.pallas_call(
        _adaln_kernel,
        grid=(cfg.depth, 6),
        in_specs=[_const_spec((SUBLANES_V7X, d)),
                  pl.BlockSpec((1, d, d), lambda l, j: (l, 0, j)),
                  pl.BlockSpec((1, 1, d), lambda l, j: (l, 0, j))],
        out_specs=pl.BlockSpec((1, SUBLANES_V7X, d), lambda l, j: (l, 0, j)),
        out_shape=jax.ShapeDtypeStruct((cfg.depth, SUBLANES_V7X, 6 * d), F32),
        compiler_params=_cparams(("arbitrary", "arbitrary")),
        name="adaln",
    )(c_all, ada_w, ada_b.reshape(cfg.depth, 1, 6 * d))
    return out.reshape(cfg.depth, SUBLANES_V7X, 6, d)


def _rope_tables(cfg):
    n = cfg.n_lat
    rows = n // GRID_W
    row = jnp.broadcast_to(jnp.arange(rows, dtype=F32)[:, None], (rows, GRID_W)).reshape(n)
    col = jnp.broadcast_to(jnp.arange(GRID_W, dtype=F32)[None, :], (rows, GRID_W)).reshape(n)
    axis_dim = DIFF_QK_DIM // 2
    inv_freq = ROPE_BASE ** (-jnp.arange(0, axis_dim, 2, dtype=F32) / axis_dim)
    ang_r = row[:, None] * inv_freq
    ang_c = col[:, None] * inv_freq
    zeros = jnp.zeros_like(ang_r)
    cos64 = jnp.concatenate([jnp.cos(ang_r), jnp.cos(ang_r), jnp.cos(ang_c), jnp.cos(ang_c)], axis=1)
    sa64 = jnp.concatenate([zeros, jnp.sin(ang_r), zeros, jnp.sin(ang_c)], axis=1)
    sb64 = jnp.concatenate([-jnp.sin(ang_r), zeros, -jnp.sin(ang_c), zeros], axis=1)
    ident = cfg.tm
    cos = jnp.concatenate([jnp.tile(cos64, (1, 2)), jnp.ones((ident, LANES_V7X), F32)], axis=0)
    sa = jnp.concatenate([jnp.tile(sa64, (1, 2)), jnp.zeros((ident, LANES_V7X), F32)], axis=0)
    sb = jnp.concatenate([jnp.tile(sb64, (1, 2)), jnp.zeros((ident, LANES_V7X), F32)], axis=0)
    return cos, sa, sb


def _dual_row_specs(cfg, width, pair):
    lat_arr, ctx_arr = pair
    nl = cfg.t_lat // cfg.tm
    c0 = nl if ctx_arr is lat_arr else 0
    return (pl.BlockSpec((cfg.tm, width), lambda i: (jnp.minimum(i, nl - 1), 0)),
            pl.BlockSpec((cfg.tm, width), lambda i: (jnp.maximum(i - nl, 0) + c0, 0)))


def _dual_rows(cfg, lat_ref, ctx_ref):
    return jnp.where(pl.program_id(0) >= cfg.t_lat // cfg.tm, ctx_ref[...], lat_ref[...])


def _ab_in_kernel(xl_ref, xc_ref, mod_ref, g_ref, w_ref, cos_ref, sa_ref, sb_ref,
                  q_ref, k_ref, v_ref, gb_ref, u_ref, *, cfg):
    m = mod_ref[0]
    hb = _norm_mod(_dual_rows(cfg, xl_ref, xc_ref), g_ref[...], m[0:1], m[1:2]).astype(BF16)
    cos, sa, sb = cos_ref[...], sa_ref[...], sb_ref[...]

    def proj(j):
        return jnp.dot(hb, w_ref[:, j * DIFF_WIDTH:(j + 1) * DIFF_WIDTH], preferred_element_type=F32)

    def rope(t, out_ref, scale):
        for gi in range(DIFF_WIDTH // LANES_V7X):
            xg = t[:, gi * LANES_V7X:(gi + 1) * LANES_V7X]
            r = xg * cos + pltpu.roll(xg, 16, 1) * sa + pltpu.roll(xg, LANES_V7X - 16, 1) * sb
            out_ref[:, gi * LANES_V7X:(gi + 1) * LANES_V7X] = (r * scale).astype(out_ref.dtype)

    rope(proj(0), q_ref, math.log2(math.e) * DIFF_QK_DIM ** -0.5)
    rope(proj(1), k_ref, 1.0)
    v_ref[...] = proj(2).astype(v_ref.dtype)
    gb_ref[...] = proj(3)
    u_ref[...] = proj(4) * proj(5)


def _ab_in(cfg, x_pair, mod_l, g1, w_stack, w_index, tables):
    t, d, tm = cfg.t_all, cfg.d_model, cfg.tm
    n_lat_tiles, per_seq = cfg.t_lat // tm, cfg.n_lat // tm
    tab_spec = pl.BlockSpec((tm, LANES_V7X), lambda i: (jnp.where(i < n_lat_tiles, i % per_seq, per_seq), 0))
    row = lambda w: pl.BlockSpec((tm, w), lambda i: (i, 0))
    return pl.pallas_call(
        functools.partial(_ab_in_kernel, cfg=cfg),
        grid=(t // tm,),
        in_specs=[*_dual_row_specs(cfg, d, x_pair), _mod_spec(cfg), _const_spec((1, d)),
                  _layer_spec(w_stack, w_index), tab_spec, tab_spec, tab_spec],
        out_specs=[row(DIFF_WIDTH)] * 5,
        out_shape=[jax.ShapeDtypeStruct((t, DIFF_WIDTH), BF16)] * 3
        + [jax.ShapeDtypeStruct((t, CONV_WIDTH), F32)] * 2,
        compiler_params=_cparams(("arbitrary",)),
        name="ab_in",
    )(*x_pair, mod_l, g1, w_stack, *tables)


def _key_norm_kernel(kc_ref, kl_ref, o_ref):
    def comp_max(k_ref):
        k = k_ref[...].astype(F32)
        sq = k * k
        lane = lax.broadcasted_iota(jnp.int32, sq.shape, 1)
        n0 = jnp.sum(jnp.where(lane < DIFF_QK_DIM, sq, 0.0), axis=-1, keepdims=True)
        n1 = jnp.sum(jnp.where(lane >= DIFF_QK_DIM, sq, 0.0), axis=-1, keepdims=True)
        return jnp.sqrt(jnp.max(n0, axis=0, keepdims=True)), jnp.sqrt(jnp.max(n1, axis=0, keepdims=True))

    c0, c1 = comp_max(kc_ref)
    l0, l1 = comp_max(kl_ref)
    row = lax.broadcasted_iota(jnp.int32, o_ref.shape, 0)
    o_ref[...] = jnp.where(row == 0, jnp.maximum(c0, l0),
                           jnp.where(row == 1, jnp.maximum(c1, l1), jnp.where(row == 2, c0, c1)))


def _attn_body(lvec_ref, q_ref, kc_ref, kl_ref, vc_ref, vl_ref, kn_ref, g_ref, o_ref, *, tq, lam_init, kchunk):
    lv = lvec_ref[...]
    lam = (jnp.exp(jnp.sum(lv[0:1] * lv[1:2], axis=-1, keepdims=True))
           - jnp.exp(jnp.sum(lv[2:3] * lv[3:4], axis=-1, keepdims=True)) + lam_init)
    nt = (((1,), (1,)), ((), ()))
    half = DIFF_QK_DIM
    chunks = [(kc_ref, vc_ref, 0, kc_ref.shape[0])]
    if kl_ref is not None:
        chunks += [(kl_ref, vl_ref, r0, kchunk) for r0 in range(0, kl_ref.shape[0], kchunk)]
    kn = kn_ref[...]
    kn_row = 0 if kl_ref is not None else 2

    def tile(q2, bound):
        ps, ms, ls = [], [], []
        for k_ref, _, r0, n in chunks:
            s = lax.dot_general(q2, k_ref[r0:r0 + n, :], nt, preferred_element_type=F32)
            m_c = jnp.max(s, axis=-1, keepdims=True) if bound is None else bound
            p = jnp.exp2(s - m_c)
            ps.append(p)
            ms.append(m_c)
            ls.append(jnp.sum(p, axis=-1, keepdims=True))
        if bound is None:
            mx = functools.reduce(jnp.maximum, ms)
            scale = [jnp.exp2(m_c - mx) for m_c in ms]
            l = functools.reduce(jnp.add, [sc * l_c for sc, l_c in zip(scale, ls)])
        else:
            scale = [None] * len(ls)
            l = functools.reduce(jnp.add, ls)
        r = 1.0 / l
        pv = None
        for p, sc, (_, v_ref, r0, n) in zip(ps, scale, chunks):
            w = r if sc is None else sc * r
            pd = (p[:tq] * w[:tq] - p[tq:] * (lam * w[tq:])).astype(BF16)
            part = jnp.dot(pd, v_ref[r0:r0 + n, :], preferred_element_type=F32)
            pv = part if pv is None else pv + part
        return pv, l

    def emit(j, pv):
        o = pv * lax.rsqrt(jnp.mean(pv * pv, axis=-1, keepdims=True) + RMS_EPS)
        o_ref[j * tq:(j + 1) * tq, :] = ((o * g_ref[...]) * (1.0 - lam_init)).astype(o_ref.dtype)

    q2s, smallest = [], None
    for j in range(q_ref.shape[0] // tq):
        q = q_ref[j * tq:(j + 1) * tq, :].astype(F32)
        lane = lax.broadcasted_iota(jnp.int32, q.shape, 1)
        q0, q1 = jnp.where(lane < half, q, 0.0), jnp.where(lane >= half, q, 0.0)
        q2 = jnp.concatenate([q0, q1], axis=0).astype(BF16)
        qn = jnp.sqrt(jnp.concatenate([jnp.sum(q0 * q0, axis=-1, keepdims=True),
                                       jnp.sum(q1 * q1, axis=-1, keepdims=True)], axis=0))
        kmax = jnp.concatenate([jnp.broadcast_to(kn[kn_row:kn_row + 1, 0:1], (tq, 1)),
                                jnp.broadcast_to(kn[kn_row + 1:kn_row + 2, 0:1], (tq, 1))], axis=0)
        pv, l = tile(q2, qn * kmax)
        emit(j, pv)
        q2s.append(q2)
        lmin = jnp.min(l)
        smallest = lmin if smallest is None else jnp.minimum(smallest, lmin)

    @pl.when(jnp.logical_not(smallest > SOFTMAX_MIN_DENOMINATOR))
    def _():
        for j, q2 in enumerate(q2s):
            emit(j, tile(q2, None)[0])


def _attn_lat_kernel(lvec_ref, q_ref, kc_ref, kl_ref, vc_ref, vl_ref, kn_ref, g_ref, o_ref, **kw):
    _attn_body(lvec_ref, q_ref, kc_ref, kl_ref, vc_ref, vl_ref, kn_ref, g_ref, o_ref, **kw)


def _attn_ctx_kernel(lvec_ref, q_ref, kc_ref, vc_ref, kn_ref, g_ref, o_ref, **kw):
    _attn_body(lvec_ref, q_ref, kc_ref, None, vc_ref, None, kn_ref, g_ref, o_ref, **kw)


def _attention(cfg, q, k, v, lvec, subln_g, lam_init):
    tq, nsub = cfg.tq, cfg.attn_subtiles
    tstep = tq * nsub
    nqb = cfg.n_lat // tstep
    ctx_blk0 = cfg.t_lat // cfg.n_ctx
    hw = DIFF_V_DIM
    ctx_spec = pl.BlockSpec((cfg.n_ctx, hw), lambda b, h, *_: (ctx_blk0 + b, h))
    lat_spec = pl.BlockSpec((cfg.n_lat, hw), lambda b, h, *_: (b, h))
    q_spec = pl.BlockSpec((tstep, hw), lambda b, h, i: (b * nqb + i, h))
    kw = dict(tq=tq, lam_init=lam_init, kchunk=min(cfg.kchunk, cfg.n_lat))
    kn_spec = pl.BlockSpec((SUBLANES_V7X, LANES_V7X), lambda b, h, *_: (b * N_DIFF_HEADS + h, 0))
    key_norms = pl.pallas_call(
        _key_norm_kernel,
        grid=(cfg.batch, N_DIFF_HEADS),
        in_specs=[ctx_spec, lat_spec],
        out_specs=kn_spec,
        out_shape=jax.ShapeDtypeStruct((cfg.batch * N_DIFF_HEADS * SUBLANES_V7X, LANES_V7X), F32),
        compiler_params=_cparams(("arbitrary", "arbitrary")),
        name="key_norms",
    )(k, k)
    o_lat = pl.pallas_call(
        functools.partial(_attn_lat_kernel, **kw),
        grid=(cfg.batch, N_DIFF_HEADS, nqb),
        in_specs=[_const_spec(lvec.shape), q_spec, ctx_spec, lat_spec, ctx_spec, lat_spec, kn_spec,
                  _const_spec((1, hw))],
        out_specs=q_spec,
        out_shape=jax.ShapeDtypeStruct((cfg.t_lat, DIFF_WIDTH), BF16),
        compiler_params=_cparams(("arbitrary", "arbitrary", "arbitrary")),
        name="diff_attn",
    )(lvec, q, k, k, v, v, key_norms, subln_g)
    tqc = min(tq, cfg.n_ctx)
    o_ctx = pl.pallas_call(
        functools.partial(_attn_ctx_kernel, tq=tqc, lam_init=lam_init, kchunk=cfg.n_ctx),
        grid=(cfg.batch, N_DIFF_HEADS),
        in_specs=[_const_spec(lvec.shape), ctx_spec, ctx_spec, ctx_spec, kn_spec, _const_spec((1, hw))],
        out_specs=pl.BlockSpec((cfg.n_ctx, hw), lambda b, h: (b, h)),
        out_shape=jax.ShapeDtypeStruct((cfg.t_ctx, DIFF_WIDTH), BF16),
        compiler_params=_cparams(("arbitrary", "arbitrary")),
        name="diff_attn_ctx",
    )(lvec, q, k, v, key_norms, subln_g)
    return o_lat, o_ctx


def _route_from_logits(lg):
    lane = lax.broadcasted_iota(jnp.int32, lg.shape, 1)
    neg = -jnp.inf
    big = jnp.int32(ROUTE_LANES)
    gl = jnp.where(lane < MOE_GROUPS, lg, neg)
    gmax = jnp.max(gl, axis=-1, keepdims=True)
    gidx = jnp.min(jnp.where(gl == gmax, lane, big), axis=-1, keepdims=True)
    gw = 1.0 / jnp.sum(jnp.exp(gl - gmax), axis=-1, keepdims=True)
    lo = MOE_GROUPS + gidx * MOE_EXPERTS_PER_GROUP
    el = jnp.where((lane >= lo) & (lane < lo + MOE_EXPERTS_PER_GROUP), lg, neg)
    m1 = jnp.max(el, axis=-1, keepdims=True)
    i1 = jnp.min(jnp.where(el == m1, lane, big), axis=-1, keepdims=True)
    el2 = jnp.where(lane == i1, neg, el)
    m2 = jnp.max(el2, axis=-1, keepdims=True)
    i2 = jnp.min(jnp.where(el2 == m2, lane, big), axis=-1, keepdims=True)
    w1 = gw / (1.0 + jnp.exp(m2 - m1))
    w2 = gw - w1
    e1 = (i1 - MOE_GROUPS).astype(F32)
    e2 = (i2 - MOE_GROUPS).astype(F32)
    return jnp.where(lane == 0, e1, jnp.where(lane == 1, e2, jnp.where(lane == 2, w1, jnp.where(lane == 3, w2, 0.0))))


def _pack_bf16_pairs(v):
    w = v.shape[1] // 2
    lo = pltpu.bitcast(v[:, :w].astype(BF16).astype(F32), jnp.uint32)
    hi = pltpu.bitcast(v[:, w:].astype(BF16).astype(F32), jnp.uint32)
    return pltpu.bitcast((hi & jnp.uint32(0xFFFF0000)) | (lo >> 16), F32)


def _unpack_bf16_pairs(words):
    u = pltpu.bitcast(words, jnp.uint32)
    lo = pltpu.bitcast(u << 16, F32)
    hi = pltpu.bitcast(u & jnp.uint32(0xFFFF0000), F32)
    return lo, hi


def _router_weights(wg, bg, we, be):
    d = wg.shape[0]
    pad = ROUTE_LANES - MOE_GROUPS - MOE_EXPERTS
    wr = jnp.concatenate([wg, we, jnp.zeros((d, pad), F32)], axis=1)
    hi, lo = _hi_lo(wr)
    br = jnp.concatenate([bg, be, jnp.zeros((pad,), F32)]).reshape(1, ROUTE_LANES)
    return jnp.concatenate([hi, lo], axis=1), hi, br


def _expert_onehots(rt):
    lane = lax.broadcasted_iota(jnp.int32, rt.shape, 1)
    return (jnp.where(lane == rt[:, 0:1].astype(jnp.int32), 1.0, 0.0),
            jnp.where(lane == rt[:, 1:2].astype(jnp.int32), 1.0, 0.0))


def _residual_norm_route(x, mix, m, g2, wr2_ref, wrh_ref, br_ref, xo_ref, h2_ref, rt_ref, cnt_ref):
    xn = x + m[2:3] * mix
    xo_ref[...] = xn
    h2 = _norm_mod(xn, g2, m[3:4], m[4:5])
    h2_ref[...] = _pack_bf16_pairs(h2)
    hh, hl = _hi_lo(h2)
    a = jnp.dot(hh, wr2_ref[...], preferred_element_type=F32)
    lg = (a[:, :ROUTE_LANES] + a[:, ROUTE_LANES:]) + jnp.dot(hl, wrh_ref[...], preferred_element_type=F32)
    rt = _route_from_logits(lg + br_ref[...])
    rt_ref[...] = rt

    @pl.when(pl.program_id(0) == 0)
    def _():
        cnt_ref[...] = jnp.zeros_like(cnt_ref)

    o1, o2 = _expert_onehots(rt)
    cnt_ref[...] += jnp.sum(o1 + o2, axis=0, keepdims=True)


def _epilogue_specs(cfg, x_pair):
    d, tm = cfg.d_model, cfg.tm
    row = lambda w: pl.BlockSpec((tm, w), lambda i: (i, 0))
    in_specs = [*_dual_row_specs(cfg, d, x_pair), _mod_spec(cfg), _const_spec((1, d)),
                _const_spec((d, 2 * ROUTE_LANES)),
                _const_spec((d, ROUTE_LANES)), _const_spec((1, ROUTE_LANES))]
    out_specs = [row(d), row(d // 2), row(ROUTE_LANES), _const_spec((SUBLANES_V7X, ROUTE_LANES))]
    out_shape = [jax.ShapeDtypeStruct((cfg.t_all, d), F32), jax.ShapeDtypeStruct((cfg.t_all, d // 2), F32),
                 jax.ShapeDtypeStruct((cfg.t_all, ROUTE_LANES), F32),
                 jax.ShapeDtypeStruct((SUBLANES_V7X, ROUTE_LANES), F32)]
    return in_specs, out_specs, out_shape


def _halo_specs(cfg, width, col_block=0):
    per = cfg.tm // SUBLANES_V7X
    last = cfg.t_all // SUBLANES_V7X - 1
    prev = pl.BlockSpec((SUBLANES_V7X, width), lambda i, *_: (jnp.maximum(i * per - 1, 0), col_block))
    nxt = pl.BlockSpec((SUBLANES_V7X, width), lambda i, *_: (jnp.minimum((i + 1) * per, last), col_block))
    return prev, nxt


def _shifted_rows(pad_ref, u, prev_blk, next_blk, pos, seqlen):
    tm = u.shape[0]
    s = SUBLANES_V7X
    pad_ref[s:s + tm, :] = u
    pad_ref[0:s, :] = prev_blk
    pad_ref[s + tm:2 * s + tm, :] = next_blk
    um1 = jnp.where(pos == 0, 0.0, pad_ref[s - 1:s - 1 + tm, :])
    up1 = jnp.where(pos == seqlen - 1, 0.0, pad_ref[s + 1:s + 1 + tm, :])
    return um1, up1


def _ab_out_kernel(ol_ref, oc_ref, gb_ref, u_ref, up_ref, un_ref, cw_ref, wo_ref,
                   xl_ref, xc_ref, mod_ref, g2_ref, wr2_ref, wrh_ref, br_ref,
                   xo_ref, h2_ref, rt_ref, cnt_ref, pad_ref, *, cfg):
    pos, seqlen = _seq_pos(cfg, cfg.tm)
    u = u_ref[...]
    um1, up1 = _shifted_rows(pad_ref, u, up_ref[...], un_ref[...], pos, seqlen)
    cw = cw_ref[...]
    conv = gb_ref[...] * (um1 * cw[0:1] + u * cw[1:2] + up1 * cw[2:3])
    mix = (jnp.dot(_dual_rows(cfg, ol_ref, oc_ref), wo_ref[:DIFF_WIDTH, :], preferred_element_type=F32)
           + jnp.dot(conv.astype(BF16), wo_ref[DIFF_WIDTH:, :], preferred_element_type=F32))
    _residual_norm_route(_dual_rows(cfg, xl_ref, xc_ref), mix, mod_ref[0], g2_ref[...], wr2_ref, wrh_ref, br_ref,
                         xo_ref, h2_ref, rt_ref, cnt_ref)


def _ab_out(cfg, o_pair, gb, u, conv_w, wo_stack, w_index, x_pair, mod_l, g2, router):
    tm = cfg.tm
    row = lambda w: pl.BlockSpec((tm, w), lambda i: (i, 0))
    prev, nxt = _halo_specs(cfg, CONV_WIDTH)
    ep_in, ep_out, ep_shape = _epilogue_specs(cfg, x_pair)
    return pl.pallas_call(
        functools.partial(_ab_out_kernel, cfg=cfg),
        grid=(cfg.t_all // tm,),
        in_specs=[*_dual_row_specs(cfg, DIFF_WIDTH, o_pair), row(CONV_WIDTH), row(CONV_WIDTH), prev, nxt,
                  _const_spec(conv_w.shape),
                  _layer_spec(wo_stack, w_index)] + ep_in,
        out_specs=ep_out,
        out_shape=ep_shape,
        scratch_shapes=[pltpu.VMEM((tm + 2 * SUBLANES_V7X, CONV_WIDTH), F32)],
        compiler_params=_cparams(("arbitrary",)),
        name="ab_out",
    )(*o_pair, gb, u, u, u, conv_w, wo_stack, *x_pair, mod_l, g2, *router)


def _ssd_in_kernel(x_ref, xp_ref, xn_ref, mod_ref, g_ref, w_ref, cw_ref, cb_ref,
                   z_ref, xbc_ref, dt_ref, pad_ref, *, cfg):
    tm = cfg.tm
    s = SUBLANES_V7X
    m = mod_ref[0]
    g = g_ref[...]
    h = _norm_mod(x_ref[...], g, m[0:1], m[1:2])
    hb = h.astype(BF16)
    h_ext = jnp.concatenate([_norm_mod(xp_ref[...], g, m[0:1], m[1:2]), h,
                             _norm_mod(xn_ref[...], g, m[0:1], m[1:2])], axis=0).astype(BF16)
    pos, seqlen = _seq_pos(cfg, tm)
    cw = 512
    for j in range(SSD_D_INNER // cw):
        z_ref[:, j * cw:(j + 1) * cw] = jnp.dot(hb, w_ref[:, j * cw:(j + 1) * cw],
                                                preferred_element_type=F32).astype(z_ref.dtype)
    for j in range(SSD_CONV_CH // cw):
        c0 = SSD_D_INNER + j * cw
        pad = pad_ref.at[j]
        pad[...] = jnp.dot(h_ext, w_ref[:, c0:c0 + cw], preferred_element_type=F32)
        taps = cw_ref[:, j * cw:(j + 1) * cw]
        um1 = jnp.where(pos == 0, 0.0, pad[s - 1:s - 1 + tm, :])
        up1 = jnp.where(pos == seqlen - 1, 0.0, pad[s + 1:s + 1 + tm, :])
        conv = um1 * taps[0:1] + pad[s:s + tm, :] * taps[1:2] + up1 * taps[2:3] + cb_ref[:, j * cw:(j + 1) * cw]
        xbc_ref[:, j * cw:(j + 1) * cw] = _silu(conv).astype(xbc_ref.dtype)
    dt_ref[...] = jnp.dot(hb, w_ref[:, SSD_D_INNER + SSD_CONV_CH:], preferred_element_type=F32)


def _ssd_in(cfg, x, mod_l, g1, w_stack, w_index, conv_w, conv_b):
    t, d, tm = cfg.t_all, cfg.d_model, cfg.tm
    row = lambda w: pl.BlockSpec((tm, w), lambda i: (i, 0))
    prev, nxt = _halo_specs(cfg, d)
    return pl.pallas_call(
        functools.partial(_ssd_in_kernel, cfg=cfg),
        grid=(t // tm,),
        in_specs=[row(d), prev, nxt, _mod_spec(cfg), _const_spec((1, d)), _layer_spec(w_stack, w_index),
                  _const_spec(conv_w.shape), _const_spec(conv_b.shape)],
        out_specs=[row(SSD_D_INNER), row(SSD_CONV_CH), row(LANES_V7X)],
        out_shape=[jax.ShapeDtypeStruct((t, SSD_D_INNER), BF16), jax.ShapeDtypeStruct((t, SSD_CONV_CH), BF16),
                   jax.ShapeDtypeStruct((t, LANES_V7X), F32)],
        scratch_shapes=[pltpu.VMEM((SSD_CONV_CH // 512, tm + 2 * SUBLANES_V7X, 512), F32)],
        compiler_params=_cparams(("arbitrary",)),
        name="ssd_in",
    )(x, x, x, mod_l, g1, w_stack, conv_w, conv_b)


def _split3(v):
    b1 = v.astype(BF16)
    r1 = v - b1.astype(F32)
    b2 = r1.astype(BF16)
    b3 = (r1 - b2.astype(F32)).astype(BF16)
    return jnp.concatenate([b1, b2, b3], axis=1)


def _softplus(v):
    return jnp.maximum(v, 0.0) + jnp.log(1.0 + jnp.exp(-jnp.abs(v)))


def _ssd_scan_kernel(*refs):
    n_in = 5
    fwd_in, bwd_in = refs[:n_in], refs[n_in:2 * n_in]
    bias_ref, biast_ref, alog_ref, alogt_ref, ewide_ref, yf_ref, yb_ref, sf_ref, sb_ref = refs[2 * n_in:]

    @pl.when(pl.program_id(1) == 0)
    def _():
        sf_ref[...] = jnp.zeros_like(sf_ref)
        sb_ref[...] = jnp.zeros_like(sb_ref)

    for d, (ins, y_ref, state_ref) in enumerate(((fwd_in, yf_ref, sf_ref), (bwd_in, yb_ref, sb_ref))):
        _ssd_chunk(d == 0, *ins, bias_ref[d], biast_ref[d], alog_ref[d], alogt_ref[d], ewide_ref, y_ref, state_ref)


def _ssd_chunk(fwd, xs_ref, b_ref, c_ref, dt_ref, dtt_ref, bias, biast, alog, alogt, ewide_ref, y_ref, state_ref):
    q = SSD_CHUNK
    dt = _softplus(dt_ref[0] + bias)
    dtt = _softplus(dtt_ref[0] + biast)
    a = dt * (-jnp.exp(alog))
    at = dtt * (-jnp.exp(alogt))

    ri = lax.broadcasted_iota(jnp.int32, (q, q), 0)
    ci = lax.broadcasted_iota(jnp.int32, (q, q), 1)
    keep = (ci <= ri) if fwd else (ci >= ri)
    tri = jnp.where(keep, 1.0, 0.0)
    cs = jnp.dot(tri, a, precision=HIGHEST, preferred_element_type=F32)
    cst = lax.dot_general(at, tri, (((1,), (1,)), ((), ())), precision=HIGHEST,
                          preferred_element_type=F32)
    tot = jnp.sum(a, axis=0, keepdims=True)

    ewide = ewide_ref[...]
    w_wide = jnp.dot(_split3(dt * jnp.exp(tot - cs)), ewide, preferred_element_type=F32)
    ecs_wide = jnp.dot(_split3(jnp.exp(cs)), ewide, preferred_element_type=F32)
    dec_wide = jnp.dot(_split3(jnp.broadcast_to(jnp.exp(tot), (SUBLANES_V7X, SSD_HEADS))), ewide,
                       preferred_element_type=F32)[0:1]

    lane = lax.broadcasted_iota(jnp.int32, (q, LANES_V7X), 1)
    lo = lane < SSD_HEAD_DIM
    hpg = SSD_HEADS // SSD_GROUPS
    gw = hpg * SSD_HEAD_DIM
    nt = (((1,), (1,)), ((), ()))
    tn = (((0,), (0,)), ((), ()))
    for g in range(SSD_GROUPS):
        bm = b_ref[:, g * SSD_STATE:(g + 1) * SSD_STATE]
        cm = c_ref[:, g * SSD_STATE:(g + 1) * SSD_STATE]
        cb = lax.dot_general(cm, bm, nt, preferred_element_type=F32)
        sl = slice(g * gw, (g + 1) * gw)
        y_off = jnp.dot(cm, state_ref[:, sl].astype(BF16), preferred_element_type=F32) * ecs_wide[:, sl]
        for pr in range(hpg // 2):
            lhs, rhs = [], []
            c0 = g * gw + pr * LANES_V7X
            xs_pair = xs_ref[:, c0:c0 + LANES_V7X]
            zero = jnp.zeros_like(xs_pair)
            for k in range(2):
                h = g * hpg + pr * 2 + k
                seg = cs[:, h:h + 1] - cst[h:h + 1, :]
                lmat = jnp.exp(jnp.where(keep, seg, -jnp.inf))
                lhs.append((cb * lmat * dtt[h:h + 1, :]).astype(BF16))
                rhs.append(jnp.where(lo if k == 0 else jnp.logical_not(lo), xs_pair, zero))
            y = jnp.dot(jnp.concatenate(lhs, axis=1), jnp.concatenate(rhs, axis=0), preferred_element_type=F32)
            y_ref[:, c0:c0 + LANES_V7X] = (y + y_off[:, pr * LANES_V7X:(pr + 1) * LANES_V7X]).astype(y_ref.dtype)
        x2 = (xs_ref[:, sl].astype(F32) * w_wide[:, sl]).astype(BF16)
        upd = lax.dot_general(bm, x2, tn, preferred_element_type=F32)
        state_ref[:, sl] = state_ref[:, sl] * dec_wide[:, sl] + upd


def _ssd_scan(cfg, xbc_act, dt_raw, dt_bias, a_log):
    q = SSD_CHUNK
    n_cc, n_lc = cfg.n_ctx // q, cfg.n_lat // q
    ctx0 = cfg.t_lat // q
    h = SSD_HEADS
    dt2 = dt_raw[:, :2 * h].reshape(cfg.t_all, 2, h).transpose(1, 0, 2)
    dt2t = dt2.transpose(0, 2, 1)
    bias = dt_bias.reshape(2, 1, h)
    biast = dt_bias.reshape(2, h, 1)
    alog = a_log.reshape(2, 1, h)
    alogt = a_log.reshape(2, h, 1)
    head_of_row = jnp.tile(jnp.arange(h), 3)[:, None]
    ewide = (head_of_row == (jnp.arange(SSD_D_INNER) // SSD_HEAD_DIM)[None, :]).astype(BF16)

    def blk(d):
        def index(b, s):
            cs = s if d == 0 else n_cc - 1 - s
            ls = s - n_cc if d == 0 else n_lc - 1 - (s - n_cc)
            return jnp.where(s < n_cc, ctx0 + b * n_cc + cs, b * n_lc + ls)
        return index

    gs = SSD_GROUPS * SSD_STATE

    def dir_specs(d):
        at = blk(d)
        col = lambda w, cblk: pl.BlockSpec((q, w), lambda b, s: (at(b, s), cblk))
        return [col(SSD_D_INNER, 0), col(gs, SSD_D_INNER // gs), col(gs, SSD_D_INNER // gs + 1),
                pl.BlockSpec((1, q, h), lambda b, s: (d, at(b, s), 0)),
                pl.BlockSpec((1, h, q), lambda b, s: (d, 0, at(b, s)))]

    dir_args = (xbc_act, xbc_act, xbc_act, dt2, dt2t)
    y_spec = lambda d: pl.BlockSpec((q, SSD_D_INNER), lambda b, s: (blk(d)(b, s), 0))
    return pl.pallas_call(
        _ssd_scan_kernel,
        grid=(cfg.batch, n_cc + n_lc),
        in_specs=dir_specs(0) + dir_specs(1) + [_const_spec(bias.shape), _const_spec(biast.shape),
                                                _const_spec(alog.shape), _const_spec(alogt.shape),
                                                _const_spec(ewide.shape)],
        out_specs=[y_spec(0), y_spec(1)],
        out_shape=[jax.ShapeDtypeStruct((cfg.t_all, SSD_D_INNER), BF16)] * 2,
        scratch_shapes=[pltpu.VMEM((SSD_STATE, SSD_D_INNER), F32)] * 2,
        compiler_params=_cparams(("arbitrary", "arbitrary")),
        name="ssd_scan",
    )(*dir_args, *dir_args, bias, biast, alog, alogt, ewide)


def _ssd_out_kernel(yf_ref, yb_ref, xs_ref, z_ref, dw_ref, ng_ref, wo_ref,
                    xl_ref, xc_ref, mod_ref, g2_ref, wr2_ref, wrh_ref, br_ref, xo_ref, h2_ref, rt_ref, cnt_ref,
                    *, cfg):
    gw = SSD_D_INNER // SSD_GROUPS
    mix = None
    for g in range(SSD_GROUPS):
        sl = slice(g * gw, (g + 1) * gw)
        y = yf_ref[:, sl].astype(F32) + yb_ref[:, sl].astype(F32) + xs_ref[:, sl].astype(F32) * dw_ref[:, sl]
        y = y * _silu(z_ref[:, sl].astype(F32))
        y = (y * lax.rsqrt(jnp.mean(y * y, axis=-1, keepdims=True) + RMS_EPS)) * ng_ref[:, sl]
        part = jnp.dot(y.astype(BF16), wo_ref[sl, :], preferred_element_type=F32)
        mix = part if mix is None else mix + part
    _residual_norm_route(_dual_rows(cfg, xl_ref, xc_ref), mix, mod_ref[0], g2_ref[...], wr2_ref, wrh_ref, br_ref,
                         xo_ref, h2_ref, rt_ref, cnt_ref)


def _ssd_out(cfg, y2, xbc_act, z, d_wide, norm_g, wo_stack, w_index, x, mod_l, g2, router):
    tm = cfg.tm
    row = lambda w: pl.BlockSpec((tm, w), lambda i: (i, 0))
    ep_in, ep_out, ep_shape = _epilogue_specs(cfg, (x, x))
    return pl.pallas_call(
        functools.partial(_ssd_out_kernel, cfg=cfg),
        grid=(cfg.t_all // tm,),
        in_specs=[row(SSD_D_INNER)] * 4 + [_const_spec((1, SSD_D_INNER)),
                  _const_spec((1, SSD_D_INNER)), _layer_spec(wo_stack, w_index)] + ep_in,
        out_specs=ep_out,
        out_shape=ep_shape,
        compiler_params=_cparams(("arbitrary",)),
        name="ssd_out",
    )(*y2, xbc_act, z, d_wide, norm_g, wo_stack, x, x, mod_l, g2, *router)


def _expert_kernel(eid_ref, nused_ref, *refs, blocks_per_part):
    xb_refs = refs[:MOE_GATHER_PARTS]
    wg_ref, wu_ref, wd_ref, y_ref, wg_s, wu_s, wd_s = refs[MOE_GATHER_PARTS:]
    i = pl.program_id(0)
    changed = jnp.logical_or(i == 0, eid_ref[i] != eid_ref[jnp.maximum(i - 1, 0)])

    @pl.when(jnp.logical_and(changed, i < nused_ref[0]))
    def _():
        wg_s[...] = wg_ref[0, 0].astype(BF16)
        wu_s[...] = wu_ref[0, 0].astype(BF16)
        wd_s[...] = wd_ref[0, 0].astype(BF16)

    @pl.when(i < nused_ref[0])
    def _():
        half = wg_s.shape[0] // 2
        words = xb_refs[0][...]
        for k in range(1, MOE_GATHER_PARTS):
            words = jnp.where(i >= k * blocks_per_part, xb_refs[k][...], words)
        lo, hi = _unpack_bf16_pairs(words)
        lo, hi = lo.astype(BF16), hi.astype(BF16)
        hg = (jnp.dot(lo, wg_s[:half, :], preferred_element_type=F32)
              + jnp.dot(hi, wg_s[half:, :], preferred_element_type=F32))
        hu = (jnp.dot(lo, wu_s[:half, :], preferred_element_type=F32)
              + jnp.dot(hi, wu_s[half:, :], preferred_element_type=F32))
        y = jnp.dot((_silu(hg) * hu).astype(BF16), wd_s[...], preferred_element_type=F32)
        y_ref[...] = _pack_bf16_pairs(y)

    @pl.when(i >= nused_ref[0])
    def _():
        y_ref[...] = jnp.zeros_like(y_ref)


def _experts(cfg, layer, block_eid, n_used, xb_parts, w_gate, w_up, w_down):
    d, tile = cfg.d_model, cfg.moe_tile
    per = xb_parts[0].shape[0] // tile
    n_blocks = per * MOE_GATHER_PARTS

    def part_spec(k):
        return pl.BlockSpec((tile, d // 2), lambda i, e, n: (jnp.clip(i - k * per, 0, per - 1), 0))

    grid_spec = pltpu.PrefetchScalarGridSpec(
        num_scalar_prefetch=2,
        grid=(n_blocks,),
        in_specs=[part_spec(k) for k in range(MOE_GATHER_PARTS)] + [
                  pl.BlockSpec((1, 1, d, MOE_HIDDEN), lambda i, e, n: (layer, e[i], 0, 0)),
                  pl.BlockSpec((1, 1, d, MOE_HIDDEN), lambda i, e, n: (layer, e[i], 0, 0)),
                  pl.BlockSpec((1, 1, MOE_HIDDEN, d), lambda i, e, n: (layer, e[i], 0, 0))],
        out_specs=pl.BlockSpec((tile, d // 2), lambda i, e, n: (i, 0)),
        scratch_shapes=[pltpu.VMEM((d, MOE_HIDDEN), BF16), pltpu.VMEM((d, MOE_HIDDEN), BF16),
                        pltpu.VMEM((MOE_HIDDEN, d), BF16)],
    )
    return pl.pallas_call(
        functools.partial(_expert_kernel, blocks_per_part=per),
        grid_spec=grid_spec,
        out_shape=jax.ShapeDtypeStruct((n_blocks * tile, d // 2), F32),
        compiler_params=_cparams(("arbitrary",)),
        name="moe_experts",
    )(block_eid, n_used, *xb_parts, w_gate, w_up, w_down)


def _dispatch_kernel(rt_ref, cnt_ref, pos_ref, carry_ref, start_ref, *, tile):
    tm = rt_ref.shape[0]
    rt = rt_ref[...]
    lane = lax.broadcasted_iota(jnp.int32, rt.shape, 1)
    o1, o2 = _expert_onehots(rt)
    cnt1 = jnp.sum(o1, axis=0, keepdims=True)
    cnt2 = jnp.sum(o2, axis=0, keepdims=True)

    @pl.when(pl.program_id(0) == 0)
    def _():
        padded = jnp.floor((cnt_ref[...] + (tile - 1)) * (1.0 / tile)) * tile
        r = lax.broadcasted_iota(jnp.int32, (ROUTE_LANES, ROUTE_LANES), 0)
        c = lax.broadcasted_iota(jnp.int32, (ROUTE_LANES, ROUTE_LANES), 1)
        excl = jnp.where(r < c, 1.0, 0.0)
        start_ref[...] = jnp.dot(padded, excl, precision=HIGHEST, preferred_element_type=F32)
        carry_ref[...] = jnp.zeros_like(carry_ref)

    ri = lax.broadcasted_iota(jnp.int32, (tm, tm), 0)
    ci = lax.broadcasted_iota(jnp.int32, (tm, tm), 1)
    earlier = jnp.where(ci < ri, 1.0, 0.0).astype(BF16)
    p1 = jnp.dot(earlier, o1.astype(BF16), preferred_element_type=F32)
    p2 = jnp.dot(earlier, o2.astype(BF16), preferred_element_type=F32)
    base = start_ref[0:1] + carry_ref[0:1]
    pos1 = jnp.sum(o1 * (base + p1), axis=-1, keepdims=True)
    pos2 = jnp.sum(o2 * (base + cnt1 + p2), axis=-1, keepdims=True)
    cols = jnp.where(lane == 0, pos1, jnp.where(lane == 1, pos2, 0.0))
    r = lax.broadcasted_iota(jnp.int32, (SUBLANES_V7X, ROUTE_LANES), 0)
    c = lax.broadcasted_iota(jnp.int32, (SUBLANES_V7X, ROUTE_LANES), 1)
    pick = jnp.where(r == c, 1.0, 0.0)
    rows = lax.dot_general(pick, cols, (((1,), (1,)), ((), ())), precision=HIGHEST, preferred_element_type=F32)
    pos_ref[...] = rows.astype(jnp.int32)
    carry_ref[...] += cnt1 + cnt2


def _dispatch(cfg, route, cnt):
    tile, tm = cfg.moe_tile, cfg.tm
    t = cfg.t_all
    a_total = 2 * t
    pos = pl.pallas_call(
        functools.partial(_dispatch_kernel, tile=tile),
        grid=(t // tm,),
        in_specs=[pl.BlockSpec((tm, ROUTE_LANES), lambda i: (i, 0)), _const_spec((SUBLANES_V7X, ROUTE_LANES))],
        out_specs=pl.BlockSpec((SUBLANES_V7X, tm), lambda i: (0, i)),
        out_shape=jax.ShapeDtypeStruct((SUBLANES_V7X, t), jnp.int32),
        scratch_shapes=[pltpu.VMEM((SUBLANES_V7X, ROUTE_LANES), F32), pltpu.VMEM((SUBLANES_V7X, ROUTE_LANES), F32)],
        compiler_params=_cparams(("arbitrary",)),
        name="moe_dispatch",
    )(route, cnt)
    counts = cnt[0, :MOE_EXPERTS].astype(jnp.int32)
    pend = jnp.cumsum((counts + tile - 1) // tile * tile)
    n_blocks = (a_total + MOE_EXPERTS * (tile - 1)) // tile
    n_blocks = -(-n_blocks // MOE_GATHER_PARTS) * MOE_GATHER_PARTS
    block_eid = jnp.minimum(jnp.sum(pend[None, :] <= (jnp.arange(n_blocks, dtype=jnp.int32) * tile)[:, None], axis=1),
                            MOE_EXPERTS - 1).astype(jnp.int32)
    n_used = (pend[-1] // tile).astype(jnp.int32).reshape(1)
    slots = pos[0:2].reshape(-1)
    slot_tok = (jnp.arange(n_blocks * tile, dtype=jnp.int32) % t).at[slots].set(
        jnp.arange(a_total, dtype=jnp.int32) % t, unique_indices=True, mode="promise_in_bounds")
    return slot_tok, block_eid, n_used, slots


def _combine_kernel(x_ref, y1_ref, y2_ref, rt_ref, mod_ref, g_ref, o_ref, *, final):
    rt = rt_ref[...]
    a_lo, a_hi = _unpack_bf16_pairs(y1_ref[...])
    b_lo, b_hi = _unpack_bf16_pairs(y2_ref[...])
    w1, w2 = rt[:, 2:3], rt[:, 3:4]
    f = jnp.concatenate([w1 * a_lo + w2 * b_lo, w1 * a_hi + w2 * b_hi], axis=1)
    xn = x_ref[...] + mod_ref[0][5:6] * f
    if final:
        xn = (xn * lax.rsqrt(jnp.mean(xn * xn, axis=-1, keepdims=True) + RMS_EPS)) * g_ref[...]
    o_ref[...] = xn


def _combine(cfg, x, y12, route, mod_l, g, final):
    d, tm = cfg.d_model, cfg.tm
    n_tiles = cfg.t_all // tm
    row = lambda w: pl.BlockSpec((tm, w), lambda i: (i, 0))
    t_out = cfg.t_lat if final else cfg.t_all
    return pl.pallas_call(
        functools.partial(_combine_kernel, final=final),
        grid=(t_out // tm,),
        in_specs=[row(d), row(d // 2), pl.BlockSpec((tm, d // 2), lambda i: (n_tiles + i, 0)),
                  row(ROUTE_LANES), _mod_spec(cfg), _const_spec((1, d))],
        out_specs=row(d),
        out_shape=jax.ShapeDtypeStruct((t_out, d), F32),
        compiler_params=_cparams(("arbitrary",)),
        name="moe_combine",
    )(x, y12, y12, route, mod_l, g)


def _forward(cfg, x, c, ctx, c_ctx, ada_w, ada_b, norm1_g, norm2_g, ab_w_in, ab_w_out,
             diff_lq1, diff_lk1, diff_lq2, diff_lk2, diff_subln_g, bconv_w,
             ssd_w_in, ssd_conv_w, ssd_conv_b, ssd_A_log, ssd_dt_bias, ssd_D, ssd_norm_g, ssd_w_out,
             moe_wg, moe_bg, moe_we, moe_be, moe_w_gate, moe_w_up, moe_w_down, final_norm_g):
    d = cfg.d_model
    b = cfg.batch
    pow2 = lambda n: n & (n - 1) == 0
    assert pow2(cfg.n_lat) and pow2(cfg.n_ctx), "sequence positions are taken with a bit mask"
    assert cfg.n_lat % cfg.tm == 0 and cfg.t_ctx % cfg.tm == 0, "row tiles must not straddle latent / context rows"
    assert cfg.tm % cfg.n_ctx == 0 or cfg.n_ctx % cfg.tm == 0, "row tiles hold whole context sequences or parts of one"
    assert cfg.n_lat % (cfg.tq * cfg.attn_subtiles) == 0 and cfg.n_lat % min(cfg.kchunk, cfg.n_lat) == 0
    assert cfg.n_lat % SSD_CHUNK == 0 and cfg.n_ctx % SSD_CHUNK == 0 and cfg.n_lat % GRID_W == 0
    x_pair = (x.reshape(cfg.t_lat, d), ctx.reshape(cfg.t_ctx, d))
    c_all = jnp.zeros((SUBLANES_V7X, d), F32).at[:b].set(c).at[b].set(c_ctx)
    mod = _adaln(cfg, c_all, ada_w, ada_b)
    tables = _rope_tables(cfg)
    ab_w_in_b, ab_w_out_b = _to_bf16(ab_w_in), _to_bf16(ab_w_out)
    ssd_w_in_b, ssd_w_out_b = _to_bf16(ssd_w_in), _to_bf16(ssd_w_out)

    for layer in range(cfg.depth):
        i = layer // 2
        mod_l = mod[layer]
        g1 = norm1_g[layer].reshape(1, d)
        g2 = norm2_g[layer].reshape(1, d)
        router = _router_weights(moe_wg[layer], moe_bg[layer], moe_we[layer], moe_be[layer])
        if layer % 2 == 0:
            lam_init = 0.8 - 0.6 * math.exp(-0.3 * layer)
            q, k, v, gb, u = _ab_in(cfg, x_pair, mod_l, g1, ab_w_in_b, i, tables)
            lvec = jnp.stack([diff_lq1[i], diff_lk1[i], diff_lq2[i], diff_lk2[i]])
            o_pair = _attention(cfg, q, k, v, lvec, diff_subln_g[i].reshape(1, DIFF_V_DIM), lam_init)
            xa, h2, route, cnt = _ab_out(cfg, o_pair, gb, u, bconv_w[i], ab_w_out_b, i, x_pair, mod_l, g2, router)
        else:
            z, xbc_act, dt_raw = _ssd_in(cfg, xa, mod_l, g1, ssd_w_in_b, i, ssd_conv_w[i],
                                         ssd_conv_b[i].reshape(1, SSD_CONV_CH))
            y2 = _ssd_scan(cfg, xbc_act, dt_raw, ssd_dt_bias[i], ssd_A_log[i])
            d_wide = jnp.repeat(ssd_D[i], SSD_HEAD_DIM).reshape(1, SSD_D_INNER)
            xa, h2, route, cnt = _ssd_out(cfg, y2, xbc_act, z, d_wide, ssd_norm_g[i].reshape(1, SSD_D_INNER),
                                          ssd_w_out_b, i, xa, mod_l, g2, router)
        slot_tok, block_eid, n_used, slots = _dispatch(cfg, route, cnt)
        xb_parts = [h2[p_] for p_ in jnp.split(slot_tok, MOE_GATHER_PARTS)]
        yb = _experts(cfg, layer, block_eid, n_used, xb_parts, moe_w_gate, moe_w_up, moe_w_down)
        last = layer == cfg.depth - 1
        y12 = yb[slots]
        xa = _combine(cfg, xa, y12, route, mod_l, final_norm_g.reshape(1, d), last)
        x_pair = (xa, xa)
    return xa.reshape(b, cfg.n_lat, d)


def kernel(x, c, ctx, c_ctx, ada_w, ada_b, norm1_g, norm2_g, ab_w_in, ab_w_out, diff_lq1, diff_lk1, diff_lq2, diff_lk2, diff_subln_g, bconv_w, ssd_w_in, ssd_conv_w, ssd_conv_b, ssd_A_log, ssd_dt_bias, ssd_D, ssd_norm_g, ssd_w_out, moe_wg, moe_bg, moe_we, moe_be, moe_w_gate, moe_w_up, moe_w_down, final_norm_g):
    cfg = Cfg(batch=x.shape[0], n_lat=x.shape[1], n_ctx=ctx.shape[1], d_model=x.shape[2], depth=ada_w.shape[0],
              tm=512, tq=256, moe_tile=512, attn_subtiles=2, kchunk=1024)
    return _forward(cfg, x, c, ctx, c_ctx, ada_w, ada_b, norm1_g, norm2_g, ab_w_in, ab_w_out,
                    diff_lq1, diff_lk1, diff_lq2, diff_lk2, diff_subln_g, bconv_w,
                    ssd_w_in, ssd_conv_w, ssd_conv_b, ssd_A_log, ssd_dt_bias, ssd_D, ssd_norm_g, ssd_w_out,
                    moe_wg, moe_bg, moe_we, moe_be, moe_w_gate, moe_w_up, moe_w_down, final_norm_g)
```

```python
import functools
import math
from typing import NamedTuple

import jax
import jax.numpy as jnp
from jax import lax
from jax.experimental import pallas as pl
from jax.experimental.pallas import tpu as pltpu

F32 = jnp.float32
BF16 = jnp.bfloat16
HIGHEST = lax.Precision.HIGHEST

LANES_V7X = 128
SUBLANES_V7X = 8
VMEM_LIMIT_BYTES_V7X = 56 * 1024 * 1024

RMS_EPS = 1e-6
SOFTMAX_MIN_DENOMINATOR = 2.0 ** -60
GRID_W = 64
N_DIFF_HEADS = 4
DIFF_QK_DIM = 64
DIFF_V_DIM = 128
DIFF_WIDTH = 512
CONV_WIDTH = 512
ROPE_BASE = 10000.0
SSD_D_INNER = 2048
SSD_HEAD_DIM = 64
SSD_HEADS = 32
SSD_GROUPS = 4
SSD_STATE = 128
SSD_CHUNK = 128
SSD_CONV_CH = SSD_D_INNER + 2 * SSD_GROUPS * SSD_STATE
MOE_GROUPS = 4
MOE_EXPERTS_PER_GROUP = 8
MOE_EXPERTS = 32
MOE_HIDDEN = 512
ROUTE_LANES = LANES_V7X
MOE_GATHER_PARTS = 3


class Cfg(NamedTuple):
    batch: int
    n_lat: int
    n_ctx: int
    d_model: int
    depth: int
    tm: int
    tq: int
    moe_tile: int
    attn_subtiles: int
    kchunk: int

    @property
    def t_lat(self):
        return self.batch * self.n_lat

    @property
    def t_ctx(self):
        return self.batch * self.n_ctx

    @property
    def t_all(self):
        return self.t_lat + self.t_ctx


def _cparams(sem):
    return pltpu.CompilerParams(dimension_semantics=sem, vmem_limit_bytes=VMEM_LIMIT_BYTES_V7X)


def _silu(v):
    return v * (1.0 / (1.0 + jnp.exp(-v)))


def _const_spec(shape):
    nd = len(shape)
    return pl.BlockSpec(shape, lambda *_: (0,) * nd)


def _layer_spec(stacked, index):
    rest = stacked.shape[1:]
    return pl.BlockSpec((None,) + rest, lambda *_: (index,) + (0,) * len(rest))


def _mod_spec(cfg):
    return pl.BlockSpec((1, 6, cfg.d_model),
                        lambda i: (jnp.minimum((i * cfg.tm) // cfg.n_lat, cfg.batch), 0, 0))


def _seq_pos(cfg, tm):
    r0 = pl.program_id(0) * tm
    row = r0 + lax.broadcasted_iota(jnp.int32, (tm, 1), 0)
    seqlen = jnp.where(r0 >= cfg.t_lat, cfg.n_ctx, cfg.n_lat)
    return row & (seqlen - 1), seqlen


def _norm_mod(x, g, shift, scale):
    ms = jnp.mean(x * x, axis=-1, keepdims=True)
    return (x * lax.rsqrt(ms + RMS_EPS) * g) * (1.0 + scale) + shift


def _to_bf16(w):
    n = w.shape[-1]
    return jnp.pad(w.astype(BF16), ((0, 0), (0, 0), (0, -n % LANES_V7X)))


def _hi_lo(v):
    hi = v.astype(BF16)
    return hi, (v - hi.astype(F32)).astype(BF16)


def _adaln_kernel(c_ref, w_ref, b_ref, o_ref):
    s_hi, s_lo = _hi_lo(_silu(c_ref[...]))
    w_hi, w_lo = _hi_lo(w_ref[0])
    o_ref[0] = (jnp.dot(s_hi, w_hi, preferred_element_type=F32) + jnp.dot(s_hi, w_lo, preferred_element_type=F32)
                + jnp.dot(s_lo, w_hi, preferred_element_type=F32)) + b_ref[0]


def _adaln(cfg, c_all, ada_w, ada_b):
    d = cfg.d_model
    out = pl.pallas_call(
        _adaln_kernel,
        grid=(cfg.depth, 6),
        in_specs=[_const_spec((SUBLANES_V7X, d)),
                  pl.BlockSpec((1, d, d), lambda l, j: (l, 0, j)),
                  pl.BlockSpec((1, 1, d), lambda l, j: (l, 0, j))],
        out_specs=pl.BlockSpec((1, SUBLANES_V7X, d), lambda l, j: (l, 0, j)),
        out_shape=jax.ShapeDtypeStruct((cfg.depth, SUBLANES_V7X, 6 * d), F32),
        compiler_params=_cparams(("arbitrary", "arbitrary")),
        name="adaln",
    )(c_all, ada_w, ada_b.reshape(cfg.depth, 1, 6 * d))
    return out.reshape(cfg.depth, SUBLANES_V7X, 6, d)


def _rope_tables(cfg):
    n = cfg.n_lat
    rows = n // GRID_W
    row = jnp.broadcast_to(jnp.arange(rows, dtype=F32)[:, None], (rows, GRID_W)).reshape(n)
    col = jnp.broadcast_to(jnp.arange(GRID_W, dtype=F32)[None, :], (rows, GRID_W)).reshape(n)
    axis_dim = DIFF_QK_DIM // 2
    inv_freq = ROPE_BASE ** (-jnp.arange(0, axis_dim, 2, dtype=F32) / axis_dim)
    ang_r = row[:, None] * inv_freq
    ang_c = col[:, None] * inv_freq
    zeros = jnp.zeros_like(ang_r)
    cos64 = jnp.concatenate([jnp.cos(ang_r), jnp.cos(ang_r), jnp.cos(ang_c), jnp.cos(ang_c)], axis=1)
    sa64 = jnp.concatenate([zeros, jnp.sin(ang_r), zeros, jnp.sin(ang_c)], axis=1)
    sb64 = jnp.concatenate([-jnp.sin(ang_r), zeros, -jnp.sin(ang_c), zeros], axis=1)
    ident = cfg.tm
    cos = jnp.concatenate([jnp.tile(cos64, (1, 2)), jnp.ones((ident, LANES_V7X), F32)], axis=0)
    sa = jnp.concatenate([jnp.tile(sa64, (1, 2)), jnp.zeros((ident, LANES_V7X), F32)], axis=0)
    sb = jnp.concatenate([jnp.tile(sb64, (1, 2)), jnp.zeros((ident, LANES_V7X), F32)], axis=0)
    return cos, sa, sb


def _dual_row_specs(cfg, width, pair):
    lat_arr, ctx_arr = pair
    nl = cfg.t_lat // cfg.tm
    c0 = nl if ctx_arr is lat_arr else 0
    return (pl.BlockSpec((cfg.tm, width), lambda i: (jnp.minimum(i, nl - 1), 0)),
            pl.BlockSpec((cfg.tm, width), lambda i: (jnp.maximum(i - nl, 0) + c0, 0)))


def _dual_rows(cfg, lat_ref, ctx_ref):
    return jnp.where(pl.program_id(0) >= cfg.t_lat // cfg.tm, ctx_ref[...], lat_ref[...])


def _ab_in_kernel(xl_ref, xc_ref, mod_ref, g_ref, w_ref, cos_ref, sa_ref, sb_ref,
                  q_ref, k_ref, v_ref, gb_ref, u_ref, *, cfg):
    m = mod_ref[0]
    hb = _norm_mod(_dual_rows(cfg, xl_ref, xc_ref), g_ref[...], m[0:1], m[1:2]).astype(BF16)
    cos, sa, sb = cos_ref[...], sa_ref[...], sb_ref[...]

    def proj(j):
        return jnp.dot(hb, w_ref[:, j * DIFF_WIDTH:(j + 1) * DIFF_WIDTH], preferred_element_type=F32)

    def rope(t, out_ref, scale):
        for gi in range(DIFF_WIDTH // LANES_V7X):
            xg = t[:, gi * LANES_V7X:(gi + 1) * LANES_V7X]
            r = xg * cos + pltpu.roll(xg, 16, 1) * sa + pltpu.roll(xg, LANES_V7X - 16, 1) * sb
            out_ref[:, gi * LANES_V7X:(gi + 1) * LANES_V7X] = (r * scale).astype(out_ref.dtype)

    rope(proj(0), q_ref, math.log2(math.e) * DIFF_QK_DIM ** -0.5)
    rope(proj(1), k_ref, 1.0)
    v_ref[...] = proj(2).astype(v_ref.dtype)
    gb_ref[...] = proj(3)
    u_ref[...] = proj(4) * proj(5)


def _ab_in(cfg, x_pair, mod_l, g1, w_stack, w_index, tables):
    t, d, tm = cfg.t_all, cfg.d_model, cfg.tm
    n_lat_tiles, per_seq = cfg.t_lat // tm, cfg.n_lat // tm
    tab_spec = pl.BlockSpec((tm, LANES_V7X), lambda i: (jnp.where(i < n_lat_tiles, i % per_seq, per_seq), 0))
    row = lambda w: pl.BlockSpec((tm, w), lambda i: (i, 0))
    return pl.pallas_call(
        functools.partial(_ab_in_kernel, cfg=cfg),
        grid=(t // tm,),
        in_specs=[*_dual_row_specs(cfg, d, x_pair), _mod_spec(cfg), _const_spec((1, d)),
                  _layer_spec(w_stack, w_index), tab_spec, tab_spec, tab_spec],
        out_specs=[row(DIFF_WIDTH)] * 5,
        out_shape=[jax.ShapeDtypeStruct((t, DIFF_WIDTH), BF16)] * 3
        + [jax.ShapeDtypeStruct((t, CONV_WIDTH), F32)] * 2,
        compiler_params=_cparams(("arbitrary",)),
        name="ab_in",
    )(*x_pair, mod_l, g1, w_stack, *tables)


def _key_norm_kernel(kc_ref, kl_ref, o_ref):
    def comp_max(k_ref):
        k = k_ref[...].astype(F32)
        sq = k * k
        lane = lax.broadcasted_iota(jnp.int32, sq.shape, 1)
        n0 = jnp.sum(jnp.where(lane < DIFF_QK_DIM, sq, 0.0), axis=-1, keepdims=True)
        n1 = jnp.sum(jnp.where(lane >= DIFF_QK_DIM, sq, 0.0), axis=-1, keepdims=True)
        return jnp.sqrt(jnp.max(n0, axis=0, keepdims=True)), jnp.sqrt(jnp.max(n1, axis=0, keepdims=True))

    c0, c1 = comp_max(kc_ref)
    l0, l1 = comp_max(kl_ref)
    row = lax.broadcasted_iota(jnp.int32, o_ref.shape, 0)
    o_ref[...] = jnp.where(row == 0, jnp.maximum(c0, l0),
                           jnp.where(row == 1, jnp.maximum(c1, l1), jnp.where(row == 2, c0, c1)))


def _attn_body(lvec_ref, q_ref, kc_ref, kl_ref, vc_ref, vl_ref, kn_ref, g_ref, o_ref, *, tq, lam_init, kchunk):
    lv = lvec_ref[...]
    lam = (jnp.exp(jnp.sum(lv[0:1] * lv[1:2], axis=-1, keepdims=True))
           - jnp.exp(jnp.sum(lv[2:3] * lv[3:4], axis=-1, keepdims=True)) + lam_init)
    nt = (((1,), (1,)), ((), ()))
    half = DIFF_QK_DIM
    chunks = [(kc_ref, vc_ref, 0, kc_ref.shape[0])]
    if kl_ref is not None:
        chunks += [(kl_ref, vl_ref, r0, kchunk) for r0 in range(0, kl_ref.shape[0], kchunk)]
    kn = kn_ref[...]
    kn_row = 0 if kl_ref is not None else 2

    def tile(q2, bound):
        ps, ms, ls = [], [], []
        for k_ref, _, r0, n in chunks:
            s = lax.dot_general(q2, k_ref[r0:r0 + n, :], nt, preferred_element_type=F32)
            m_c = jnp.max(s, axis=-1, keepdims=True) if bound is None else bound
            p = jnp.exp2(s - m_c)
            ps.append(p)
            ms.append(m_c)
            ls.append(jnp.sum(p, axis=-1, keepdims=True))
        if bound is None:
            mx = functools.reduce(jnp.maximum, ms)
            scale = [jnp.exp2(m_c - mx) for m_c in ms]
            l = functools.reduce(jnp.add, [sc * l_c for sc, l_c in zip(scale, ls)])
        else:
            scale = [None] * len(ls)
            l = functools.reduce(jnp.add, ls)
        r = 1.0 / l
        pv = None
        for p, sc, (_, v_ref, r0, n) in zip(ps, scale, chunks):
            w = r if sc is None else sc * r
            pd = (p[:tq] * w[:tq] - p[tq:] * (lam * w[tq:])).astype(BF16)
            part = jnp.dot(pd, v_ref[r0:r0 + n, :], preferred_element_type=F32)
            pv = part if pv is None else pv + part
        return pv, l

    def emit(j, pv):
        o = pv * lax.rsqrt(jnp.mean(pv * pv, axis=-1, keepdims=True) + RMS_EPS)
        o_ref[j * tq:(j + 1) * tq, :] = ((o * g_ref[...]) * (1.0 - lam_init)).astype(o_ref.dtype)

    q2s, smallest = [], None
    for j in range(q_ref.shape[0] // tq):
        q = q_ref[j * tq:(j + 1) * tq, :].astype(F32)
        lane = lax.broadcasted_iota(jnp.int32, q.shape, 1)
        q0, q1 = jnp.where(lane < half, q, 0.0), jnp.where(lane >= half, q, 0.0)
        q2 = jnp.concatenate([q0, q1], axis=0).astype(BF16)
        qn = jnp.sqrt(jnp.concatenate([jnp.sum(q0 * q0, axis=-1, keepdims=True),
                                       jnp.sum(q1 * q1, axis=-1, keepdims=True)], axis=0))
        kmax = jnp.concatenate([jnp.broadcast_to(kn[kn_row:kn_row + 1, 0:1], (tq, 1)),
                                jnp.broadcast_to(kn[kn_row + 1:kn_row + 2, 0:1], (tq, 1))], axis=0)
        pv, l = tile(q2, qn * kmax)
        emit(j, pv)
        q2s.append(q2)
        lmin = jnp.min(l)
        smallest = lmin if smallest is None else jnp.minimum(smallest, lmin)

    @pl.when(jnp.logical_not(smallest > SOFTMAX_MIN_DENOMINATOR))
    def _():
        for j, q2 in enumerate(q2s):
            emit(j, tile(q2, None)[0])


def _attn_lat_kernel(lvec_ref, q_ref, kc_ref, kl_ref, vc_ref, vl_ref, kn_ref, g_ref, o_ref, **kw):
    _attn_body(lvec_ref, q_ref, kc_ref, kl_ref, vc_ref, vl_ref, kn_ref, g_ref, o_ref, **kw)


def _attn_ctx_kernel(lvec_ref, q_ref, kc_ref, vc_ref, kn_ref, g_ref, o_ref, **kw):
    _attn_body(lvec_ref, q_ref, kc_ref, None, vc_ref, None, kn_ref, g_ref, o_ref, **kw)


def _attention(cfg, q, k, v, lvec, subln_g, lam_init):
    tq, nsub = cfg.tq, cfg.attn_subtiles
    tstep = tq * nsub
    nqb = cfg.n_lat // tstep
    ctx_blk0 = cfg.t_lat // cfg.n_ctx
    hw = DIFF_V_DIM
    ctx_spec = pl.BlockSpec((cfg.n_ctx, hw), lambda b, h, *_: (ctx_blk0 + b, h))
    lat_spec = pl.BlockSpec((cfg.n_lat, hw), lambda b, h, *_: (b, h))
    q_spec = pl.BlockSpec((tstep, hw), lambda b, h, i: (b * nqb + i, h))
    kw = dict(tq=tq, lam_init=lam_init, kchunk=min(cfg.kchunk, cfg.n_lat))
    kn_spec = pl.BlockSpec((SUBLANES_V7X, LANES_V7X), lambda b, h, *_: (b * N_DIFF_HEADS + h, 0))
    key_norms = pl.pallas_call(
        _key_norm_kernel,
        grid=(cfg.batch, N_DIFF_HEADS),
        in_specs=[ctx_spec, lat_spec],
        out_specs=kn_spec,
        out_shape=jax.ShapeDtypeStruct((cfg.batch * N_DIFF_HEADS * SUBLANES_V7X, LANES_V7X), F32),
        compiler_params=_cparams(("arbitrary", "arbitrary")),
        name="key_norms",
    )(k, k)
    o_lat = pl.pallas_call(
        functools.partial(_attn_lat_kernel, **kw),
        grid=(cfg.batch, N_DIFF_HEADS, nqb),
        in_specs=[_const_spec(lvec.shape), q_spec, ctx_spec, lat_spec, ctx_spec, lat_spec, kn_spec,
                  _const_spec((1, hw))],
        out_specs=q_spec,
        out_shape=jax.ShapeDtypeStruct((cfg.t_lat, DIFF_WIDTH), BF16),
        compiler_params=_cparams(("arbitrary", "arbitrary", "arbitrary")),
        name="diff_attn",
    )(lvec, q, k, k, v, v, key_norms, subln_g)
    tqc = min(tq, cfg.n_ctx)
    o_ctx = pl.pallas_call(
        functools.partial(_attn_ctx_kernel, tq=tqc, lam_init=lam_init, kchunk=cfg.n_ctx),
        grid=(cfg.batch, N_DIFF_HEADS),
        in_specs=[_const_spec(lvec.shape), ctx_spec, ctx_spec, ctx_spec, kn_spec, _const_spec((1, hw))],
        out_specs=pl.BlockSpec((cfg.n_ctx, hw), lambda b, h: (b, h)),
        out_shape=jax.ShapeDtypeStruct((cfg.t_ctx, DIFF_WIDTH), BF16),
        compiler_params=_cparams(("arbitrary", "arbitrary")),
        name="diff_attn_ctx",
    )(lvec, q, k, v, key_norms, subln_g)
    return o_lat, o_ctx


def _route_from_logits(lg):
    lane = lax.broadcasted_iota(jnp.int32, lg.shape, 1)
    neg = -jnp.inf
    big = jnp.int32(ROUTE_LANES)
    gl = jnp.where(lane < MOE_GROUPS, lg, neg)
    gmax = jnp.max(gl, axis=-1, keepdims=True)
    gidx = jnp.min(jnp.where(gl == gmax, lane, big), axis=-1, keepdims=True)
    gw = 1.0 / jnp.sum(jnp.exp(gl - gmax), axis=-1, keepdims=True)
    lo = MOE_GROUPS + gidx * MOE_EXPERTS_PER_GROUP
    el = jnp.where((lane >= lo) & (lane < lo + MOE_EXPERTS_PER_GROUP), lg, neg)
    m1 = jnp.max(el, axis=-1, keepdims=True)
    i1 = jnp.min(jnp.where(el == m1, lane, big), axis=-1, keepdims=True)
    el2 = jnp.where(lane == i1, neg, el)
    m2 = jnp.max(el2, axis=-1, keepdims=True)
    i2 = jnp.min(jnp.where(el2 == m2, lane, big), axis=-1, keepdims=True)
    w1 = gw / (1.0 + jnp.exp(m2 - m1))
    w2 = gw - w1
    e1 = (i1 - MOE_GROUPS).astype(F32)
    e2 = (i2 - MOE_GROUPS).astype(F32)
    return jnp.where(lane == 0, e1, jnp.where(lane == 1, e2, jnp.where(lane == 2, w1, jnp.where(lane == 3, w2, 0.0))))


def _pack_bf16_pairs(v):
    w = v.shape[1] // 2
    lo = pltpu.bitcast(v[:, :w].astype(BF16).astype(F32), jnp.uint32)
    hi = pltpu.bitcast(v[:, w:].astype(BF16).astype(F32), jnp.uint32)
    return pltpu.bitcast((hi & jnp.uint32(0xFFFF0000)) | (lo >> 16), F32)


def _unpack_bf16_pairs(words):
    u = pltpu.bitcast(words, jnp.uint32)
    lo = pltpu.bitcast(u << 16, F32)
    hi = pltpu.bitcast(u & jnp.uint32(0xFFFF0000), F32)
    return lo, hi


def _router_weights(wg, bg, we, be):
    d = wg.shape[0]
    pad = ROUTE_LANES - MOE_GROUPS - MOE_EXPERTS
    wr = jnp.concatenate([wg, we, jnp.zeros((d, pad), F32)], axis=1)
    hi, lo = _hi_lo(wr)
    br = jnp.concatenate([bg, be, jnp.zeros((pad,), F32)]).reshape(1, ROUTE_LANES)
    return jnp.concatenate([hi, lo], axis=1), hi, br


def _expert_onehots(rt):
    lane = lax.broadcasted_iota(jnp.int32, rt.shape, 1)
    return (jnp.where(lane == rt[:, 0:1].astype(jnp.int32), 1.0, 0.0),
            jnp.where(lane == rt[:, 1:2].astype(jnp.int32), 1.0, 0.0))


def _residual_norm_route(x, mix, m, g2, wr2_ref, wrh_ref, br_ref, xo_ref, h2_ref, rt_ref, cnt_ref):
    xn = x + m[2:3] * mix
    xo_ref[...] = xn
    h2 = _norm_mod(xn, g2, m[3:4], m[4:5])
    h2_ref[...] = _pack_bf16_pairs(h2)
    hh, hl = _hi_lo(h2)
    a = jnp.dot(hh, wr2_ref[...], preferred_element_type=F32)
    lg = (a[:, :ROUTE_LANES] + a[:, ROUTE_LANES:]) + jnp.dot(hl, wrh_ref[...], preferred_element_type=F32)
    rt = _route_from_logits(lg + br_ref[...])
    rt_ref[...] = rt

    @pl.when(pl.program_id(0) == 0)
    def _():
        cnt_ref[...] = jnp.zeros_like(cnt_ref)

    o1, o2 = _expert_onehots(rt)
    cnt_ref[...] += jnp.sum(o1 + o2, axis=0, keepdims=True)


def _epilogue_specs(cfg, x_pair):
    d, tm = cfg.d_model, cfg.tm
    row = lambda w: pl.BlockSpec((tm, w), lambda i: (i, 0))
    in_specs = [*_dual_row_specs(cfg, d, x_pair), _mod_spec(cfg), _const_spec((1, d)),
                _const_spec((d, 2 * ROUTE_LANES)),
                _const_spec((d, ROUTE_LANES)), _const_spec((1, ROUTE_LANES))]
    out_specs = [row(d), row(d // 2), row(ROUTE_LANES), _const_spec((SUBLANES_V7X, ROUTE_LANES))]
    out_shape = [jax.ShapeDtypeStruct((cfg.t_all, d), F32), jax.ShapeDtypeStruct((cfg.t_all, d // 2), F32),
                 jax.ShapeDtypeStruct((cfg.t_all, ROUTE_LANES), F32),
                 jax.ShapeDtypeStruct((SUBLANES_V7X, ROUTE_LANES), F32)]
    return in_specs, out_specs, out_shape


def _halo_specs(cfg, width, col_block=0):
    per = cfg.tm // SUBLANES_V7X
    last = cfg.t_all // SUBLANES_V7X - 1
    prev = pl.BlockSpec((SUBLANES_V7X, width), lambda i, *_: (jnp.maximum(i * per - 1, 0), col_block))
    nxt = pl.BlockSpec((SUBLANES_V7X, width), lambda i, *_: (jnp.minimum((i + 1) * per, last), col_block))
    return prev, nxt


def _shifted_rows(pad_ref, u, prev_blk, next_blk, pos, seqlen):
    tm = u.shape[0]
    s = SUBLANES_V7X
    pad_ref[s:s + tm, :] = u
    pad_ref[0:s, :] = prev_blk
    pad_ref[s + tm:2 * s + tm, :] = next_blk
    um1 = jnp.where(pos == 0, 0.0, pad_ref[s - 1:s - 1 + tm, :])
    up1 = jnp.where(pos == seqlen - 1, 0.0, pad_ref[s + 1:s + 1 + tm, :])
    return um1, up1


def _ab_out_kernel(ol_ref, oc_ref, gb_ref, u_ref, up_ref, un_ref, cw_ref, wo_ref,
                   xl_ref, xc_ref, mod_ref, g2_ref, wr2_ref, wrh_ref, br_ref,
                   xo_ref, h2_ref, rt_ref, cnt_ref, pad_ref, *, cfg):
    pos, seqlen = _seq_pos(cfg, cfg.tm)
    u = u_ref[...]
    um1, up1 = _shifted_rows(pad_ref, u, up_ref[...], un_ref[...], pos, seqlen)
    cw = cw_ref[...]
    conv = gb_ref[...] * (um1 * cw[0:1] + u * cw[1:2] + up1 * cw[2:3])
    mix = (jnp.dot(_dual_rows(cfg, ol_ref, oc_ref), wo_ref[:DIFF_WIDTH, :], preferred_element_type=F32)
           + jnp.dot(conv.astype(BF16), wo_ref[DIFF_WIDTH:, :], preferred_element_type=F32))
    _residual_norm_route(_dual_rows(cfg, xl_ref, xc_ref), mix, mod_ref[0], g2_ref[...], wr2_ref, wrh_ref, br_ref,
                         xo_ref, h2_ref, rt_ref, cnt_ref)


def _ab_out(cfg, o_pair, gb, u, conv_w, wo_stack, w_index, x_pair, mod_l, g2, router):
    tm = cfg.tm
    row = lambda w: pl.BlockSpec((tm, w), lambda i: (i, 0))
    prev, nxt = _halo_specs(cfg, CONV_WIDTH)
    ep_in, ep_out, ep_shape = _epilogue_specs(cfg, x_pair)
    return pl.pallas_call(
        functools.partial(_ab_out_kernel, cfg=cfg),
        grid=(cfg.t_all // tm,),
        in_specs=[*_dual_row_specs(cfg, DIFF_WIDTH, o_pair), row(CONV_WIDTH), row(CONV_WIDTH), prev, nxt,
                  _const_spec(conv_w.shape),
                  _layer_spec(wo_stack, w_index)] + ep_in,
        out_specs=ep_out,
        out_shape=ep_shape,
        scratch_shapes=[pltpu.VMEM((tm + 2 * SUBLANES_V7X, CONV_WIDTH), F32)],
        compiler_params=_cparams(("arbitrary",)),
        name="ab_out",
    )(*o_pair, gb, u, u, u, conv_w, wo_stack, *x_pair, mod_l, g2, *router)


def _ssd_in_kernel(x_ref, xp_ref, xn_ref, mod_ref, g_ref, w_ref, cw_ref, cb_ref,
                   z_ref, xbc_ref, dt_ref, pad_ref, *, cfg):
    tm = cfg.tm
    s = SUBLANES_V7X
    m = mod_ref[0]
    g = g_ref[...]
    h = _norm_mod(x_ref[...], g, m[0:1], m[1:2])
    hb = h.astype(BF16)
    h_ext = jnp.concatenate([_norm_mod(xp_ref[...], g, m[0:1], m[1:2]), h,
                             _norm_mod(xn_ref[...], g, m[0:1], m[1:2])], axis=0).astype(BF16)
    pos, seqlen = _seq_pos(cfg, tm)
    cw = 512
    for j in range(SSD_D_INNER // cw):
        z_ref[:, j * cw:(j + 1) * cw] = jnp.dot(hb, w_ref[:, j * cw:(j + 1) * cw],
                                                preferred_element_type=F32).astype(z_ref.dtype)
    for j in range(SSD_CONV_CH // cw):
        c0 = SSD_D_INNER + j * cw
        pad = pad_ref.at[j]
        pad[...] = jnp.dot(h_ext, w_ref[:, c0:c0 + cw], preferred_element_type=F32)
        taps = cw_ref[:, j * cw:(j + 1) * cw]
        um1 = jnp.where(pos == 0, 0.0, pad[s - 1:s - 1 + tm, :])
        up1 = jnp.where(pos == seqlen - 1, 0.0, pad[s + 1:s + 1 + tm, :])
        conv = um1 * taps[0:1] + pad[s:s + tm, :] * taps[1:2] + up1 * taps[2:3] + cb_ref[:, j * cw:(j + 1) * cw]
        xbc_ref[:, j * cw:(j + 1) * cw] = _silu(conv).astype(xbc_ref.dtype)
    dt_ref[...] = jnp.dot(hb, w_ref[:, SSD_D_INNER + SSD_CONV_CH:], preferred_element_type=F32)


def _ssd_in(cfg, x, mod_l, g1, w_stack, w_index, conv_w, conv_b):
    t, d, tm = cfg.t_all, cfg.d_model, cfg.tm
    row = lambda w: pl.BlockSpec((tm, w), lambda i: (i, 0))
    prev, nxt = _halo_specs(cfg, d)
    return pl.pallas_call(
        functools.partial(_ssd_in_kernel, cfg=cfg),
        grid=(t // tm,),
        in_specs=[row(d), prev, nxt, _mod_spec(cfg), _const_spec((1, d)), _layer_spec(w_stack, w_index),
                  _const_spec(conv_w.shape), _const_spec(conv_b.shape)],
        out_specs=[row(SSD_D_INNER), row(SSD_CONV_CH), row(LANES_V7X)],
        out_shape=[jax.ShapeDtypeStruct((t, SSD_D_INNER), BF16), jax.ShapeDtypeStruct((t, SSD_CONV_CH), BF16),
                   jax.ShapeDtypeStruct((t, LANES_V7X), F32)],
        scratch_shapes=[pltpu.VMEM((SSD_CONV_CH // 512, tm + 2 * SUBLANES_V7X, 512), F32)],
        compiler_params=_cparams(("arbitrary",)),
        name="ssd_in",
    )(x, x, x, mod_l, g1, w_stack, conv_w, conv_b)


def _split3(v):
    b1 = v.astype(BF16)
    r1 = v - b1.astype(F32)
    b2 = r1.astype(BF16)
    b3 = (r1 - b2.astype(F32)).astype(BF16)
    return jnp.concatenate([b1, b2, b3], axis=1)


def _softplus(v):
    return jnp.maximum(v, 0.0) + jnp.log(1.0 + jnp.exp(-jnp.abs(v)))


def _ssd_scan_kernel(*refs):
    n_in = 5
    fwd_in, bwd_in = refs[:n_in], refs[n_in:2 * n_in]
    bias_ref, biast_ref, alog_ref, alogt_ref, ewide_ref, yf_ref, yb_ref, sf_ref, sb_ref = refs[2 * n_in:]

    @pl.when(pl.program_id(1) == 0)
    def _():
        sf_ref[...] = jnp.zeros_like(sf_ref)
        sb_ref[...] = jnp.zeros_like(sb_ref)

    for d, (ins, y_ref, state_ref) in enumerate(((fwd_in, yf_ref, sf_ref), (bwd_in, yb_ref, sb_ref))):
        _ssd_chunk(d == 0, *ins, bias_ref[d], biast_ref[d], alog_ref[d], alogt_ref[d], ewide_ref, y_ref, state_ref)


def _ssd_chunk(fwd, xs_ref, b_ref, c_ref, dt_ref, dtt_ref, bias, biast, alog, alogt, ewide_ref, y_ref, state_ref):
    q = SSD_CHUNK
    dt = _softplus(dt_ref[0] + bias)
    dtt = _softplus(dtt_ref[0] + biast)
    a = dt * (-jnp.exp(alog))
    at = dtt * (-jnp.exp(alogt))

    ri = lax.broadcasted_iota(jnp.int32, (q, q), 0)
    ci = lax.broadcasted_iota(jnp.int32, (q, q), 1)
    keep = (ci <= ri) if fwd else (ci >= ri)
    tri = jnp.where(keep, 1.0, 0.0)
    cs = jnp.dot(tri, a, precision=HIGHEST, preferred_element_type=F32)
    cst = lax.dot_general(at, tri, (((1,), (1,)), ((), ())), precision=HIGHEST,
                          preferred_element_type=F32)
    tot = jnp.sum(a, axis=0, keepdims=True)

    ewide = ewide_ref[...]
    w_wide = jnp.dot(_split3(dt * jnp.exp(tot - cs)), ewide, preferred_element_type=F32)
    ecs_wide = jnp.dot(_split3(jnp.exp(cs)), ewide, preferred_element_type=F32)
    dec_wide = jnp.dot(_split3(jnp.broadcast_to(jnp.exp(tot), (SUBLANES_V7X, SSD_HEADS))), ewide,
                       preferred_element_type=F32)[0:1]

    lane = lax.broadcasted_iota(jnp.int32, (q, LANES_V7X), 1)
    lo = lane < SSD_HEAD_DIM
    hpg = SSD_HEADS // SSD_GROUPS
    gw = hpg * SSD_HEAD_DIM
    nt = (((1,), (1,)), ((), ()))
    tn = (((0,), (0,)), ((), ()))
    for g in range(SSD_GROUPS):
        bm = b_ref[:, g * SSD_STATE:(g + 1) * SSD_STATE]
        cm = c_ref[:, g * SSD_STATE:(g + 1) * SSD_STATE]
        cb = lax.dot_general(cm, bm, nt, preferred_element_type=F32)
        sl = slice(g * gw, (g + 1) * gw)
        y_off = jnp.dot(cm, state_ref[:, sl].astype(BF16), preferred_element_type=F32) * ecs_wide[:, sl]
        for pr in range(hpg // 2):
            lhs, rhs = [], []
            c0 = g * gw + pr * LANES_V7X
            xs_pair = xs_ref[:, c0:c0 + LANES_V7X]
            zero = jnp.zeros_like(xs_pair)
            for k in range(2):
                h = g * hpg + pr * 2 + k
                seg = cs[:, h:h + 1] - cst[h:h + 1, :]
                lmat = jnp.exp(jnp.where(keep, seg, -jnp.inf))
                lhs.append((cb * lmat * dtt[h:h + 1, :]).astype(BF16))
                rhs.append(jnp.where(lo if k == 0 else jnp.logical_not(lo), xs_pair, zero))
            y = jnp.dot(jnp.concatenate(lhs, axis=1), jnp.concatenate(rhs, axis=0), preferred_element_type=F32)
            y_ref[:, c0:c0 + LANES_V7X] = (y + y_off[:, pr * LANES_V7X:(pr + 1) * LANES_V7X]).astype(y_ref.dtype)
        x2 = (xs_ref[:, sl].astype(F32) * w_wide[:, sl]).astype(BF16)
        upd = lax.dot_general(bm, x2, tn, preferred_element_type=F32)
        state_ref[:, sl] = state_ref[:, sl] * dec_wide[:, sl] + upd


def _ssd_scan(cfg, xbc_act, dt_raw, dt_bias, a_log):
    q = SSD_CHUNK
    n_cc, n_lc = cfg.n_ctx // q, cfg.n_lat // q
    ctx0 = cfg.t_lat // q
    h = SSD_HEADS
    dt2 = dt_raw[:, :2 * h].reshape(cfg.t_all, 2, h).transpose(1, 0, 2)
    dt2t = dt2.transpose(0, 2, 1)
    bias = dt_bias.reshape(2, 1, h)
    biast = dt_bias.reshape(2, h, 1)
    alog = a_log.reshape(2, 1, h)
    alogt = a_log.reshape(2, h, 1)
    head_of_row = jnp.tile(jnp.arange(h), 3)[:, None]
    ewide = (head_of_row == (jnp.arange(SSD_D_INNER) // SSD_HEAD_DIM)[None, :]).astype(BF16)

    def blk(d):
        def index(b, s):
            cs = s if d == 0 else n_cc - 1 - s
            ls = s - n_cc if d == 0 else n_lc - 1 - (s - n_cc)
            return jnp.where(s < n_cc, ctx0 + b * n_cc + cs, b * n_lc + ls)
        return index

    gs = SSD_GROUPS * SSD_STATE

    def dir_specs(d):
        at = blk(d)
        col = lambda w, cblk: pl.BlockSpec((q, w), lambda b, s: (at(b, s), cblk))
        return [col(SSD_D_INNER, 0), col(gs, SSD_D_INNER // gs), col(gs, SSD_D_INNER // gs + 1),
                pl.BlockSpec((1, q, h), lambda b, s: (d, at(b, s), 0)),
                pl.BlockSpec((1, h, q), lambda b, s: (d, 0, at(b, s)))]

    dir_args = (xbc_act, xbc_act, xbc_act, dt2, dt2t)
    y_spec = lambda d: pl.BlockSpec((q, SSD_D_INNER), lambda b, s: (blk(d)(b, s), 0))
    return pl.pallas_call(
        _ssd_scan_kernel,
        grid=(cfg.batch, n_cc + n_lc),
        in_specs=dir_specs(0) + dir_specs(1) + [_const_spec(bias.shape), _const_spec(biast.shape),
                                                _const_spec(alog.shape), _const_spec(alogt.shape),
                                                _const_spec(ewide.shape)],
        out_specs=[y_spec(0), y_spec(1)],
        out_shape=[jax.ShapeDtypeStruct((cfg.t_all, SSD_D_INNER), BF16)] * 2,
        scratch_shapes=[pltpu.VMEM((SSD_STATE, SSD_D_INNER), F32)] * 2,
        compiler_params=_cparams(("arbitrary", "arbitrary")),
        name="ssd_scan",
    )(*dir_args, *dir_args, bias, biast, alog, alogt, ewide)


def _ssd_out_kernel(yf_ref, yb_ref, xs_ref, z_ref, dw_ref, ng_ref, wo_ref,
                    xl_ref, xc_ref, mod_ref, g2_ref, wr2_ref, wrh_ref, br_ref, xo_ref, h2_ref, rt_ref, cnt_ref,
                    *, cfg):
    gw = SSD_D_INNER // SSD_GROUPS
    mix = None
    for g in range(SSD_GROUPS):
        sl = slice(g * gw, (g + 1) * gw)
        y = yf_ref[:, sl].astype(F32) + yb_ref[:, sl].astype(F32) + xs_ref[:, sl].astype(F32) * dw_ref[:, sl]
        y = y * _silu(z_ref[:, sl].astype(F32))
        y = (y * lax.rsqrt(jnp.mean(y * y, axis=-1, keepdims=True) + RMS_EPS)) * ng_ref[:, sl]
        part = jnp.dot(y.astype(BF16), wo_ref[sl, :], preferred_element_type=F32)
        mix = part if mix is None else mix + part
    _residual_norm_route(_dual_rows(cfg, xl_ref, xc_ref), mix, mod_ref[0], g2_ref[...], wr2_ref, wrh_ref, br_ref,
                         xo_ref, h2_ref, rt_ref, cnt_ref)


def _ssd_out(cfg, y2, xbc_act, z, d_wide, norm_g, wo_stack, w_index, x, mod_l, g2, router):
    tm = cfg.tm
    row = lambda w: pl.BlockSpec((tm, w), lambda i: (i, 0))
    ep_in, ep_out, ep_shape = _epilogue_specs(cfg, (x, x))
    return pl.pallas_call(
        functools.partial(_ssd_out_kernel, cfg=cfg),
        grid=(cfg.t_all // tm,),
        in_specs=[row(SSD_D_INNER)] * 4 + [_const_spec((1, SSD_D_INNER)),
                  _const_spec((1, SSD_D_INNER)), _layer_spec(wo_stack, w_index)] + ep_in,
        out_specs=ep_out,
        out_shape=ep_shape,
        compiler_params=_cparams(("arbitrary",)),
        name="ssd_out",
    )(*y2, xbc_act, z, d_wide, norm_g, wo_stack, x, x, mod_l, g2, *router)


def _expert_kernel(eid_ref, nused_ref, *refs, blocks_per_part):
    xb_refs = refs[:MOE_GATHER_PARTS]
    wg_ref, wu_ref, wd_ref, y_ref, wg_s, wu_s, wd_s = refs[MOE_GATHER_PARTS:]
    i = pl.program_id(0)
    changed = jnp.logical_or(i == 0, eid_ref[i] != eid_ref[jnp.maximum(i - 1, 0)])

    @pl.when(jnp.logical_and(changed, i < nused_ref[0]))
    def _():
        wg_s[...] = wg_ref[0, 0].astype(BF16)
        wu_s[...] = wu_ref[0, 0].astype(BF16)
        wd_s[...] = wd_ref[0, 0].astype(BF16)

    @pl.when(i < nused_ref[0])
    def _():
        half = wg_s.shape[0] // 2
        words = xb_refs[0][...]
        for k in range(1, MOE_GATHER_PARTS):
            words = jnp.where(i >= k * blocks_per_part, xb_refs[k][...], words)
        lo, hi = _unpack_bf16_pairs(words)
        lo, hi = lo.astype(BF16), hi.astype(BF16)
        hg = (jnp.dot(lo, wg_s[:half, :], preferred_element_type=F32)
              + jnp.dot(hi, wg_s[half:, :], preferred_element_type=F32))
        hu = (jnp.dot(lo, wu_s[:half, :], preferred_element_type=F32)
              + jnp.dot(hi, wu_s[half:, :], preferred_element_type=F32))
        y = jnp.dot((_silu(hg) * hu).astype(BF16), wd_s[...], preferred_element_type=F32)
        y_ref[...] = _pack_bf16_pairs(y)

    @pl.when(i >= nused_ref[0])
    def _():
        y_ref[...] = jnp.zeros_like(y_ref)


def _experts(cfg, layer, block_eid, n_used, xb_parts, w_gate, w_up, w_down):
    d, tile = cfg.d_model, cfg.moe_tile
    per = xb_parts[0].shape[0] // tile
    n_blocks = per * MOE_GATHER_PARTS

    def part_spec(k):
        return pl.BlockSpec((tile, d // 2), lambda i, e, n: (jnp.clip(i - k * per, 0, per - 1), 0))

    grid_spec = pltpu.PrefetchScalarGridSpec(
        num_scalar_prefetch=2,
        grid=(n_blocks,),
        in_specs=[part_spec(k) for k in range(MOE_GATHER_PARTS)] + [
                  pl.BlockSpec((1, 1, d, MOE_HIDDEN), lambda i, e, n: (layer, e[i], 0, 0)),
                  pl.BlockSpec((1, 1, d, MOE_HIDDEN), lambda i, e, n: (layer, e[i], 0, 0)),
                  pl.BlockSpec((1, 1, MOE_HIDDEN, d), lambda i, e, n: (layer, e[i], 0, 0))],
        out_specs=pl.BlockSpec((tile, d // 2), lambda i, e, n: (i, 0)),
        scratch_shapes=[pltpu.VMEM((d, MOE_HIDDEN), BF16), pltpu.VMEM((d, MOE_HIDDEN), BF16),
                        pltpu.VMEM((MOE_HIDDEN, d), BF16)],
    )
    return pl.pallas_call(
        functools.partial(_expert_kernel, blocks_per_part=per),
        grid_spec=grid_spec,
        out_shape=jax.ShapeDtypeStruct((n_blocks * tile, d // 2), F32),
        compiler_params=_cparams(("arbitrary",)),
        name="moe_experts",
    )(block_eid, n_used, *xb_parts, w_gate, w_up, w_down)


def _dispatch_kernel(rt_ref, cnt_ref, pos_ref, carry_ref, start_ref, *, tile):
    tm = rt_ref.shape[0]
    rt = rt_ref[...]
    lane = lax.broadcasted_iota(jnp.int32, rt.shape, 1)
    o1, o2 = _expert_onehots(rt)
    cnt1 = jnp.sum(o1, axis=0, keepdims=True)
    cnt2 = jnp.sum(o2, axis=0, keepdims=True)

    @pl.when(pl.program_id(0) == 0)
    def _():
        padded = jnp.floor((cnt_ref[...] + (tile - 1)) * (1.0 / tile)) * tile
        r = lax.broadcasted_iota(jnp.int32, (ROUTE_LANES, ROUTE_LANES), 0)
        c = lax.broadcasted_iota(jnp.int32, (ROUTE_LANES, ROUTE_LANES), 1)
        excl = jnp.where(r < c, 1.0, 0.0)
        start_ref[...] = jnp.dot(padded, excl, precision=HIGHEST, preferred_element_type=F32)
        carry_ref[...] = jnp.zeros_like(carry_ref)

    ri = lax.broadcasted_iota(jnp.int32, (tm, tm), 0)
    ci = lax.broadcasted_iota(jnp.int32, (tm, tm), 1)
    earlier = jnp.where(ci < ri, 1.0, 0.0).astype(BF16)
    p1 = jnp.dot(earlier, o1.astype(BF16), preferred_element_type=F32)
    p2 = jnp.dot(earlier, o2.astype(BF16), preferred_element_type=F32)
    base = start_ref[0:1] + carry_ref[0:1]
    pos1 = jnp.sum(o1 * (base + p1), axis=-1, keepdims=True)
    pos2 = jnp.sum(o2 * (base + cnt1 + p2), axis=-1, keepdims=True)
    cols = jnp.where(lane == 0, pos1, jnp.where(lane == 1, pos2, 0.0))
    r = lax.broadcasted_iota(jnp.int32, (SUBLANES_V7X, ROUTE_LANES), 0)
    c = lax.broadcasted_iota(jnp.int32, (SUBLANES_V7X, ROUTE_LANES), 1)
    pick = jnp.where(r == c, 1.0, 0.0)
    rows = lax.dot_general(pick, cols, (((1,), (1,)), ((), ())), precision=HIGHEST, preferred_element_type=F32)
    pos_ref[...] = rows.astype(jnp.int32)
    carry_ref[...] += cnt1 + cnt2


def _dispatch(cfg, route, cnt):
    tile, tm = cfg.moe_tile, cfg.tm
    t = cfg.t_all
    a_total = 2 * t
    pos = pl.pallas_call(
        functools.partial(_dispatch_kernel, tile=tile),
        grid=(t // tm,),
        in_specs=[pl.BlockSpec((tm, ROUTE_LANES), lambda i: (i, 0)), _const_spec((SUBLANES_V7X, ROUTE_LANES))],
        out_specs=pl.BlockSpec((SUBLANES_V7X, tm), lambda i: (0, i)),
        out_shape=jax.ShapeDtypeStruct((SUBLANES_V7X, t), jnp.int32),
        scratch_shapes=[pltpu.VMEM((SUBLANES_V7X, ROUTE_LANES), F32), pltpu.VMEM((SUBLANES_V7X, ROUTE_LANES), F32)],
        compiler_params=_cparams(("arbitrary",)),
        name="moe_dispatch",
    )(route, cnt)
    counts = cnt[0, :MOE_EXPERTS].astype(jnp.int32)
    pend = jnp.cumsum((counts + tile - 1) // tile * tile)
    n_blocks = (a_total + MOE_EXPERTS * (tile - 1)) // tile
    n_blocks = -(-n_blocks // MOE_GATHER_PARTS) * MOE_GATHER_PARTS
    block_eid = jnp.minimum(jnp.sum(pend[None, :] <= (jnp.arange(n_blocks, dtype=jnp.int32) * tile)[:, None], axis=1),
                            MOE_EXPERTS - 1).astype(jnp.int32)
    n_used = (pend[-1] // tile).astype(jnp.int32).reshape(1)
    slots = pos[0:2].reshape(-1)
    slot_tok = (jnp.arange(n_blocks * tile, dtype=jnp.int32) % t).at[slots].set(
        jnp.arange(a_total, dtype=jnp.int32) % t, unique_indices=True, mode="promise_in_bounds")
    return slot_tok, block_eid, n_used, slots


def _combine_kernel(x_ref, y1_ref, y2_ref, rt_ref, mod_ref, g_ref, o_ref, *, final):
    rt = rt_ref[...]
    a_lo, a_hi = _unpack_bf16_pairs(y1_ref[...])
    b_lo, b_hi = _unpack_bf16_pairs(y2_ref[...])
    w1, w2 = rt[:, 2:3], rt[:, 3:4]
    f = jnp.concatenate([w1 * a_lo + w2 * b_lo, w1 * a_hi + w2 * b_hi], axis=1)
    xn = x_ref[...] + mod_ref[0][5:6] * f
    if final:
        xn = (xn * lax.rsqrt(jnp.mean(xn * xn, axis=-1, keepdims=True) + RMS_EPS)) * g_ref[...]
    o_ref[...] = xn


def _combine(cfg, x, y12, route, mod_l, g, final):
    d, tm = cfg.d_model, cfg.tm
    n_tiles = cfg.t_all // tm
    row = lambda w: pl.BlockSpec((tm, w), lambda i: (i, 0))
    t_out = cfg.t_lat if final else cfg.t_all
    return pl.pallas_call(
        functools.partial(_combine_kernel, final=final),
        grid=(t_out // tm,),
        in_specs=[row(d), row(d // 2), pl.BlockSpec((tm, d // 2), lambda i: (n_tiles + i, 0)),
                  row(ROUTE_LANES), _mod_spec(cfg), _const_spec((1, d))],
        out_specs=row(d),
        out_shape=jax.ShapeDtypeStruct((t_out, d), F32),
        compiler_params=_cparams(("arbitrary",)),
        name="moe_combine",
    )(x, y12, y12, route, mod_l, g)


def _forward(cfg, x, c, ctx, c_ctx, ada_w, ada_b, norm1_g, norm2_g, ab_w_in, ab_w_out,
             diff_lq1, diff_lk1, diff_lq2, diff_lk2, diff_subln_g, bconv_w,
             ssd_w_in, ssd_conv_w, ssd_conv_b, ssd_A_log, ssd_dt_bias, ssd_D, ssd_norm_g, ssd_w_out,
             moe_wg, moe_bg, moe_we, moe_be, moe_w_gate, moe_w_up, moe_w_down, final_norm_g):
    d = cfg.d_model
    b = cfg.batch
    pow2 = lambda n: n & (n - 1) == 0
    assert pow2(cfg.n_lat) and pow2(cfg.n_ctx), "sequence positions are taken with a bit mask"
    assert cfg.n_lat % cfg.tm == 0 and cfg.t_ctx % cfg.tm == 0, "row tiles must not straddle latent / context rows"
    assert cfg.tm % cfg.n_ctx == 0 or cfg.n_ctx % cfg.tm == 0, "row tiles hold whole context sequences or parts of one"
    assert cfg.n_lat % (cfg.tq * cfg.attn_subtiles) == 0 and cfg.n_lat % min(cfg.kchunk, cfg.n_lat) == 0
    assert cfg.n_lat % SSD_CHUNK == 0 and cfg.n_ctx % SSD_CHUNK == 0 and cfg.n_lat % GRID_W == 0
    x_pair = (x.reshape(cfg.t_lat, d), ctx.reshape(cfg.t_ctx, d))
    c_all = jnp.zeros((SUBLANES_V7X, d), F32).at[:b].set(c).at[b].set(c_ctx)
    mod = _adaln(cfg, c_all, ada_w, ada_b)
    tables = _rope_tables(cfg)
    ab_w_in_b, ab_w_out_b = _to_bf16(ab_w_in), _to_bf16(ab_w_out)
    ssd_w_in_b, ssd_w_out_b = _to_bf16(ssd_w_in), _to_bf16(ssd_w_out)

    for layer in range(cfg.depth):
        i = layer // 2
        mod_l = mod[layer]
        g1 = norm1_g[layer].reshape(1, d)
        g2 = norm2_g[layer].reshape(1, d)
        router = _router_weights(moe_wg[layer], moe_bg[layer], moe_we[layer], moe_be[layer])
        if layer % 2 == 0:
            lam_init = 0.8 - 0.6 * math.exp(-0.3 * layer)
            q, k, v, gb, u = _ab_in(cfg, x_pair, mod_l, g1, ab_w_in_b, i, tables)
            lvec = jnp.stack([diff_lq1[i], diff_lk1[i], diff_lq2[i], diff_lk2[i]])
            o_pair = _attention(cfg, q, k, v, lvec, diff_subln_g[i].reshape(1, DIFF_V_DIM), lam_init)
            xa, h2, route, cnt = _ab_out(cfg, o_pair, gb, u, bconv_w[i], ab_w_out_b, i, x_pair, mod_l, g2, router)
        else:
            z, xbc_act, dt_raw = _ssd_in(cfg, xa, mod_l, g1, ssd_w_in_b, i, ssd_conv_w[i],
                                         ssd_conv_b[i].reshape(1, SSD_CONV_CH))
            y2 = _ssd_scan(cfg, xbc_act, dt_raw, ssd_dt_bias[i], ssd_A_log[i])
            d_wide = jnp.repeat(ssd_D[i], SSD_HEAD_DIM).reshape(1, SSD_D_INNER)
            xa, h2, route, cnt = _ssd_out(cfg, y2, xbc_act, z, d_wide, ssd_norm_g[i].reshape(1, SSD_D_INNER),
                                          ssd_w_out_b, i, xa, mod_l, g2, router)
        slot_tok, block_eid, n_used, slots = _dispatch(cfg, route, cnt)
        xb_parts = [h2[p_] for p_ in jnp.split(slot_tok, MOE_GATHER_PARTS)]
        yb = _experts(cfg, layer, block_eid, n_used, xb_parts, moe_w_gate, moe_w_up, moe_w_down)
        last = layer == cfg.depth - 1
        y12 = yb[slots]
        xa = _combine(cfg, xa, y12, route, mod_l, final_norm_g.reshape(1, d), last)
        x_pair = (xa, xa)
    return xa.reshape(b, cfg.n_lat, d)


def kernel(x, c, ctx, c_ctx, ada_w, ada_b, norm1_g, norm2_g, ab_w_in, ab_w_out, diff_lq1, diff_lk1, diff_lq2, diff_lk2, diff_subln_g, bconv_w, ssd_w_in, ssd_conv_w, ssd_conv_b, ssd_A_log, ssd_dt_bias, ssd_D, ssd_norm_g, ssd_w_out, moe_wg, moe_bg, moe_we, moe_be, moe_w_gate, moe_w_up, moe_w_down, final_norm_g):
    cfg = Cfg(batch=x.shape[0], n_lat=x.shape[1], n_ctx=ctx.shape[1], d_model=x.shape[2], depth=ada_w.shape[0],
              tm=512, tq=256, moe_tile=512, attn_subtiles=4, kchunk=1024)
    return _forward(cfg, x, c, ctx, c_ctx, ada_w, ada_b, norm1_g, norm2_g, ab_w_in, ab_w_out,
                    diff_lq1, diff_lk1, diff_lq2, diff_lk2, diff_subln_g, bconv_w,
                    ssd_w_in, ssd_conv_w, ssd_conv_b, ssd_A_log, ssd_dt_bias, ssd_D, ssd_norm_g, ssd_w_out,
                    moe_wg, moe_bg, moe_we, moe_be, moe_w_gate, moe_w_up, moe_w_down, final_norm_g)
```
